```python
import jax, jax.numpy as jnp
from jax import lax
import numpy as np

D_MODEL = 1024
BATCH = 4
SEQ = 8192
DEPTH = 4

CTX_LEN = 256
GRID_W = 64
FOURIER_WIDTH = 256
FOURIER_GROUPS = 4
SGU_WIDTH = 256
SGU_GROUPS = 4
CHUNK = 128
N_HEADS = 8
N_KV_HEADS = 2
HEAD_DIM = 64
ATTN_WIDTH = N_HEADS * HEAD_DIM
KV_WIDTH = N_KV_HEADS * HEAD_DIM
WINDOW = 128
BLOCK = 128
ROPE_THETA = 10000.0
N_BRANCHES = 3
EPS = 1e-6
NEG_INF = -1e30
IN_WIDTH = 2 * FOURIER_WIDTH + 3 * SGU_WIDTH + 2 * ATTN_WIDTH + 2 * KV_WIDTH + N_BRANCHES * D_MODEL
KV_OFFSET = 2 * FOURIER_WIDTH + 3 * SGU_WIDTH + ATTN_WIDTH

kernel_name = "hybrid_fourier_sgu_window_gqa_trunk"


def _split_proj(p):
    sizes = (FOURIER_WIDTH, FOURIER_WIDTH, SGU_WIDTH, SGU_WIDTH, SGU_WIDTH,
             ATTN_WIDTH, KV_WIDTH, KV_WIDTH, ATTN_WIDTH, N_BRANCHES * D_MODEL)
    offs = [int(o) for o in np.cumsum(sizes)[:-1]]
    return jnp.split(p, offs, axis=-1)


def _rms(x):
    x32 = x.astype(jnp.float32)
    return (x32 * lax.rsqrt(jnp.mean(x32 * x32, axis=-1, keepdims=True) + EPS)).astype(x.dtype)


def _axial_rope(x, row, col):
    half = HEAD_DIM // 2
    nf = half // 2
    inv = ROPE_THETA ** (-jnp.arange(nf, dtype=jnp.float32) / nf)

    def rot(xp, pos):
        ang = pos.astype(jnp.float32)[:, None] * inv[None, :]
        cos = jnp.cos(ang)[None, :, None, :]
        sin = jnp.sin(ang)[None, :, None, :]
        x1, x2 = xp[..., :nf], xp[..., nf:]
        return jnp.concatenate([x1 * cos - x2 * sin, x2 * cos + x1 * sin], axis=-1)

    x32 = x.astype(jnp.float32)
    return jnp.concatenate([rot(x32[..., :half], row), rot(x32[..., half:], col)], axis=-1).astype(x.dtype)


def _fourier(f):
    B, L, W = f.shape
    fg = f.reshape(B, L, FOURIER_GROUPS, W // FOURIER_GROUPS).astype(jnp.float32)
    y = jnp.fft.fft2(fg, axes=(1, 3), norm="ortho").real
    return y.reshape(B, L, W).astype(f.dtype)


def _sgu(u, v, w_s, b_s):
    B, L, W = v.shape
    nc = L // CHUNK
    gw = W // SGU_GROUPS
    vn = _rms(v).reshape(B, nc, CHUNK, SGU_GROUPS, gw)
    mixed = jnp.einsum('gst,bctgd->bcsgd', w_s, vn) + b_s.T[:, :, None]
    return u * mixed.reshape(B, L, W)


def _latent_attn(q, k, v, kc, vc, sinks):
    B, L = q.shape[0], q.shape[1]
    C = kc.shape[1]
    nb = L // BLOCK
    G = N_HEADS // N_KV_HEADS
    qb = q.reshape(B, nb, BLOCK, N_KV_HEADS, G, HEAD_DIM)
    pad = ((0, 0), (BLOCK, BLOCK), (0, 0), (0, 0))
    kp = jnp.pad(k, pad).reshape(B, nb + 2, BLOCK, N_KV_HEADS, HEAD_DIM)
    vp = jnp.pad(v, pad).reshape(B, nb + 2, BLOCK, N_KV_HEADS, HEAD_DIM)

    def band(t):
        return jnp.concatenate([t[:, :-2], t[:, 1:-1], t[:, 2:]], axis=2)

    kb, vb = band(kp), band(vp)
    qi = jnp.arange(BLOCK)[:, None]
    kj = jnp.arange(3 * BLOCK)[None, :]
    in_win = jnp.abs(kj - BLOCK - qi) <= WINDOW
    kpos = jnp.arange(nb)[:, None] * BLOCK - BLOCK + jnp.arange(3 * BLOCK)[None, :]
    in_seq = (kpos >= 0) & (kpos < L)
    mask = in_win[None] & in_seq[:, None, :]
    scale = HEAD_DIM ** -0.5
    s_loc = jnp.einsum('bnqhgd,bnkhd->bnhgqk', qb, kb).astype(jnp.float32) * scale
    s_loc = jnp.where(mask[None, :, None, None], s_loc, NEG_INF)
    s_ctx = jnp.einsum('bnqhgd,bchd->bnhgqc', qb, kc).astype(jnp.float32) * scale
    s_sink = jnp.broadcast_to(sinks.astype(jnp.float32).reshape(N_KV_HEADS, G, 1, 1), s_loc.shape[:-1] + (1,))
    probs = jax.nn.softmax(jnp.concatenate([s_loc, s_ctx, s_sink], axis=-1), axis=-1)
    p_loc = probs[..., :3 * BLOCK].astype(v.dtype)
    p_ctx = probs[..., 3 * BLOCK:3 * BLOCK + C].astype(v.dtype)
    o = (jnp.einsum('bnhgqk,bnkhd->bnqhgd', p_loc, vb)
         + jnp.einsum('bnhgqc,bchd->bnqhgd', p_ctx, vc))
    return o.reshape(B, L, ATTN_WIDTH)


def _ctx_attn(q, k, v, sinks):
    B, C = q.shape[0], q.shape[1]
    G = N_HEADS // N_KV_HEADS
    qg = q.reshape(B, C, N_KV_HEADS, G, HEAD_DIM)
    s = jnp.einsum('bqhgd,bkhd->bhgqk', qg, k).astype(jnp.float32) * (HEAD_DIM ** -0.5)
    s_sink = jnp.broadcast_to(sinks.astype(jnp.float32).reshape(N_KV_HEADS, G, 1, 1), s.shape[:-1] + (1,))
    probs = jax.nn.softmax(jnp.concatenate([s, s_sink], axis=-1), axis=-1)[..., :C].astype(v.dtype)
    return jnp.einsum('bhgqk,bkhd->bqhgd', probs, v).reshape(B, C, ATTN_WIDTH)


def _mixer_out(fa, za, u, vs, zb, attn, zc, g, w_s, b_s, w_pa, w_pb, w_pc, w_out):
    ya = _fourier(fa) * jax.nn.silu(za)
    yb = _sgu(u, vs, w_s, b_s) * jax.nn.silu(zb)
    yc = attn * jax.nn.silu(zc)
    ga, gb, gc = jnp.split(jax.nn.sigmoid(g), N_BRANCHES, axis=-1)
    merged = ga * (ya @ w_pa) + gb * (yb @ w_pb) + gc * (yc @ w_pc)
    return merged @ w_out


def _qk(t, n_heads, gain):
    B, L = t.shape[0], t.shape[1]
    return _rms(t.reshape(B, L, n_heads, HEAD_DIM)) * gain


def setup_inputs(seed: int = 0) -> dict:
    key = jax.random.key(seed)
    ks = jax.random.split(key, 16)
    f32 = jnp.float32
    D = D_MODEL
    return {
        "x": jax.random.normal(ks[0], (BATCH, SEQ, D), f32),
        "c": jax.random.normal(ks[1], (BATCH, D), f32),
        "ctx": jax.random.normal(ks[2], (BATCH, CTX_LEN, D), f32),
        "c_ctx": jax.random.normal(ks[3], (D,), f32),
        "w_ada": jax.random.normal(ks[4], (DEPTH, D, 3 * D), f32) * (0.5 * D ** -0.5),
        "b_ada": jax.random.normal(ks[5], (DEPTH, 3 * D), f32) * 0.02,
        "w_in": jax.random.normal(ks[6], (DEPTH, D, IN_WIDTH), f32) * D ** -0.5,
        "sgu_w": jax.random.normal(ks[7], (DEPTH, SGU_GROUPS, CHUNK, CHUNK), f32) * CHUNK ** -0.5,
        "sgu_b": 1.0 + 0.02 * jax.random.normal(ks[8], (DEPTH, SGU_GROUPS, CHUNK), f32),
        "q_norm_g": 1.0 + 0.02 * jax.random.normal(ks[9], (DEPTH, HEAD_DIM), f32),
        "k_norm_g": 1.0 + 0.02 * jax.random.normal(ks[10], (DEPTH, HEAD_DIM), f32),
        "attn_sink": jax.random.normal(ks[11], (DEPTH, N_HEADS), f32),
        "w_pa": jax.random.normal(ks[12], (DEPTH, FOURIER_WIDTH, D), f32) * FOURIER_WIDTH ** -0.5,
        "w_pb": jax.random.normal(ks[13], (DEPTH, SGU_WIDTH, D), f32) * SGU_WIDTH ** -0.5,
        "w_pc": jax.random.normal(ks[14], (DEPTH, ATTN_WIDTH, D), f32) * ATTN_WIDTH ** -0.5,
        "w_out": jax.random.normal(ks[15], (DEPTH, D, D), f32) * D ** -0.5,
    }


def reference(x, c, ctx, c_ctx, w_ada, b_ada, w_in, sgu_w, sgu_b, q_norm_g, k_norm_g,
              attn_sink, w_pa, w_pb, w_pc, w_out):
    B, L, _ = x.shape
    C = ctx.shape[1]
    rows = L // GRID_W
    row = jnp.repeat(jnp.arange(rows), GRID_W)
    col = jnp.tile(jnp.arange(GRID_W), rows)
    s_c = jax.nn.silu(c)
    s_cc = jax.nn.silu(c_ctx)
    xc = ctx
    for l in range(DEPTH):
        last = l == DEPTH - 1
        shift, scale, gate = jnp.split(s_c @ w_ada[l] + b_ada[l], 3, axis=-1)
        shift_c, scale_c, gate_c = jnp.split(s_cc @ w_ada[l] + b_ada[l], 3, axis=-1)
        h = _rms(x) * (1.0 + scale[:, None, :]) + shift[:, None, :]
        hc = _rms(xc) * (1.0 + scale_c) + shift_c

        fa, za, u, vs, zb, q, k, va, zc, g = _split_proj(h @ w_in[l])
        q = _axial_rope(_qk(q, N_HEADS, q_norm_g[l]), row, col)
        k = _axial_rope(_qk(k, N_KV_HEADS, k_norm_g[l]), row, col)
        va = va.reshape(B, L, N_KV_HEADS, HEAD_DIM)

        if last:
            kc, vc = jnp.split(hc @ w_in[l][:, KV_OFFSET:KV_OFFSET + 2 * KV_WIDTH], 2, axis=-1)
        else:
            fa_c, za_c, u_c, vs_c, zb_c, qc, kc, vc, zc_c, g_c = _split_proj(hc @ w_in[l])
        kc = _qk(kc, N_KV_HEADS, k_norm_g[l])
        vc = vc.reshape(B, C, N_KV_HEADS, HEAD_DIM)

        attn = _latent_attn(q, k, va, kc, vc, attn_sink[l])
        out = _mixer_out(fa, za, u, vs, zb, attn, zc, g, sgu_w[l], sgu_b[l],
                         w_pa[l], w_pb[l], w_pc[l], w_out[l])

        if not last:
            qc = _qk(qc, N_HEADS, q_norm_g[l])
            attn_c = _ctx_attn(qc, kc, vc, attn_sink[l])
            out_c = _mixer_out(fa_c, za_c, u_c, vs_c, zb_c, attn_c, zc_c, g_c, sgu_w[l], sgu_b[l],
                               w_pa[l], w_pb[l], w_pc[l], w_out[l])
            xc = xc + gate_c * out_c
        x = x + gate[:, None, :] * out
    return x
```

```python
import functools
import math

import jax
import jax.numpy as jnp
from jax import lax
from jax.experimental import pallas as pl
from jax.experimental.pallas import tpu as pltpu

F32 = jnp.float32
BF16 = jnp.bfloat16

D_MODEL = 1024
DEPTH = 4
GRID_W = 64
FOURIER_WIDTH = 256
FOURIER_GROUPS = 4
SGU_WIDTH = 256
SGU_GROUPS = 4
CHUNK = 128
N_HEADS = 8
N_KV_HEADS = 2
HEAD_DIM = 64
ATTN_WIDTH = N_HEADS * HEAD_DIM
KV_WIDTH = N_KV_HEADS * HEAD_DIM
BLOCK = 128
ROPE_THETA = 10000.0
EPS = 1e-6
NEG_INF = -1e30

LANES = 128
MOD_ROWS = 8

KV2_WIDTH = 2 * KV_WIDTH
W_FA = 0
W_MAIN = FOURIER_WIDTH
MAIN_ZA = 0
MAIN_U = MAIN_ZA + FOURIER_WIDTH
MAIN_VS = MAIN_U + SGU_WIDTH
MAIN_ZB = MAIN_VS + SGU_WIDTH
MAIN_ZC = MAIN_ZB + SGU_WIDTH
MAIN_G = MAIN_ZC + ATTN_WIDTH
MAIN_Q = MAIN_G + 3 * D_MODEL
MAIN_WIDTH = MAIN_Q + ATTN_WIDTH
W_Q = W_MAIN + MAIN_Q
W_K2 = W_MAIN + MAIN_WIDTH
W_V2 = W_K2 + KV2_WIDTH
W_TOTAL = W_V2 + KV2_WIDTH
PROJ_CHUNK = 512

DFT_L1 = 64
DFT_L2 = 128
FOURIER_CW = 128

VMEM_LIMIT = 56 * 1024 * 1024


def _silu(z):
    return 0.5 * z * (1.0 + jnp.tanh(0.5 * z))


def _sigmoid(z):
    return 0.5 * (1.0 + jnp.tanh(0.5 * z))


def _dot(a, b):
    return jnp.dot(a, b, preferred_element_type=F32)


def _dot_nt(a, b):
    return lax.dot_general(a, b, (((1,), (1,)), ((), ())), preferred_element_type=F32)


def _ada_kernel(c_ref, w_ref, b_ref, o_ref):
    s = _silu(c_ref[...]).astype(BF16)
    o_ref[0] = _dot(s, w_ref[0].astype(BF16)) + b_ref[0]


def _ada_call(cs, w_ada, b_ada):
    d = D_MODEL
    return pl.pallas_call(
        _ada_kernel,
        grid=(DEPTH, 3),
        in_specs=[
            pl.BlockSpec((MOD_ROWS, d), lambda l, j: (0, 0)),
            pl.BlockSpec((1, d, d), lambda l, j: (l, 0, j)),
            pl.BlockSpec((1, 1, d), lambda l, j: (l, 0, j)),
        ],
        out_specs=pl.BlockSpec((1, MOD_ROWS, d), lambda l, j: (l, 0, j)),
        out_shape=jax.ShapeDtypeStruct((DEPTH, MOD_ROWS, 3 * d), F32),
        compiler_params=pltpu.CompilerParams(
            dimension_semantics=("arbitrary", "arbitrary"), vmem_limit_bytes=VMEM_LIMIT),
        name="ada_mod",
    )(cs, w_ada, b_ada.reshape(DEPTH, 1, 3 * d))


def _norm_rope(t, bd_ref, gain, cos, sin, rope, out_scale):
    width = t.shape[-1]
    ssq = _dot((t * t).astype(BF16), bd_ref[...])
    tn = t * lax.rsqrt(ssq * (1.0 / HEAD_DIM) + EPS) * gain
    if rope:
        lane = lax.broadcasted_iota(jnp.int32, tn.shape, 1)
        first = (lane % 32) < 16
        partner = jnp.where(first, pltpu.roll(tn, width - 16, 1), pltpu.roll(tn, 16, 1))
        tn = tn * cos + partner * sin
    if out_scale != 1.0:
        tn = tn * out_scale
    return tn


def _proj_kernel(x_ref, mod_ref, w_ref, bdq_ref, bdk_ref, gq_ref, gk_ref,
                 cosq_ref, sinq_ref, cosk_ref, sinkv_ref,
                 fa_ref, main_ref, k2_ref, v2_ref, *, ctx_row, rope):
    d = D_MODEL
    xt = x_ref[0]
    ms = jnp.mean(xt * xt, axis=-1, keepdims=True)
    row = pl.program_id(0) if ctx_row is None else ctx_row
    modrow = mod_ref[pl.ds(row, 1), :]
    shift = modrow[:, 0:d]
    scale = modrow[:, d:2 * d]
    hb = ((xt * lax.rsqrt(ms + EPS)) * (1.0 + scale) + shift).astype(BF16)

    fa_ref[0] = _dot(hb, w_ref[:, W_FA:W_FA + FOURIER_WIDTH])
    for c0 in range(0, MAIN_Q, PROJ_CHUNK):
        main_ref[0, :, c0:c0 + PROJ_CHUNK] = _dot(
            hb, w_ref[:, W_MAIN + c0:W_MAIN + c0 + PROJ_CHUNK]).astype(BF16)

    q = _dot(hb, w_ref[:, W_Q:W_Q + ATTN_WIDTH])
    cq = cosq_ref[...] if rope else None
    sq = sinq_ref[...] if rope else None
    q = _norm_rope(q, bdq_ref, gq_ref[...], cq, sq, rope, HEAD_DIM ** -0.5)
    main_ref[0, :, MAIN_Q:MAIN_Q + ATTN_WIDTH] = q.astype(BF16)

    k2 = _dot(hb, w_ref[:, W_K2:W_K2 + KV2_WIDTH])
    ck = cosk_ref[...] if rope else None
    sk = sinkv_ref[...] if rope else None
    k2 = _norm_rope(k2, bdk_ref, gk_ref[...], ck, sk, rope, 1.0)
    k2_ref[0] = k2.astype(BF16)
    v2_ref[0] = _dot(hb, w_ref[:, W_V2:W_V2 + KV2_WIDTH]).astype(BF16)


def _proj_call(x, mod_l, wcat, bdq, bdk, gq, gk, rope_tables, *, tile, ctx_row):
    b, length, d = x.shape
    nt = length // tile
    const = lambda bi, i: (0, 0)
    tok = lambda bi, i: (bi, i, 0)
    rope = rope_tables is not None
    if rope:
        cosq, sinq, cosk, sink = rope_tables
        pos, pos_rows = (lambda bi, i: (i, 0)), tile
    else:
        cosq = sinq = jnp.zeros((MOD_ROWS, ATTN_WIDTH), F32)
        cosk = sink = jnp.zeros((MOD_ROWS, KV2_WIDTH), F32)
        pos, pos_rows = const, MOD_ROWS
    kern = functools.partial(_proj_kernel, ctx_row=ctx_row, rope=rope)
    return pl.pallas_call(
        kern,
        grid=(b, nt),
        in_specs=[
            pl.BlockSpec((1, tile, d), tok),
            pl.BlockSpec((MOD_ROWS, 3 * d), const),
            pl.BlockSpec((d, W_TOTAL), const, pipeline_mode=pl.Buffered(1)),
            pl.BlockSpec((ATTN_WIDTH, ATTN_WIDTH), const),
            pl.BlockSpec((KV2_WIDTH, KV2_WIDTH), const),
            pl.BlockSpec((1, ATTN_WIDTH), const),
            pl.BlockSpec((1, KV2_WIDTH), const),
            pl.BlockSpec((pos_rows, ATTN_WIDTH), pos),
            pl.BlockSpec((pos_rows, ATTN_WIDTH), pos),
            pl.BlockSpec((pos_rows, KV2_WIDTH), pos),
            pl.BlockSpec((pos_rows, KV2_WIDTH), pos),
        ],
        out_specs=[
            pl.BlockSpec((1, tile, FOURIER_WIDTH), tok),
            pl.BlockSpec((1, tile, MAIN_WIDTH), tok),
            pl.BlockSpec((1, tile, KV2_WIDTH), tok),
            pl.BlockSpec((1, tile, KV2_WIDTH), tok),
        ],
        out_shape=[
            jax.ShapeDtypeStruct((b, length, FOURIER_WIDTH), F32),
            jax.ShapeDtypeStruct((b, length, MAIN_WIDTH), BF16),
            jax.ShapeDtypeStruct((b, length, KV2_WIDTH), BF16),
            jax.ShapeDtypeStruct((b, length, KV2_WIDTH), BF16),
        ],
        compiler_params=pltpu.CompilerParams(
            dimension_semantics=("arbitrary", "arbitrary"), vmem_limit_bytes=VMEM_LIMIT),
        name="proj_ctx" if ctx_row is not None else "proj_lat",
    )(x, mod_l, wcat, bdq, bdk, gq, gk, cosq, sinq, cosk, sink)


def _fourier_kernel(fa_ref, w1_ref, w2_ref, cs_ref, y_ref, s_ref):
    l1n, l2n = DFT_L1, DFT_L2

    def stage1(l2, carry):
        xs = fa_ref[0, pl.ds(l2, l1n, stride=l2n), :].astype(BF16)
        a = _dot(w1_ref[l2], xs)
        s_ref[pl.ds(pl.multiple_of(l2 * (2 * l1n), 2 * l1n), 2 * l1n), :] = a
        return carry

    lax.fori_loop(0, l2n, stage1, 0, unroll=4)

    def stage2(k1, carry):
        rr = s_ref[pl.ds(k1, l2n, stride=2 * l1n), :]
        ri = s_ref[pl.ds(l1n + k1, l2n, stride=2 * l1n), :]
        r = jnp.concatenate([rr, ri], axis=0).astype(BF16)
        f = _dot(w2_ref[...], r)
        fc = jnp.concatenate([f[:l2n], f[l2n:]], axis=1).astype(BF16)
        y_ref[0, pl.ds(k1, l2n, stride=l1n), :] = _dot(fc, cs_ref[...])
        return carry

    lax.fori_loop(0, l1n, stage2, 0, unroll=2)


def _fourier_call(fa, w1, w2, cs):
    b, length, width = fa.shape
    cw = FOURIER_CW
    return pl.pallas_call(
        _fourier_kernel,
        grid=(b, width // cw),
        in_specs=[
            pl.BlockSpec((1, length, cw), lambda bi, j: (bi, 0, j)),
            pl.BlockSpec(w1.shape, lambda bi, j: (0, 0, 0)),
            pl.BlockSpec(w2.shape, lambda bi, j: (0, 0)),
            pl.BlockSpec(cs.shape, lambda bi, j: (0, 0)),
        ],
        out_specs=pl.BlockSpec((1, length, cw), lambda bi, j: (bi, 0, j)),
        out_shape=jax.ShapeDtypeStruct((b, length, width), F32),
        scratch_shapes=[pltpu.VMEM((DFT_L2 * 2 * DFT_L1, cw), F32)],
        compiler_params=pltpu.CompilerParams(
            dimension_semantics=("arbitrary", "arbitrary"), vmem_limit_bytes=VMEM_LIMIT),
        name="fourier_lat",
    )(fa, w1, w2, cs)


def _fourier_ctx_kernel(fa_ref, wc_ref, cs_ref, y_ref):
    n = fa_ref.shape[1]
    f = _dot(wc_ref[...], fa_ref[0].astype(BF16))
    fc = jnp.concatenate([f[:n], f[n:]], axis=1).astype(BF16)
    y_ref[0] = _dot(fc, cs_ref[...])


def _fourier_ctx_call(fa, wc, cs):
    b, length, width = fa.shape
    return pl.pallas_call(
        _fourier_ctx_kernel,
        grid=(b,),
        in_specs=[
            pl.BlockSpec((1, length, width), lambda bi: (bi, 0, 0)),
            pl.BlockSpec(wc.shape, lambda bi: (0, 0)),
            pl.BlockSpec(cs.shape, lambda bi: (0, 0)),
        ],
        out_specs=pl.BlockSpec((1, length, width), lambda bi: (bi, 0, 0)),
        out_shape=jax.ShapeDtypeStruct((b, length, width), F32),
        compiler_params=pltpu.CompilerParams(
            dimension_semantics=("arbitrary",), vmem_limit_bytes=VMEM_LIMIT),
        name="fourier_ctx",
    )(fa, wc, cs)


def _mix_kernel(*refs, local, tile, n_blocks, ctx_row):
    if local:
        (x_ref, main_ref, y_ref, kp_ref, km_ref, kn_ref, vp_ref, vm_ref, vn_ref,
         kc_ref, vc_ref, mod_ref, sink_ref, wsg_ref, bsg_ref,
         wpa_ref, wpb_ref, wpc_ref, wout_ref, o_ref,
         kmask, vmask, kcmask, vcmask, ya_s, yb_s, yc_s) = refs
    else:
        (x_ref, main_ref, y_ref, kc_ref, vc_ref, mod_ref, sink_ref, wsg_ref, bsg_ref,
         wpa_ref, wpb_ref, wpc_ref, wout_ref, o_ref,
         kcmask, vcmask, ya_s, yb_s, yc_s) = refs
    d = D_MODEL
    nsub = tile // BLOCK
    tile_idx = pl.program_id(1)

    def fill_masked(dst, rows, src):
        half = lax.broadcasted_iota(jnp.int32, (src.shape[0], LANES), 1) // HEAD_DIM
        for h in range(N_KV_HEADS):
            grp = src[:, h * LANES:(h + 1) * LANES]
            for p in range(2):
                dst[h * 2 + p, rows, :] = jnp.where(half == p, grp, jnp.zeros_like(grp))

    fill_masked(kcmask, slice(None), kc_ref[0])
    fill_masked(vcmask, slice(None), vc_ref[0])
    if local:
        for dst, (a, m, c) in ((kmask, (kp_ref, km_ref, kn_ref)), (vmask, (vp_ref, vm_ref, vn_ref))):
            fill_masked(dst, slice(0, BLOCK), a[0])
            fill_masked(dst, slice(BLOCK, BLOCK + tile), m[0])
            fill_masked(dst, slice(BLOCK + tile, 2 * BLOCK + tile), c[0])

    ya_s[...] = (y_ref[0] * _silu(main_ref[0, :, MAIN_ZA:MAIN_ZA + FOURIER_WIDTH].astype(F32))).astype(BF16)

    qi = lax.broadcasted_iota(jnp.int32, (BLOCK, BLOCK), 0)
    kj = lax.broadcasted_iota(jnp.int32, (BLOCK, BLOCK), 1)
    sgu_grp = lax.broadcasted_iota(jnp.int32, (CHUNK, SGU_WIDTH), 1) // (SGU_WIDTH // SGU_GROUPS)

    def sub_block(n, carry):
        r0 = pl.multiple_of(n * BLOCK, BLOCK)
        rows = pl.ds(r0, BLOCK)

        vs = main_ref[0, rows, MAIN_VS:MAIN_VS + SGU_WIDTH].astype(F32)
        vn = (vs * lax.rsqrt(jnp.mean(vs * vs, axis=-1, keepdims=True) + EPS)).astype(BF16)
        rhs = jnp.concatenate(
            [jnp.where(sgu_grp == g, vn, jnp.zeros_like(vn)) for g in range(SGU_GROUPS)], axis=0)
        mixed = _dot(wsg_ref[...], rhs) + bsg_ref[...]
        u = main_ref[0, rows, MAIN_U:MAIN_U + SGU_WIDTH].astype(F32)
        zb = main_ref[0, rows, MAIN_ZB:MAIN_ZB + SGU_WIDTH].astype(F32)
        yb_s[rows, :] = (u * mixed * _silu(zb)).astype(BF16)

        if local:
            gblk = tile_idx * nsub + n
            prev_ok = kj >= qi + jnp.where(gblk > 0, 0, BLOCK)
            next_ok = kj <= qi - jnp.where(gblk < n_blocks - 1, 0, BLOCK)
        for j in range(ATTN_WIDTH // LANES):
            h = j // 2
            qc = main_ref[0, rows, MAIN_Q + j * LANES:MAIN_Q + (j + 1) * LANES]
            acc = None
            for p in range(2):
                idx = h * 2 + p
                sink = sink_ref[2 * j + p]
                s_ctx = _dot_nt(qc, kcmask[idx])
                m = jnp.maximum(jnp.max(s_ctx, axis=-1, keepdims=True), sink)
                if local:
                    s_loc = _dot_nt(qc, kmask[idx, pl.ds(r0, 3 * BLOCK), :])
                    s_prev = jnp.where(prev_ok, s_loc[:, 0:BLOCK], NEG_INF)
                    s_own = s_loc[:, BLOCK:2 * BLOCK]
                    s_next = jnp.where(next_ok, s_loc[:, 2 * BLOCK:3 * BLOCK], NEG_INF)
                    m_loc = jnp.max(jnp.maximum(jnp.maximum(s_prev, s_own), s_next),
                                    axis=-1, keepdims=True)
                    m = jnp.maximum(m, m_loc)
                e_ctx = jnp.exp(s_ctx - m)
                den = jnp.sum(e_ctx, axis=-1, keepdims=True) + jnp.exp(sink - m)
                o = _dot(e_ctx.astype(BF16), vcmask[idx])
                if local:
                    e_loc = jnp.concatenate(
                        [jnp.exp(s_prev - m), jnp.exp(s_own - m), jnp.exp(s_next - m)], axis=1)
                    den = den + jnp.sum(e_loc, axis=-1, keepdims=True)
                    o = o + _dot(e_loc.astype(BF16), vmask[idx, pl.ds(r0, 3 * BLOCK), :])
                o = o / den
                acc = o if acc is None else acc + o
            zc = main_ref[0, rows, MAIN_ZC + j * LANES:MAIN_ZC + (j + 1) * LANES].astype(F32)
            yc_s[rows, j * LANES:(j + 1) * LANES] = (acc * _silu(zc)).astype(BF16)
        return carry

    lax.fori_loop(0, nsub, sub_block, 0)

    g = main_ref[0, :, MAIN_G:MAIN_G + 3 * d].astype(F32)
    merged = _sigmoid(g[:, 0:d]) * _dot(ya_s[...], wpa_ref[...])
    merged = merged + _sigmoid(g[:, d:2 * d]) * _dot(yb_s[...], wpb_ref[...])
    merged = merged + _sigmoid(g[:, 2 * d:3 * d]) * _dot(yc_s[...], wpc_ref[...])
    out = _dot(merged.astype(BF16), wout_ref[...])
    row = pl.program_id(0) if ctx_row is None else ctx_row
    gate = mod_ref[pl.ds(row, 1), 2 * d:3 * d]
    o_ref[0] = x_ref[0] + gate * out


def _mix_call(x, main, y, k2, v2, kc2, vc2, mod_l, sinks, wsg, bsg, wpa, wpb, wpc, wout,
              *, tile, local, ctx_row):
    b, length, d = x.shape
    nt = length // tile
    n_ctx = kc2.shape[1]
    sub = tile // BLOCK
    n_blocks = length // BLOCK
    const2 = lambda bi, i: (0, 0)
    tok = lambda bi, i: (bi, i, 0)
    per_b = lambda bi, i: (bi, 0, 0)
    prev_map = lambda bi, i: (bi, jnp.maximum(i * sub - 1, 0), 0)
    next_map = lambda bi, i: (bi, jnp.minimum((i + 1) * sub, n_blocks - 1), 0)

    in_specs = [
        pl.BlockSpec((1, tile, d), tok),
        pl.BlockSpec((1, tile, MAIN_WIDTH), tok),
        pl.BlockSpec((1, tile, FOURIER_WIDTH), tok),
    ]
    args = [x, main, y]
    if local:
        for arr in (k2, v2):
            in_specs += [
                pl.BlockSpec((1, BLOCK, KV2_WIDTH), prev_map),
                pl.BlockSpec((1, tile, KV2_WIDTH), tok),
                pl.BlockSpec((1, BLOCK, KV2_WIDTH), next_map),
            ]
            args += [arr, arr, arr]
    in_specs += [
        pl.BlockSpec((1, n_ctx, KV2_WIDTH), per_b),
        pl.BlockSpec((1, n_ctx, KV2_WIDTH), per_b),
        pl.BlockSpec((MOD_ROWS, 3 * d), const2),
        pl.BlockSpec(memory_space=pltpu.SMEM),
        pl.BlockSpec(wsg.shape, const2),
        pl.BlockSpec(bsg.shape, const2),
        pl.BlockSpec(wpa.shape, const2),
        pl.BlockSpec(wpb.shape, const2),
        pl.BlockSpec(wpc.shape, const2),
        pl.BlockSpec(wout.shape, const2),
    ]
    args += [kc2, vc2, mod_l, sinks, wsg, bsg, wpa, wpb, wpc, wout]

    scratch = []
    if local:
        scratch += [pltpu.VMEM((4, tile + 2 * BLOCK, LANES), BF16),
                    pltpu.VMEM((4, tile + 2 * BLOCK, LANES), BF16)]
    scratch += [
        pltpu.VMEM((4, n_ctx, LANES), BF16),
        pltpu.VMEM((4, n_ctx, LANES), BF16),
        pltpu.VMEM((tile, FOURIER_WIDTH), BF16),
        pltpu.VMEM((tile, SGU_WIDTH), BF16),
        pltpu.VMEM((tile, ATTN_WIDTH), BF16),
    ]
    kern = functools.partial(_mix_kernel, local=local, tile=tile, n_blocks=n_blocks, ctx_row=ctx_row)
    return pl.pallas_call(
        kern,
        grid=(b, nt),
        in_specs=in_specs,
        out_specs=pl.BlockSpec((1, tile, d), tok),
        out_shape=jax.ShapeDtypeStruct((b, length, d), F32),
        scratch_shapes=scratch,
        compiler_params=pltpu.CompilerParams(
            dimension_semantics=("arbitrary", "arbitrary"), vmem_limit_bytes=VMEM_LIMIT),
        name="mix_lat" if local else "mix_ctx",
    )(*args)


def _rope_tables(length):
    pos = jnp.arange(length)
    nf = HEAD_DIM // 4
    inv = ROPE_THETA ** (-jnp.arange(nf, dtype=F32) / nf)
    ang_r = (pos // GRID_W).astype(F32)[:, None] * inv[None, :]
    ang_c = (pos % GRID_W).astype(F32)[:, None] * inv[None, :]
    ang = jnp.concatenate([ang_r, ang_r, ang_c, ang_c], axis=1)
    sign = jnp.concatenate([-jnp.ones((nf,), F32), jnp.ones((nf,), F32)] * 2)
    cos = jnp.cos(ang)
    sin = jnp.sin(ang) * sign[None, :]
    return (jnp.tile(cos, (1, N_HEADS)), jnp.tile(sin, (1, N_HEADS)),
            jnp.tile(cos, (1, 2 * N_KV_HEADS)), jnp.tile(sin, (1, 2 * N_KV_HEADS)))


def _angle(num, den):
    return (num % den).astype(F32) * (2.0 * math.pi / den)


def _channel_dft(width):
    gw = FOURIER_WIDTH // FOURIER_GROUPS
    m = jnp.arange(width)
    same = (m[:, None] // gw) == (m[None, :] // gw)
    th = _angle((m[:, None] % gw) * (m[None, :] % gw), gw)
    cc = jnp.where(same, jnp.cos(th), 0.0) * gw ** -0.5
    sc = jnp.where(same, jnp.sin(th), 0.0) * gw ** -0.5
    return jnp.concatenate([cc, sc], axis=0).astype(BF16)


def _fourier_tables():
    l1n, l2n = DFT_L1, DFT_L2
    length = l1n * l2n
    l2 = jnp.arange(l2n)[:, None, None]
    k1 = jnp.arange(l1n)[None, :, None]
    l1 = jnp.arange(l1n)[None, None, :]
    th = _angle(k1 * l1 * l2n + k1 * l2, length)
    w1 = (jnp.concatenate([jnp.cos(th), -jnp.sin(th)], axis=1) * l1n ** -0.5).astype(BF16)
    k2 = jnp.arange(l2n)
    th2 = _angle(k2[:, None] * k2[None, :], l2n)
    c2, s2 = jnp.cos(th2), jnp.sin(th2)
    w2 = (jnp.concatenate([jnp.concatenate([c2, s2], axis=1),
                           jnp.concatenate([-s2, c2], axis=1)], axis=0) * l2n ** -0.5).astype(BF16)
    return w1, w2, _channel_dft(FOURIER_CW)


def _fourier_ctx_tables(n):
    k = jnp.arange(n)
    th = _angle(k[:, None] * k[None, :], n)
    wc = (jnp.concatenate([jnp.cos(th), -jnp.sin(th)], axis=0) * n ** -0.5).astype(BF16)
    return wc, _channel_dft(FOURIER_WIDTH)


def _block_diag_ones(width):
    m = jnp.arange(width) // HEAD_DIM
    return (m[:, None] == m[None, :]).astype(BF16)


def _dup_heads(w):
    h0, h1 = w[..., :HEAD_DIM], w[..., HEAD_DIM:]
    return jnp.concatenate([h0, h0, h1, h1], axis=-1)


def _reorder_w_in(w):
    fw, sw, aw, kw = FOURIER_WIDTH, SGU_WIDTH, ATTN_WIDTH, KV_WIDTH
    o_q = 2 * fw + 3 * sw
    o_k = o_q + aw
    o_v = o_k + kw
    o_zc = o_v + kw
    o_g = o_zc + aw
    parts = [w[:, 0:fw], w[:, fw:o_q], w[:, o_zc:o_g], w[:, o_g:], w[:, o_q:o_k],
             _dup_heads(w[:, o_k:o_v]), _dup_heads(w[:, o_v:o_zc])]
    return jnp.concatenate(parts, axis=1).astype(BF16)


def kernel(x, c, ctx, c_ctx, w_ada, b_ada, w_in, sgu_w, sgu_b, q_norm_g, k_norm_g,
           attn_sink, w_pa, w_pb, w_pc, w_out):
    b, length, d = x.shape
    n_ctx = ctx.shape[1]
    assert length == DFT_L1 * DFT_L2 and d == D_MODEL and b + 1 <= MOD_ROWS
    lat_tile = 512
    ctx_row = b

    cs = jnp.zeros((MOD_ROWS, d), F32).at[:b].set(c).at[b].set(c_ctx)
    mod = _ada_call(cs, w_ada, b_ada)

    rope_tables = _rope_tables(length)
    w1, w2, cs_lat = _fourier_tables()
    wc, cs_ctx = _fourier_ctx_tables(n_ctx)
    bdq = _block_diag_ones(ATTN_WIDTH)
    bdk = _block_diag_ones(KV2_WIDTH)

    xc = ctx
    for l in range(DEPTH):
        last = l == DEPTH - 1
        wcat = _reorder_w_in(w_in[l])
        gq = jnp.tile(q_norm_g[l], N_HEADS)[None, :]
        gk = jnp.tile(k_norm_g[l], 2 * N_KV_HEADS)[None, :]
        wsg = jnp.concatenate([sgu_w[l, g] for g in range(SGU_GROUPS)], axis=1).astype(BF16)
        bsg = jnp.repeat(sgu_b[l].T, SGU_WIDTH // SGU_GROUPS, axis=1)
        wpa, wpb, wpc, wout = (w_pa[l].astype(BF16), w_pb[l].astype(BF16),
                               w_pc[l].astype(BF16), w_out[l].astype(BF16))
        sinks = attn_sink[l]

        fa_c, main_c, kc2, vc2 = _proj_call(
            xc, mod[l], wcat, bdq, bdk, gq, gk, None, tile=n_ctx, ctx_row=ctx_row)
        fa, main, k2, v2 = _proj_call(
            x, mod[l], wcat, bdq, bdk, gq, gk, rope_tables, tile=lat_tile, ctx_row=None)
        y = _fourier_call(fa, w1, w2, cs_lat)
        x = _mix_call(x, main, y, k2, v2, kc2, vc2, mod[l], sinks, wsg, bsg, wpa, wpb, wpc, wout,
                      tile=lat_tile, local=True, ctx_row=None)
        if not last:
            y_c = _fourier_ctx_call(fa_c, wc, cs_ctx)
            xc = _mix_call(xc, main_c, y_c, None, None, kc2, vc2, mod[l], sinks, wsg, bsg,
                           wpa, wpb, wpc, wout, tile=n_ctx, local=False, ctx_row=ctx_row)
    return x
```

```python
import functools
import math

import jax
import jax.numpy as jnp
from jax import lax
from jax.experimental import pallas as pl
from jax.experimental.pallas import tpu as pltpu

F32 = jnp.float32
BF16 = jnp.bfloat16

D_MODEL = 1024
DEPTH = 4
GRID_W = 64
FOURIER_WIDTH = 256
FOURIER_GROUPS = 4
SGU_WIDTH = 256
SGU_GROUPS = 4
CHUNK = 128
N_HEADS = 8
N_KV_HEADS = 2
HEAD_DIM = 64
ATTN_WIDTH = N_HEADS * HEAD_DIM
KV_WIDTH = N_KV_HEADS * HEAD_DIM
BLOCK = 128
ROPE_THETA = 10000.0
EPS = 1e-6
NEG_INF = -1e30

LANES = 128
MOD_ROWS = 8

W_FA = 0
W_MAIN = FOURIER_WIDTH
MAIN_ZA = 0
MAIN_U = MAIN_ZA + FOURIER_WIDTH
MAIN_VS = MAIN_U + SGU_WIDTH
MAIN_ZB = MAIN_VS + SGU_WIDTH
MAIN_ZC = MAIN_ZB + SGU_WIDTH
MAIN_G = MAIN_ZC + ATTN_WIDTH
MAIN_WIDTH = MAIN_G + 3 * D_MODEL
W_Q = W_MAIN + MAIN_WIDTH
W_K = W_Q + ATTN_WIDTH
W_V = W_K + KV_WIDTH
W_TOTAL = W_V + KV_WIDTH
PROJ_CHUNK = 512

DFT_L1 = 64
DFT_L2 = 128
FOURIER_CW = 128

VMEM_LIMIT = 56 * 1024 * 1024


def _silu(z):
    return 0.5 * z * (1.0 + jnp.tanh(0.5 * z))


def _sigmoid(z):
    return 0.5 * (1.0 + jnp.tanh(0.5 * z))


def _dot(a, b):
    return jnp.dot(a, b, preferred_element_type=F32)


def _ada_kernel(c_ref, w_ref, b_ref, o_ref):
    s = _silu(c_ref[...]).astype(BF16)
    o_ref[0] = _dot(s, w_ref[0].astype(BF16)) + b_ref[0]


def _ada_call(cs, w_ada, b_ada):
    d = D_MODEL
    return pl.pallas_call(
        _ada_kernel,
        grid=(DEPTH, 3),
        in_specs=[
            pl.BlockSpec((MOD_ROWS, d), lambda l, j: (0, 0)),
            pl.BlockSpec((1, d, d), lambda l, j: (l, 0, j)),
            pl.BlockSpec((1, 1, d), lambda l, j: (l, 0, j)),
        ],
        out_specs=pl.BlockSpec((1, MOD_ROWS, d), lambda l, j: (l, 0, j)),
        out_shape=jax.ShapeDtypeStruct((DEPTH, MOD_ROWS, 3 * d), F32),
        compiler_params=pltpu.CompilerParams(
            dimension_semantics=("arbitrary", "arbitrary"), vmem_limit_bytes=VMEM_LIMIT),
        name="ada_mod",
    )(cs, w_ada, b_ada.reshape(DEPTH, 1, 3 * d))


def _norm_rope(t, bd_ref, gain, cos, sin, rope, out_scale):
    width = t.shape[-1]
    ssq = _dot((t * t).astype(BF16), bd_ref[...])
    tn = t * lax.rsqrt(ssq * (1.0 / HEAD_DIM) + EPS) * gain
    if rope:
        lane = lax.broadcasted_iota(jnp.int32, tn.shape, 1)
        first = (lane % 32) < 16
        partner = jnp.where(first, pltpu.roll(tn, width - 16, 1), pltpu.roll(tn, 16, 1))
        tn = tn * cos + partner * sin
    if out_scale != 1.0:
        tn = tn * out_scale
    return tn


def _proj_kernel(x_ref, mod_ref, w_ref, bdq_ref, bdk_ref, gq_ref, gk_ref,
                 cosq_ref, sinq_ref, cosk_ref, sinkv_ref,
                 fa_ref, main_ref, qt_ref, k_ref, vt_ref, *, ctx_row, rope):
    d = D_MODEL
    nsub = x_ref.shape[1] // BLOCK
    xt = x_ref[0]
    ms = jnp.mean(xt * xt, axis=-1, keepdims=True)
    row = pl.program_id(0) if ctx_row is None else ctx_row
    modrow = mod_ref[pl.ds(row, 1), :]
    shift = modrow[:, 0:d]
    scale = modrow[:, d:2 * d]
    hb = ((xt * lax.rsqrt(ms + EPS)) * (1.0 + scale) + shift).astype(BF16)

    fa_ref[0] = _dot(hb, w_ref[:, W_FA:W_FA + FOURIER_WIDTH])
    for c0 in range(0, MAIN_WIDTH, PROJ_CHUNK):
        main_ref[0, :, c0:c0 + PROJ_CHUNK] = _dot(
            hb, w_ref[:, W_MAIN + c0:W_MAIN + c0 + PROJ_CHUNK]).astype(BF16)

    q = _dot(hb, w_ref[:, W_Q:W_Q + ATTN_WIDTH])
    cq = cosq_ref[...] if rope else None
    sq = sinq_ref[...] if rope else None
    q = _norm_rope(q, bdq_ref, gq_ref[...], cq, sq, rope, HEAD_DIM ** -0.5)
    for s in range(nsub):
        qt_ref[0, s] = q[s * BLOCK:(s + 1) * BLOCK, :].T.astype(BF16)

    k = _dot(hb, w_ref[:, W_K:W_K + KV_WIDTH])
    ck = cosk_ref[...] if rope else None
    sk = sinkv_ref[...] if rope else None
    k = _norm_rope(k, bdk_ref, gk_ref[...], ck, sk, rope, 1.0)
    k_ref[0] = k.astype(BF16)

    v = _dot(hb, w_ref[:, W_V:W_V + KV_WIDTH])
    for s in range(nsub):
        vt_ref[0, s] = v[s * BLOCK:(s + 1) * BLOCK, :].T.astype(BF16)


def _proj_call(x, mod_l, wcat, bdq, bdk, gq, gk, rope_tables, *, tile, ctx_row):
    b, length, d = x.shape
    nt = length // tile
    nsub = tile // BLOCK
    const = lambda bi, i: (0, 0)
    tok = lambda bi, i: (bi, i, 0)
    blk = lambda bi, i: (bi, i, 0, 0)
    rope = rope_tables is not None
    if rope:
        cosq, sinq, cosk, sink = rope_tables
        pos, pos_rows = (lambda bi, i: (i, 0)), tile
    else:
        cosq = sinq = jnp.zeros((MOD_ROWS, ATTN_WIDTH), F32)
        cosk = sink = jnp.zeros((MOD_ROWS, KV_WIDTH), F32)
        pos, pos_rows = const, MOD_ROWS
    kern = functools.partial(_proj_kernel, ctx_row=ctx_row, rope=rope)
    return pl.pallas_call(
        kern,
        grid=(b, nt),
        in_specs=[
            pl.BlockSpec((1, tile, d), tok),
            pl.BlockSpec((MOD_ROWS, 3 * d), const),
            pl.BlockSpec((d, W_TOTAL), const, pipeline_mode=pl.Buffered(1)),
            pl.BlockSpec((ATTN_WIDTH, ATTN_WIDTH), const),
            pl.BlockSpec((KV_WIDTH, KV_WIDTH), const),
            pl.BlockSpec((1, ATTN_WIDTH), const),
            pl.BlockSpec((1, KV_WIDTH), const),
            pl.BlockSpec((pos_rows, ATTN_WIDTH), pos),
            pl.BlockSpec((pos_rows, ATTN_WIDTH), pos),
            pl.BlockSpec((pos_rows, KV_WIDTH), pos),
            pl.BlockSpec((pos_rows, KV_WIDTH), pos),
        ],
        out_specs=[
            pl.BlockSpec((1, tile, FOURIER_WIDTH), tok),
            pl.BlockSpec((1, tile, MAIN_WIDTH), tok),
            pl.BlockSpec((1, nsub, ATTN_WIDTH, BLOCK), blk),
            pl.BlockSpec((1, tile, KV_WIDTH), tok),
            pl.BlockSpec((1, nsub, KV_WIDTH, BLOCK), blk),
        ],
        out_shape=[
            jax.ShapeDtypeStruct((b, length, FOURIER_WIDTH), F32),
            jax.ShapeDtypeStruct((b, length, MAIN_WIDTH), BF16),
            jax.ShapeDtypeStruct((b, length // BLOCK, ATTN_WIDTH, BLOCK), BF16),
            jax.ShapeDtypeStruct((b, length, KV_WIDTH), BF16),
            jax.ShapeDtypeStruct((b, length // BLOCK, KV_WIDTH, BLOCK), BF16),
        ],
        compiler_params=pltpu.CompilerParams(
            dimension_semantics=("arbitrary", "arbitrary"), vmem_limit_bytes=VMEM_LIMIT),
        name="proj_ctx" if ctx_row is not None else "proj_lat",
    )(x, mod_l, wcat, bdq, bdk, gq, gk, cosq, sinq, cosk, sink)


def _fourier_kernel(fa_ref, w1_ref, w2_ref, cs_ref, y_ref, s_ref):
    l1n, l2n = DFT_L1, DFT_L2

    def stage1(l2, carry):
        xs = fa_ref[0, pl.ds(l2, l1n, stride=l2n), :].astype(BF16)
        a = _dot(w1_ref[l2], xs)
        s_ref[pl.ds(pl.multiple_of(l2 * (2 * l1n), 2 * l1n), 2 * l1n), :] = a
        return carry

    lax.fori_loop(0, l2n, stage1, 0, unroll=4)

    def stage2(k1, carry):
        rr = s_ref[pl.ds(k1, l2n, stride=2 * l1n), :]
        ri = s_ref[pl.ds(l1n + k1, l2n, stride=2 * l1n), :]
        r = jnp.concatenate([rr, ri], axis=0).astype(BF16)
        f = _dot(w2_ref[...], r)
        fc = jnp.concatenate([f[:l2n], f[l2n:]], axis=1).astype(BF16)
        y_ref[0, pl.ds(k1, l2n, stride=l1n), :] = _dot(fc, cs_ref[...])
        return carry

    lax.fori_loop(0, l1n, stage2, 0, unroll=2)


def _fourier_call(fa, w1, w2, cs):
    b, length, width = fa.shape
    cw = FOURIER_CW
    return pl.pallas_call(
        _fourier_kernel,
        grid=(b, width // cw),
        in_specs=[
            pl.BlockSpec((1, length, cw), lambda bi, j: (bi, 0, j)),
            pl.BlockSpec(w1.shape, lambda bi, j: (0, 0, 0)),
            pl.BlockSpec(w2.shape, lambda bi, j: (0, 0)),
            pl.BlockSpec(cs.shape, lambda bi, j: (0, 0)),
        ],
        out_specs=pl.BlockSpec((1, length, cw), lambda bi, j: (bi, 0, j)),
        out_shape=jax.ShapeDtypeStruct((b, length, width), F32),
        scratch_shapes=[pltpu.VMEM((DFT_L2 * 2 * DFT_L1, cw), F32)],
        compiler_params=pltpu.CompilerParams(
            dimension_semantics=("arbitrary", "arbitrary"), vmem_limit_bytes=VMEM_LIMIT),
        name="fourier_lat",
    )(fa, w1, w2, cs)


def _fourier_ctx_kernel(fa_ref, wc_ref, cs_ref, y_ref):
    n = fa_ref.shape[1]
    f = _dot(wc_ref[...], fa_ref[0].astype(BF16))
    fc = jnp.concatenate([f[:n], f[n:]], axis=1).astype(BF16)
    y_ref[0] = _dot(fc, cs_ref[...])


def _fourier_ctx_call(fa, wc, cs):
    b, length, width = fa.shape
    return pl.pallas_call(
        _fourier_ctx_kernel,
        grid=(b,),
        in_specs=[
            pl.BlockSpec((1, length, width), lambda bi: (bi, 0, 0)),
            pl.BlockSpec(wc.shape, lambda bi: (0, 0)),
            pl.BlockSpec(cs.shape, lambda bi: (0, 0)),
        ],
        out_specs=pl.BlockSpec((1, length, width), lambda bi: (bi, 0, 0)),
        out_shape=jax.ShapeDtypeStruct((b, length, width), F32),
        compiler_params=pltpu.CompilerParams(
            dimension_semantics=("arbitrary",), vmem_limit_bytes=VMEM_LIMIT),
        name="fourier_ctx",
    )(fa, wc, cs)


def _mix_kernel(*refs, local, tile, n_blocks, ctx_row):
    if local:
        (x_ref, main_ref, y_ref, qt_ref, kp_ref, km_ref, kn_ref, vtp_ref, vtm_ref, vtn_ref,
         kc_ref, vtc_ref, mod_ref, sink_ref, wsg_ref, bsg_ref,
         wpa_ref, wpb_ref, wpc_ref, wout_ref, o_ref,
         kbuf, vtbuf, ya_s, yb_s, yc_s) = refs
    else:
        (x_ref, main_ref, y_ref, qt_ref, kc_ref, vtc_ref, mod_ref, sink_ref, wsg_ref, bsg_ref,
         wpa_ref, wpb_ref, wpc_ref, wout_ref, o_ref,
         ya_s, yb_s, yc_s) = refs
    d = D_MODEL
    nsub = tile // BLOCK
    tile_idx = pl.program_id(1)
    heads_per_kv = N_HEADS // N_KV_HEADS
    cols_all = N_HEADS * BLOCK

    if local:
        kbuf[0:BLOCK, :] = kp_ref[0]
        kbuf[BLOCK:BLOCK + tile, :] = km_ref[0]
        kbuf[BLOCK + tile:2 * BLOCK + tile, :] = kn_ref[0]
        vtbuf[0] = vtp_ref[0, 0]
        for s in range(nsub):
            vtbuf[1 + s] = vtm_ref[0, s]
        vtbuf[1 + nsub] = vtn_ref[0, 0]

    vt_ctx = jnp.concatenate([vtc_ref[0, s] for s in range(vtc_ref.shape[1])], axis=1)

    ya_s[...] = (y_ref[0] * _silu(main_ref[0, :, MAIN_ZA:MAIN_ZA + FOURIER_WIDTH].astype(F32))).astype(BF16)

    sgu_grp = lax.broadcasted_iota(jnp.int32, (CHUNK, SGU_WIDTH), 1) // (SGU_WIDTH // SGU_GROUPS)
    zero_head = jnp.zeros((HEAD_DIM, BLOCK), BF16)
    sink_row = jnp.concatenate(
        [jnp.full((1, BLOCK), sink_ref[hh], F32) for hh in range(N_HEADS)], axis=1)
    kq_diff = (lax.broadcasted_iota(jnp.int32, (BLOCK, cols_all), 0)
               - lax.broadcasted_iota(jnp.int32, (BLOCK, cols_all), 1) % BLOCK)

    def sub_block(n, carry):
        r0 = pl.multiple_of(n * BLOCK, BLOCK)
        rows = pl.ds(r0, BLOCK)

        vs = main_ref[0, rows, MAIN_VS:MAIN_VS + SGU_WIDTH].astype(F32)
        vn = (vs * lax.rsqrt(jnp.mean(vs * vs, axis=-1, keepdims=True) + EPS)).astype(BF16)
        rhs = jnp.concatenate(
            [jnp.where(sgu_grp == g, vn, jnp.zeros_like(vn)) for g in range(SGU_GROUPS)], axis=0)
        mixed = _dot(wsg_ref[...], rhs) + bsg_ref[...]
        u = main_ref[0, rows, MAIN_U:MAIN_U + SGU_WIDTH].astype(F32)
        zb = main_ref[0, rows, MAIN_ZB:MAIN_ZB + SGU_WIDTH].astype(F32)
        yb_s[rows, :] = (u * mixed * _silu(zb)).astype(BF16)

        qt = qt_ref[0, n]
        cols = []
        for hh in range(N_HEADS):
            qh = qt[hh * HEAD_DIM:(hh + 1) * HEAD_DIM, :]
            cols.append(jnp.concatenate(
                [qh, zero_head] if hh < heads_per_kv else [zero_head, qh], axis=0))
        qst = jnp.concatenate(cols, axis=1)

        s_ctx = _dot(kc_ref[0], qst)
        m = jnp.maximum(jnp.max(s_ctx, axis=0, keepdims=True), sink_row)
        if local:
            gblk = tile_idx * nsub + n
            s_loc = _dot(kbuf[pl.ds(r0, 3 * BLOCK), :], qst)
            s_prev = jnp.where(kq_diff >= jnp.where(gblk > 0, 0, BLOCK), s_loc[0:BLOCK], NEG_INF)
            s_own = s_loc[BLOCK:2 * BLOCK]
            s_next = jnp.where(kq_diff <= jnp.where(gblk < n_blocks - 1, 0, -BLOCK),
                               s_loc[2 * BLOCK:3 * BLOCK], NEG_INF)
            m_loc = jnp.max(jnp.maximum(jnp.maximum(s_prev, s_own), s_next), axis=0, keepdims=True)
            m = jnp.maximum(m, m_loc)
        e_ctx = jnp.exp(s_ctx - m)
        den = jnp.sum(e_ctx, axis=0, keepdims=True) + jnp.exp(sink_row - m)
        ot = _dot(vt_ctx, e_ctx.astype(BF16))
        if local:
            e_loc = jnp.concatenate(
                [jnp.exp(s_prev - m), jnp.exp(s_own - m), jnp.exp(s_next - m)], axis=0)
            den = den + jnp.sum(e_loc, axis=0, keepdims=True)
            vt_loc = jnp.concatenate([vtbuf[n], vtbuf[n + 1], vtbuf[n + 2]], axis=1)
            ot = ot + _dot(vt_loc, e_loc.astype(BF16))
        ot = ot * (1.0 / den)
        for j in range(ATTN_WIDTH // LANES):
            h = (2 * j) // heads_per_kv
            pair = jnp.concatenate(
                [ot[h * HEAD_DIM:(h + 1) * HEAD_DIM, (2 * j) * BLOCK:(2 * j + 1) * BLOCK],
                 ot[h * HEAD_DIM:(h + 1) * HEAD_DIM, (2 * j + 1) * BLOCK:(2 * j + 2) * BLOCK]], axis=0)
            zc = main_ref[0, rows, MAIN_ZC + j * LANES:MAIN_ZC + (j + 1) * LANES].astype(F32)
            yc_s[rows, j * LANES:(j + 1) * LANES] = (pair.T * _silu(zc)).astype(BF16)
        return carry

    lax.fori_loop(0, nsub, sub_block, 0)

    g = main_ref[0, :, MAIN_G:MAIN_G + 3 * d].astype(F32)
    merged = _sigmoid(g[:, 0:d]) * _dot(ya_s[...], wpa_ref[...])
    merged = merged + _sigmoid(g[:, d:2 * d]) * _dot(yb_s[...], wpb_ref[...])
    merged = merged + _sigmoid(g[:, 2 * d:3 * d]) * _dot(yc_s[...], wpc_ref[...])
    out = _dot(merged.astype(BF16), wout_ref[...])
    row = pl.program_id(0) if ctx_row is None else ctx_row
    gate = mod_ref[pl.ds(row, 1), 2 * d:3 * d]
    o_ref[0] = x_ref[0] + gate * out


def _mix_call(x, main, y, qt, k, vt, kc, vtc, mod_l, sinks, wsg, bsg, wpa, wpb, wpc, wout,
              *, tile, local, ctx_row):
    b, length, d = x.shape
    nt = length // tile
    n_ctx = kc.shape[1]
    sub = tile // BLOCK
    n_blocks = length // BLOCK
    const2 = lambda bi, i: (0, 0)
    tok = lambda bi, i: (bi, i, 0)
    blk = lambda bi, i: (bi, i, 0, 0)
    per_b = lambda bi, i: (bi, 0, 0)
    per_b4 = lambda bi, i: (bi, 0, 0, 0)

    in_specs = [
        pl.BlockSpec((1, tile, d), tok),
        pl.BlockSpec((1, tile, MAIN_WIDTH), tok),
        pl.BlockSpec((1, tile, FOURIER_WIDTH), tok),
        pl.BlockSpec((1, sub, ATTN_WIDTH, BLOCK), blk),
    ]
    args = [x, main, y, qt]
    if local:
        prev_blk = lambda bi, i: jnp.maximum(i * sub - 1, 0)
        next_blk = lambda bi, i: jnp.minimum((i + 1) * sub, n_blocks - 1)
        in_specs += [
            pl.BlockSpec((1, BLOCK, KV_WIDTH), lambda bi, i: (bi, prev_blk(bi, i), 0)),
            pl.BlockSpec((1, tile, KV_WIDTH), tok),
            pl.BlockSpec((1, BLOCK, KV_WIDTH), lambda bi, i: (bi, next_blk(bi, i), 0)),
            pl.BlockSpec((1, 1, KV_WIDTH, BLOCK), lambda bi, i: (bi, prev_blk(bi, i), 0, 0)),
            pl.BlockSpec((1, sub, KV_WIDTH, BLOCK), blk),
            pl.BlockSpec((1, 1, KV_WIDTH, BLOCK), lambda bi, i: (bi, next_blk(bi, i), 0, 0)),
        ]
        args += [k, k, k, vt, vt, vt]
    in_specs += [
        pl.BlockSpec((1, n_ctx, KV_WIDTH), per_b),
        pl.BlockSpec((1, n_ctx // BLOCK, KV_WIDTH, BLOCK), per_b4),
        pl.BlockSpec((MOD_ROWS, 3 * d), const2),
        pl.BlockSpec(memory_space=pltpu.SMEM),
        pl.BlockSpec(wsg.shape, const2),
        pl.BlockSpec(bsg.shape, const2),
        pl.BlockSpec(wpa.shape, const2),
        pl.BlockSpec(wpb.shape, const2),
        pl.BlockSpec(wpc.shape, const2),
        pl.BlockSpec(wout.shape, const2),
    ]
    args += [kc, vtc, mod_l, sinks, wsg, bsg, wpa, wpb, wpc, wout]

    scratch = []
    if local:
        scratch += [pltpu.VMEM((tile + 2 * BLOCK, KV_WIDTH), BF16),
                    pltpu.VMEM((sub + 2, KV_WIDTH, BLOCK), BF16)]
    scratch += [
        pltpu.VMEM((tile, FOURIER_WIDTH), BF16),
        pltpu.VMEM((tile, SGU_WIDTH), BF16),
        pltpu.VMEM((tile, ATTN_WIDTH), BF16),
    ]
    kern = functools.partial(_mix_kernel, local=local, tile=tile, n_blocks=n_blocks, ctx_row=ctx_row)
    return pl.pallas_call(
        kern,
        grid=(b, nt),
        in_specs=in_specs,
        out_specs=pl.BlockSpec((1, tile, d), tok),
        out_shape=jax.ShapeDtypeStruct((b, length, d), F32),
        scratch_shapes=scratch,
        compiler_params=pltpu.CompilerParams(
            dimension_semantics=("arbitrary", "arbitrary"), vmem_limit_bytes=VMEM_LIMIT),
        name="mix_lat" if local else "mix_ctx",
    )(*args)


def _rope_tables(length):
    pos = jnp.arange(length)
    nf = HEAD_DIM // 4
    inv = ROPE_THETA ** (-jnp.arange(nf, dtype=F32) / nf)
    ang_r = (pos // GRID_W).astype(F32)[:, None] * inv[None, :]
    ang_c = (pos % GRID_W).astype(F32)[:, None] * inv[None, :]
    ang = jnp.concatenate([ang_r, ang_r, ang_c, ang_c], axis=1)
    sign = jnp.concatenate([-jnp.ones((nf,), F32), jnp.ones((nf,), F32)] * 2)
    cos = jnp.cos(ang)
    sin = jnp.sin(ang) * sign[None, :]
    return (jnp.tile(cos, (1, N_HEADS)), jnp.tile(sin, (1, N_HEADS)),
            jnp.tile(cos, (1, N_KV_HEADS)), jnp.tile(sin, (1, N_KV_HEADS)))


def _angle(num, den):
    return (num % den).astype(F32) * (2.0 * math.pi / den)


def _channel_dft(width):
    gw = FOURIER_WIDTH // FOURIER_GROUPS
    m = jnp.arange(width)
    same = (m[:, None] // gw) == (m[None, :] // gw)
    th = _angle((m[:, None] % gw) * (m[None, :] % gw), gw)
    cc = jnp.where(same, jnp.cos(th), 0.0) * gw ** -0.5
    sc = jnp.where(same, jnp.sin(th), 0.0) * gw ** -0.5
    return jnp.concatenate([cc, sc], axis=0).astype(BF16)


def _fourier_tables():
    l1n, l2n = DFT_L1, DFT_L2
    length = l1n * l2n
    l2 = jnp.arange(l2n)[:, None, None]
    k1 = jnp.arange(l1n)[None, :, None]
    l1 = jnp.arange(l1n)[None, None, :]
    th = _angle(k1 * l1 * l2n + k1 * l2, length)
    w1 = (jnp.concatenate([jnp.cos(th), -jnp.sin(th)], axis=1) * l1n ** -0.5).astype(BF16)
    k2 = jnp.arange(l2n)
    th2 = _angle(k2[:, None] * k2[None, :], l2n)
    c2, s2 = jnp.cos(th2), jnp.sin(th2)
    w2 = (jnp.concatenate([jnp.concatenate([c2, s2], axis=1),
                           jnp.concatenate([-s2, c2], axis=1)], axis=0) * l2n ** -0.5).astype(BF16)
    return w1, w2, _channel_dft(FOURIER_CW)


def _fourier_ctx_tables(n):
    k = jnp.arange(n)
    th = _angle(k[:, None] * k[None, :], n)
    wc = (jnp.concatenate([jnp.cos(th), -jnp.sin(th)], axis=0) * n ** -0.5).astype(BF16)
    return wc, _channel_dft(FOURIER_WIDTH)


def _block_diag_ones(width):
    m = jnp.arange(width) // HEAD_DIM
    return (m[:, None] == m[None, :]).astype(BF16)


def _reorder_w_in(w):
    fw, sw, aw, kw = FOURIER_WIDTH, SGU_WIDTH, ATTN_WIDTH, KV_WIDTH
    o_q = 2 * fw + 3 * sw
    o_k = o_q + aw
    o_zc = o_k + 2 * kw
    parts = [w[:, 0:o_q], w[:, o_zc:], w[:, o_q:o_zc]]
    return jnp.concatenate(parts, axis=1).astype(BF16)


def kernel(x, c, ctx, c_ctx, w_ada, b_ada, w_in, sgu_w, sgu_b, q_norm_g, k_norm_g,
           attn_sink, w_pa, w_pb, w_pc, w_out):
    b, length, d = x.shape
    n_ctx = ctx.shape[1]
    assert length == DFT_L1 * DFT_L2 and d == D_MODEL and b + 1 <= MOD_ROWS
    lat_tile = 512
    ctx_row = b

    cs = jnp.zeros((MOD_ROWS, d), F32).at[:b].set(c).at[b].set(c_ctx)
    mod = _ada_call(cs, w_ada, b_ada)

    rope_tables = _rope_tables(length)
    w1, w2, cs_lat = _fourier_tables()
    wc, cs_ctx = _fourier_ctx_tables(n_ctx)
    bdq = _block_diag_ones(ATTN_WIDTH)
    bdk = _block_diag_ones(KV_WIDTH)

    xc = ctx
    for l in range(DEPTH):
        last = l == DEPTH - 1
        wcat = _reorder_w_in(w_in[l])
        gq = jnp.tile(q_norm_g[l], N_HEADS)[None, :]
        gk = jnp.tile(k_norm_g[l], N_KV_HEADS)[None, :]
        wsg = jnp.concatenate([sgu_w[l, g] for g in range(SGU_GROUPS)], axis=1).astype(BF16)
        bsg = jnp.repeat(sgu_b[l].T, SGU_WIDTH // SGU_GROUPS, axis=1)
        wpa, wpb, wpc, wout = (w_pa[l].astype(BF16), w_pb[l].astype(BF16),
                               w_pc[l].astype(BF16), w_out[l].astype(BF16))
        sinks = attn_sink[l]

        fa_c, main_c, qt_c, kc, vtc = _proj_call(
            xc, mod[l], wcat, bdq, bdk, gq, gk, None, tile=n_ctx, ctx_row=ctx_row)
        fa, main, qt, k, vt = _proj_call(
            x, mod[l], wcat, bdq, bdk, gq, gk, rope_tables, tile=lat_tile, ctx_row=None)
        y = _fourier_call(fa, w1, w2, cs_lat)
        x = _mix_call(x, main, y, qt, k, vt, kc, vtc, mod[l], sinks, wsg, bsg, wpa, wpb, wpc, wout,
                      tile=lat_tile, local=True, ctx_row=None)
        if not last:
            y_c = _fourier_ctx_call(fa_c, wc, cs_ctx)
            xc = _mix_call(xc, main_c, y_c, qt_c, None, None, kc, vtc, mod[l], sinks, wsg, bsg,
                           wpa, wpb, wpc, wout, tile=n_ctx, local=False, ctx_row=ctx_row)
    return x
```

```python
import functools
import math

import jax
import jax.numpy as jnp
from jax import lax
from jax.experimental import pallas as pl
from jax.experimental.pallas import tpu as pltpu

F32 = jnp.float32
BF16 = jnp.bfloat16

D_MODEL = 1024
DEPTH = 4
GRID_W = 64
FOURIER_WIDTH = 256
FOURIER_GROUPS = 4
SGU_WIDTH = 256
SGU_GROUPS = 4
CHUNK = 128
N_HEADS = 8
N_KV_HEADS = 2
HEAD_DIM = 64
ATTN_WIDTH = N_HEADS * HEAD_DIM
KV_WIDTH = N_KV_HEADS * HEAD_DIM
BLOCK = 128
ROPE_THETA = 10000.0
EPS = 1e-6
NEG_INF = -1e30

LANES = 128
MOD_ROWS = 8

W_FA = 0
W_MAIN = FOURIER_WIDTH
MAIN_ZA = 0
MAIN_U = MAIN_ZA + FOURIER_WIDTH
MAIN_VS = MAIN_U + SGU_WIDTH
MAIN_ZB = MAIN_VS + SGU_WIDTH
MAIN_ZC = MAIN_ZB + SGU_WIDTH
MAIN_G = MAIN_ZC + ATTN_WIDTH
MAIN_WIDTH = MAIN_G + 3 * D_MODEL
W_Q = W_MAIN + MAIN_WIDTH
W_K = W_Q + ATTN_WIDTH
W_V = W_K + KV_WIDTH
W_TOTAL = W_V + KV_WIDTH
PROJ_CHUNK = 512

DFT_L1 = 64
DFT_L2 = 128
DFT1_SLABS = 32
DFT2_GROUP = 8

VMEM_LIMIT = 56 * 1024 * 1024


def _silu(z):
    return 0.5 * z * (1.0 + jnp.tanh(0.5 * z))


def _sigmoid(z):
    return 0.5 * (1.0 + jnp.tanh(0.5 * z))


def _dot(a, b):
    return jnp.dot(a, b, preferred_element_type=F32)


def _ada_kernel(c_ref, w_ref, b_ref, o_ref):
    s = _silu(c_ref[...]).astype(BF16)
    o_ref[0] = _dot(s, w_ref[0].astype(BF16)) + b_ref[0]


def _ada_call(cs, w_ada, b_ada):
    d = D_MODEL
    return pl.pallas_call(
        _ada_kernel,
        grid=(DEPTH, 3),
        in_specs=[
            pl.BlockSpec((MOD_ROWS, d), lambda l, j: (0, 0)),
            pl.BlockSpec((1, d, d), lambda l, j: (l, 0, j)),
            pl.BlockSpec((1, 1, d), lambda l, j: (l, 0, j)),
        ],
        out_specs=pl.BlockSpec((1, MOD_ROWS, d), lambda l, j: (l, 0, j)),
        out_shape=jax.ShapeDtypeStruct((DEPTH, MOD_ROWS, 3 * d), F32),
        compiler_params=pltpu.CompilerParams(
            dimension_semantics=("arbitrary", "arbitrary"), vmem_limit_bytes=VMEM_LIMIT),
        name="ada_mod",
    )(cs, w_ada, b_ada.reshape(DEPTH, 1, 3 * d))


def _norm_rope(t, bd_ref, gain, cos, sin, rope, out_scale):
    width = t.shape[-1]
    ssq = _dot((t * t).astype(BF16), bd_ref[...])
    tn = t * lax.rsqrt(ssq * (1.0 / HEAD_DIM) + EPS) * gain
    if rope:
        lane = lax.broadcasted_iota(jnp.int32, tn.shape, 1)
        first = (lane % 32) < 16
        partner = jnp.where(first, pltpu.roll(tn, width - 16, 1), pltpu.roll(tn, 16, 1))
        tn = tn * cos + partner * sin
    if out_scale != 1.0:
        tn = tn * out_scale
    return tn


def _proj_kernel(x_ref, mod_ref, w_ref, bdq_ref, bdk_ref, gq_ref, gk_ref,
                 cosq_ref, sinq_ref, cosk_ref, sinkv_ref,
                 fa_ref, main_ref, qt_ref, k_ref, vt_ref, *, ctx_row, rope):
    d = D_MODEL
    nsub = x_ref.shape[1] // BLOCK
    xt = x_ref[0]
    ms = jnp.mean(xt * xt, axis=-1, keepdims=True)
    row = pl.program_id(0) if ctx_row is None else ctx_row
    modrow = mod_ref[pl.ds(row, 1), :]
    shift = modrow[:, 0:d]
    scale = modrow[:, d:2 * d]
    hb = ((xt * lax.rsqrt(ms + EPS)) * (1.0 + scale) + shift).astype(BF16)

    fa_ref[0] = _dot(hb, w_ref[:, W_FA:W_FA + FOURIER_WIDTH]).astype(BF16)
    for c0 in range(0, MAIN_WIDTH, PROJ_CHUNK):
        main_ref[0, :, c0:c0 + PROJ_CHUNK] = _dot(
            hb, w_ref[:, W_MAIN + c0:W_MAIN + c0 + PROJ_CHUNK]).astype(BF16)

    q = _dot(hb, w_ref[:, W_Q:W_Q + ATTN_WIDTH])
    cq = cosq_ref[...] if rope else None
    sq = sinq_ref[...] if rope else None
    q = _norm_rope(q, bdq_ref, gq_ref[...], cq, sq, rope, HEAD_DIM ** -0.5)
    for s in range(nsub):
        qt_ref[0, s] = q[s * BLOCK:(s + 1) * BLOCK, :].T.astype(BF16)

    k = _dot(hb, w_ref[:, W_K:W_K + KV_WIDTH])
    ck = cosk_ref[...] if rope else None
    sk = sinkv_ref[...] if rope else None
    k = _norm_rope(k, bdk_ref, gk_ref[...], ck, sk, rope, 1.0)
    k_ref[0] = k.astype(BF16)

    v = _dot(hb, w_ref[:, W_V:W_V + KV_WIDTH])
    for s in range(nsub):
        vt_ref[0, s] = v[s * BLOCK:(s + 1) * BLOCK, :].T.astype(BF16)


def _proj_call(x, mod_l, wcat, bdq, bdk, gq, gk, rope_tables, *, tile, ctx_row):
    b, length, d = x.shape
    nt = length // tile
    nsub = tile // BLOCK
    const = lambda bi, i: (0, 0)
    tok = lambda bi, i: (bi, i, 0)
    blk = lambda bi, i: (bi, i, 0, 0)
    rope = rope_tables is not None
    if rope:
        cosq, sinq, cosk, sink = rope_tables
        pos, pos_rows = (lambda bi, i: (i, 0)), tile
    else:
        cosq = sinq = jnp.zeros((MOD_ROWS, ATTN_WIDTH), F32)
        cosk = sink = jnp.zeros((MOD_ROWS, KV_WIDTH), F32)
        pos, pos_rows = const, MOD_ROWS
    kern = functools.partial(_proj_kernel, ctx_row=ctx_row, rope=rope)
    return pl.pallas_call(
        kern,
        grid=(b, nt),
        in_specs=[
            pl.BlockSpec((1, tile, d), tok),
            pl.BlockSpec((MOD_ROWS, 3 * d), const),
            pl.BlockSpec((d, W_TOTAL), const, pipeline_mode=pl.Buffered(1)),
            pl.BlockSpec((ATTN_WIDTH, ATTN_WIDTH), const),
            pl.BlockSpec((KV_WIDTH, KV_WIDTH), const),
            pl.BlockSpec((1, ATTN_WIDTH), const),
            pl.BlockSpec((1, KV_WIDTH), const),
            pl.BlockSpec((pos_rows, ATTN_WIDTH), pos),
            pl.BlockSpec((pos_rows, ATTN_WIDTH), pos),
            pl.BlockSpec((pos_rows, KV_WIDTH), pos),
            pl.BlockSpec((pos_rows, KV_WIDTH), pos),
        ],
        out_specs=[
            pl.BlockSpec((1, tile, FOURIER_WIDTH), tok),
            pl.BlockSpec((1, tile, MAIN_WIDTH), tok),
            pl.BlockSpec((1, nsub, ATTN_WIDTH, BLOCK), blk),
            pl.BlockSpec((1, tile, KV_WIDTH), tok),
            pl.BlockSpec((1, nsub, KV_WIDTH, BLOCK), blk),
        ],
        out_shape=[
            jax.ShapeDtypeStruct((b, length, FOURIER_WIDTH), BF16),
            jax.ShapeDtypeStruct((b, length, MAIN_WIDTH), BF16),
            jax.ShapeDtypeStruct((b, length // BLOCK, ATTN_WIDTH, BLOCK), BF16),
            jax.ShapeDtypeStruct((b, length, KV_WIDTH), BF16),
            jax.ShapeDtypeStruct((b, length // BLOCK, KV_WIDTH, BLOCK), BF16),
        ],
        compiler_params=pltpu.CompilerParams(
            dimension_semantics=("arbitrary", "arbitrary"), vmem_limit_bytes=VMEM_LIMIT),
        name="proj_ctx" if ctx_row is not None else "proj_lat",
    )(x, mod_l, wcat, bdq, bdk, gq, gk, cosq, sinq, cosk, sink)


def _dft1_kernel(x_ref, w1_ref, a_ref):
    cw = FOURIER_WIDTH
    for i in range(w1_ref.shape[0]):
        cols = slice(i * cw, (i + 1) * cw)
        a_ref[0, :, cols] = _dot(w1_ref[i], x_ref[0, :, cols]).astype(BF16)


def _dft2_kernel(a_ref, w2_ref, cs_ref, y_ref):
    cw, l2n = FOURIER_WIDTH, DFT_L2
    grp = a_ref.shape[2]
    rhs = jnp.concatenate(
        [jnp.concatenate([a_ref[0, 0, j], a_ref[0, 1, j]], axis=0) for j in range(grp)], axis=1)
    f = _dot(w2_ref[...], rhs)
    for j in range(grp):
        cols = slice(j * cw, (j + 1) * cw)
        fc = jnp.concatenate([f[:l2n, cols], f[l2n:, cols]], axis=1).astype(BF16)
        y_ref[0, :, cols] = _dot(fc, cs_ref[...])


def _fourier_call(fa, w1, w2, cs):
    b, length, cw = fa.shape
    l1n, l2n = DFT_L1, DFT_L2
    params = pltpu.CompilerParams(
        dimension_semantics=("arbitrary", "arbitrary"), vmem_limit_bytes=VMEM_LIMIT)
    n1 = DFT1_SLABS
    a = pl.pallas_call(
        _dft1_kernel,
        grid=(b, l2n // n1),
        in_specs=[
            pl.BlockSpec((1, l1n, n1 * cw), lambda bi, j: (bi, 0, j)),
            pl.BlockSpec((n1, 2 * l1n, l1n), lambda bi, j: (j, 0, 0)),
        ],
        out_specs=pl.BlockSpec((1, 2 * l1n, n1 * cw), lambda bi, j: (bi, 0, j)),
        out_shape=jax.ShapeDtypeStruct((b, 2 * l1n, l2n * cw), BF16),
        compiler_params=params,
        name="dft1_lat",
    )(fa.reshape(b, l1n, l2n * cw), w1)
    grp = DFT2_GROUP
    y = pl.pallas_call(
        _dft2_kernel,
        grid=(b, l1n // grp),
        in_specs=[
            pl.BlockSpec((1, 2, grp, l2n, cw), lambda bi, g: (bi, 0, g, 0, 0)),
            pl.BlockSpec(w2.shape, lambda bi, g: (0, 0)),
            pl.BlockSpec(cs.shape, lambda bi, g: (0, 0)),
        ],
        out_specs=pl.BlockSpec((1, l2n, grp * cw), lambda bi, g: (bi, 0, g)),
        out_shape=jax.ShapeDtypeStruct((b, l2n, l1n * cw), F32),
        compiler_params=params,
        name="dft2_lat",
    )(a.reshape(b, 2, l1n, l2n, cw), w2, cs)
    return y.reshape(b, length, cw)


def _fourier_ctx_kernel(fa_ref, wc_ref, cs_ref, y_ref):
    n = fa_ref.shape[1]
    f = _dot(wc_ref[...], fa_ref[0])
    fc = jnp.concatenate([f[:n], f[n:]], axis=1).astype(BF16)
    y_ref[0] = _dot(fc, cs_ref[...])


def _fourier_ctx_call(fa, wc, cs):
    b, length, width = fa.shape
    return pl.pallas_call(
        _fourier_ctx_kernel,
        grid=(b,),
        in_specs=[
            pl.BlockSpec((1, length, width), lambda bi: (bi, 0, 0)),
            pl.BlockSpec(wc.shape, lambda bi: (0, 0)),
            pl.BlockSpec(cs.shape, lambda bi: (0, 0)),
        ],
        out_specs=pl.BlockSpec((1, length, width), lambda bi: (bi, 0, 0)),
        out_shape=jax.ShapeDtypeStruct((b, length, width), F32),
        compiler_params=pltpu.CompilerParams(
            dimension_semantics=("arbitrary",), vmem_limit_bytes=VMEM_LIMIT),
        name="fourier_ctx",
    )(fa, wc, cs)


def _mix_kernel(*refs, local, tile, n_blocks, ctx_row):
    if local:
        (x_ref, main_ref, y_ref, qt_ref, kp_ref, km_ref, kn_ref, vtp_ref, vtm_ref, vtn_ref,
         kc_ref, vtc_ref, mod_ref, sink_ref, wsg_ref, bsg_ref,
         wpa_ref, wpb_ref, wpc_ref, wout_ref, o_ref,
         kbuf, vtbuf, ya_s, yb_s, yc_s) = refs
    else:
        (x_ref, main_ref, y_ref, qt_ref, kc_ref, vtc_ref, mod_ref, sink_ref, wsg_ref, bsg_ref,
         wpa_ref, wpb_ref, wpc_ref, wout_ref, o_ref,
         ya_s, yb_s, yc_s) = refs
    d = D_MODEL
    nsub = tile // BLOCK
    tile_idx = pl.program_id(1)
    heads_per_kv = N_HEADS // N_KV_HEADS
    cols_all = N_HEADS * BLOCK

    if local:
        kbuf[0:BLOCK, :] = kp_ref[0]
        kbuf[BLOCK:BLOCK + tile, :] = km_ref[0]
        kbuf[BLOCK + tile:2 * BLOCK + tile, :] = kn_ref[0]
        vtbuf[0] = vtp_ref[0, 0]
        for s in range(nsub):
            vtbuf[1 + s] = vtm_ref[0, s]
        vtbuf[1 + nsub] = vtn_ref[0, 0]

    vt_ctx = jnp.concatenate([vtc_ref[0, s] for s in range(vtc_ref.shape[1])], axis=1)

    ya_s[...] = (y_ref[0] * _silu(main_ref[0, :, MAIN_ZA:MAIN_ZA + FOURIER_WIDTH].astype(F32))).astype(BF16)

    sgu_grp = lax.broadcasted_iota(jnp.int32, (CHUNK, SGU_WIDTH), 1) // (SGU_WIDTH // SGU_GROUPS)
    zero_head = jnp.zeros((HEAD_DIM, BLOCK), BF16)
    sink_row = jnp.concatenate(
        [jnp.full((1, BLOCK), sink_ref[hh], F32) for hh in range(N_HEADS)], axis=1)
    kq_diff = (lax.broadcasted_iota(jnp.int32, (BLOCK, cols_all), 0)
               - lax.broadcasted_iota(jnp.int32, (BLOCK, cols_all), 1) % BLOCK)

    def sub_block(n, carry):
        r0 = pl.multiple_of(n * BLOCK, BLOCK)
        rows = pl.ds(r0, BLOCK)

        vs = main_ref[0, rows, MAIN_VS:MAIN_VS + SGU_WIDTH].astype(F32)
        vn = (vs * lax.rsqrt(jnp.mean(vs * vs, axis=-1, keepdims=True) + EPS)).astype(BF16)
        rhs = jnp.concatenate(
            [jnp.where(sgu_grp == g, vn, jnp.zeros_like(vn)) for g in range(SGU_GROUPS)], axis=0)
        mixed = _dot(wsg_ref[...], rhs) + bsg_ref[...]
        u = main_ref[0, rows, MAIN_U:MAIN_U + SGU_WIDTH].astype(F32)
        zb = main_ref[0, rows, MAIN_ZB:MAIN_ZB + SGU_WIDTH].astype(F32)
        yb_s[rows, :] = (u * mixed * _silu(zb)).astype(BF16)

        qt = qt_ref[0, n]
        cols = []
        for hh in range(N_HEADS):
            qh = qt[hh * HEAD_DIM:(hh + 1) * HEAD_DIM, :]
            cols.append(jnp.concatenate(
                [qh, zero_head] if hh < heads_per_kv else [zero_head, qh], axis=0))
        qst = jnp.concatenate(cols, axis=1)

        s_ctx = _dot(kc_ref[0], qst)
        m = jnp.maximum(jnp.max(s_ctx, axis=0, keepdims=True), sink_row)
        if local:
            gblk = tile_idx * nsub + n
            s_loc = _dot(kbuf[pl.ds(r0, 3 * BLOCK), :], qst)
            s_prev = jnp.where(kq_diff >= jnp.where(gblk > 0, 0, BLOCK), s_loc[0:BLOCK], NEG_INF)
            s_own = s_loc[BLOCK:2 * BLOCK]
            s_next = jnp.where(kq_diff <= jnp.where(gblk < n_blocks - 1, 0, -BLOCK),
                               s_loc[2 * BLOCK:3 * BLOCK], NEG_INF)
            m_loc = jnp.max(jnp.maximum(jnp.maximum(s_prev, s_own), s_next), axis=0, keepdims=True)
            m = jnp.maximum(m, m_loc)
        e_ctx = jnp.exp(s_ctx - m)
        den = jnp.sum(e_ctx, axis=0, keepdims=True) + jnp.exp(sink_row - m)
        ot = _dot(vt_ctx, e_ctx.astype(BF16))
        if local:
            e_loc = jnp.concatenate(
                [jnp.exp(s_prev - m), jnp.exp(s_own - m), jnp.exp(s_next - m)], axis=0)
            den = den + jnp.sum(e_loc, axis=0, keepdims=True)
            vt_loc = jnp.concatenate([vtbuf[n], vtbuf[n + 1], vtbuf[n + 2]], axis=1)
            ot = ot + _dot(vt_loc, e_loc.astype(BF16))
        ot = ot * (1.0 / den)
        for j in range(ATTN_WIDTH // LANES):
            h = (2 * j) // heads_per_kv
            pair = jnp.concatenate(
                [ot[h * HEAD_DIM:(h + 1) * HEAD_DIM, (2 * j) * BLOCK:(2 * j + 1) * BLOCK],
                 ot[h * HEAD_DIM:(h + 1) * HEAD_DIM, (2 * j + 1) * BLOCK:(2 * j + 2) * BLOCK]], axis=0)
            zc = main_ref[0, rows, MAIN_ZC + j * LANES:MAIN_ZC + (j + 1) * LANES].astype(F32)
            yc_s[rows, j * LANES:(j + 1) * LANES] = (pair.T * _silu(zc)).astype(BF16)
        return carry

    lax.fori_loop(0, nsub, sub_block, 0)

    g = main_ref[0, :, MAIN_G:MAIN_G + 3 * d].astype(F32)
    merged = _sigmoid(g[:, 0:d]) * _dot(ya_s[...], wpa_ref[...])
    merged = merged + _sigmoid(g[:, d:2 * d]) * _dot(yb_s[...], wpb_ref[...])
    merged = merged + _sigmoid(g[:, 2 * d:3 * d]) * _dot(yc_s[...], wpc_ref[...])
    out = _dot(merged.astype(BF16), wout_ref[...])
    row = pl.program_id(0) if ctx_row is None else ctx_row
    gate = mod_ref[pl.ds(row, 1), 2 * d:3 * d]
    o_ref[0] = x_ref[0] + gate * out


def _mix_call(x, main, y, qt, k, vt, kc, vtc, mod_l, sinks, wsg, bsg, wpa, wpb, wpc, wout,
              *, tile, local, ctx_row):
    b, length, d = x.shape
    nt = length // tile
    n_ctx = kc.shape[1]
    sub = tile // BLOCK
    n_blocks = length // BLOCK
    const2 = lambda bi, i: (0, 0)
    tok = lambda bi, i: (bi, i, 0)
    blk = lambda bi, i: (bi, i, 0, 0)
    per_b = lambda bi, i: (bi, 0, 0)
    per_b4 = lambda bi, i: (bi, 0, 0, 0)

    in_specs = [
        pl.BlockSpec((1, tile, d), tok),
        pl.BlockSpec((1, tile, MAIN_WIDTH), tok),
        pl.BlockSpec((1, tile, FOURIER_WIDTH), tok),
        pl.BlockSpec((1, sub, ATTN_WIDTH, BLOCK), blk),
    ]
    args = [x, main, y, qt]
    if local:
        prev_blk = lambda bi, i: jnp.maximum(i * sub - 1, 0)
        next_blk = lambda bi, i: jnp.minimum((i + 1) * sub, n_blocks - 1)
        in_specs += [
            pl.BlockSpec((1, BLOCK, KV_WIDTH), lambda bi, i: (bi, prev_blk(bi, i), 0)),
            pl.BlockSpec((1, tile, KV_WIDTH), tok),
            pl.BlockSpec((1, BLOCK, KV_WIDTH), lambda bi, i: (bi, next_blk(bi, i), 0)),
            pl.BlockSpec((1, 1, KV_WIDTH, BLOCK), lambda bi, i: (bi, prev_blk(bi, i), 0, 0)),
            pl.BlockSpec((1, sub, KV_WIDTH, BLOCK), blk),
            pl.BlockSpec((1, 1, KV_WIDTH, BLOCK), lambda bi, i: (bi, next_blk(bi, i), 0, 0)),
        ]
        args += [k, k, k, vt, vt, vt]
    in_specs += [
        pl.BlockSpec((1, n_ctx, KV_WIDTH), per_b),
        pl.BlockSpec((1, n_ctx // BLOCK, KV_WIDTH, BLOCK), per_b4),
        pl.BlockSpec((MOD_ROWS, 3 * d), const2),
        pl.BlockSpec(memory_space=pltpu.SMEM),
        pl.BlockSpec(wsg.shape, const2),
        pl.BlockSpec(bsg.shape, const2),
        pl.BlockSpec(wpa.shape, const2),
        pl.BlockSpec(wpb.shape, const2),
        pl.BlockSpec(wpc.shape, const2),
        pl.BlockSpec(wout.shape, const2),
    ]
    args += [kc, vtc, mod_l, sinks, wsg, bsg, wpa, wpb, wpc, wout]

    scratch = []
    if local:
        scratch += [pltpu.VMEM((tile + 2 * BLOCK, KV_WIDTH), BF16),
                    pltpu.VMEM((sub + 2, KV_WIDTH, BLOCK), BF16)]
    scratch += [
        pltpu.VMEM((tile, FOURIER_WIDTH), BF16),
        pltpu.VMEM((tile, SGU_WIDTH), BF16),
        pltpu.VMEM((tile, ATTN_WIDTH), BF16),
    ]
    kern = functools.partial(_mix_kernel, local=local, tile=tile, n_blocks=n_blocks, ctx_row=ctx_row)
    return pl.pallas_call(
        kern,
        grid=(b, nt),
        in_specs=in_specs,
        out_specs=pl.BlockSpec((1, tile, d), tok),
        out_shape=jax.ShapeDtypeStruct((b, length, d), F32),
        scratch_shapes=scratch,
        compiler_params=pltpu.CompilerParams(
            dimension_semantics=("arbitrary", "arbitrary"), vmem_limit_bytes=VMEM_LIMIT),
        name="mix_lat" if local else "mix_ctx",
    )(*args)


def _rope_tables(length):
    pos = jnp.arange(length)
    nf = HEAD_DIM // 4
    inv = ROPE_THETA ** (-jnp.arange(nf, dtype=F32) / nf)
    ang_r = (pos // GRID_W).astype(F32)[:, None] * inv[None, :]
    ang_c = (pos % GRID_W).astype(F32)[:, None] * inv[None, :]
    ang = jnp.concatenate([ang_r, ang_r, ang_c, ang_c], axis=1)
    sign = jnp.concatenate([-jnp.ones((nf,), F32), jnp.ones((nf,), F32)] * 2)
    cos = jnp.cos(ang)
    sin = jnp.sin(ang) * sign[None, :]
    return (jnp.tile(cos, (1, N_HEADS)), jnp.tile(sin, (1, N_HEADS)),
            jnp.tile(cos, (1, N_KV_HEADS)), jnp.tile(sin, (1, N_KV_HEADS)))


def _angle(num, den):
    return (num % den).astype(F32) * (2.0 * math.pi / den)


def _channel_dft(width):
    gw = FOURIER_WIDTH // FOURIER_GROUPS
    m = jnp.arange(width)
    same = (m[:, None] // gw) == (m[None, :] // gw)
    th = _angle((m[:, None] % gw) * (m[None, :] % gw), gw)
    cc = jnp.where(same, jnp.cos(th), 0.0) * gw ** -0.5
    sc = jnp.where(same, jnp.sin(th), 0.0) * gw ** -0.5
    return jnp.concatenate([cc, sc], axis=0).astype(BF16)


def _fourier_tables():
    l1n, l2n = DFT_L1, DFT_L2
    length = l1n * l2n
    l2 = jnp.arange(l2n)[:, None, None]
    k1 = jnp.arange(l1n)[None, :, None]
    l1 = jnp.arange(l1n)[None, None, :]
    th = _angle(k1 * l1 * l2n + k1 * l2, length)
    w1 = (jnp.concatenate([jnp.cos(th), -jnp.sin(th)], axis=1) * l1n ** -0.5).astype(BF16)
    k2 = jnp.arange(l2n)
    th2 = _angle(k2[:, None] * k2[None, :], l2n)
    c2, s2 = jnp.cos(th2), jnp.sin(th2)
    w2 = (jnp.concatenate([jnp.concatenate([c2, s2], axis=1),
                           jnp.concatenate([-s2, c2], axis=1)], axis=0) * l2n ** -0.5).astype(BF16)
    return w1, w2, _channel_dft(FOURIER_WIDTH)


def _fourier_ctx_tables(n):
    k = jnp.arange(n)
    th = _angle(k[:, None] * k[None, :], n)
    wc = (jnp.concatenate([jnp.cos(th), -jnp.sin(th)], axis=0) * n ** -0.5).astype(BF16)
    return wc, _channel_dft(FOURIER_WIDTH)


def _block_diag_ones(width):
    m = jnp.arange(width) // HEAD_DIM
    return (m[:, None] == m[None, :]).astype(BF16)


def _reorder_w_in(w):
    fw, sw, aw, kw = FOURIER_WIDTH, SGU_WIDTH, ATTN_WIDTH, KV_WIDTH
    o_q = 2 * fw + 3 * sw
    o_k = o_q + aw
    o_zc = o_k + 2 * kw
    parts = [w[:, 0:o_q], w[:, o_zc:], w[:, o_q:o_zc]]
    return jnp.concatenate(parts, axis=1).astype(BF16)


def kernel(x, c, ctx, c_ctx, w_ada, b_ada, w_in, sgu_w, sgu_b, q_norm_g, k_norm_g,
           attn_sink, w_pa, w_pb, w_pc, w_out):
    b, length, d = x.shape
    n_ctx = ctx.shape[1]
    assert length == DFT_L1 * DFT_L2 and d == D_MODEL and b + 1 <= MOD_ROWS
    lat_tile = 512
    ctx_row = b

    cs = jnp.zeros((MOD_ROWS, d), F32).at[:b].set(c).at[b].set(c_ctx)
    mod = _ada_call(cs, w_ada, b_ada)

    rope_tables = _rope_tables(length)
    w1, w2, cs_lat = _fourier_tables()
    wc, cs_ctx = _fourier_ctx_tables(n_ctx)
    bdq = _block_diag_ones(ATTN_WIDTH)
    bdk = _block_diag_ones(KV_WIDTH)

    xc = ctx
    for l in range(DEPTH):
        last = l == DEPTH - 1
        wcat = _reorder_w_in(w_in[l])
        gq = jnp.tile(q_norm_g[l], N_HEADS)[None, :]
        gk = jnp.tile(k_norm_g[l], N_KV_HEADS)[None, :]
        wsg = jnp.concatenate([sgu_w[l, g] for g in range(SGU_GROUPS)], axis=1).astype(BF16)
        bsg = jnp.repeat(sgu_b[l].T, SGU_WIDTH // SGU_GROUPS, axis=1)
        wpa, wpb, wpc, wout = (w_pa[l].astype(BF16), w_pb[l].astype(BF16),
                               w_pc[l].astype(BF16), w_out[l].astype(BF16))
        sinks = attn_sink[l]

        fa_c, main_c, qt_c, kc, vtc = _proj_call(
            xc, mod[l], wcat, bdq, bdk, gq, gk, None, tile=n_ctx, ctx_row=ctx_row)
        fa, main, qt, k, vt = _proj_call(
            x, mod[l], wcat, bdq, bdk, gq, gk, rope_tables, tile=lat_tile, ctx_row=None)
        y = _fourier_call(fa, w1, w2, cs_lat)
        x = _mix_call(x, main, y, qt, k, vt, kc, vtc, mod[l], sinks, wsg, bsg, wpa, wpb, wpc, wout,
                      tile=lat_tile, local=True, ctx_row=None)
        if not last:
            y_c = _fourier_ctx_call(fa_c, wc, cs_ctx)
            xc = _mix_call(xc, main_c, y_c, qt_c, None, None, kc, vtc, mod[l], sinks, wsg, bsg,
                           wpa, wpb, wpc, wout, tile=n_ctx, local=False, ctx_row=ctx_row)
    return x
```

```python
import functools
import math

import jax
import jax.numpy as jnp
from jax import lax
from jax.experimental import pallas as pl
from jax.experimental.pallas import tpu as pltpu

F32 = jnp.float32
BF16 = jnp.bfloat16

D_MODEL = 1024
DEPTH = 4
GRID_W = 64
FOURIER_WIDTH = 256
FOURIER_GROUPS = 4
SGU_WIDTH = 256
SGU_GROUPS = 4
CHUNK = 128
N_HEADS = 8
N_KV_HEADS = 2
HEAD_DIM = 64
ATTN_WIDTH = N_HEADS * HEAD_DIM
KV_WIDTH = N_KV_HEADS * HEAD_DIM
BLOCK = 128
ROPE_THETA = 10000.0
EPS = 1e-6
NEG_INF = -1e30
LOG2E = math.log2(math.e)

LANES = 128
MOD_ROWS = 8
ONES_ROWS = 16

W_FA = 0
W_MAIN = FOURIER_WIDTH
MAIN_ZA = 0
MAIN_U = MAIN_ZA + FOURIER_WIDTH
MAIN_VS = MAIN_U + SGU_WIDTH
MAIN_ZB = MAIN_VS + SGU_WIDTH
MAIN_ZC = MAIN_ZB + SGU_WIDTH
MAIN_G = MAIN_ZC + ATTN_WIDTH
MAIN_WIDTH = MAIN_G + 3 * D_MODEL
W_Q = W_MAIN + MAIN_WIDTH
W_K = W_Q + ATTN_WIDTH
W_V = W_K + KV_WIDTH
W_TOTAL = W_V + KV_WIDTH
PROJ_CHUNK = 512

DFT_L1 = 64
DFT_L2 = 128
DFT1_SLABS = 32
DFT2_GROUP = 8

VMEM_LIMIT = 56 * 1024 * 1024


def _silu(z):
    return 0.5 * z * (1.0 + jnp.tanh(0.5 * z))


def _silu_of_half(zh):
    return zh * (1.0 + jnp.tanh(zh))


def _dot(a, b):
    return jnp.dot(a, b, preferred_element_type=F32)


def _ada_kernel(c_ref, w_ref, b_ref, o_ref):
    s = _silu(c_ref[...]).astype(BF16)
    o_ref[0] = _dot(s, w_ref[0].astype(BF16)) + b_ref[0]


def _ada_call(cs, w_ada, b_ada):
    d = D_MODEL
    return pl.pallas_call(
        _ada_kernel,
        grid=(DEPTH, 3),
        in_specs=[
            pl.BlockSpec((MOD_ROWS, d), lambda l, j: (0, 0)),
            pl.BlockSpec((1, d, d), lambda l, j: (l, 0, j)),
            pl.BlockSpec((1, 1, d), lambda l, j: (l, 0, j)),
        ],
        out_specs=pl.BlockSpec((1, MOD_ROWS, d), lambda l, j: (l, 0, j)),
        out_shape=jax.ShapeDtypeStruct((DEPTH, MOD_ROWS, 3 * d), F32),
        compiler_params=pltpu.CompilerParams(
            dimension_semantics=("arbitrary", "arbitrary"), vmem_limit_bytes=VMEM_LIMIT),
        name="ada_mod",
    )(cs, w_ada, b_ada.reshape(DEPTH, 1, 3 * d))


def _norm_rope(t, bd_ref, gain, cos, sin, rope, out_scale):
    width = t.shape[-1]
    ssq = _dot((t * t).astype(BF16), bd_ref[...])
    tn = t * lax.rsqrt(ssq * (1.0 / HEAD_DIM) + EPS) * gain
    if rope:
        lane = lax.broadcasted_iota(jnp.int32, tn.shape, 1)
        first = (lane % 32) < 16
        partner = jnp.where(first, pltpu.roll(tn, width - 16, 1), pltpu.roll(tn, 16, 1))
        tn = tn * cos + partner * sin
    if out_scale != 1.0:
        tn = tn * out_scale
    return tn


def _proj_kernel(x_ref, mod_ref, w_ref, bdq_ref, bdk_ref, gq_ref, gk_ref,
                 cosq_ref, sinq_ref, cosk_ref, sinkv_ref,
                 fa_ref, main_ref, qt_ref, k_ref, vt_ref, *, ctx_row, rope):
    d = D_MODEL
    nsub = x_ref.shape[1] // BLOCK
    xt = x_ref[0]
    ms = jnp.mean(xt * xt, axis=-1, keepdims=True)
    row = pl.program_id(0) if ctx_row is None else ctx_row
    modrow = mod_ref[pl.ds(row, 1), :]
    shift = modrow[:, 0:d]
    scale = modrow[:, d:2 * d]
    hb = ((xt * lax.rsqrt(ms + EPS)) * (1.0 + scale) + shift).astype(BF16)

    fa_ref[0] = _dot(hb, w_ref[:, W_FA:W_FA + FOURIER_WIDTH]).astype(BF16)
    for c0 in range(0, MAIN_WIDTH, PROJ_CHUNK):
        main_ref[0, :, c0:c0 + PROJ_CHUNK] = _dot(
            hb, w_ref[:, W_MAIN + c0:W_MAIN + c0 + PROJ_CHUNK]).astype(BF16)

    q = _dot(hb, w_ref[:, W_Q:W_Q + ATTN_WIDTH])
    cq = cosq_ref[...] if rope else None
    sq = sinq_ref[...] if rope else None
    q = _norm_rope(q, bdq_ref, gq_ref[...], cq, sq, rope, HEAD_DIM ** -0.5 * LOG2E)
    for s in range(nsub):
        qt_ref[0, s] = q[s * BLOCK:(s + 1) * BLOCK, :].T.astype(BF16)

    k = _dot(hb, w_ref[:, W_K:W_K + KV_WIDTH])
    ck = cosk_ref[...] if rope else None
    sk = sinkv_ref[...] if rope else None
    k = _norm_rope(k, bdk_ref, gk_ref[...], ck, sk, rope, 1.0)
    k_ref[0] = k.astype(BF16)

    v = _dot(hb, w_ref[:, W_V:W_V + KV_WIDTH])
    for s in range(nsub):
        vt_ref[0, s] = v[s * BLOCK:(s + 1) * BLOCK, :].T.astype(BF16)


def _proj_call(x, mod_l, wcat, bdq, bdk, gq, gk, rope_tables, *, tile, ctx_row):
    b, length, d = x.shape
    nt = length // tile
    nsub = tile // BLOCK
    const = lambda bi, i: (0, 0)
    tok = lambda bi, i: (bi, i, 0)
    blk = lambda bi, i: (bi, i, 0, 0)
    rope = rope_tables is not None
    if rope:
        cosq, sinq, cosk, sink = rope_tables
        pos, pos_rows = (lambda bi, i: (i, 0)), tile
    else:
        cosq = sinq = jnp.zeros((MOD_ROWS, ATTN_WIDTH), F32)
        cosk = sink = jnp.zeros((MOD_ROWS, KV_WIDTH), F32)
        pos, pos_rows = const, MOD_ROWS
    kern = functools.partial(_proj_kernel, ctx_row=ctx_row, rope=rope)
    return pl.pallas_call(
        kern,
        grid=(b, nt),
        in_specs=[
            pl.BlockSpec((1, tile, d), tok),
            pl.BlockSpec((MOD_ROWS, 3 * d), const),
            pl.BlockSpec((d, W_TOTAL), const, pipeline_mode=pl.Buffered(1)),
            pl.BlockSpec((ATTN_WIDTH, ATTN_WIDTH), const),
            pl.BlockSpec((KV_WIDTH, KV_WIDTH), const),
            pl.BlockSpec((1, ATTN_WIDTH), const),
            pl.BlockSpec((1, KV_WIDTH), const),
            pl.BlockSpec((pos_rows, ATTN_WIDTH), pos),
            pl.BlockSpec((pos_rows, ATTN_WIDTH), pos),
            pl.BlockSpec((pos_rows, KV_WIDTH), pos),
            pl.BlockSpec((pos_rows, KV_WIDTH), pos),
        ],
        out_specs=[
            pl.BlockSpec((1, tile, FOURIER_WIDTH), tok),
            pl.BlockSpec((1, tile, MAIN_WIDTH), tok),
            pl.BlockSpec((1, nsub, ATTN_WIDTH, BLOCK), blk),
            pl.BlockSpec((1, tile, KV_WIDTH), tok),
            pl.BlockSpec((1, nsub, KV_WIDTH, BLOCK), blk),
        ],
        out_shape=[
            jax.ShapeDtypeStruct((b, length, FOURIER_WIDTH), BF16),
            jax.ShapeDtypeStruct((b, length, MAIN_WIDTH), BF16),
            jax.ShapeDtypeStruct((b, length // BLOCK, ATTN_WIDTH, BLOCK), BF16),
            jax.ShapeDtypeStruct((b, length, KV_WIDTH), BF16),
            jax.ShapeDtypeStruct((b, length // BLOCK, KV_WIDTH, BLOCK), BF16),
        ],
        compiler_params=pltpu.CompilerParams(
            dimension_semantics=("arbitrary", "arbitrary"), vmem_limit_bytes=VMEM_LIMIT),
        name="proj_ctx" if ctx_row is not None else "proj_lat",
    )(x, mod_l, wcat, bdq, bdk, gq, gk, cosq, sinq, cosk, sink)


def _dft1_kernel(x_ref, w1_ref, a_ref):
    cw = FOURIER_WIDTH
    for i in range(w1_ref.shape[0]):
        cols = slice(i * cw, (i + 1) * cw)
        a_ref[0, :, cols] = _dot(w1_ref[i], x_ref[0, :, cols]).astype(BF16)


def _dft2_kernel(a_ref, w2_ref, cs_ref, y_ref):
    cw, l2n = FOURIER_WIDTH, DFT_L2
    grp = a_ref.shape[2]
    rhs = jnp.concatenate(
        [jnp.concatenate([a_ref[0, 0, j], a_ref[0, 1, j]], axis=0) for j in range(grp)], axis=1)
    f = _dot(w2_ref[...], rhs)
    for j in range(grp):
        cols = slice(j * cw, (j + 1) * cw)
        fc = jnp.concatenate([f[:l2n, cols], f[l2n:, cols]], axis=1).astype(BF16)
        y_ref[0, :, cols] = _dot(fc, cs_ref[...])


def _fourier_call(fa, w1, w2, cs):
    b, length, cw = fa.shape
    l1n, l2n = DFT_L1, DFT_L2
    params = pltpu.CompilerParams(
        dimension_semantics=("arbitrary", "arbitrary"), vmem_limit_bytes=VMEM_LIMIT)
    n1 = DFT1_SLABS
    a = pl.pallas_call(
        _dft1_kernel,
        grid=(b, l2n // n1),
        in_specs=[
            pl.BlockSpec((1, l1n, n1 * cw), lambda bi, j: (bi, 0, j)),
            pl.BlockSpec((n1, 2 * l1n, l1n), lambda bi, j: (j, 0, 0)),
        ],
        out_specs=pl.BlockSpec((1, 2 * l1n, n1 * cw), lambda bi, j: (bi, 0, j)),
        out_shape=jax.ShapeDtypeStruct((b, 2 * l1n, l2n * cw), BF16),
        compiler_params=params,
        name="dft1_lat",
    )(fa.reshape(b, l1n, l2n * cw), w1)
    grp = DFT2_GROUP
    y = pl.pallas_call(
        _dft2_kernel,
        grid=(b, l1n // grp),
        in_specs=[
            pl.BlockSpec((1, 2, grp, l2n, cw), lambda bi, g: (bi, 0, g, 0, 0)),
            pl.BlockSpec(w2.shape, lambda bi, g: (0, 0)),
            pl.BlockSpec(cs.shape, lambda bi, g: (0, 0)),
        ],
        out_specs=pl.BlockSpec((1, l2n, grp * cw), lambda bi, g: (bi, 0, g)),
        out_shape=jax.ShapeDtypeStruct((b, l2n, l1n * cw), F32),
        compiler_params=params,
        name="dft2_lat",
    )(a.reshape(b, 2, l1n, l2n, cw), w2, cs)
    return y.reshape(b, length, cw)


def _fourier_ctx_kernel(fa_ref, wc_ref, cs_ref, y_ref):
    n = fa_ref.shape[1]
    f = _dot(wc_ref[...], fa_ref[0])
    fc = jnp.concatenate([f[:n], f[n:]], axis=1).astype(BF16)
    y_ref[0] = _dot(fc, cs_ref[...])


def _fourier_ctx_call(fa, wc, cs):
    b, length, width = fa.shape
    return pl.pallas_call(
        _fourier_ctx_kernel,
        grid=(b,),
        in_specs=[
            pl.BlockSpec((1, length, width), lambda bi: (bi, 0, 0)),
            pl.BlockSpec(wc.shape, lambda bi: (0, 0)),
            pl.BlockSpec(cs.shape, lambda bi: (0, 0)),
        ],
        out_specs=pl.BlockSpec((1, length, width), lambda bi: (bi, 0, 0)),
        out_shape=jax.ShapeDtypeStruct((b, length, width), F32),
        compiler_params=pltpu.CompilerParams(
            dimension_semantics=("arbitrary",), vmem_limit_bytes=VMEM_LIMIT),
        name="fourier_ctx",
    )(fa, wc, cs)


def _mix_kernel(*refs, local, tile, n_blocks, ctx_row):
    if local:
        (x_ref, main_ref, y_ref, qt_ref, kp_ref, km_ref, kn_ref, vtp_ref, vtm_ref, vtn_ref,
         kc_ref, vtc_ref, mod_ref, sink_ref, wsg_ref, bsg_ref,
         wpa_ref, wpb_ref, wpc_ref, wout_ref, o_ref,
         kbuf, vtbuf, ya_s, yb_s, yc_s) = refs
    else:
        (x_ref, main_ref, y_ref, qt_ref, kc_ref, vtc_ref, mod_ref, sink_ref, wsg_ref, bsg_ref,
         wpa_ref, wpb_ref, wpc_ref, wout_ref, o_ref,
         ya_s, yb_s, yc_s) = refs
    d = D_MODEL
    nsub = tile // BLOCK
    tile_idx = pl.program_id(1)
    heads_per_kv = N_HEADS // N_KV_HEADS
    cols_all = N_HEADS * BLOCK

    if local:
        kbuf[0:BLOCK, :] = kp_ref[0]
        kbuf[BLOCK:BLOCK + tile, :] = km_ref[0]
        kbuf[BLOCK + tile:2 * BLOCK + tile, :] = kn_ref[0]
        vtbuf[0] = vtp_ref[0, 0]
        for s in range(nsub):
            vtbuf[1 + s] = vtm_ref[0, s]
        vtbuf[1 + nsub] = vtn_ref[0, 0]

    def with_ones(vt):
        return jnp.concatenate([vt, jnp.ones((ONES_ROWS, vt.shape[1]), BF16)], axis=0)

    vt_ctx = with_ones(jnp.concatenate([vtc_ref[0, s] for s in range(vtc_ref.shape[1])], axis=1))

    ya_s[...] = (y_ref[0] * _silu_of_half(main_ref[0, :, MAIN_ZA:MAIN_ZA + FOURIER_WIDTH].astype(F32))).astype(BF16)

    sgu_grp = lax.broadcasted_iota(jnp.int32, (CHUNK, SGU_WIDTH), 1) // (SGU_WIDTH // SGU_GROUPS)
    zero_head = jnp.zeros((HEAD_DIM, BLOCK), BF16)
    sink_row = jnp.concatenate(
        [jnp.full((1, BLOCK), sink_ref[hh] * LOG2E, F32) for hh in range(N_HEADS)], axis=1)
    kq_diff = (lax.broadcasted_iota(jnp.int32, (BLOCK, cols_all), 0)
               - lax.broadcasted_iota(jnp.int32, (BLOCK, cols_all), 1) % BLOCK)

    def sub_block(n, carry):
        r0 = pl.multiple_of(n * BLOCK, BLOCK)
        rows = pl.ds(r0, BLOCK)

        vs = main_ref[0, rows, MAIN_VS:MAIN_VS + SGU_WIDTH].astype(F32)
        vn = (vs * lax.rsqrt(jnp.mean(vs * vs, axis=-1, keepdims=True) + EPS)).astype(BF16)
        rhs = jnp.concatenate(
            [jnp.where(sgu_grp == g, vn, jnp.zeros_like(vn)) for g in range(SGU_GROUPS)], axis=0)
        mixed = _dot(wsg_ref[...], rhs) + bsg_ref[...]
        u = main_ref[0, rows, MAIN_U:MAIN_U + SGU_WIDTH].astype(F32)
        zb = main_ref[0, rows, MAIN_ZB:MAIN_ZB + SGU_WIDTH].astype(F32)
        yb_s[rows, :] = (u * mixed * _silu_of_half(zb)).astype(BF16)

        qt = qt_ref[0, n]
        cols = []
        for hh in range(N_HEADS):
            qh = qt[hh * HEAD_DIM:(hh + 1) * HEAD_DIM, :]
            cols.append(jnp.concatenate(
                [qh, zero_head] if hh < heads_per_kv else [zero_head, qh], axis=0))
        qst = jnp.concatenate(cols, axis=1)

        s_ctx = _dot(kc_ref[0], qst)
        m = jnp.maximum(jnp.max(s_ctx, axis=0, keepdims=True), sink_row)
        if local:
            gblk = tile_idx * nsub + n
            s_loc = _dot(kbuf[pl.ds(r0, 3 * BLOCK), :], qst)
            s_prev = jnp.where(kq_diff >= jnp.where(gblk > 0, 0, BLOCK), s_loc[0:BLOCK], NEG_INF)
            s_own = s_loc[BLOCK:2 * BLOCK]
            s_next = jnp.where(kq_diff <= jnp.where(gblk < n_blocks - 1, 0, -BLOCK),
                               s_loc[2 * BLOCK:3 * BLOCK], NEG_INF)
            m_loc = jnp.max(jnp.maximum(jnp.maximum(s_prev, s_own), s_next), axis=0, keepdims=True)
            m = jnp.maximum(m, m_loc)
        e_ctx = jnp.exp2(s_ctx - m)
        ot = _dot(vt_ctx, e_ctx.astype(BF16))
        if local:
            e_loc = jnp.concatenate(
                [jnp.exp2(s_prev - m), jnp.exp2(s_own - m), jnp.exp2(s_next - m)], axis=0)
            vt_loc = with_ones(jnp.concatenate([vtbuf[n], vtbuf[n + 1], vtbuf[n + 2]], axis=1))
            ot = ot + _dot(vt_loc, e_loc.astype(BF16))
        den = ot[KV_WIDTH:KV_WIDTH + 1] + jnp.exp2(sink_row - m)
        ot = ot[0:KV_WIDTH] * (1.0 / den)
        for j in range(ATTN_WIDTH // LANES):
            h = (2 * j) // heads_per_kv
            pair = jnp.concatenate(
                [ot[h * HEAD_DIM:(h + 1) * HEAD_DIM, (2 * j) * BLOCK:(2 * j + 1) * BLOCK],
                 ot[h * HEAD_DIM:(h + 1) * HEAD_DIM, (2 * j + 1) * BLOCK:(2 * j + 2) * BLOCK]], axis=0)
            zc = main_ref[0, rows, MAIN_ZC + j * LANES:MAIN_ZC + (j + 1) * LANES].astype(F32)
            yc_s[rows, j * LANES:(j + 1) * LANES] = (pair.T * _silu_of_half(zc)).astype(BF16)
        return carry

    lax.fori_loop(0, nsub, sub_block, 0, unroll=True)

    t = jnp.tanh(main_ref[0, :, MAIN_G:MAIN_G + 3 * d].astype(F32))
    ma = _dot(ya_s[...], wpa_ref[...])
    mb = _dot(yb_s[...], wpb_ref[...])
    mc = _dot(yc_s[...], wpc_ref[...])
    merged2 = (ma + t[:, 0:d] * ma) + (mb + t[:, d:2 * d] * mb) + (mc + t[:, 2 * d:3 * d] * mc)
    out = _dot(merged2.astype(BF16), wout_ref[...])
    row = pl.program_id(0) if ctx_row is None else ctx_row
    gate = mod_ref[pl.ds(row, 1), 2 * d:3 * d]
    o_ref[0] = x_ref[0] + gate * out


def _mix_call(x, main, y, qt, k, vt, kc, vtc, mod_l, sinks, wsg, bsg, wpa, wpb, wpc, wout,
              *, tile, local, ctx_row):
    b, length, d = x.shape
    nt = length // tile
    n_ctx = kc.shape[1]
    sub = tile // BLOCK
    n_blocks = length // BLOCK
    const2 = lambda bi, i: (0, 0)
    tok = lambda bi, i: (bi, i, 0)
    blk = lambda bi, i: (bi, i, 0, 0)
    per_b = lambda bi, i: (bi, 0, 0)
    per_b4 = lambda bi, i: (bi, 0, 0, 0)

    in_specs = [
        pl.BlockSpec((1, tile, d), tok),
        pl.BlockSpec((1, tile, MAIN_WIDTH), tok),
        pl.BlockSpec((1, tile, FOURIER_WIDTH), tok),
        pl.BlockSpec((1, sub, ATTN_WIDTH, BLOCK), blk),
    ]
    args = [x, main, y, qt]
    if local:
        prev_blk = lambda bi, i: jnp.maximum(i * sub - 1, 0)
        next_blk = lambda bi, i: jnp.minimum((i + 1) * sub, n_blocks - 1)
        in_specs += [
            pl.BlockSpec((1, BLOCK, KV_WIDTH), lambda bi, i: (bi, prev_blk(bi, i), 0)),
            pl.BlockSpec((1, tile, KV_WIDTH), tok),
            pl.BlockSpec((1, BLOCK, KV_WIDTH), lambda bi, i: (bi, next_blk(bi, i), 0)),
            pl.BlockSpec((1, 1, KV_WIDTH, BLOCK), lambda bi, i: (bi, prev_blk(bi, i), 0, 0)),
            pl.BlockSpec((1, sub, KV_WIDTH, BLOCK), blk),
            pl.BlockSpec((1, 1, KV_WIDTH, BLOCK), lambda bi, i: (bi, next_blk(bi, i), 0, 0)),
        ]
        args += [k, k, k, vt, vt, vt]
    in_specs += [
        pl.BlockSpec((1, n_ctx, KV_WIDTH), per_b),
        pl.BlockSpec((1, n_ctx // BLOCK, KV_WIDTH, BLOCK), per_b4),
        pl.BlockSpec((MOD_ROWS, 3 * d), const2),
        pl.BlockSpec(memory_space=pltpu.SMEM),
        pl.BlockSpec(wsg.shape, const2),
        pl.BlockSpec(bsg.shape, const2),
        pl.BlockSpec(wpa.shape, const2),
        pl.BlockSpec(wpb.shape, const2),
        pl.BlockSpec(wpc.shape, const2),
        pl.BlockSpec(wout.shape, const2),
    ]
    args += [kc, vtc, mod_l, sinks, wsg, bsg, wpa, wpb, wpc, wout]

    scratch = []
    if local:
        scratch += [pltpu.VMEM((tile + 2 * BLOCK, KV_WIDTH), BF16),
                    pltpu.VMEM((sub + 2, KV_WIDTH, BLOCK), BF16)]
    scratch += [
        pltpu.VMEM((tile, FOURIER_WIDTH), BF16),
        pltpu.VMEM((tile, SGU_WIDTH), BF16),
        pltpu.VMEM((tile, ATTN_WIDTH), BF16),
    ]
    kern = functools.partial(_mix_kernel, local=local, tile=tile, n_blocks=n_blocks, ctx_row=ctx_row)
    return pl.pallas_call(
        kern,
        grid=(b, nt),
        in_specs=in_specs,
        out_specs=pl.BlockSpec((1, tile, d), tok),
        out_shape=jax.ShapeDtypeStruct((b, length, d), F32),
        scratch_shapes=scratch,
        compiler_params=pltpu.CompilerParams(
            dimension_semantics=("arbitrary", "arbitrary"), vmem_limit_bytes=VMEM_LIMIT),
        name="mix_lat" if local else "mix_ctx",
    )(*args)


def _rope_tables(length):
    pos = jnp.arange(length)
    nf = HEAD_DIM // 4
    inv = ROPE_THETA ** (-jnp.arange(nf, dtype=F32) / nf)
    ang_r = (pos // GRID_W).astype(F32)[:, None] * inv[None, :]
    ang_c = (pos % GRID_W).astype(F32)[:, None] * inv[None, :]
    ang = jnp.concatenate([ang_r, ang_r, ang_c, ang_c], axis=1)
    sign = jnp.concatenate([-jnp.ones((nf,), F32), jnp.ones((nf,), F32)] * 2)
    cos = jnp.cos(ang)
    sin = jnp.sin(ang) * sign[None, :]
    return (jnp.tile(cos, (1, N_HEADS)), jnp.tile(sin, (1, N_HEADS)),
            jnp.tile(cos, (1, N_KV_HEADS)), jnp.tile(sin, (1, N_KV_HEADS)))


def _angle(num, den):
    return (num % den).astype(F32) * (2.0 * math.pi / den)


def _channel_dft(width):
    gw = FOURIER_WIDTH // FOURIER_GROUPS
    m = jnp.arange(width)
    same = (m[:, None] // gw) == (m[None, :] // gw)
    th = _angle((m[:, None] % gw) * (m[None, :] % gw), gw)
    cc = jnp.where(same, jnp.cos(th), 0.0) * gw ** -0.5
    sc = jnp.where(same, jnp.sin(th), 0.0) * gw ** -0.5
    return jnp.concatenate([cc, sc], axis=0).astype(BF16)


def _fourier_tables():
    l1n, l2n = DFT_L1, DFT_L2
    length = l1n * l2n
    l2 = jnp.arange(l2n)[:, None, None]
    k1 = jnp.arange(l1n)[None, :, None]
    l1 = jnp.arange(l1n)[None, None, :]
    th = _angle(k1 * l1 * l2n + k1 * l2, length)
    w1 = (jnp.concatenate([jnp.cos(th), -jnp.sin(th)], axis=1) * l1n ** -0.5).astype(BF16)
    k2 = jnp.arange(l2n)
    th2 = _angle(k2[:, None] * k2[None, :], l2n)
    c2, s2 = jnp.cos(th2), jnp.sin(th2)
    w2 = (jnp.concatenate([jnp.concatenate([c2, s2], axis=1),
                           jnp.concatenate([-s2, c2], axis=1)], axis=0) * l2n ** -0.5).astype(BF16)
    return w1, w2, _channel_dft(FOURIER_WIDTH)


def _fourier_ctx_tables(n):
    k = jnp.arange(n)
    th = _angle(k[:, None] * k[None, :], n)
    wc = (jnp.concatenate([jnp.cos(th), -jnp.sin(th)], axis=0) * n ** -0.5).astype(BF16)
    return wc, _channel_dft(FOURIER_WIDTH)


def _block_diag_ones(width):
    m = jnp.arange(width) // HEAD_DIM
    return (m[:, None] == m[None, :]).astype(BF16)


def _reorder_w_in(w):
    fw, sw, aw, kw = FOURIER_WIDTH, SGU_WIDTH, ATTN_WIDTH, KV_WIDTH
    o_q = 2 * fw + 3 * sw
    o_k = o_q + aw
    o_zc = o_k + 2 * kw
    col = jnp.arange(w.shape[1])
    halved = ((col >= fw) & (col < 2 * fw)) | ((col >= o_q - sw) & (col < o_q)) | (col >= o_zc)
    w = w * jnp.where(halved, 0.5, 1.0)
    parts = [w[:, 0:o_q], w[:, o_zc:], w[:, o_q:o_zc]]
    return jnp.concatenate(parts, axis=1).astype(BF16)


def kernel(x, c, ctx, c_ctx, w_ada, b_ada, w_in, sgu_w, sgu_b, q_norm_g, k_norm_g,
           attn_sink, w_pa, w_pb, w_pc, w_out):
    b, length, d = x.shape
    n_ctx = ctx.shape[1]
    assert length == DFT_L1 * DFT_L2 and d == D_MODEL and b + 1 <= MOD_ROWS
    lat_tile = 512
    ctx_row = b

    cs = jnp.zeros((MOD_ROWS, d), F32).at[:b].set(c).at[b].set(c_ctx)
    mod = _ada_call(cs, w_ada, b_ada)

    rope_tables = _rope_tables(length)
    w1, w2, cs_lat = _fourier_tables()
    wc, cs_ctx = _fourier_ctx_tables(n_ctx)
    bdq = _block_diag_ones(ATTN_WIDTH)
    bdk = _block_diag_ones(KV_WIDTH)

    xc = ctx
    for l in range(DEPTH):
        last = l == DEPTH - 1
        wcat = _reorder_w_in(w_in[l])
        gq = jnp.tile(q_norm_g[l], N_HEADS)[None, :]
        gk = jnp.tile(k_norm_g[l], N_KV_HEADS)[None, :]
        wsg = jnp.concatenate([sgu_w[l, g] for g in range(SGU_GROUPS)], axis=1).astype(BF16)
        bsg = jnp.repeat(sgu_b[l].T, SGU_WIDTH // SGU_GROUPS, axis=1)
        wpa, wpb, wpc, wout = (w_pa[l].astype(BF16), w_pb[l].astype(BF16),
                               w_pc[l].astype(BF16), (0.5 * w_out[l]).astype(BF16))
        sinks = attn_sink[l]

        fa_c, main_c, qt_c, kc, vtc = _proj_call(
            xc, mod[l], wcat, bdq, bdk, gq, gk, None, tile=n_ctx, ctx_row=ctx_row)
        fa, main, qt, k, vt = _proj_call(
            x, mod[l], wcat, bdq, bdk, gq, gk, rope_tables, tile=lat_tile, ctx_row=None)
        y = _fourier_call(fa, w1, w2, cs_lat)
        x = _mix_call(x, main, y, qt, k, vt, kc, vtc, mod[l], sinks, wsg, bsg, wpa, wpb, wpc, wout,
                      tile=lat_tile, local=True, ctx_row=None)
        if not last:
            y_c = _fourier_ctx_call(fa_c, wc, cs_ctx)
            xc = _mix_call(xc, main_c, y_c, qt_c, None, None, kc, vtc, mod[l], sinks, wsg, bsg,
                           wpa, wpb, wpc, wout, tile=n_ctx, local=False, ctx_row=ctx_row)
    return x
```

```python
import functools
import math

import jax
import jax.numpy as jnp
from jax import lax
from jax.experimental import pallas as pl
from jax.experimental.pallas import tpu as pltpu

F32 = jnp.float32
BF16 = jnp.bfloat16

D_MODEL = 1024
DEPTH = 4
GRID_W = 64
FOURIER_WIDTH = 256
FOURIER_GROUPS = 4
SGU_WIDTH = 256
SGU_GROUPS = 4
CHUNK = 128
N_HEADS = 8
N_KV_HEADS = 2
HEAD_DIM = 64
ATTN_WIDTH = N_HEADS * HEAD_DIM
KV_WIDTH = N_KV_HEADS * HEAD_DIM
BLOCK = 128
ROPE_THETA = 10000.0
EPS = 1e-6
NEG_INF = -1e30
LOG2E = math.log2(math.e)

LANES = 128
MOD_ROWS = 8
ONES_ROWS = 16

W_FA = 0
W_ZA = W_FA + FOURIER_WIDTH
W_ZB = W_ZA + FOURIER_WIDTH + 2 * SGU_WIDTH
W_Q = W_ZB + SGU_WIDTH
W_K = W_Q + ATTN_WIDTH
W_V = W_K + KV_WIDTH
W_ZC = W_V + KV_WIDTH
W_G = W_ZC + ATTN_WIDTH
W_TOTAL = W_G + 3 * D_MODEL
MAIN_ZA = 0
MAIN_U = MAIN_ZA + FOURIER_WIDTH
MAIN_VS = MAIN_U + SGU_WIDTH
MAIN_ZB = MAIN_VS + SGU_WIDTH
MAIN_ZC = MAIN_ZB + SGU_WIDTH
MAIN_G = MAIN_ZC + ATTN_WIDTH
MAIN_WIDTH = MAIN_G + 3 * D_MODEL
PROJ_CHUNK = 512

DFT_L1 = 64
DFT_L2 = 128
DFT1_SLABS = 32
DFT2_GROUP = 8

VMEM_LIMIT = 56 * 1024 * 1024


def _silu(z):
    return 0.5 * z * (1.0 + jnp.tanh(0.5 * z))


def _silu_of_half(zh):
    return zh * (1.0 + jnp.tanh(zh))


def _dot(a, b):
    return jnp.dot(a, b, preferred_element_type=F32)


def _layer(l, rank):
    return lambda *_: (l,) + (0,) * (rank - 1)


def _ada_kernel(c_ref, w_ref, b_ref, o_ref):
    s = _silu(c_ref[...]).astype(BF16)
    o_ref[0] = _dot(s, w_ref[0].astype(BF16)) + b_ref[0]


def _ada_call(cs, w_ada, b_ada):
    d = D_MODEL
    return pl.pallas_call(
        _ada_kernel,
        grid=(DEPTH, 3),
        in_specs=[
            pl.BlockSpec((MOD_ROWS, d), lambda l, j: (0, 0)),
            pl.BlockSpec((1, d, d), lambda l, j: (l, 0, j)),
            pl.BlockSpec((1, 1, d), lambda l, j: (l, 0, j)),
        ],
        out_specs=pl.BlockSpec((1, MOD_ROWS, d), lambda l, j: (l, 0, j)),
        out_shape=jax.ShapeDtypeStruct((DEPTH, MOD_ROWS, 3 * d), F32),
        compiler_params=pltpu.CompilerParams(
            dimension_semantics=("arbitrary", "arbitrary"), vmem_limit_bytes=VMEM_LIMIT),
        name="ada_mod",
    )(cs, w_ada, b_ada.reshape(DEPTH, 1, 3 * d))


def _norm_rope(t, bd_ref, gain, cos, sin, out_scale):
    width = t.shape[-1]
    ssq = _dot((t * t).astype(BF16), bd_ref[...])
    tn = t * lax.rsqrt(ssq * (1.0 / HEAD_DIM) + EPS) * gain
    if cos is not None:
        reps = width // LANES
        cos_w = jnp.concatenate([cos] * reps, axis=1) if reps > 1 else cos
        sin_w = jnp.concatenate([sin] * reps, axis=1) if reps > 1 else sin
        lane = lax.broadcasted_iota(jnp.int32, tn.shape, 1)
        first = (lane % 32) < 16
        partner = jnp.where(first, pltpu.roll(tn, width - 16, 1), pltpu.roll(tn, 16, 1))
        tn = tn * cos_w + partner * sin_w
    if out_scale != 1.0:
        tn = tn * out_scale
    return tn


def _proj_kernel(x_ref, mod_ref, w_ref, bdq_ref, bdk_ref, gq_ref, gk_ref, cos_ref, sin_ref,
                 fa_ref, main_ref, qt_ref, k_ref, vt_ref, *, ctx_row, rope):
    d = D_MODEL
    nsub = x_ref.shape[1] // BLOCK
    xt = x_ref[0]
    ms = jnp.mean(xt * xt, axis=-1, keepdims=True)
    row = pl.program_id(0) if ctx_row is None else ctx_row
    modrow = mod_ref[0, pl.ds(row, 1), :]
    shift = modrow[:, 0:d]
    scale = modrow[:, d:2 * d]
    hb = ((xt * lax.rsqrt(ms + EPS)) * (1.0 + scale) + shift).astype(BF16)
    cos = cos_ref[...] if rope else None
    sin = sin_ref[...] if rope else None

    def proj(c0, width):
        return _dot(hb, w_ref[0, :, c0:c0 + width])

    q = _norm_rope(proj(W_Q, ATTN_WIDTH), bdq_ref, gq_ref[0], cos, sin, HEAD_DIM ** -0.5 * LOG2E)
    for s in range(nsub):
        qt_ref[0, s] = q[s * BLOCK:(s + 1) * BLOCK, :].T.astype(BF16)
    k = _norm_rope(proj(W_K, KV_WIDTH), bdk_ref, gk_ref[0], cos, sin, 1.0)
    k_ref[0] = k.astype(BF16)
    v = proj(W_V, KV_WIDTH)
    for s in range(nsub):
        vt_ref[0, s] = v[s * BLOCK:(s + 1) * BLOCK, :].T.astype(BF16)

    fa_ref[0] = proj(W_FA, FOURIER_WIDTH).astype(BF16)
    for src, dst, width in ((W_ZA, MAIN_ZA, W_Q - W_ZA), (W_ZC, MAIN_ZC, W_TOTAL - W_ZC)):
        for c0 in range(0, width, PROJ_CHUNK):
            main_ref[0, :, dst + c0:dst + c0 + PROJ_CHUNK] = proj(src + c0, PROJ_CHUNK).astype(BF16)


def _proj_call(x, mod, w_all, bdq, bdk, gq_all, gk_all, rope_tables, l, *, tile, ctx_row):
    b, length, d = x.shape
    nt = length // tile
    nsub = tile // BLOCK
    const = lambda bi, i: (0, 0)
    tok = lambda bi, i: (bi, i, 0)
    blk = lambda bi, i: (bi, i, 0, 0)
    rope = rope_tables is not None
    if rope:
        cos, sin = rope_tables
        pos, pos_rows = (lambda bi, i: (i, 0)), tile
    else:
        cos = sin = jnp.zeros((MOD_ROWS, LANES), F32)
        pos, pos_rows = const, MOD_ROWS
    kern = functools.partial(_proj_kernel, ctx_row=ctx_row, rope=rope)
    return pl.pallas_call(
        kern,
        grid=(b, nt),
        in_specs=[
            pl.BlockSpec((1, tile, d), tok),
            pl.BlockSpec((1, MOD_ROWS, 3 * d), _layer(l, 3)),
            pl.BlockSpec((1, d, W_TOTAL), _layer(l, 3), pipeline_mode=pl.Buffered(1)),
            pl.BlockSpec((ATTN_WIDTH, ATTN_WIDTH), const),
            pl.BlockSpec((KV_WIDTH, KV_WIDTH), const),
            pl.BlockSpec((1, 1, ATTN_WIDTH), _layer(l, 3)),
            pl.BlockSpec((1, 1, KV_WIDTH), _layer(l, 3)),
            pl.BlockSpec((pos_rows, LANES), pos),
            pl.BlockSpec((pos_rows, LANES), pos),
        ],
        out_specs=[
            pl.BlockSpec((1, tile, FOURIER_WIDTH), tok),
            pl.BlockSpec((1, tile, MAIN_WIDTH), tok),
            pl.BlockSpec((1, nsub, ATTN_WIDTH, BLOCK), blk),
            pl.BlockSpec((1, tile, KV_WIDTH), tok),
            pl.BlockSpec((1, nsub, KV_WIDTH, BLOCK), blk),
        ],
        out_shape=[
            jax.ShapeDtypeStruct((b, length, FOURIER_WIDTH), BF16),
            jax.ShapeDtypeStruct((b, length, MAIN_WIDTH), BF16),
            jax.ShapeDtypeStruct((b, length // BLOCK, ATTN_WIDTH, BLOCK), BF16),
            jax.ShapeDtypeStruct((b, length, KV_WIDTH), BF16),
            jax.ShapeDtypeStruct((b, length // BLOCK, KV_WIDTH, BLOCK), BF16),
        ],
        compiler_params=pltpu.CompilerParams(
            dimension_semantics=("arbitrary", "arbitrary"), vmem_limit_bytes=VMEM_LIMIT),
        name="proj_ctx" if ctx_row is not None else "proj_lat",
    )(x, mod, w_all, bdq, bdk, gq_all, gk_all, cos, sin)


def _dft1_kernel(x_ref, w1_ref, a_ref):
    cw = FOURIER_WIDTH
    for i in range(w1_ref.shape[0]):
        cols = slice(i * cw, (i + 1) * cw)
        a_ref[0, :, cols] = _dot(w1_ref[i], x_ref[0, :, cols]).astype(BF16)


def _dft2_kernel(a_ref, w2_ref, cs_ref, y_ref):
    cw, l2n = FOURIER_WIDTH, DFT_L2
    grp = a_ref.shape[2]
    rhs = jnp.concatenate(
        [jnp.concatenate([a_ref[0, 0, j], a_ref[0, 1, j]], axis=0) for j in range(grp)], axis=1)
    f = _dot(w2_ref[...], rhs)
    for j in range(grp):
        cols = slice(j * cw, (j + 1) * cw)
        fc = jnp.concatenate([f[:l2n, cols], f[l2n:, cols]], axis=1).astype(BF16)
        y_ref[0, :, cols] = _dot(fc, cs_ref[...])


def _fourier_call(fa, w1, w2, cs):
    b, length, cw = fa.shape
    l1n, l2n = DFT_L1, DFT_L2
    params = pltpu.CompilerParams(
        dimension_semantics=("arbitrary", "arbitrary"), vmem_limit_bytes=VMEM_LIMIT)
    n1 = DFT1_SLABS
    a = pl.pallas_call(
        _dft1_kernel,
        grid=(b, l2n // n1),
        in_specs=[
            pl.BlockSpec((1, l1n, n1 * cw), lambda bi, j: (bi, 0, j)),
            pl.BlockSpec((n1, 2 * l1n, l1n), lambda bi, j: (j, 0, 0)),
        ],
        out_specs=pl.BlockSpec((1, 2 * l1n, n1 * cw), lambda bi, j: (bi, 0, j)),
        out_shape=jax.ShapeDtypeStruct((b, 2 * l1n, l2n * cw), BF16),
        compiler_params=params,
        name="dft1_lat",
    )(fa.reshape(b, l1n, l2n * cw), w1)
    grp = DFT2_GROUP
    y = pl.pallas_call(
        _dft2_kernel,
        grid=(b, l1n // grp),
        in_specs=[
            pl.BlockSpec((1, 2, grp, l2n, cw), lambda bi, g: (bi, 0, g, 0, 0)),
            pl.BlockSpec(w2.shape, lambda bi, g: (0, 0)),
            pl.BlockSpec(cs.shape, lambda bi, g: (0, 0)),
        ],
        out_specs=pl.BlockSpec((1, l2n, grp * cw), lambda bi, g: (bi, 0, g)),
        out_shape=jax.ShapeDtypeStruct((b, l2n, l1n * cw), F32),
        compiler_params=params,
        name="dft2_lat",
    )(a.reshape(b, 2, l1n, l2n, cw), w2, cs)
    return y.reshape(b, length, cw)


def _fourier_ctx_kernel(fa_ref, wc_ref, cs_ref, y_ref):
    n = fa_ref.shape[1]
    f = _dot(wc_ref[...], fa_ref[0])
    fc = jnp.concatenate([f[:n], f[n:]], axis=1).astype(BF16)
    y_ref[0] = _dot(fc, cs_ref[...])


def _fourier_ctx_call(fa, wc, cs):
    b, length, width = fa.shape
    return pl.pallas_call(
        _fourier_ctx_kernel,
        grid=(b,),
        in_specs=[
            pl.BlockSpec((1, length, width), lambda bi: (bi, 0, 0)),
            pl.BlockSpec(wc.shape, lambda bi: (0, 0)),
            pl.BlockSpec(cs.shape, lambda bi: (0, 0)),
        ],
        out_specs=pl.BlockSpec((1, length, width), lambda bi: (bi, 0, 0)),
        out_shape=jax.ShapeDtypeStruct((b, length, width), F32),
        compiler_params=pltpu.CompilerParams(
            dimension_semantics=("arbitrary",), vmem_limit_bytes=VMEM_LIMIT),
        name="fourier_ctx",
    )(fa, wc, cs)


def _mix_kernel(*refs, local, tile, n_blocks, ctx_row, layer):
    if local:
        (x_ref, main_ref, y_ref, qt_ref, kp_ref, km_ref, kn_ref, vtp_ref, vtm_ref, vtn_ref,
         kc_ref, vtc_ref, mod_ref, sink_ref, wsg_ref, bsg_ref,
         wpa_ref, wpb_ref, wpc_ref, wout_ref, o_ref,
         kbuf, vtbuf, ya_s, yb_s, yc_s) = refs
    else:
        (x_ref, main_ref, y_ref, qt_ref, kc_ref, vtc_ref, mod_ref, sink_ref, wsg_ref, bsg_ref,
         wpa_ref, wpb_ref, wpc_ref, wout_ref, o_ref,
         ya_s, yb_s, yc_s) = refs
    d = D_MODEL
    nsub = tile // BLOCK
    tile_idx = pl.program_id(1)
    heads_per_kv = N_HEADS // N_KV_HEADS
    cols_all = N_HEADS * BLOCK

    if local:
        kbuf[0:BLOCK, :] = kp_ref[0]
        kbuf[BLOCK:BLOCK + tile, :] = km_ref[0]
        kbuf[BLOCK + tile:2 * BLOCK + tile, :] = kn_ref[0]
        vtbuf[0] = vtp_ref[0, 0]
        for s in range(nsub):
            vtbuf[1 + s] = vtm_ref[0, s]
        vtbuf[1 + nsub] = vtn_ref[0, 0]

    def with_ones(vt):
        return jnp.concatenate([vt, jnp.ones((ONES_ROWS, vt.shape[1]), BF16)], axis=0)

    vt_ctx = with_ones(jnp.concatenate([vtc_ref[0, s] for s in range(vtc_ref.shape[1])], axis=1))

    ya_s[...] = (y_ref[0] * _silu_of_half(main_ref[0, :, MAIN_ZA:MAIN_ZA + FOURIER_WIDTH].astype(F32))).astype(BF16)

    sgu_grp = lax.broadcasted_iota(jnp.int32, (CHUNK, SGU_WIDTH), 1) // (SGU_WIDTH // SGU_GROUPS)
    zero_head = jnp.zeros((HEAD_DIM, BLOCK), BF16)
    sink_row = jnp.concatenate(
        [jnp.full((1, BLOCK), sink_ref[layer, hh] * LOG2E, F32) for hh in range(N_HEADS)], axis=1)
    kq_diff = (lax.broadcasted_iota(jnp.int32, (BLOCK, cols_all), 0)
               - lax.broadcasted_iota(jnp.int32, (BLOCK, cols_all), 1) % BLOCK)

    def sub_block(n, carry):
        r0 = pl.multiple_of(n * BLOCK, BLOCK)
        rows = pl.ds(r0, BLOCK)

        vs = main_ref[0, rows, MAIN_VS:MAIN_VS + SGU_WIDTH].astype(F32)
        vn = (vs * lax.rsqrt(jnp.mean(vs * vs, axis=-1, keepdims=True) + EPS)).astype(BF16)
        rhs = jnp.concatenate(
            [jnp.where(sgu_grp == g, vn, jnp.zeros_like(vn)) for g in range(SGU_GROUPS)], axis=0)
        mixed = _dot(wsg_ref[0], rhs) + bsg_ref[0]
        u = main_ref[0, rows, MAIN_U:MAIN_U + SGU_WIDTH].astype(F32)
        zb = main_ref[0, rows, MAIN_ZB:MAIN_ZB + SGU_WIDTH].astype(F32)
        yb_s[rows, :] = (u * mixed * _silu_of_half(zb)).astype(BF16)

        qt = qt_ref[0, n]
        cols = []
        for hh in range(N_HEADS):
            qh = qt[hh * HEAD_DIM:(hh + 1) * HEAD_DIM, :]
            cols.append(jnp.concatenate(
                [qh, zero_head] if hh < heads_per_kv else [zero_head, qh], axis=0))
        qst = jnp.concatenate(cols, axis=1)

        s_ctx = _dot(kc_ref[0], qst)
        m = jnp.maximum(jnp.max(s_ctx, axis=0, keepdims=True), sink_row)
        if local:
            gblk = tile_idx * nsub + n
            s_loc = _dot(kbuf[pl.ds(r0, 3 * BLOCK), :], qst)
            s_prev = jnp.where(kq_diff >= jnp.where(gblk > 0, 0, BLOCK), s_loc[0:BLOCK], NEG_INF)
            s_own = s_loc[BLOCK:2 * BLOCK]
            s_next = jnp.where(kq_diff <= jnp.where(gblk < n_blocks - 1, 0, -BLOCK),
                               s_loc[2 * BLOCK:3 * BLOCK], NEG_INF)
            m_loc = jnp.max(jnp.maximum(jnp.maximum(s_prev, s_own), s_next), axis=0, keepdims=True)
            m = jnp.maximum(m, m_loc)
        e_ctx = jnp.exp2(s_ctx - m)
        ot = _dot(vt_ctx, e_ctx.astype(BF16))
        if local:
            e_loc = jnp.concatenate(
                [jnp.exp2(s_prev - m), jnp.exp2(s_own - m), jnp.exp2(s_next - m)], axis=0)
            vt_loc = with_ones(jnp.concatenate([vtbuf[n], vtbuf[n + 1], vtbuf[n + 2]], axis=1))
            ot = ot + _dot(vt_loc, e_loc.astype(BF16))
        den = ot[KV_WIDTH:KV_WIDTH + 1] + jnp.exp2(sink_row - m)
        ot = ot[0:KV_WIDTH] * (1.0 / den)
        for j in range(ATTN_WIDTH // LANES):
            h = (2 * j) // heads_per_kv
            pair = jnp.concatenate(
                [ot[h * HEAD_DIM:(h + 1) * HEAD_DIM, (2 * j) * BLOCK:(2 * j + 1) * BLOCK],
                 ot[h * HEAD_DIM:(h + 1) * HEAD_DIM, (2 * j + 1) * BLOCK:(2 * j + 2) * BLOCK]], axis=0)
            zc = main_ref[0, rows, MAIN_ZC + j * LANES:MAIN_ZC + (j + 1) * LANES].astype(F32)
            yc_s[rows, j * LANES:(j + 1) * LANES] = (pair.T * _silu_of_half(zc)).astype(BF16)
        return carry

    lax.fori_loop(0, nsub, sub_block, 0, unroll=True)

    t = jnp.tanh(main_ref[0, :, MAIN_G:MAIN_G + 3 * d].astype(F32))
    ma = _dot(ya_s[...], wpa_ref[0])
    mb = _dot(yb_s[...], wpb_ref[0])
    mc = _dot(yc_s[...], wpc_ref[0])
    merged2 = (ma + t[:, 0:d] * ma) + (mb + t[:, d:2 * d] * mb) + (mc + t[:, 2 * d:3 * d] * mc)
    out = _dot(merged2.astype(BF16), wout_ref[0])
    row = pl.program_id(0) if ctx_row is None else ctx_row
    gate = mod_ref[0, pl.ds(row, 1), 2 * d:3 * d]
    o_ref[0] = x_ref[0] + gate * out


def _mix_call(x, main, y, qt, k, vt, kc, vtc, mod, sinks, wsg, bsg, wpa, wpb, wpc, wout, l,
              *, tile, local, ctx_row):
    b, length, d = x.shape
    nt = length // tile
    n_ctx = kc.shape[1]
    sub = tile // BLOCK
    n_blocks = length // BLOCK
    tok = lambda bi, i: (bi, i, 0)
    blk = lambda bi, i: (bi, i, 0, 0)
    per_b = lambda bi, i: (bi, 0, 0)
    per_b4 = lambda bi, i: (bi, 0, 0, 0)

    in_specs = [
        pl.BlockSpec((1, tile, d), tok),
        pl.BlockSpec((1, tile, MAIN_WIDTH), tok),
        pl.BlockSpec((1, tile, FOURIER_WIDTH), tok),
        pl.BlockSpec((1, sub, ATTN_WIDTH, BLOCK), blk),
    ]
    args = [x, main, y, qt]
    if local:
        prev_blk = lambda bi, i: jnp.maximum(i * sub - 1, 0)
        next_blk = lambda bi, i: jnp.minimum((i + 1) * sub, n_blocks - 1)
        in_specs += [
            pl.BlockSpec((1, BLOCK, KV_WIDTH), lambda bi, i: (bi, prev_blk(bi, i), 0)),
            pl.BlockSpec((1, tile, KV_WIDTH), tok),
            pl.BlockSpec((1, BLOCK, KV_WIDTH), lambda bi, i: (bi, next_blk(bi, i), 0)),
            pl.BlockSpec((1, 1, KV_WIDTH, BLOCK), lambda bi, i: (bi, prev_blk(bi, i), 0, 0)),
            pl.BlockSpec((1, sub, KV_WIDTH, BLOCK), blk),
            pl.BlockSpec((1, 1, KV_WIDTH, BLOCK), lambda bi, i: (bi, next_blk(bi, i), 0, 0)),
        ]
        args += [k, k, k, vt, vt, vt]
    in_specs += [
        pl.BlockSpec((1, n_ctx, KV_WIDTH), per_b),
        pl.BlockSpec((1, n_ctx // BLOCK, KV_WIDTH, BLOCK), per_b4),
        pl.BlockSpec((1, MOD_ROWS, 3 * d), _layer(l, 3)),
        pl.BlockSpec(memory_space=pltpu.SMEM),
    ]
    args += [kc, vtc, mod, sinks]
    for w in (wsg, bsg, wpa, wpb, wpc, wout):
        in_specs.append(pl.BlockSpec((1,) + w.shape[1:], _layer(l, 3)))
        args.append(w)

    scratch = []
    if local:
        scratch += [pltpu.VMEM((tile + 2 * BLOCK, KV_WIDTH), BF16),
                    pltpu.VMEM((sub + 2, KV_WIDTH, BLOCK), BF16)]
    scratch += [
        pltpu.VMEM((tile, FOURIER_WIDTH), BF16),
        pltpu.VMEM((tile, SGU_WIDTH), BF16),
        pltpu.VMEM((tile, ATTN_WIDTH), BF16),
    ]
    kern = functools.partial(_mix_kernel, local=local, tile=tile, n_blocks=n_blocks,
                             ctx_row=ctx_row, layer=l)
    return pl.pallas_call(
        kern,
        grid=(b, nt),
        in_specs=in_specs,
        out_specs=pl.BlockSpec((1, tile, d), tok),
        out_shape=jax.ShapeDtypeStruct((b, length, d), F32),
        scratch_shapes=scratch,
        compiler_params=pltpu.CompilerParams(
            dimension_semantics=("arbitrary", "arbitrary"), vmem_limit_bytes=VMEM_LIMIT),
        name="mix_lat" if local else "mix_ctx",
    )(*args)


def _rope_tables(length):
    pos = jnp.arange(length)
    nf = HEAD_DIM // 4
    inv = ROPE_THETA ** (-jnp.arange(nf, dtype=F32) / nf)
    ang_r = (pos // GRID_W).astype(F32)[:, None] * inv[None, :]
    ang_c = (pos % GRID_W).astype(F32)[:, None] * inv[None, :]
    ang = jnp.concatenate([ang_r, ang_r, ang_c, ang_c] * 2, axis=1)
    sign = jnp.concatenate([-jnp.ones((nf,), F32), jnp.ones((nf,), F32)] * 4)
    return jnp.cos(ang), jnp.sin(ang) * sign[None, :]


def _angle(num, den):
    return (num % den).astype(F32) * (2.0 * math.pi / den)


def _channel_dft(width):
    gw = FOURIER_WIDTH // FOURIER_GROUPS
    m = jnp.arange(width)
    same = (m[:, None] // gw) == (m[None, :] // gw)
    th = _angle((m[:, None] % gw) * (m[None, :] % gw), gw)
    cc = jnp.where(same, jnp.cos(th), 0.0) * gw ** -0.5
    sc = jnp.where(same, jnp.sin(th), 0.0) * gw ** -0.5
    return jnp.concatenate([cc, sc], axis=0).astype(BF16)


def _fourier_tables():
    l1n, l2n = DFT_L1, DFT_L2
    length = l1n * l2n
    l2 = jnp.arange(l2n)[:, None, None]
    k1 = jnp.arange(l1n)[None, :, None]
    l1 = jnp.arange(l1n)[None, None, :]
    th = _angle(k1 * l1 * l2n + k1 * l2, length)
    w1 = (jnp.concatenate([jnp.cos(th), -jnp.sin(th)], axis=1) * l1n ** -0.5).astype(BF16)
    k2 = jnp.arange(l2n)
    th2 = _angle(k2[:, None] * k2[None, :], l2n)
    c2, s2 = jnp.cos(th2), jnp.sin(th2)
    w2 = (jnp.concatenate([jnp.concatenate([c2, s2], axis=1),
                           jnp.concatenate([-s2, c2], axis=1)], axis=0) * l2n ** -0.5).astype(BF16)
    return w1, w2


def _fourier_ctx_table(n):
    k = jnp.arange(n)
    th = _angle(k[:, None] * k[None, :], n)
    return (jnp.concatenate([jnp.cos(th), -jnp.sin(th)], axis=0) * n ** -0.5).astype(BF16)


def _block_diag_ones(width):
    m = jnp.arange(width) // HEAD_DIM
    return (m[:, None] == m[None, :]).astype(BF16)


def _prep_w_in(w_in):
    col = jnp.arange(W_TOTAL)
    halved = ((col >= W_ZA) & (col < W_ZA + FOURIER_WIDTH)) | ((col >= W_ZB) & (col < W_Q)) | (col >= W_ZC)
    return (w_in * jnp.where(halved, 0.5, 1.0)).astype(BF16)


def kernel(x, c, ctx, c_ctx, w_ada, b_ada, w_in, sgu_w, sgu_b, q_norm_g, k_norm_g,
           attn_sink, w_pa, w_pb, w_pc, w_out):
    b, length, d = x.shape
    n_ctx = ctx.shape[1]
    assert length == DFT_L1 * DFT_L2 and d == D_MODEL and b + 1 <= MOD_ROWS
    assert w_in.shape == (DEPTH, d, W_TOTAL)
    lat_tile = 512
    ctx_row = b

    cs = jnp.zeros((MOD_ROWS, d), F32).at[:b].set(c).at[b].set(c_ctx)
    mod = _ada_call(cs, w_ada, b_ada)

    rope_tables = _rope_tables(length)
    w1, w2 = _fourier_tables()
    cs_dft = _channel_dft(FOURIER_WIDTH)
    wc = _fourier_ctx_table(n_ctx)
    bdq = _block_diag_ones(ATTN_WIDTH)
    bdk = _block_diag_ones(KV_WIDTH)

    w_all = _prep_w_in(w_in)
    gq_all = jnp.tile(q_norm_g, (1, N_HEADS))[:, None, :]
    gk_all = jnp.tile(k_norm_g, (1, N_KV_HEADS))[:, None, :]
    wsg = jnp.transpose(sgu_w, (0, 2, 1, 3)).reshape(DEPTH, CHUNK, SGU_GROUPS * CHUNK).astype(BF16)
    bsg = jnp.repeat(jnp.swapaxes(sgu_b, 1, 2), SGU_WIDTH // SGU_GROUPS, axis=2)
    wpa, wpb, wpc = w_pa.astype(BF16), w_pb.astype(BF16), w_pc.astype(BF16)
    wout = (0.5 * w_out).astype(BF16)

    xc = ctx
    for l in range(DEPTH):
        last = l == DEPTH - 1
        fa_c, main_c, qt_c, kc, vtc = _proj_call(
            xc.reshape(1, b * n_ctx, d), mod, w_all, bdq, bdk, gq_all, gk_all, None, l,
            tile=2 * n_ctx, ctx_row=ctx_row)
        kc = kc.reshape(b, n_ctx, KV_WIDTH)
        vtc = vtc.reshape(b, n_ctx // BLOCK, KV_WIDTH, BLOCK)
        fa, main, qt, k, vt = _proj_call(
            x, mod, w_all, bdq, bdk, gq_all, gk_all, rope_tables, l, tile=lat_tile, ctx_row=None)
        y = _fourier_call(fa, w1, w2, cs_dft)
        x = _mix_call(x, main, y, qt, k, vt, kc, vtc, mod, attn_sink, wsg, bsg, wpa, wpb, wpc, wout,
                      l, tile=lat_tile, local=True, ctx_row=None)
        if not last:
            y_c = _fourier_ctx_call(fa_c.reshape(b, n_ctx, FOURIER_WIDTH), wc, cs_dft)
            xc = _mix_call(xc, main_c.reshape(b, n_ctx, MAIN_WIDTH), y_c,
                           qt_c.reshape(b, n_ctx // BLOCK, ATTN_WIDTH, BLOCK), None, None, kc, vtc,
                           mod, attn_sink, wsg, bsg, wpa, wpb, wpc, wout, l,
                           tile=n_ctx, local=False, ctx_row=ctx_row)
    return x
```

```python
import functools
import math

import jax
import jax.numpy as jnp
from jax import lax
from jax.experimental import pallas as pl
from jax.experimental.pallas import tpu as pltpu

F32 = jnp.float32
BF16 = jnp.bfloat16

D_MODEL = 1024
DEPTH = 4
GRID_W = 64
FOURIER_WIDTH = 256
FOURIER_GROUPS = 4
SGU_WIDTH = 256
SGU_GROUPS = 4
CHUNK = 128
N_HEADS = 8
N_KV_HEADS = 2
HEAD_DIM = 64
ATTN_WIDTH = N_HEADS * HEAD_DIM
KV_WIDTH = N_KV_HEADS * HEAD_DIM
BLOCK = 128
ROPE_THETA = 10000.0
EPS = 1e-6
NEG_INF = -1e30
LOG2E = math.log2(math.e)

LANES = 128
MOD_ROWS = 8
ONES_ROWS = 16

W_FA = 0
W_ZA = W_FA + FOURIER_WIDTH
W_ZB = W_ZA + FOURIER_WIDTH + 2 * SGU_WIDTH
W_Q = W_ZB + SGU_WIDTH
W_K = W_Q + ATTN_WIDTH
W_V = W_K + KV_WIDTH
W_ZC = W_V + KV_WIDTH
W_G = W_ZC + ATTN_WIDTH
W_TOTAL = W_G + 3 * D_MODEL
MAIN_ZA = 0
MAIN_U = MAIN_ZA + FOURIER_WIDTH
MAIN_VS = MAIN_U + SGU_WIDTH
MAIN_ZB = MAIN_VS + SGU_WIDTH
MAIN_ZC = MAIN_ZB + SGU_WIDTH
MAIN_G = MAIN_ZC + ATTN_WIDTH
MAIN_WIDTH = MAIN_G + 3 * D_MODEL
PROJ_CHUNK = 512

DFT_L1 = 32
DFT_L2 = 256
DFT_PACK = 16
DFT1_GROUPS = 4
DFT2_GROUP = 8

VMEM_LIMIT = 56 * 1024 * 1024


def _silu(z):
    return 0.5 * z * (1.0 + jnp.tanh(0.5 * z))


def _silu_of_half(zh):
    return zh * (1.0 + jnp.tanh(zh))


def _dot(a, b):
    return jnp.dot(a, b, preferred_element_type=F32)


def _layer(l, rank):
    return lambda *_: (l,) + (0,) * (rank - 1)


def _ada_kernel(c_ref, w_ref, b_ref, o_ref):
    s = _silu(c_ref[...]).astype(BF16)
    o_ref[0] = _dot(s, w_ref[0].astype(BF16)) + b_ref[0]


def _ada_call(cs, w_ada, b_ada):
    d = D_MODEL
    return pl.pallas_call(
        _ada_kernel,
        grid=(DEPTH, 3),
        in_specs=[
            pl.BlockSpec((MOD_ROWS, d), lambda l, j: (0, 0)),
            pl.BlockSpec((1, d, d), lambda l, j: (l, 0, j)),
            pl.BlockSpec((1, 1, d), lambda l, j: (l, 0, j)),
        ],
        out_specs=pl.BlockSpec((1, MOD_ROWS, d), lambda l, j: (l, 0, j)),
        out_shape=jax.ShapeDtypeStruct((DEPTH, MOD_ROWS, 3 * d), F32),
        compiler_params=pltpu.CompilerParams(
            dimension_semantics=("arbitrary", "arbitrary"), vmem_limit_bytes=VMEM_LIMIT),
        name="ada_mod",
    )(cs, w_ada, b_ada.reshape(DEPTH, 1, 3 * d))


def _norm_rope(t, bd_ref, gain, cos, sin, out_scale):
    width = t.shape[-1]
    ssq = _dot((t * t).astype(BF16), bd_ref[...])
    tn = t * lax.rsqrt(ssq * (1.0 / HEAD_DIM) + EPS) * gain
    if cos is not None:
        reps = width // LANES
        cos_w = jnp.concatenate([cos] * reps, axis=1) if reps > 1 else cos
        sin_w = jnp.concatenate([sin] * reps, axis=1) if reps > 1 else sin
        lane = lax.broadcasted_iota(jnp.int32, tn.shape, 1)
        first = (lane % 32) < 16
        partner = jnp.where(first, pltpu.roll(tn, width - 16, 1), pltpu.roll(tn, 16, 1))
        tn = tn * cos_w + partner * sin_w
    if out_scale != 1.0:
        tn = tn * out_scale
    return tn


def _proj_kernel(x_ref, mod_ref, w_ref, bdq_ref, bdk_ref, gq_ref, gk_ref, cos_ref, sin_ref,
                 fa_ref, main_ref, qt_ref, k_ref, vt_ref, *, ctx_row, rope):
    d = D_MODEL
    nsub = x_ref.shape[1] // BLOCK
    xt = x_ref[0]
    ms = jnp.mean(xt * xt, axis=-1, keepdims=True)
    row = pl.program_id(0) if ctx_row is None else ctx_row
    modrow = mod_ref[0, pl.ds(row, 1), :]
    shift = modrow[:, 0:d]
    scale = modrow[:, d:2 * d]
    hb = ((xt * lax.rsqrt(ms + EPS)) * (1.0 + scale) + shift).astype(BF16)
    cos = cos_ref[...] if rope else None
    sin = sin_ref[...] if rope else None

    def proj(c0, width):
        return _dot(hb, w_ref[0, :, c0:c0 + width])

    q = _norm_rope(proj(W_Q, ATTN_WIDTH), bdq_ref, gq_ref[0], cos, sin, HEAD_DIM ** -0.5 * LOG2E)
    for s in range(nsub):
        qt_ref[0, s] = q[s * BLOCK:(s + 1) * BLOCK, :].T.astype(BF16)
    k = _norm_rope(proj(W_K, KV_WIDTH), bdk_ref, gk_ref[0], cos, sin, 1.0)
    k_ref[0] = k.astype(BF16)
    v = proj(W_V, KV_WIDTH)
    for s in range(nsub):
        vt_ref[0, s] = v[s * BLOCK:(s + 1) * BLOCK, :].T.astype(BF16)

    fa_ref[0] = proj(W_FA, FOURIER_WIDTH).astype(BF16)
    for src, dst, width in ((W_ZA, MAIN_ZA, W_Q - W_ZA), (W_ZC, MAIN_ZC, W_TOTAL - W_ZC)):
        for c0 in range(0, width, PROJ_CHUNK):
            main_ref[0, :, dst + c0:dst + c0 + PROJ_CHUNK] = proj(src + c0, PROJ_CHUNK).astype(BF16)


def _proj_call(x, mod, w_all, bdq, bdk, gq_all, gk_all, rope_tables, l, *, tile, ctx_row):
    b, length, d = x.shape
    nt = length // tile
    nsub = tile // BLOCK
    const = lambda bi, i: (0, 0)
    tok = lambda bi, i: (bi, i, 0)
    blk = lambda bi, i: (bi, i, 0, 0)
    rope = rope_tables is not None
    if rope:
        cos, sin = rope_tables
        pos, pos_rows = (lambda bi, i: (i, 0)), tile
    else:
        cos = sin = jnp.zeros((MOD_ROWS, LANES), F32)
        pos, pos_rows = const, MOD_ROWS
    kern = functools.partial(_proj_kernel, ctx_row=ctx_row, rope=rope)
    return pl.pallas_call(
        kern,
        grid=(b, nt),
        in_specs=[
            pl.BlockSpec((1, tile, d), tok),
            pl.BlockSpec((1, MOD_ROWS, 3 * d), _layer(l, 3)),
            pl.BlockSpec((1, d, W_TOTAL), _layer(l, 3), pipeline_mode=pl.Buffered(1)),
            pl.BlockSpec((ATTN_WIDTH, ATTN_WIDTH), const),
            pl.BlockSpec((KV_WIDTH, KV_WIDTH), const),
            pl.BlockSpec((1, 1, ATTN_WIDTH), _layer(l, 3)),
            pl.BlockSpec((1, 1, KV_WIDTH), _layer(l, 3)),
            pl.BlockSpec((pos_rows, LANES), pos),
            pl.BlockSpec((pos_rows, LANES), pos),
        ],
        out_specs=[
            pl.BlockSpec((1, tile, FOURIER_WIDTH), tok),
            pl.BlockSpec((1, tile, MAIN_WIDTH), tok),
            pl.BlockSpec((1, nsub, ATTN_WIDTH, BLOCK), blk),
            pl.BlockSpec((1, tile, KV_WIDTH), tok),
            pl.BlockSpec((1, nsub, KV_WIDTH, BLOCK), blk),
        ],
        out_shape=[
            jax.ShapeDtypeStruct((b, length, FOURIER_WIDTH), BF16),
            jax.ShapeDtypeStruct((b, length, MAIN_WIDTH), BF16),
            jax.ShapeDtypeStruct((b, length // BLOCK, ATTN_WIDTH, BLOCK), BF16),
            jax.ShapeDtypeStruct((b, length, KV_WIDTH), BF16),
            jax.ShapeDtypeStruct((b, length // BLOCK, KV_WIDTH, BLOCK), BF16),
        ],
        compiler_params=pltpu.CompilerParams(
            dimension_semantics=("arbitrary", "arbitrary"), vmem_limit_bytes=VMEM_LIMIT),
        name="proj_ctx" if ctx_row is not None else "proj_lat",
    )(x, mod, w_all, bdq, bdk, gq_all, gk_all, cos, sin)


def _dft1_kernel(x_ref, w1_ref, a_ref):
    l1n, pack, cw = DFT_L1, DFT_PACK, FOURIER_WIDTH
    for g in range(w1_ref.shape[0]):
        xg = x_ref[0, :, g].reshape(l1n * pack, cw)
        ag = _dot(w1_ref[g], xg).astype(BF16)
        a_ref[0, :, :, g] = ag.reshape(2, l1n, pack, cw)


def _dft2_kernel(a_ref, w2_ref, cs_ref, y_ref):
    cw, l2n = FOURIER_WIDTH, DFT_L2
    grp = a_ref.shape[2]
    rhs = jnp.concatenate(
        [jnp.concatenate([a_ref[0, 0, j], a_ref[0, 1, j]], axis=0) for j in range(grp)], axis=1)
    f = _dot(w2_ref[...], rhs)
    for j in range(grp):
        cols = slice(j * cw, (j + 1) * cw)
        fc = jnp.concatenate([f[:l2n, cols], f[l2n:, cols]], axis=1).astype(BF16)
        y_ref[0, :, cols] = _dot(fc, cs_ref[...])


def _fourier_call(fa, w1, w2, cs):
    b, length, cw = fa.shape
    l1n, l2n, pack = DFT_L1, DFT_L2, DFT_PACK
    params = pltpu.CompilerParams(
        dimension_semantics=("arbitrary", "arbitrary"), vmem_limit_bytes=VMEM_LIMIT)
    g1 = DFT1_GROUPS
    a = pl.pallas_call(
        _dft1_kernel,
        grid=(l2n // pack // g1, b),
        in_specs=[
            pl.BlockSpec((1, l1n, g1, pack, cw), lambda j, bi: (bi, 0, j, 0, 0)),
            pl.BlockSpec((g1, 2 * l1n * pack, l1n * pack), lambda j, bi: (j, 0, 0)),
        ],
        out_specs=pl.BlockSpec((1, 2, l1n, g1, pack, cw), lambda j, bi: (bi, 0, 0, j, 0, 0)),
        out_shape=jax.ShapeDtypeStruct((b, 2, l1n, l2n // pack, pack, cw), BF16),
        compiler_params=params,
        name="dft1_lat",
    )(fa.reshape(b, l1n, l2n // pack, pack, cw), w1)
    grp = DFT2_GROUP
    y = pl.pallas_call(
        _dft2_kernel,
        grid=(b, l1n // grp),
        in_specs=[
            pl.BlockSpec((1, 2, grp, l2n, cw), lambda bi, g: (bi, 0, g, 0, 0)),
            pl.BlockSpec(w2.shape, lambda bi, g: (0, 0)),
            pl.BlockSpec(cs.shape, lambda bi, g: (0, 0)),
        ],
        out_specs=pl.BlockSpec((1, l2n, grp * cw), lambda bi, g: (bi, 0, g)),
        out_shape=jax.ShapeDtypeStruct((b, l2n, l1n * cw), F32),
        compiler_params=params,
        name="dft2_lat",
    )(a.reshape(b, 2, l1n, l2n, cw), w2, cs)
    return y.reshape(b, length, cw)


def _fourier_ctx_kernel(fa_ref, wc_ref, cs_ref, y_ref):
    n = fa_ref.shape[1]
    f = _dot(wc_ref[...], fa_ref[0])
    fc = jnp.concatenate([f[:n], f[n:]], axis=1).astype(BF16)
    y_ref[0] = _dot(fc, cs_ref[...])


def _fourier_ctx_call(fa, wc, cs):
    b, length, width = fa.shape
    return pl.pallas_call(
        _fourier_ctx_kernel,
        grid=(b,),
        in_specs=[
            pl.BlockSpec((1, length, width), lambda bi: (bi, 0, 0)),
            pl.BlockSpec(wc.shape, lambda bi: (0, 0)),
            pl.BlockSpec(cs.shape, lambda bi: (0, 0)),
        ],
        out_specs=pl.BlockSpec((1, length, width), lambda bi: (bi, 0, 0)),
        out_shape=jax.ShapeDtypeStruct((b, length, width), F32),
        compiler_params=pltpu.CompilerParams(
            dimension_semantics=("arbitrary",), vmem_limit_bytes=VMEM_LIMIT),
        name="fourier_ctx",
    )(fa, wc, cs)


def _mix_kernel(*refs, local, tile, n_blocks, ctx_row, layer):
    if local:
        (x_ref, main_ref, y_ref, qt_ref, kp_ref, km_ref, kn_ref, vtp_ref, vtm_ref, vtn_ref,
         kc_ref, vtc_ref, mod_ref, sink_ref, wsg_ref, bsg_ref,
         wpa_ref, wpb_ref, wpc_ref, wout_ref, o_ref,
         kbuf, vtbuf, ya_s, yb_s, yc_s) = refs
    else:
        (x_ref, main_ref, y_ref, qt_ref, kc_ref, vtc_ref, mod_ref, sink_ref, wsg_ref, bsg_ref,
         wpa_ref, wpb_ref, wpc_ref, wout_ref, o_ref,
         ya_s, yb_s, yc_s) = refs
    d = D_MODEL
    nsub = tile // BLOCK
    tile_idx = pl.program_id(1)
    heads_per_kv = N_HEADS // N_KV_HEADS
    cols_all = N_HEADS * BLOCK

    if local:
        kbuf[0:BLOCK, :] = kp_ref[0]
        kbuf[BLOCK:BLOCK + tile, :] = km_ref[0]
        kbuf[BLOCK + tile:2 * BLOCK + tile, :] = kn_ref[0]
        vtbuf[0] = vtp_ref[0, 0]
        for s in range(nsub):
            vtbuf[1 + s] = vtm_ref[0, s]
        vtbuf[1 + nsub] = vtn_ref[0, 0]

    def with_ones(vt):
        return jnp.concatenate([vt, jnp.ones((ONES_ROWS, vt.shape[1]), BF16)], axis=0)

    vt_ctx = with_ones(jnp.concatenate([vtc_ref[0, s] for s in range(vtc_ref.shape[1])], axis=1))

    ya_s[...] = (y_ref[0] * _silu_of_half(main_ref[0, :, MAIN_ZA:MAIN_ZA + FOURIER_WIDTH].astype(F32))).astype(BF16)

    sgu_grp = lax.broadcasted_iota(jnp.int32, (CHUNK, SGU_WIDTH), 1) // (SGU_WIDTH // SGU_GROUPS)
    zero_head = jnp.zeros((HEAD_DIM, BLOCK), BF16)
    sink_row = jnp.concatenate(
        [jnp.full((1, BLOCK), sink_ref[layer, hh] * LOG2E, F32) for hh in range(N_HEADS)], axis=1)
    kq_diff = (lax.broadcasted_iota(jnp.int32, (BLOCK, cols_all), 0)
               - lax.broadcasted_iota(jnp.int32, (BLOCK, cols_all), 1) % BLOCK)

    def sub_block(n, carry):
        r0 = pl.multiple_of(n * BLOCK, BLOCK)
        rows = pl.ds(r0, BLOCK)

        vs = main_ref[0, rows, MAIN_VS:MAIN_VS + SGU_WIDTH].astype(F32)
        vn = (vs * lax.rsqrt(jnp.mean(vs * vs, axis=-1, keepdims=True) + EPS)).astype(BF16)
        rhs = jnp.concatenate(
            [jnp.where(sgu_grp == g, vn, jnp.zeros_like(vn)) for g in range(SGU_GROUPS)], axis=0)
        mixed = _dot(wsg_ref[0], rhs) + bsg_ref[0]
        u = main_ref[0, rows, MAIN_U:MAIN_U + SGU_WIDTH].astype(F32)
        zb = main_ref[0, rows, MAIN_ZB:MAIN_ZB + SGU_WIDTH].astype(F32)
        yb_s[rows, :] = (u * mixed * _silu_of_half(zb)).astype(BF16)

        qt = qt_ref[0, n]
        cols = []
        for hh in range(N_HEADS):
            qh = qt[hh * HEAD_DIM:(hh + 1) * HEAD_DIM, :]
            cols.append(jnp.concatenate(
                [qh, zero_head] if hh < heads_per_kv else [zero_head, qh], axis=0))
        qst = jnp.concatenate(cols, axis=1)

        s_ctx = _dot(kc_ref[0], qst)
        m = jnp.maximum(jnp.max(s_ctx, axis=0, keepdims=True), sink_row)
        if local:
            gblk = tile_idx * nsub + n
            s_loc = _dot(kbuf[pl.ds(r0, 3 * BLOCK), :], qst)
            s_prev = jnp.where(kq_diff >= jnp.where(gblk > 0, 0, BLOCK), s_loc[0:BLOCK], NEG_INF)
            s_own = s_loc[BLOCK:2 * BLOCK]
            s_next = jnp.where(kq_diff <= jnp.where(gblk < n_blocks - 1, 0, -BLOCK),
                               s_loc[2 * BLOCK:3 * BLOCK], NEG_INF)
            m_loc = jnp.max(jnp.maximum(jnp.maximum(s_prev, s_own), s_next), axis=0, keepdims=True)
            m = jnp.maximum(m, m_loc)
        e_ctx = jnp.exp2(s_ctx - m)
        ot = _dot(vt_ctx, e_ctx.astype(BF16))
        if local:
            e_loc = jnp.concatenate(
                [jnp.exp2(s_prev - m), jnp.exp2(s_own - m), jnp.exp2(s_next - m)], axis=0)
            vt_loc = with_ones(jnp.concatenate([vtbuf[n], vtbuf[n + 1], vtbuf[n + 2]], axis=1))
            ot = ot + _dot(vt_loc, e_loc.astype(BF16))
        den = ot[KV_WIDTH:KV_WIDTH + 1] + jnp.exp2(sink_row - m)
        ot = ot[0:KV_WIDTH] * (1.0 / den)
        for j in range(ATTN_WIDTH // LANES):
            h = (2 * j) // heads_per_kv
            pair = jnp.concatenate(
                [ot[h * HEAD_DIM:(h + 1) * HEAD_DIM, (2 * j) * BLOCK:(2 * j + 1) * BLOCK],
                 ot[h * HEAD_DIM:(h + 1) * HEAD_DIM, (2 * j + 1) * BLOCK:(2 * j + 2) * BLOCK]], axis=0)
            zc = main_ref[0, rows, MAIN_ZC + j * LANES:MAIN_ZC + (j + 1) * LANES].astype(F32)
            yc_s[rows, j * LANES:(j + 1) * LANES] = (pair.T * _silu_of_half(zc)).astype(BF16)
        return carry

    lax.fori_loop(0, nsub, sub_block, 0, unroll=True)

    t = jnp.tanh(main_ref[0, :, MAIN_G:MAIN_G + 3 * d].astype(F32))
    ma = _dot(ya_s[...], wpa_ref[0])
    mb = _dot(yb_s[...], wpb_ref[0])
    mc = _dot(yc_s[...], wpc_ref[0])
    merged2 = (ma + t[:, 0:d] * ma) + (mb + t[:, d:2 * d] * mb) + (mc + t[:, 2 * d:3 * d] * mc)
    out = _dot(merged2.astype(BF16), wout_ref[0])
    row = pl.program_id(0) if ctx_row is None else ctx_row
    gate = mod_ref[0, pl.ds(row, 1), 2 * d:3 * d]
    o_ref[0] = x_ref[0] + gate * out


def _mix_call(x, main, y, qt, k, vt, kc, vtc, mod, sinks, wsg, bsg, wpa, wpb, wpc, wout, l,
              *, tile, local, ctx_row):
    b, length, d = x.shape
    nt = length // tile
    n_ctx = kc.shape[1]
    sub = tile // BLOCK
    n_blocks = length // BLOCK
    tok = lambda bi, i: (bi, i, 0)
    blk = lambda bi, i: (bi, i, 0, 0)
    per_b = lambda bi, i: (bi, 0, 0)
    per_b4 = lambda bi, i: (bi, 0, 0, 0)

    in_specs = [
        pl.BlockSpec((1, tile, d), tok),
        pl.BlockSpec((1, tile, MAIN_WIDTH), tok),
        pl.BlockSpec((1, tile, FOURIER_WIDTH), tok),
        pl.BlockSpec((1, sub, ATTN_WIDTH, BLOCK), blk),
    ]
    args = [x, main, y, qt]
    if local:
        prev_blk = lambda bi, i: jnp.maximum(i * sub - 1, 0)
        next_blk = lambda bi, i: jnp.minimum((i + 1) * sub, n_blocks - 1)
        in_specs += [
            pl.BlockSpec((1, BLOCK, KV_WIDTH), lambda bi, i: (bi, prev_blk(bi, i), 0)),
            pl.BlockSpec((1, tile, KV_WIDTH), tok),
            pl.BlockSpec((1, BLOCK, KV_WIDTH), lambda bi, i: (bi, next_blk(bi, i), 0)),
            pl.BlockSpec((1, 1, KV_WIDTH, BLOCK), lambda bi, i: (bi, prev_blk(bi, i), 0, 0)),
            pl.BlockSpec((1, sub, KV_WIDTH, BLOCK), blk),
            pl.BlockSpec((1, 1, KV_WIDTH, BLOCK), lambda bi, i: (bi, next_blk(bi, i), 0, 0)),
        ]
        args += [k, k, k, vt, vt, vt]
    in_specs += [
        pl.BlockSpec((1, n_ctx, KV_WIDTH), per_b),
        pl.BlockSpec((1, n_ctx // BLOCK, KV_WIDTH, BLOCK), per_b4),
        pl.BlockSpec((1, MOD_ROWS, 3 * d), _layer(l, 3)),
        pl.BlockSpec(memory_space=pltpu.SMEM),
    ]
    args += [kc, vtc, mod, sinks]
    for w in (wsg, bsg, wpa, wpb, wpc, wout):
        in_specs.append(pl.BlockSpec((1,) + w.shape[1:], _layer(l, 3)))
        args.append(w)

    scratch = []
    if local:
        scratch += [pltpu.VMEM((tile + 2 * BLOCK, KV_WIDTH), BF16),
                    pltpu.VMEM((sub + 2, KV_WIDTH, BLOCK), BF16)]
    scratch += [
        pltpu.VMEM((tile, FOURIER_WIDTH), BF16),
        pltpu.VMEM((tile, SGU_WIDTH), BF16),
        pltpu.VMEM((tile, ATTN_WIDTH), BF16),
    ]
    kern = functools.partial(_mix_kernel, local=local, tile=tile, n_blocks=n_blocks,
                             ctx_row=ctx_row, layer=l)
    return pl.pallas_call(
        kern,
        grid=(b, nt),
        in_specs=in_specs,
        out_specs=pl.BlockSpec((1, tile, d), tok),
        out_shape=jax.ShapeDtypeStruct((b, length, d), F32),
        scratch_shapes=scratch,
        compiler_params=pltpu.CompilerParams(
            dimension_semantics=("arbitrary", "arbitrary"), vmem_limit_bytes=VMEM_LIMIT),
        name="mix_lat" if local else "mix_ctx",
    )(*args)


def _rope_tables(length):
    pos = jnp.arange(length)
    nf = HEAD_DIM // 4
    inv = ROPE_THETA ** (-jnp.arange(nf, dtype=F32) / nf)
    ang_r = (pos // GRID_W).astype(F32)[:, None] * inv[None, :]
    ang_c = (pos % GRID_W).astype(F32)[:, None] * inv[None, :]
    ang = jnp.concatenate([ang_r, ang_r, ang_c, ang_c] * 2, axis=1)
    sign = jnp.concatenate([-jnp.ones((nf,), F32), jnp.ones((nf,), F32)] * 4)
    return jnp.cos(ang), jnp.sin(ang) * sign[None, :]


def _angle(num, den):
    return (num % den).astype(F32) * (2.0 * math.pi / den)


def _channel_dft(width):
    gw = FOURIER_WIDTH // FOURIER_GROUPS
    m = jnp.arange(width)
    same = (m[:, None] // gw) == (m[None, :] // gw)
    th = _angle((m[:, None] % gw) * (m[None, :] % gw), gw)
    cc = jnp.where(same, jnp.cos(th), 0.0) * gw ** -0.5
    sc = jnp.where(same, jnp.sin(th), 0.0) * gw ** -0.5
    return jnp.concatenate([cc, sc], axis=0).astype(BF16)


def _fourier_tables():
    l1n, l2n = DFT_L1, DFT_L2
    length = l1n * l2n
    pack = DFT_PACK
    l2 = jnp.arange(l2n)[:, None, None]
    k1 = jnp.arange(l1n)[None, :, None]
    l1 = jnp.arange(l1n)[None, None, :]
    th = _angle(k1 * l1 * l2n + k1 * l2, length)
    w1 = jnp.stack([jnp.cos(th), -jnp.sin(th)], axis=1) * l1n ** -0.5
    w1 = jnp.transpose(w1.reshape(l2n // pack, pack, 2, l1n, l1n), (0, 2, 3, 1, 4))
    w1 = w1[..., None] * jnp.eye(pack, dtype=F32)[None, None, None, :, None, :]
    w1 = w1.reshape(l2n // pack, 2 * l1n * pack, l1n * pack).astype(BF16)
    k2 = jnp.arange(l2n)
    th2 = _angle(k2[:, None] * k2[None, :], l2n)
    c2, s2 = jnp.cos(th2), jnp.sin(th2)
    w2 = (jnp.concatenate([jnp.concatenate([c2, s2], axis=1),
                           jnp.concatenate([-s2, c2], axis=1)], axis=0) * l2n ** -0.5).astype(BF16)
    return w1, w2


def _fourier_ctx_table(n):
    k = jnp.arange(n)
    th = _angle(k[:, None] * k[None, :], n)
    return (jnp.concatenate([jnp.cos(th), -jnp.sin(th)], axis=0) * n ** -0.5).astype(BF16)


def _block_diag_ones(width):
    m = jnp.arange(width) // HEAD_DIM
    return (m[:, None] == m[None, :]).astype(BF16)


def _prep_w_in(w_in):
    col = jnp.arange(W_TOTAL)
    halved = ((col >= W_ZA) & (col < W_ZA + FOURIER_WIDTH)) | ((col >= W_ZB) & (col < W_Q)) | (col >= W_ZC)
    return (w_in * jnp.where(halved, 0.5, 1.0)).astype(BF16)


def kernel(x, c, ctx, c_ctx, w_ada, b_ada, w_in, sgu_w, sgu_b, q_norm_g, k_norm_g,
           attn_sink, w_pa, w_pb, w_pc, w_out):
    b, length, d = x.shape
    n_ctx = ctx.shape[1]
    assert length == DFT_L1 * DFT_L2 and d == D_MODEL and b + 1 <= MOD_ROWS
    assert w_in.shape == (DEPTH, d, W_TOTAL)
    lat_tile = 512
    ctx_row = b

    cs = jnp.zeros((MOD_ROWS, d), F32).at[:b].set(c).at[b].set(c_ctx)
    mod = _ada_call(cs, w_ada, b_ada)

    rope_tables = _rope_tables(length)
    w1, w2 = _fourier_tables()
    cs_dft = _channel_dft(FOURIER_WIDTH)
    wc = _fourier_ctx_table(n_ctx)
    bdq = _block_diag_ones(ATTN_WIDTH)
    bdk = _block_diag_ones(KV_WIDTH)

    w_all = _prep_w_in(w_in)
    gq_all = jnp.tile(q_norm_g, (1, N_HEADS))[:, None, :]
    gk_all = jnp.tile(k_norm_g, (1, N_KV_HEADS))[:, None, :]
    wsg = jnp.transpose(sgu_w, (0, 2, 1, 3)).reshape(DEPTH, CHUNK, SGU_GROUPS * CHUNK).astype(BF16)
    bsg = jnp.repeat(jnp.swapaxes(sgu_b, 1, 2), SGU_WIDTH // SGU_GROUPS, axis=2)
    wpa, wpb, wpc = w_pa.astype(BF16), w_pb.astype(BF16), w_pc.astype(BF16)
    wout = (0.5 * w_out).astype(BF16)

    xc = ctx
    for l in range(DEPTH):
        last = l == DEPTH - 1
        fa_c, main_c, qt_c, kc, vtc = _proj_call(
            xc.reshape(1, b * n_ctx, d), mod, w_all, bdq, bdk, gq_all, gk_all, None, l,
            tile=2 * n_ctx, ctx_row=ctx_row)
        kc = kc.reshape(b, n_ctx, KV_WIDTH)
        vtc = vtc.reshape(b, n_ctx // BLOCK, KV_WIDTH, BLOCK)
        fa, main, qt, k, vt = _proj_call(
            x, mod, w_all, bdq, bdk, gq_all, gk_all, rope_tables, l, tile=lat_tile, ctx_row=None)
        y = _fourier_call(fa, w1, w2, cs_dft)
        x = _mix_call(x, main, y, qt, k, vt, kc, vtc, mod, attn_sink, wsg, bsg, wpa, wpb, wpc, wout,
                      l, tile=lat_tile, local=True, ctx_row=None)
        if not last:
            y_c = _fourier_ctx_call(fa_c.reshape(b, n_ctx, FOURIER_WIDTH), wc, cs_dft)
            xc = _mix_call(xc, main_c.reshape(b, n_ctx, MAIN_WIDTH), y_c,
                           qt_c.reshape(b, n_ctx // BLOCK, ATTN_WIDTH, BLOCK), None, None, kc, vtc,
                           mod, attn_sink, wsg, bsg, wpa, wpb, wpc, wout, l,
                           tile=n_ctx, local=False, ctx_row=ctx_row)
    return x
```

```python
import functools
import math

import jax
import jax.numpy as jnp
from jax import lax
from jax.experimental import pallas as pl
from jax.experimental.pallas import tpu as pltpu

F32 = jnp.float32
BF16 = jnp.bfloat16

D_MODEL = 1024
DEPTH = 4
GRID_W = 64
FOURIER_WIDTH = 256
FOURIER_GROUPS = 4
SGU_WIDTH = 256
SGU_GROUPS = 4
CHUNK = 128
N_HEADS = 8
N_KV_HEADS = 2
HEAD_DIM = 64
ATTN_WIDTH = N_HEADS * HEAD_DIM
KV_WIDTH = N_KV_HEADS * HEAD_DIM
BLOCK = 128
ROPE_THETA = 10000.0
EPS = 1e-6
NEG_INF = -1e30
LOG2E = math.log2(math.e)

LANES = 128
MOD_ROWS = 8
ONES_ROWS = 16

W_FA = 0
W_ZA = W_FA + FOURIER_WIDTH
W_ZB = W_ZA + FOURIER_WIDTH + 2 * SGU_WIDTH
W_Q = W_ZB + SGU_WIDTH
W_K = W_Q + ATTN_WIDTH
W_V = W_K + KV_WIDTH
W_ZC = W_V + KV_WIDTH
W_G = W_ZC + ATTN_WIDTH
W_TOTAL = W_G + 3 * D_MODEL
MAIN_ZA = 0
MAIN_U = MAIN_ZA + FOURIER_WIDTH
MAIN_VS = MAIN_U + SGU_WIDTH
MAIN_ZB = MAIN_VS + SGU_WIDTH
MAIN_ZC = MAIN_ZB + SGU_WIDTH
MAIN_G = MAIN_ZC + ATTN_WIDTH
MAIN_WIDTH = MAIN_G + 3 * D_MODEL
PROJ_CHUNK = 512

DFT_L1 = 32
DFT_L2 = 256
DFT_PACK = 16
DFT1_GROUPS = 4
DFT2_GROUP = 8

VMEM_LIMIT = 56 * 1024 * 1024


def _silu(z):
    return 0.5 * z * (1.0 + jnp.tanh(0.5 * z))


def _silu_of_half(zh):
    return zh * (1.0 + jnp.tanh(zh))


def _dot(a, b):
    return jnp.dot(a, b, preferred_element_type=F32)


def _layer(l, rank):
    return lambda *_: (l,) + (0,) * (rank - 1)


def _ada_kernel(c_ref, w_ref, b_ref, o_ref):
    s = _silu(c_ref[...]).astype(BF16)
    o_ref[0] = _dot(s, w_ref[0].astype(BF16)) + b_ref[0]


def _ada_call(cs, w_ada, b_ada):
    d = D_MODEL
    return pl.pallas_call(
        _ada_kernel,
        grid=(DEPTH, 3),
        in_specs=[
            pl.BlockSpec((MOD_ROWS, d), lambda l, j: (0, 0)),
            pl.BlockSpec((1, d, d), lambda l, j: (l, 0, j)),
            pl.BlockSpec((1, 1, d), lambda l, j: (l, 0, j)),
        ],
        out_specs=pl.BlockSpec((1, MOD_ROWS, d), lambda l, j: (l, 0, j)),
        out_shape=jax.ShapeDtypeStruct((DEPTH, MOD_ROWS, 3 * d), F32),
        compiler_params=pltpu.CompilerParams(
            dimension_semantics=("arbitrary", "arbitrary"), vmem_limit_bytes=VMEM_LIMIT),
        name="ada_mod",
    )(cs, w_ada, b_ada.reshape(DEPTH, 1, 3 * d))


def _norm_rope(t, bd_ref, gain, cos, sin, out_scale):
    width = t.shape[-1]
    ssq = _dot((t * t).astype(BF16), bd_ref[...])
    tn = t * lax.rsqrt(ssq * (1.0 / HEAD_DIM) + EPS) * gain
    if cos is not None:
        reps = width // LANES
        cos_w = jnp.concatenate([cos] * reps, axis=1) if reps > 1 else cos
        sin_w = jnp.concatenate([sin] * reps, axis=1) if reps > 1 else sin
        lane = lax.broadcasted_iota(jnp.int32, tn.shape, 1)
        first = (lane % 32) < 16
        partner = jnp.where(first, pltpu.roll(tn, width - 16, 1), pltpu.roll(tn, 16, 1))
        tn = tn * cos_w + partner * sin_w
    if out_scale != 1.0:
        tn = tn * out_scale
    return tn


def _proj_kernel(x_ref, mod_ref, w_ref, bdq_ref, bdk_ref, gq_ref, gk_ref, cos_ref, sin_ref,
                 fa_ref, main_ref, qt_ref, k_ref, vt_ref, *, ctx_row, rope):
    d = D_MODEL
    nsub = x_ref.shape[1] // BLOCK
    xt = x_ref[0]
    ms = jnp.mean(xt * xt, axis=-1, keepdims=True)
    row = pl.program_id(0) if ctx_row is None else ctx_row
    modrow = mod_ref[0, pl.ds(row, 1), :]
    shift = modrow[:, 0:d]
    scale = modrow[:, d:2 * d]
    hb = ((xt * lax.rsqrt(ms + EPS)) * (1.0 + scale) + shift).astype(BF16)
    cos = cos_ref[...] if rope else None
    sin = sin_ref[...] if rope else None

    def proj(c0, width):
        return _dot(hb, w_ref[0, :, c0:c0 + width])

    q = _norm_rope(proj(W_Q, ATTN_WIDTH), bdq_ref, gq_ref[0], cos, sin, HEAD_DIM ** -0.5 * LOG2E)
    for s in range(nsub):
        qt_ref[0, s] = q[s * BLOCK:(s + 1) * BLOCK, :].T.astype(BF16)
    k = _norm_rope(proj(W_K, KV_WIDTH), bdk_ref, gk_ref[0], cos, sin, 1.0)
    k_ref[0] = k.astype(BF16)
    v = proj(W_V, KV_WIDTH)
    for s in range(nsub):
        vt_ref[0, s] = v[s * BLOCK:(s + 1) * BLOCK, :].T.astype(BF16)

    fa_ref[0] = proj(W_FA, FOURIER_WIDTH).astype(BF16)
    for src, dst, width in ((W_ZA, MAIN_ZA, W_Q - W_ZA), (W_ZC, MAIN_ZC, W_TOTAL - W_ZC)):
        for c0 in range(0, width, PROJ_CHUNK):
            main_ref[0, :, dst + c0:dst + c0 + PROJ_CHUNK] = proj(src + c0, PROJ_CHUNK).astype(BF16)


def _proj_call(x, mod, w_all, bdq, bdk, gq_all, gk_all, rope_tables, l, *, tile, ctx_row):
    b, length, d = x.shape
    nt = length // tile
    nsub = tile // BLOCK
    const = lambda bi, i: (0, 0)
    tok = lambda bi, i: (bi, i, 0)
    blk = lambda bi, i: (bi, i, 0, 0)
    rope = rope_tables is not None
    if rope:
        cos, sin = rope_tables
        pos, pos_rows = (lambda bi, i: (i, 0)), tile
    else:
        cos = sin = jnp.zeros((MOD_ROWS, LANES), F32)
        pos, pos_rows = const, MOD_ROWS
    kern = functools.partial(_proj_kernel, ctx_row=ctx_row, rope=rope)
    return pl.pallas_call(
        kern,
        grid=(b, nt),
        in_specs=[
            pl.BlockSpec((1, tile, d), tok),
            pl.BlockSpec((1, MOD_ROWS, 3 * d), _layer(l, 3)),
            pl.BlockSpec((1, d, W_TOTAL), _layer(l, 3), pipeline_mode=pl.Buffered(1)),
            pl.BlockSpec((ATTN_WIDTH, ATTN_WIDTH), const),
            pl.BlockSpec((KV_WIDTH, KV_WIDTH), const),
            pl.BlockSpec((1, 1, ATTN_WIDTH), _layer(l, 3)),
            pl.BlockSpec((1, 1, KV_WIDTH), _layer(l, 3)),
            pl.BlockSpec((pos_rows, LANES), pos),
            pl.BlockSpec((pos_rows, LANES), pos),
        ],
        out_specs=[
            pl.BlockSpec((1, tile, FOURIER_WIDTH), tok),
            pl.BlockSpec((1, tile, MAIN_WIDTH), tok),
            pl.BlockSpec((1, nsub, ATTN_WIDTH, BLOCK), blk),
            pl.BlockSpec((1, tile, KV_WIDTH), tok),
            pl.BlockSpec((1, nsub, KV_WIDTH, BLOCK), blk),
        ],
        out_shape=[
            jax.ShapeDtypeStruct((b, length, FOURIER_WIDTH), BF16),
            jax.ShapeDtypeStruct((b, length, MAIN_WIDTH), BF16),
            jax.ShapeDtypeStruct((b, length // BLOCK, ATTN_WIDTH, BLOCK), BF16),
            jax.ShapeDtypeStruct((b, length, KV_WIDTH), BF16),
            jax.ShapeDtypeStruct((b, length // BLOCK, KV_WIDTH, BLOCK), BF16),
        ],
        compiler_params=pltpu.CompilerParams(
            dimension_semantics=("arbitrary", "arbitrary"), vmem_limit_bytes=VMEM_LIMIT),
        name="proj_ctx" if ctx_row is not None else "proj_lat",
    )(x, mod, w_all, bdq, bdk, gq_all, gk_all, cos, sin)


def _dft1_kernel(x_ref, w1_ref, a_ref):
    l1n, pack, cw = DFT_L1, DFT_PACK, FOURIER_WIDTH
    for g in range(w1_ref.shape[0]):
        xg = x_ref[0, :, g].reshape(l1n * pack, cw)
        ag = _dot(w1_ref[g], xg).astype(BF16)
        a_ref[0, :, :, g] = ag.reshape(2, l1n, pack, cw)


def _dft2_kernel(a_ref, w2_ref, cs_ref, y_ref):
    cw, l2n = FOURIER_WIDTH, DFT_L2
    grp = a_ref.shape[2]
    rhs = jnp.concatenate(
        [jnp.concatenate([a_ref[0, 0, j], a_ref[0, 1, j]], axis=0) for j in range(grp)], axis=1)
    f = _dot(w2_ref[...], rhs)
    for j in range(grp):
        cols = slice(j * cw, (j + 1) * cw)
        fc = jnp.concatenate([f[:l2n, cols], f[l2n:, cols]], axis=1).astype(BF16)
        y_ref[0, :, cols] = _dot(fc, cs_ref[...]).astype(BF16)


def _fourier_call(fa, w1, w2, cs):
    b, length, cw = fa.shape
    l1n, l2n, pack = DFT_L1, DFT_L2, DFT_PACK
    params = pltpu.CompilerParams(
        dimension_semantics=("arbitrary", "arbitrary"), vmem_limit_bytes=VMEM_LIMIT)
    g1 = DFT1_GROUPS
    a = pl.pallas_call(
        _dft1_kernel,
        grid=(l2n // pack // g1, b),
        in_specs=[
            pl.BlockSpec((1, l1n, g1, pack, cw), lambda j, bi: (bi, 0, j, 0, 0)),
            pl.BlockSpec((g1, 2 * l1n * pack, l1n * pack), lambda j, bi: (j, 0, 0)),
        ],
        out_specs=pl.BlockSpec((1, 2, l1n, g1, pack, cw), lambda j, bi: (bi, 0, 0, j, 0, 0)),
        out_shape=jax.ShapeDtypeStruct((b, 2, l1n, l2n // pack, pack, cw), BF16),
        compiler_params=params,
        name="dft1_lat",
    )(fa.reshape(b, l1n, l2n // pack, pack, cw), w1)
    grp = DFT2_GROUP
    y = pl.pallas_call(
        _dft2_kernel,
        grid=(b, l1n // grp),
        in_specs=[
            pl.BlockSpec((1, 2, grp, l2n, cw), lambda bi, g: (bi, 0, g, 0, 0)),
            pl.BlockSpec(w2.shape, lambda bi, g: (0, 0)),
            pl.BlockSpec(cs.shape, lambda bi, g: (0, 0)),
        ],
        out_specs=pl.BlockSpec((1, l2n, grp * cw), lambda bi, g: (bi, 0, g)),
        out_shape=jax.ShapeDtypeStruct((b, l2n, l1n * cw), BF16),
        compiler_params=params,
        name="dft2_lat",
    )(a.reshape(b, 2, l1n, l2n, cw), w2, cs)
    return y.reshape(b, length, cw)


def _fourier_ctx_kernel(fa_ref, wc_ref, cs_ref, y_ref):
    n = fa_ref.shape[1]
    f = _dot(wc_ref[...], fa_ref[0])
    fc = jnp.concatenate([f[:n], f[n:]], axis=1).astype(BF16)
    y_ref[0] = _dot(fc, cs_ref[...])


def _fourier_ctx_call(fa, wc, cs):
    b, length, width = fa.shape
    return pl.pallas_call(
        _fourier_ctx_kernel,
        grid=(b,),
        in_specs=[
            pl.BlockSpec((1, length, width), lambda bi: (bi, 0, 0)),
            pl.BlockSpec(wc.shape, lambda bi: (0, 0)),
            pl.BlockSpec(cs.shape, lambda bi: (0, 0)),
        ],
        out_specs=pl.BlockSpec((1, length, width), lambda bi: (bi, 0, 0)),
        out_shape=jax.ShapeDtypeStruct((b, length, width), F32),
        compiler_params=pltpu.CompilerParams(
            dimension_semantics=("arbitrary",), vmem_limit_bytes=VMEM_LIMIT),
        name="fourier_ctx",
    )(fa, wc, cs)


def _mix_kernel(*refs, local, tile, n_blocks, ctx_row, layer):
    if local:
        (x_ref, main_ref, y_ref, qt_ref, kp_ref, km_ref, kn_ref, vtp_ref, vtm_ref, vtn_ref,
         kc_ref, vtc_ref, mod_ref, sink_ref, wsg_ref, bsg_ref,
         wpa_ref, wpb_ref, wpc_ref, wout_ref, o_ref,
         kbuf, vtbuf, ya_s, yb_s, yc_s) = refs
    else:
        (x_ref, main_ref, y_ref, qt_ref, kc_ref, vtc_ref, mod_ref, sink_ref, wsg_ref, bsg_ref,
         wpa_ref, wpb_ref, wpc_ref, wout_ref, o_ref,
         ya_s, yb_s, yc_s) = refs
    d = D_MODEL
    nsub = tile // BLOCK
    tile_idx = pl.program_id(1)
    heads_per_kv = N_HEADS // N_KV_HEADS
    cols_all = N_HEADS * BLOCK

    if local:
        kbuf[0:BLOCK, :] = kp_ref[0]
        kbuf[BLOCK:BLOCK + tile, :] = km_ref[0]
        kbuf[BLOCK + tile:2 * BLOCK + tile, :] = kn_ref[0]
        vtbuf[0] = vtp_ref[0, 0]
        for s in range(nsub):
            vtbuf[1 + s] = vtm_ref[0, s]
        vtbuf[1 + nsub] = vtn_ref[0, 0]

    def with_ones(vt):
        return jnp.concatenate([vt, jnp.ones((ONES_ROWS, vt.shape[1]), BF16)], axis=0)

    vt_ctx = with_ones(jnp.concatenate([vtc_ref[0, s] for s in range(vtc_ref.shape[1])], axis=1))

    ya_s[...] = (y_ref[0].astype(F32) * _silu_of_half(main_ref[0, :, MAIN_ZA:MAIN_ZA + FOURIER_WIDTH].astype(F32))).astype(BF16)

    sgu_grp = lax.broadcasted_iota(jnp.int32, (CHUNK, SGU_WIDTH), 1) // (SGU_WIDTH // SGU_GROUPS)
    zero_head = jnp.zeros((HEAD_DIM, BLOCK), BF16)
    sink_row = jnp.concatenate(
        [jnp.full((1, BLOCK), sink_ref[layer, hh] * LOG2E, F32) for hh in range(N_HEADS)], axis=1)
    kq_diff = (lax.broadcasted_iota(jnp.int32, (BLOCK, cols_all), 0)
               - lax.broadcasted_iota(jnp.int32, (BLOCK, cols_all), 1) % BLOCK)

    def sub_block(n, carry):
        r0 = pl.multiple_of(n * BLOCK, BLOCK)
        rows = pl.ds(r0, BLOCK)

        vs = main_ref[0, rows, MAIN_VS:MAIN_VS + SGU_WIDTH].astype(F32)
        vn = (vs * lax.rsqrt(jnp.mean(vs * vs, axis=-1, keepdims=True) + EPS)).astype(BF16)
        rhs = jnp.concatenate(
            [jnp.where(sgu_grp == g, vn, jnp.zeros_like(vn)) for g in range(SGU_GROUPS)], axis=0)
        mixed = _dot(wsg_ref[0], rhs) + bsg_ref[0]
        u = main_ref[0, rows, MAIN_U:MAIN_U + SGU_WIDTH].astype(F32)
        zb = main_ref[0, rows, MAIN_ZB:MAIN_ZB + SGU_WIDTH].astype(F32)
        yb_s[rows, :] = (u * mixed * _silu_of_half(zb)).astype(BF16)

        qt = qt_ref[0, n]
        cols = []
        for hh in range(N_HEADS):
            qh = qt[hh * HEAD_DIM:(hh + 1) * HEAD_DIM, :]
            cols.append(jnp.concatenate(
                [qh, zero_head] if hh < heads_per_kv else [zero_head, qh], axis=0))
        qst = jnp.concatenate(cols, axis=1)

        s_ctx = _dot(kc_ref[0], qst)
        m = jnp.maximum(jnp.max(s_ctx, axis=0, keepdims=True), sink_row)
        if local:
            gblk = tile_idx * nsub + n
            s_loc = _dot(kbuf[pl.ds(r0, 3 * BLOCK), :], qst)
            s_prev = jnp.where(kq_diff >= jnp.where(gblk > 0, 0, BLOCK), s_loc[0:BLOCK], NEG_INF)
            s_own = s_loc[BLOCK:2 * BLOCK]
            s_next = jnp.where(kq_diff <= jnp.where(gblk < n_blocks - 1, 0, -BLOCK),
                               s_loc[2 * BLOCK:3 * BLOCK], NEG_INF)
            m_loc = jnp.max(jnp.maximum(jnp.maximum(s_prev, s_own), s_next), axis=0, keepdims=True)
            m = jnp.maximum(m, m_loc)
        e_ctx = jnp.exp2(s_ctx - m)
        ot = _dot(vt_ctx, e_ctx.astype(BF16))
        if local:
            e_loc = jnp.concatenate(
                [jnp.exp2(s_prev - m), jnp.exp2(s_own - m), jnp.exp2(s_next - m)], axis=0)
            vt_loc = with_ones(jnp.concatenate([vtbuf[n], vtbuf[n + 1], vtbuf[n + 2]], axis=1))
            ot = ot + _dot(vt_loc, e_loc.astype(BF16))
        den = ot[KV_WIDTH:KV_WIDTH + 1] + jnp.exp2(sink_row - m)
        ot = ot[0:KV_WIDTH] * (1.0 / den)
        for j in range(ATTN_WIDTH // LANES):
            h = (2 * j) // heads_per_kv
            pair = jnp.concatenate(
                [ot[h * HEAD_DIM:(h + 1) * HEAD_DIM, (2 * j) * BLOCK:(2 * j + 1) * BLOCK],
                 ot[h * HEAD_DIM:(h + 1) * HEAD_DIM, (2 * j + 1) * BLOCK:(2 * j + 2) * BLOCK]], axis=0)
            zc = main_ref[0, rows, MAIN_ZC + j * LANES:MAIN_ZC + (j + 1) * LANES].astype(F32)
            yc_s[rows, j * LANES:(j + 1) * LANES] = (pair.T * _silu_of_half(zc)).astype(BF16)
        return carry

    lax.fori_loop(0, nsub, sub_block, 0, unroll=True)

    t = jnp.tanh(main_ref[0, :, MAIN_G:MAIN_G + 3 * d].astype(F32))
    ma = _dot(ya_s[...], wpa_ref[0])
    mb = _dot(yb_s[...], wpb_ref[0])
    mc = _dot(yc_s[...], wpc_ref[0])
    merged2 = (ma + t[:, 0:d] * ma) + (mb + t[:, d:2 * d] * mb) + (mc + t[:, 2 * d:3 * d] * mc)
    out = _dot(merged2.astype(BF16), wout_ref[0])
    row = pl.program_id(0) if ctx_row is None else ctx_row
    gate = mod_ref[0, pl.ds(row, 1), 2 * d:3 * d]
    o_ref[0] = x_ref[0] + gate * out


def _mix_call(x, main, y, qt, k, vt, kc, vtc, mod, sinks, wsg, bsg, wpa, wpb, wpc, wout, l,
              *, tile, local, ctx_row):
    b, length, d = x.shape
    nt = length // tile
    n_ctx = kc.shape[1]
    sub = tile // BLOCK
    n_blocks = length // BLOCK
    tok = lambda bi, i: (bi, i, 0)
    blk = lambda bi, i: (bi, i, 0, 0)
    per_b = lambda bi, i: (bi, 0, 0)
    per_b4 = lambda bi, i: (bi, 0, 0, 0)

    in_specs = [
        pl.BlockSpec((1, tile, d), tok),
        pl.BlockSpec((1, tile, MAIN_WIDTH), tok),
        pl.BlockSpec((1, tile, FOURIER_WIDTH), tok),
        pl.BlockSpec((1, sub, ATTN_WIDTH, BLOCK), blk),
    ]
    args = [x, main, y, qt]
    if local:
        prev_blk = lambda bi, i: jnp.maximum(i * sub - 1, 0)
        next_blk = lambda bi, i: jnp.minimum((i + 1) * sub, n_blocks - 1)
        in_specs += [
            pl.BlockSpec((1, BLOCK, KV_WIDTH), lambda bi, i: (bi, prev_blk(bi, i), 0)),
            pl.BlockSpec((1, tile, KV_WIDTH), tok),
            pl.BlockSpec((1, BLOCK, KV_WIDTH), lambda bi, i: (bi, next_blk(bi, i), 0)),
            pl.BlockSpec((1, 1, KV_WIDTH, BLOCK), lambda bi, i: (bi, prev_blk(bi, i), 0, 0)),
            pl.BlockSpec((1, sub, KV_WIDTH, BLOCK), blk),
            pl.BlockSpec((1, 1, KV_WIDTH, BLOCK), lambda bi, i: (bi, next_blk(bi, i), 0, 0)),
        ]
        args += [k, k, k, vt, vt, vt]
    in_specs += [
        pl.BlockSpec((1, n_ctx, KV_WIDTH), per_b),
        pl.BlockSpec((1, n_ctx // BLOCK, KV_WIDTH, BLOCK), per_b4),
        pl.BlockSpec((1, MOD_ROWS, 3 * d), _layer(l, 3)),
        pl.BlockSpec(memory_space=pltpu.SMEM),
    ]
    args += [kc, vtc, mod, sinks]
    for w in (wsg, bsg, wpa, wpb, wpc, wout):
        in_specs.append(pl.BlockSpec((1,) + w.shape[1:], _layer(l, 3)))
        args.append(w)

    scratch = []
    if local:
        scratch += [pltpu.VMEM((tile + 2 * BLOCK, KV_WIDTH), BF16),
                    pltpu.VMEM((sub + 2, KV_WIDTH, BLOCK), BF16)]
    scratch += [
        pltpu.VMEM((tile, FOURIER_WIDTH), BF16),
        pltpu.VMEM((tile, SGU_WIDTH), BF16),
        pltpu.VMEM((tile, ATTN_WIDTH), BF16),
    ]
    kern = functools.partial(_mix_kernel, local=local, tile=tile, n_blocks=n_blocks,
                             ctx_row=ctx_row, layer=l)
    return pl.pallas_call(
        kern,
        grid=(b, nt),
        in_specs=in_specs,
        out_specs=pl.BlockSpec((1, tile, d), tok),
        out_shape=jax.ShapeDtypeStruct((b, length, d), F32),
        scratch_shapes=scratch,
        compiler_params=pltpu.CompilerParams(
            dimension_semantics=("arbitrary", "arbitrary"), vmem_limit_bytes=VMEM_LIMIT),
        name="mix_lat" if local else "mix_ctx",
    )(*args)


def _rope_tables(length):
    pos = jnp.arange(length)
    nf = HEAD_DIM // 4
    inv = ROPE_THETA ** (-jnp.arange(nf, dtype=F32) / nf)
    ang_r = (pos // GRID_W).astype(F32)[:, None] * inv[None, :]
    ang_c = (pos % GRID_W).astype(F32)[:, None] * inv[None, :]
    ang = jnp.concatenate([ang_r, ang_r, ang_c, ang_c] * 2, axis=1)
    sign = jnp.concatenate([-jnp.ones((nf,), F32), jnp.ones((nf,), F32)] * 4)
    return jnp.cos(ang), jnp.sin(ang) * sign[None, :]


def _angle(num, den):
    return (num % den).astype(F32) * (2.0 * math.pi / den)


def _channel_dft(width):
    gw = FOURIER_WIDTH // FOURIER_GROUPS
    m = jnp.arange(width)
    same = (m[:, None] // gw) == (m[None, :] // gw)
    th = _angle((m[:, None] % gw) * (m[None, :] % gw), gw)
    cc = jnp.where(same, jnp.cos(th), 0.0) * gw ** -0.5
    sc = jnp.where(same, jnp.sin(th), 0.0) * gw ** -0.5
    return jnp.concatenate([cc, sc], axis=0).astype(BF16)


def _fourier_tables():
    l1n, l2n = DFT_L1, DFT_L2
    length = l1n * l2n
    pack = DFT_PACK
    l2 = jnp.arange(l2n)[:, None, None]
    k1 = jnp.arange(l1n)[None, :, None]
    l1 = jnp.arange(l1n)[None, None, :]
    th = _angle(k1 * l1 * l2n + k1 * l2, length)
    w1 = jnp.stack([jnp.cos(th), -jnp.sin(th)], axis=1) * l1n ** -0.5
    rows, cols = 2 * l1n * pack, l1n * pack
    ws = jnp.transpose(w1.reshape(l2n // pack, pack, 2 * l1n, l1n), (0, 2, 1, 3))
    ws = ws.reshape(l2n // pack, rows, l1n).astype(BF16)
    spread = (jnp.arange(cols)[None, :] // pack == jnp.arange(l1n)[:, None]).astype(BF16)
    same_p = (jnp.arange(rows)[:, None] % pack) == (jnp.arange(cols)[None, :] % pack)
    w1 = jnp.einsum("grb,bc->grc", ws, spread, preferred_element_type=F32)
    w1 = jnp.where(same_p[None], w1, 0.0).astype(BF16)
    k2 = jnp.arange(l2n)
    th2 = _angle(k2[:, None] * k2[None, :], l2n)
    c2, s2 = jnp.cos(th2), jnp.sin(th2)
    w2 = (jnp.concatenate([jnp.concatenate([c2, s2], axis=1),
                           jnp.concatenate([-s2, c2], axis=1)], axis=0) * l2n ** -0.5).astype(BF16)
    return w1, w2


def _fourier_ctx_table(n):
    k = jnp.arange(n)
    th = _angle(k[:, None] * k[None, :], n)
    return (jnp.concatenate([jnp.cos(th), -jnp.sin(th)], axis=0) * n ** -0.5).astype(BF16)


def _block_diag_ones(width):
    m = jnp.arange(width) // HEAD_DIM
    return (m[:, None] == m[None, :]).astype(BF16)


def _prep_w_in(w_in):
    col = jnp.arange(W_TOTAL)
    halved = ((col >= W_ZA) & (col < W_ZA + FOURIER_WIDTH)) | ((col >= W_ZB) & (col < W_Q)) | (col >= W_ZC)
    return (w_in * jnp.where(halved, 0.5, 1.0)).astype(BF16)


def kernel(x, c, ctx, c_ctx, w_ada, b_ada, w_in, sgu_w, sgu_b, q_norm_g, k_norm_g,
           attn_sink, w_pa, w_pb, w_pc, w_out):
    b, length, d = x.shape
    n_ctx = ctx.shape[1]
    assert length == DFT_L1 * DFT_L2 and d == D_MODEL and b + 1 <= MOD_ROWS
    assert w_in.shape == (DEPTH, d, W_TOTAL)
    lat_tile = 512
    ctx_row = b

    cs = jnp.zeros((MOD_ROWS, d), F32).at[:b].set(c).at[b].set(c_ctx)
    mod = _ada_call(cs, w_ada, b_ada)

    rope_tables = _rope_tables(length)
    w1, w2 = _fourier_tables()
    cs_dft = _channel_dft(FOURIER_WIDTH)
    wc = _fourier_ctx_table(n_ctx)
    bdq = _block_diag_ones(ATTN_WIDTH)
    bdk = _block_diag_ones(KV_WIDTH)

    w_all = _prep_w_in(w_in)
    gq_all = jnp.tile(q_norm_g, (1, N_HEADS))[:, None, :]
    gk_all = jnp.tile(k_norm_g, (1, N_KV_HEADS))[:, None, :]
    wsg = jnp.transpose(sgu_w, (0, 2, 1, 3)).reshape(DEPTH, CHUNK, SGU_GROUPS * CHUNK).astype(BF16)
    bsg = jnp.repeat(jnp.swapaxes(sgu_b, 1, 2), SGU_WIDTH // SGU_GROUPS, axis=2)
    wpa, wpb, wpc = w_pa.astype(BF16), w_pb.astype(BF16), w_pc.astype(BF16)
    wout = (0.5 * w_out).astype(BF16)

    xc = ctx
    for l in range(DEPTH):
        last = l == DEPTH - 1
        fa_c, main_c, qt_c, kc, vtc = _proj_call(
            xc.reshape(1, b * n_ctx, d), mod, w_all, bdq, bdk, gq_all, gk_all, None, l,
            tile=2 * n_ctx, ctx_row=ctx_row)
        kc = kc.reshape(b, n_ctx, KV_WIDTH)
        vtc = vtc.reshape(b, n_ctx // BLOCK, KV_WIDTH, BLOCK)
        fa, main, qt, k, vt = _proj_call(
            x, mod, w_all, bdq, bdk, gq_all, gk_all, rope_tables, l, tile=lat_tile, ctx_row=None)
        y = _fourier_call(fa, w1, w2, cs_dft)
        x = _mix_call(x, main, y, qt, k, vt, kc, vtc, mod, attn_sink, wsg, bsg, wpa, wpb, wpc, wout,
                      l, tile=lat_tile, local=True, ctx_row=None)
        if not last:
            y_c = _fourier_ctx_call(fa_c.reshape(b, n_ctx, FOURIER_WIDTH), wc, cs_dft)
            xc = _mix_call(xc, main_c.reshape(b, n_ctx, MAIN_WIDTH), y_c,
                           qt_c.reshape(b, n_ctx // BLOCK, ATTN_WIDTH, BLOCK), None, None, kc, vtc,
                           mod, attn_sink, wsg, bsg, wpa, wpb, wpc, wout, l,
                           tile=n_ctx, local=False, ctx_row=ctx_row)
    return x
```

```python
import functools
import math

import jax
import jax.numpy as jnp
from jax import lax
from jax.experimental import pallas as pl
from jax.experimental.pallas import tpu as pltpu

F32 = jnp.float32
BF16 = jnp.bfloat16

D_MODEL = 1024
DEPTH = 4
GRID_W = 64
FOURIER_WIDTH = 256
FOURIER_GROUPS = 4
SGU_WIDTH = 256
SGU_GROUPS = 4
CHUNK = 128
N_HEADS = 8
N_KV_HEADS = 2
HEAD_DIM = 64
ATTN_WIDTH = N_HEADS * HEAD_DIM
KV_WIDTH = N_KV_HEADS * HEAD_DIM
BLOCK = 128
ROPE_THETA = 10000.0
EPS = 1e-6
NEG_INF = -1e30
LOG2E = math.log2(math.e)

LANES = 128
MOD_ROWS = 8
ONES_ROWS = 16

W_FA = 0
W_ZA = W_FA + FOURIER_WIDTH
W_ZB = W_ZA + FOURIER_WIDTH + 2 * SGU_WIDTH
W_Q = W_ZB + SGU_WIDTH
W_K = W_Q + ATTN_WIDTH
W_V = W_K + KV_WIDTH
W_ZC = W_V + KV_WIDTH
W_G = W_ZC + ATTN_WIDTH
W_TOTAL = W_G + 3 * D_MODEL
MAIN_ZA = 0
MAIN_U = MAIN_ZA + FOURIER_WIDTH
MAIN_VS = MAIN_U + SGU_WIDTH
MAIN_ZB = MAIN_VS + SGU_WIDTH
MAIN_ZC = MAIN_ZB + SGU_WIDTH
MAIN_G = MAIN_ZC + ATTN_WIDTH
MAIN_WIDTH = MAIN_G + 3 * D_MODEL
PROJ_CHUNK = 512

DFT_L1 = 32
DFT_L2 = 256
DFT_PACK = 16
DFT1_GROUPS = 4
DFT2_GROUP = 8

VMEM_LIMIT = 56 * 1024 * 1024


def _silu(z):
    return 0.5 * z * (1.0 + jnp.tanh(0.5 * z))


def _silu_of_half(zh):
    return zh * (1.0 + jnp.tanh(zh))


def _dot(a, b):
    return jnp.dot(a, b, preferred_element_type=F32)


def _layer(l, rank):
    return lambda *_: (l,) + (0,) * (rank - 1)


def _ada_kernel(c_ref, w_ref, b_ref, o_ref):
    s = _silu(c_ref[...]).astype(BF16)
    o_ref[0] = _dot(s, w_ref[0].astype(BF16)) + b_ref[0]


def _ada_call(cs, w_ada, b_ada):
    d = D_MODEL
    return pl.pallas_call(
        _ada_kernel,
        grid=(DEPTH, 3),
        in_specs=[
            pl.BlockSpec((MOD_ROWS, d), lambda l, j: (0, 0)),
            pl.BlockSpec((1, d, d), lambda l, j: (l, 0, j)),
            pl.BlockSpec((1, 1, d), lambda l, j: (l, 0, j)),
        ],
        out_specs=pl.BlockSpec((1, MOD_ROWS, d), lambda l, j: (l, 0, j)),
        out_shape=jax.ShapeDtypeStruct((DEPTH, MOD_ROWS, 3 * d), F32),
        compiler_params=pltpu.CompilerParams(
            dimension_semantics=("arbitrary", "arbitrary"), vmem_limit_bytes=VMEM_LIMIT),
        name="ada_mod",
    )(cs, w_ada, b_ada.reshape(DEPTH, 1, 3 * d))


def _norm_rope(t, bd_ref, gain, cos, sin, out_scale):
    width = t.shape[-1]
    ssq = _dot((t * t).astype(BF16), bd_ref[...])
    tn = t * lax.rsqrt(ssq * (1.0 / HEAD_DIM) + EPS) * gain
    if cos is not None:
        reps = width // LANES
        cos_w = jnp.concatenate([cos] * reps, axis=1) if reps > 1 else cos
        sin_w = jnp.concatenate([sin] * reps, axis=1) if reps > 1 else sin
        lane = lax.broadcasted_iota(jnp.int32, tn.shape, 1)
        first = (lane % 32) < 16
        partner = jnp.where(first, pltpu.roll(tn, width - 16, 1), pltpu.roll(tn, 16, 1))
        tn = tn * cos_w + partner * sin_w
    if out_scale != 1.0:
        tn = tn * out_scale
    return tn


def _proj_kernel(x_ref, mod_ref, w_ref, bdq_ref, bdk_ref, gq_ref, gk_ref, cos_ref, sin_ref,
                 fa_ref, main_ref, qt_ref, k_ref, vt_ref, *, ctx_row, rope):
    d = D_MODEL
    nsub = x_ref.shape[1] // BLOCK
    xt = x_ref[0]
    ms = jnp.mean(xt * xt, axis=-1, keepdims=True)
    row = pl.program_id(0) if ctx_row is None else ctx_row
    modrow = mod_ref[0, pl.ds(row, 1), :]
    shift = modrow[:, 0:d]
    scale = modrow[:, d:2 * d]
    hb = ((xt * lax.rsqrt(ms + EPS)) * (1.0 + scale) + shift).astype(BF16)
    cos = cos_ref[...] if rope else None
    sin = sin_ref[...] if rope else None

    def proj(c0, width):
        return _dot(hb, w_ref[0, :, c0:c0 + width])

    q = _norm_rope(proj(W_Q, ATTN_WIDTH), bdq_ref, gq_ref[0], cos, sin, HEAD_DIM ** -0.5 * LOG2E)
    for s in range(nsub):
        qt_ref[0, s] = q[s * BLOCK:(s + 1) * BLOCK, :].T.astype(BF16)
    k = _norm_rope(proj(W_K, KV_WIDTH), bdk_ref, gk_ref[0], cos, sin, 1.0)
    k_ref[0] = k.astype(BF16)
    v = proj(W_V, KV_WIDTH)
    for s in range(nsub):
        vt_ref[0, s] = v[s * BLOCK:(s + 1) * BLOCK, :].T.astype(BF16)

    fa_ref[0] = proj(W_FA, FOURIER_WIDTH).astype(BF16)
    for src, dst, width in ((W_ZA, MAIN_ZA, W_Q - W_ZA), (W_ZC, MAIN_ZC, W_TOTAL - W_ZC)):
        for c0 in range(0, width, PROJ_CHUNK):
            main_ref[0, :, dst + c0:dst + c0 + PROJ_CHUNK] = proj(src + c0, PROJ_CHUNK).astype(BF16)


def _proj_call(x, mod, w_all, bdq, bdk, gq_all, gk_all, rope_tables, l, *, tile, ctx_row):
    b, length, d = x.shape
    nt = length // tile
    nsub = tile // BLOCK
    const = lambda bi, i: (0, 0)
    tok = lambda bi, i: (bi, i, 0)
    blk = lambda bi, i: (bi, i, 0, 0)
    rope = rope_tables is not None
    if rope:
        cos, sin = rope_tables
        pos, pos_rows = (lambda bi, i: (i, 0)), tile
    else:
        cos = sin = jnp.zeros((MOD_ROWS, LANES), F32)
        pos, pos_rows = const, MOD_ROWS
    kern = functools.partial(_proj_kernel, ctx_row=ctx_row, rope=rope)
    return pl.pallas_call(
        kern,
        grid=(b, nt),
        in_specs=[
            pl.BlockSpec((1, tile, d), tok),
            pl.BlockSpec((1, MOD_ROWS, 3 * d), _layer(l, 3)),
            pl.BlockSpec((1, d, W_TOTAL), _layer(l, 3), pipeline_mode=pl.Buffered(1)),
            pl.BlockSpec((ATTN_WIDTH, ATTN_WIDTH), const),
            pl.BlockSpec((KV_WIDTH, KV_WIDTH), const),
            pl.BlockSpec((1, 1, ATTN_WIDTH), _layer(l, 3)),
            pl.BlockSpec((1, 1, KV_WIDTH), _layer(l, 3)),
            pl.BlockSpec((pos_rows, LANES), pos),
            pl.BlockSpec((pos_rows, LANES), pos),
        ],
        out_specs=[
            pl.BlockSpec((1, tile, FOURIER_WIDTH), tok),
            pl.BlockSpec((1, tile, MAIN_WIDTH), tok),
            pl.BlockSpec((1, nsub, ATTN_WIDTH, BLOCK), blk),
            pl.BlockSpec((1, tile, KV_WIDTH), tok),
            pl.BlockSpec((1, nsub, KV_WIDTH, BLOCK), blk),
        ],
        out_shape=[
            jax.ShapeDtypeStruct((b, length, FOURIER_WIDTH), BF16),
            jax.ShapeDtypeStruct((b, length, MAIN_WIDTH), BF16),
            jax.ShapeDtypeStruct((b, length // BLOCK, ATTN_WIDTH, BLOCK), BF16),
            jax.ShapeDtypeStruct((b, length, KV_WIDTH), BF16),
            jax.ShapeDtypeStruct((b, length // BLOCK, KV_WIDTH, BLOCK), BF16),
        ],
        compiler_params=pltpu.CompilerParams(
            dimension_semantics=("arbitrary", "arbitrary"), vmem_limit_bytes=VMEM_LIMIT),
        name="proj_ctx" if ctx_row is not None else "proj_lat",
    )(x, mod, w_all, bdq, bdk, gq_all, gk_all, cos, sin)


def _dft1_kernel(x_ref, w1_ref, a_ref):
    l1n, pack, cw = DFT_L1, DFT_PACK, FOURIER_WIDTH
    for g in range(w1_ref.shape[0]):
        xg = x_ref[0, :, g].reshape(l1n * pack, cw)
        ag = _dot(w1_ref[g], xg).astype(BF16)
        a_ref[0, :, :, g] = ag.reshape(2, l1n, pack, cw)


def _dft2_kernel(a_ref, w2_ref, cs_ref, y_ref):
    cw, l2n = FOURIER_WIDTH, DFT_L2
    grp = a_ref.shape[2]
    rhs = jnp.concatenate(
        [jnp.concatenate([a_ref[0, 0, j], a_ref[0, 1, j]], axis=0) for j in range(grp)], axis=1)
    f = _dot(w2_ref[...], rhs)
    for j in range(grp):
        cols = slice(j * cw, (j + 1) * cw)
        fc = jnp.concatenate([f[:l2n, cols], f[l2n:, cols]], axis=1).astype(BF16)
        y_ref[0, :, cols] = _dot(fc, cs_ref[...]).astype(BF16)


def _fourier_call(fa, w1, w2, cs):
    b, length, cw = fa.shape
    l1n, l2n, pack = DFT_L1, DFT_L2, DFT_PACK
    params = pltpu.CompilerParams(
        dimension_semantics=("arbitrary", "arbitrary"), vmem_limit_bytes=VMEM_LIMIT)
    g1 = DFT1_GROUPS
    a = pl.pallas_call(
        _dft1_kernel,
        grid=(l2n // pack // g1, b),
        in_specs=[
            pl.BlockSpec((1, l1n, g1, pack, cw), lambda j, bi: (bi, 0, j, 0, 0)),
            pl.BlockSpec((g1, 2 * l1n * pack, l1n * pack), lambda j, bi: (j, 0, 0)),
        ],
        out_specs=pl.BlockSpec((1, 2, l1n, g1, pack, cw), lambda j, bi: (bi, 0, 0, j, 0, 0)),
        out_shape=jax.ShapeDtypeStruct((b, 2, l1n, l2n // pack, pack, cw), BF16),
        compiler_params=params,
        name="dft1_lat",
    )(fa.reshape(b, l1n, l2n // pack, pack, cw), w1)
    grp = DFT2_GROUP
    y = pl.pallas_call(
        _dft2_kernel,
        grid=(b, l1n // grp),
        in_specs=[
            pl.BlockSpec((1, 2, grp, l2n, cw), lambda bi, g: (bi, 0, g, 0, 0)),
            pl.BlockSpec(w2.shape, lambda bi, g: (0, 0)),
            pl.BlockSpec(cs.shape, lambda bi, g: (0, 0)),
        ],
        out_specs=pl.BlockSpec((1, l2n, grp * cw), lambda bi, g: (bi, 0, g)),
        out_shape=jax.ShapeDtypeStruct((b, l2n, l1n * cw), BF16),
        compiler_params=params,
        name="dft2_lat",
    )(a.reshape(b, 2, l1n, l2n, cw), w2, cs)
    return y.reshape(b, length, cw)


def _fourier_ctx_kernel(fa_ref, wc_ref, cs_ref, y_ref):
    n = fa_ref.shape[1]
    f = _dot(wc_ref[...], fa_ref[0])
    fc = jnp.concatenate([f[:n], f[n:]], axis=1).astype(BF16)
    y_ref[0] = _dot(fc, cs_ref[...])


def _fourier_ctx_call(fa, wc, cs):
    b, length, width = fa.shape
    return pl.pallas_call(
        _fourier_ctx_kernel,
        grid=(b,),
        in_specs=[
            pl.BlockSpec((1, length, width), lambda bi: (bi, 0, 0)),
            pl.BlockSpec(wc.shape, lambda bi: (0, 0)),
            pl.BlockSpec(cs.shape, lambda bi: (0, 0)),
        ],
        out_specs=pl.BlockSpec((1, length, width), lambda bi: (bi, 0, 0)),
        out_shape=jax.ShapeDtypeStruct((b, length, width), F32),
        compiler_params=pltpu.CompilerParams(
            dimension_semantics=("arbitrary",), vmem_limit_bytes=VMEM_LIMIT),
        name="fourier_ctx",
    )(fa, wc, cs)


def _mix_kernel(*refs, local, tile, n_blocks, ctx_row, layer):
    if local:
        (x_ref, main_ref, y_ref, qt_ref, kp_ref, km_ref, kn_ref, vtp_ref, vtm_ref, vtn_ref,
         kc_ref, vtc_ref, mod_ref, sink_ref, wsg_ref, bsg_ref,
         wpa_ref, wpb_ref, wpc_ref, wout_ref, o_ref,
         kbuf, vtbuf, ya_s, yb_s, yc_s) = refs
    else:
        (x_ref, main_ref, y_ref, qt_ref, kc_ref, vtc_ref, mod_ref, sink_ref, wsg_ref, bsg_ref,
         wpa_ref, wpb_ref, wpc_ref, wout_ref, o_ref,
         ya_s, yb_s, yc_s) = refs
    d = D_MODEL
    nsub = tile // BLOCK
    tile_idx = pl.program_id(1)
    heads_per_kv = N_HEADS // N_KV_HEADS
    cols_all = N_HEADS * BLOCK

    if local:
        kbuf[0:BLOCK, :] = kp_ref[0]
        kbuf[BLOCK:BLOCK + tile, :] = km_ref[0]
        kbuf[BLOCK + tile:2 * BLOCK + tile, :] = kn_ref[0]
        vtbuf[0] = vtp_ref[0, 0]
        for s in range(nsub):
            vtbuf[1 + s] = vtm_ref[0, s]
        vtbuf[1 + nsub] = vtn_ref[0, 0]

    def with_ones(vt):
        return jnp.concatenate([vt, jnp.ones((ONES_ROWS, vt.shape[1]), BF16)], axis=0)

    vt_ctx = with_ones(jnp.concatenate([vtc_ref[0, s] for s in range(vtc_ref.shape[1])], axis=1))

    ya_s[...] = (y_ref[0].astype(F32) * _silu_of_half(main_ref[0, :, MAIN_ZA:MAIN_ZA + FOURIER_WIDTH].astype(F32))).astype(BF16)

    sgu_grp = lax.broadcasted_iota(jnp.int32, (CHUNK, SGU_WIDTH), 1) // (SGU_WIDTH // SGU_GROUPS)
    zero_head = jnp.zeros((HEAD_DIM, BLOCK), BF16)
    sink_row = jnp.concatenate(
        [jnp.full((1, BLOCK), sink_ref[layer, hh] * LOG2E, F32) for hh in range(N_HEADS)], axis=1)
    kq_diff = (lax.broadcasted_iota(jnp.int32, (BLOCK, cols_all), 0)
               - lax.broadcasted_iota(jnp.int32, (BLOCK, cols_all), 1) % BLOCK)

    def sgu(n):
        rows = slice(n * BLOCK, (n + 1) * BLOCK)
        vs = main_ref[0, rows, MAIN_VS:MAIN_VS + SGU_WIDTH].astype(F32)
        vn = (vs * lax.rsqrt(jnp.mean(vs * vs, axis=-1, keepdims=True) + EPS)).astype(BF16)
        rhs = jnp.concatenate(
            [jnp.where(sgu_grp == g, vn, jnp.zeros_like(vn)) for g in range(SGU_GROUPS)], axis=0)
        mixed = _dot(wsg_ref[0], rhs) + bsg_ref[0]
        u = main_ref[0, rows, MAIN_U:MAIN_U + SGU_WIDTH].astype(F32)
        zb = main_ref[0, rows, MAIN_ZB:MAIN_ZB + SGU_WIDTH].astype(F32)
        yb_s[rows, :] = (u * mixed * _silu_of_half(zb)).astype(BF16)

    def scores(n):
        qt = qt_ref[0, n]
        cols = []
        for hh in range(N_HEADS):
            qh = qt[hh * HEAD_DIM:(hh + 1) * HEAD_DIM, :]
            cols.append(jnp.concatenate(
                [qh, zero_head] if hh < heads_per_kv else [zero_head, qh], axis=0))
        qst = jnp.concatenate(cols, axis=1)

        s_ctx = _dot(kc_ref[0], qst)
        s_loc = _dot(kbuf[n * BLOCK:(n + 3) * BLOCK, :], qst) if local else None
        return s_ctx, s_loc

    def attend(n, s_ctx, s_loc):
        rows = slice(n * BLOCK, (n + 1) * BLOCK)
        m = jnp.maximum(jnp.max(s_ctx, axis=0, keepdims=True), sink_row)
        if local:
            gblk = tile_idx * nsub + n
            s_prev = jnp.where(kq_diff >= jnp.where(gblk > 0, 0, BLOCK), s_loc[0:BLOCK], NEG_INF)
            s_own = s_loc[BLOCK:2 * BLOCK]
            s_next = jnp.where(kq_diff <= jnp.where(gblk < n_blocks - 1, 0, -BLOCK),
                               s_loc[2 * BLOCK:3 * BLOCK], NEG_INF)
            m_loc = jnp.max(jnp.maximum(jnp.maximum(s_prev, s_own), s_next), axis=0, keepdims=True)
            m = jnp.maximum(m, m_loc)
        e_ctx = jnp.exp2(s_ctx - m)
        ot = _dot(vt_ctx, e_ctx.astype(BF16))
        if local:
            e_loc = jnp.concatenate(
                [jnp.exp2(s_prev - m), jnp.exp2(s_own - m), jnp.exp2(s_next - m)], axis=0)
            vt_loc = with_ones(jnp.concatenate([vtbuf[n], vtbuf[n + 1], vtbuf[n + 2]], axis=1))
            ot = ot + _dot(vt_loc, e_loc.astype(BF16))
        den = ot[KV_WIDTH:KV_WIDTH + 1] + jnp.exp2(sink_row - m)
        ot = ot[0:KV_WIDTH] * (1.0 / den)
        for j in range(ATTN_WIDTH // LANES):
            h = (2 * j) // heads_per_kv
            pair = jnp.concatenate(
                [ot[h * HEAD_DIM:(h + 1) * HEAD_DIM, (2 * j) * BLOCK:(2 * j + 1) * BLOCK],
                 ot[h * HEAD_DIM:(h + 1) * HEAD_DIM, (2 * j + 1) * BLOCK:(2 * j + 2) * BLOCK]], axis=0)
            zc = main_ref[0, rows, MAIN_ZC + j * LANES:MAIN_ZC + (j + 1) * LANES].astype(F32)
            yc_s[rows, j * LANES:(j + 1) * LANES] = (pair.T * _silu_of_half(zc)).astype(BF16)

    ahead = scores(0)
    for n in range(nsub):
        sgu(n)
        current, ahead = ahead, (scores(n + 1) if n + 1 < nsub else None)
        attend(n, *current)

    t = jnp.tanh(main_ref[0, :, MAIN_G:MAIN_G + 3 * d].astype(F32))
    ma = _dot(ya_s[...], wpa_ref[0])
    mb = _dot(yb_s[...], wpb_ref[0])
    mc = _dot(yc_s[...], wpc_ref[0])
    merged2 = (ma + t[:, 0:d] * ma) + (mb + t[:, d:2 * d] * mb) + (mc + t[:, 2 * d:3 * d] * mc)
    out = _dot(merged2.astype(BF16), wout_ref[0])
    row = pl.program_id(0) if ctx_row is None else ctx_row
    gate = mod_ref[0, pl.ds(row, 1), 2 * d:3 * d]
    o_ref[0] = x_ref[0] + gate * out


def _mix_call(x, main, y, qt, k, vt, kc, vtc, mod, sinks, wsg, bsg, wpa, wpb, wpc, wout, l,
              *, tile, local, ctx_row):
    b, length, d = x.shape
    nt = length // tile
    n_ctx = kc.shape[1]
    sub = tile // BLOCK
    n_blocks = length // BLOCK
    tok = lambda bi, i: (bi, i, 0)
    blk = lambda bi, i: (bi, i, 0, 0)
    per_b = lambda bi, i: (bi, 0, 0)
    per_b4 = lambda bi, i: (bi, 0, 0, 0)

    in_specs = [
        pl.BlockSpec((1, tile, d), tok),
        pl.BlockSpec((1, tile, MAIN_WIDTH), tok),
        pl.BlockSpec((1, tile, FOURIER_WIDTH), tok),
        pl.BlockSpec((1, sub, ATTN_WIDTH, BLOCK), blk),
    ]
    args = [x, main, y, qt]
    if local:
        prev_blk = lambda bi, i: jnp.maximum(i * sub - 1, 0)
        next_blk = lambda bi, i: jnp.minimum((i + 1) * sub, n_blocks - 1)
        in_specs += [
            pl.BlockSpec((1, BLOCK, KV_WIDTH), lambda bi, i: (bi, prev_blk(bi, i), 0)),
            pl.BlockSpec((1, tile, KV_WIDTH), tok),
            pl.BlockSpec((1, BLOCK, KV_WIDTH), lambda bi, i: (bi, next_blk(bi, i), 0)),
            pl.BlockSpec((1, 1, KV_WIDTH, BLOCK), lambda bi, i: (bi, prev_blk(bi, i), 0, 0)),
            pl.BlockSpec((1, sub, KV_WIDTH, BLOCK), blk),
            pl.BlockSpec((1, 1, KV_WIDTH, BLOCK), lambda bi, i: (bi, next_blk(bi, i), 0, 0)),
        ]
        args += [k, k, k, vt, vt, vt]
    in_specs += [
        pl.BlockSpec((1, n_ctx, KV_WIDTH), per_b),
        pl.BlockSpec((1, n_ctx // BLOCK, KV_WIDTH, BLOCK), per_b4),
        pl.BlockSpec((1, MOD_ROWS, 3 * d), _layer(l, 3)),
        pl.BlockSpec(memory_space=pltpu.SMEM),
    ]
    args += [kc, vtc, mod, sinks]
    for w in (wsg, bsg, wpa, wpb, wpc, wout):
        in_specs.append(pl.BlockSpec((1,) + w.shape[1:], _layer(l, 3)))
        args.append(w)

    scratch = []
    if local:
        scratch += [pltpu.VMEM((tile + 2 * BLOCK, KV_WIDTH), BF16),
                    pltpu.VMEM((sub + 2, KV_WIDTH, BLOCK), BF16)]
    scratch += [
        pltpu.VMEM((tile, FOURIER_WIDTH), BF16),
        pltpu.VMEM((tile, SGU_WIDTH), BF16),
        pltpu.VMEM((tile, ATTN_WIDTH), BF16),
    ]
    kern = functools.partial(_mix_kernel, local=local, tile=tile, n_blocks=n_blocks,
                             ctx_row=ctx_row, layer=l)
    return pl.pallas_call(
        kern,
        grid=(b, nt),
        in_specs=in_specs,
        out_specs=pl.BlockSpec((1, tile, d), tok),
        out_shape=jax.ShapeDtypeStruct((b, length, d), F32),
        scratch_shapes=scratch,
        compiler_params=pltpu.CompilerParams(
            dimension_semantics=("arbitrary", "arbitrary"), vmem_limit_bytes=VMEM_LIMIT),
        name="mix_lat" if local else "mix_ctx",
    )(*args)


def _rope_tables(length):
    nf = HEAD_DIM // 4
    inv = jnp.tile(ROPE_THETA ** (-jnp.arange(nf, dtype=F32) / nf), LANES // nf)[None, :]
    pos = lax.broadcasted_iota(jnp.int32, (length, LANES), 0)
    lane = lax.broadcasted_iota(jnp.int32, (length, LANES), 1)
    axial = jnp.where((lane // (2 * nf)) % 2 == 0, pos // GRID_W, pos % GRID_W)
    ang = axial.astype(F32) * inv
    sign = jnp.where((lane // nf) % 2 == 0, -1.0, 1.0)
    return jnp.cos(ang), jnp.sin(ang) * sign


def _angle(num, den):
    return (num % den).astype(F32) * (2.0 * math.pi / den)


def _channel_dft(width):
    gw = FOURIER_WIDTH // FOURIER_GROUPS
    m = jnp.arange(width)
    same = (m[:, None] // gw) == (m[None, :] // gw)
    th = _angle((m[:, None] % gw) * (m[None, :] % gw), gw)
    cc = jnp.where(same, jnp.cos(th), 0.0) * gw ** -0.5
    sc = jnp.where(same, jnp.sin(th), 0.0) * gw ** -0.5
    return jnp.concatenate([cc, sc], axis=0).astype(BF16)


def _fourier_tables():
    l1n, l2n = DFT_L1, DFT_L2
    length = l1n * l2n
    pack = DFT_PACK
    l2 = jnp.arange(l2n)[:, None, None]
    k1 = jnp.arange(l1n)[None, :, None]
    l1 = jnp.arange(l1n)[None, None, :]
    th = _angle(k1 * l1 * l2n + k1 * l2, length)
    w1 = jnp.stack([jnp.cos(th), -jnp.sin(th)], axis=1) * l1n ** -0.5
    rows, cols = 2 * l1n * pack, l1n * pack
    ws = jnp.transpose(w1.reshape(l2n // pack, pack, 2 * l1n, l1n), (0, 2, 1, 3))
    ws = ws.reshape(l2n // pack, rows, l1n).astype(BF16)
    spread = (jnp.arange(cols)[None, :] // pack == jnp.arange(l1n)[:, None]).astype(BF16)
    same_p = (jnp.arange(rows)[:, None] % pack) == (jnp.arange(cols)[None, :] % pack)
    w1 = jnp.einsum("grb,bc->grc", ws, spread, preferred_element_type=F32)
    w1 = jnp.where(same_p[None], w1, 0.0).astype(BF16)
    k2 = jnp.arange(l2n)
    th2 = _angle(k2[:, None] * k2[None, :], l2n)
    c2, s2 = jnp.cos(th2), jnp.sin(th2)
    w2 = (jnp.concatenate([jnp.concatenate([c2, s2], axis=1),
                           jnp.concatenate([-s2, c2], axis=1)], axis=0) * l2n ** -0.5).astype(BF16)
    return w1, w2


def _fourier_ctx_table(n):
    k = jnp.arange(n)
    th = _angle(k[:, None] * k[None, :], n)
    return (jnp.concatenate([jnp.cos(th), -jnp.sin(th)], axis=0) * n ** -0.5).astype(BF16)


def _block_diag_ones(width):
    m = jnp.arange(width) // HEAD_DIM
    return (m[:, None] == m[None, :]).astype(BF16)


def _prep_w_in(w_in):
    col = jnp.arange(W_TOTAL)
    halved = ((col >= W_ZA) & (col < W_ZA + FOURIER_WIDTH)) | ((col >= W_ZB) & (col < W_Q)) | (col >= W_ZC)
    return (w_in * jnp.where(halved, 0.5, 1.0)).astype(BF16)


def kernel(x, c, ctx, c_ctx, w_ada, b_ada, w_in, sgu_w, sgu_b, q_norm_g, k_norm_g,
           attn_sink, w_pa, w_pb, w_pc, w_out):
    b, length, d = x.shape
    n_ctx = ctx.shape[1]
    assert length == DFT_L1 * DFT_L2 and d == D_MODEL and b + 1 <= MOD_ROWS
    assert w_in.shape == (DEPTH, d, W_TOTAL)
    lat_tile = 512
    ctx_row = b

    cs = jnp.zeros((MOD_ROWS, d), F32).at[:b].set(c).at[b].set(c_ctx)
    mod = _ada_call(cs, w_ada, b_ada)

    rope_tables = _rope_tables(length)
    w1, w2 = _fourier_tables()
    cs_dft = _channel_dft(FOURIER_WIDTH)
    wc = _fourier_ctx_table(n_ctx)
    bdq = _block_diag_ones(ATTN_WIDTH)
    bdk = _block_diag_ones(KV_WIDTH)

    w_all = _prep_w_in(w_in)
    gq_all = jnp.tile(q_norm_g, (1, N_HEADS))[:, None, :]
    gk_all = jnp.tile(k_norm_g, (1, N_KV_HEADS))[:, None, :]
    wsg = jnp.transpose(sgu_w, (0, 2, 1, 3)).reshape(DEPTH, CHUNK, SGU_GROUPS * CHUNK).astype(BF16)
    bsg = jnp.repeat(jnp.swapaxes(sgu_b, 1, 2), SGU_WIDTH // SGU_GROUPS, axis=2)
    wpa, wpb, wpc = w_pa.astype(BF16), w_pb.astype(BF16), w_pc.astype(BF16)
    wout = (0.5 * w_out).astype(BF16)

    xc = ctx
    for l in range(DEPTH):
        last = l == DEPTH - 1
        fa_c, main_c, qt_c, kc, vtc = _proj_call(
            xc.reshape(1, b * n_ctx, d), mod, w_all, bdq, bdk, gq_all, gk_all, None, l,
            tile=2 * n_ctx, ctx_row=ctx_row)
        kc = kc.reshape(b, n_ctx, KV_WIDTH)
        vtc = vtc.reshape(b, n_ctx // BLOCK, KV_WIDTH, BLOCK)
        fa, main, qt, k, vt = _proj_call(
            x, mod, w_all, bdq, bdk, gq_all, gk_all, rope_tables, l, tile=lat_tile, ctx_row=None)
        y = _fourier_call(fa, w1, w2, cs_dft)
        x = _mix_call(x, main, y, qt, k, vt, kc, vtc, mod, attn_sink, wsg, bsg, wpa, wpb, wpc, wout,
                      l, tile=lat_tile, local=True, ctx_row=None)
        if not last:
            y_c = _fourier_ctx_call(fa_c.reshape(b, n_ctx, FOURIER_WIDTH), wc, cs_dft)
            xc = _mix_call(xc, main_c.reshape(b, n_ctx, MAIN_WIDTH), y_c,
                           qt_c.reshape(b, n_ctx // BLOCK, ATTN_WIDTH, BLOCK), None, None, kc, vtc,
                           mod, attn_sink, wsg, bsg, wpa, wpb, wpc, wout, l,
                           tile=n_ctx, local=False, ctx_row=ctx_row)
    return x
```

```python
import functools
import math

import jax
import jax.numpy as jnp
from jax import lax
from jax.experimental import pallas as pl
from jax.experimental.pallas import tpu as pltpu

F32 = jnp.float32
BF16 = jnp.bfloat16

D_MODEL = 1024
DEPTH = 4
GRID_W = 64
FOURIER_WIDTH = 256
FOURIER_GROUPS = 4
SGU_WIDTH = 256
SGU_GROUPS = 4
CHUNK = 128
N_HEADS = 8
N_KV_HEADS = 2
HEAD_DIM = 64
ATTN_WIDTH = N_HEADS * HEAD_DIM
KV_WIDTH = N_KV_HEADS * HEAD_DIM
BLOCK = 128
ROPE_THETA = 10000.0
EPS = 1e-6
NEG_INF = -1e30
LOG2E = math.log2(math.e)

LANES = 128
MOD_ROWS = 8
ONES_ROWS = 16

W_FA = 0
W_ZA = W_FA + FOURIER_WIDTH
W_ZB = W_ZA + FOURIER_WIDTH + 2 * SGU_WIDTH
W_Q = W_ZB + SGU_WIDTH
W_K = W_Q + ATTN_WIDTH
W_V = W_K + KV_WIDTH
W_ZC = W_V + KV_WIDTH
W_G = W_ZC + ATTN_WIDTH
W_TOTAL = W_G + 3 * D_MODEL
MAIN_ZA = 0
MAIN_U = MAIN_ZA + FOURIER_WIDTH
MAIN_VS = MAIN_U + SGU_WIDTH
MAIN_ZB = MAIN_VS + SGU_WIDTH
MAIN_ZC = MAIN_ZB + SGU_WIDTH
MAIN_G = MAIN_ZC + ATTN_WIDTH
MAIN_WIDTH = MAIN_G + 3 * D_MODEL
PROJ_CHUNK = 512

DFT_L1 = 32
DFT_L2 = 256
DFT_PACK = 16
DFT1_GROUPS = 4
DFT2_GROUP = 8

VMEM_LIMIT = 56 * 1024 * 1024


def _silu(z):
    return 0.5 * z * (1.0 + jnp.tanh(0.5 * z))


def _silu_of_half(zh):
    return zh * (1.0 + jnp.tanh(zh))


def _dot(a, b):
    return jnp.dot(a, b, preferred_element_type=F32)


def _layer(l, rank):
    return lambda *_: (l,) + (0,) * (rank - 1)


def _ada_kernel(c_ref, w_ref, b_ref, o_ref):
    s = _silu(c_ref[...]).astype(BF16)
    o_ref[0] = _dot(s, w_ref[0].astype(BF16)) + b_ref[0]


def _ada_call(cs, w_ada, b_ada):
    d = D_MODEL
    return pl.pallas_call(
        _ada_kernel,
        grid=(DEPTH, 3),
        in_specs=[
            pl.BlockSpec((MOD_ROWS, d), lambda l, j: (0, 0)),
            pl.BlockSpec((1, d, d), lambda l, j: (l, 0, j)),
            pl.BlockSpec((1, 1, d), lambda l, j: (l, 0, j)),
        ],
        out_specs=pl.BlockSpec((1, MOD_ROWS, d), lambda l, j: (l, 0, j)),
        out_shape=jax.ShapeDtypeStruct((DEPTH, MOD_ROWS, 3 * d), F32),
        compiler_params=pltpu.CompilerParams(
            dimension_semantics=("arbitrary", "arbitrary"), vmem_limit_bytes=VMEM_LIMIT),
        name="ada_mod",
    )(cs, w_ada, b_ada.reshape(DEPTH, 1, 3 * d))


def _norm_rope(t, bd_ref, gain, cos, sin, out_scale):
    width = t.shape[-1]
    ssq = _dot((t * t).astype(BF16), bd_ref[...])
    tn = t * lax.rsqrt(ssq * (1.0 / HEAD_DIM) + EPS) * gain
    if cos is not None:
        reps = width // LANES
        cos_w = jnp.concatenate([cos] * reps, axis=1) if reps > 1 else cos
        sin_w = jnp.concatenate([sin] * reps, axis=1) if reps > 1 else sin
        lane = lax.broadcasted_iota(jnp.int32, tn.shape, 1)
        first = (lane % 32) < 16
        partner = jnp.where(first, pltpu.roll(tn, width - 16, 1), pltpu.roll(tn, 16, 1))
        tn = tn * cos_w + partner * sin_w
    if out_scale != 1.0:
        tn = tn * out_scale
    return tn


def _proj_kernel(x_ref, mod_ref, w_ref, bdq_ref, bdk_ref, gq_ref, gk_ref, cos_ref, sin_ref,
                 fa_ref, main_ref, qt_ref, k_ref, vt_ref, *, ctx_row, rope):
    d = D_MODEL
    nsub = x_ref.shape[1] // BLOCK
    xt = x_ref[0]
    ms = jnp.mean(xt * xt, axis=-1, keepdims=True)
    row = pl.program_id(0) if ctx_row is None else ctx_row
    modrow = mod_ref[0, pl.ds(row, 1), :]
    shift = modrow[:, 0:d]
    scale = modrow[:, d:2 * d]
    hb = ((xt * lax.rsqrt(ms + EPS)) * (1.0 + scale) + shift).astype(BF16)
    cos = cos_ref[...] if rope else None
    sin = sin_ref[...] if rope else None

    def proj(c0, width):
        return _dot(hb, w_ref[0, :, c0:c0 + width])

    chunks = [(src + c0, dst + c0)
              for src, dst, width in ((W_ZA, MAIN_ZA, W_Q - W_ZA), (W_ZC, MAIN_ZC, W_TOTAL - W_ZC))
              for c0 in range(0, width, PROJ_CHUNK)]

    def wide(count):
        for _ in range(min(count, len(chunks))):
            src, dst = chunks.pop(0)
            main_ref[0, :, dst:dst + PROJ_CHUNK] = proj(src, PROJ_CHUNK).astype(BF16)

    q_raw = proj(W_Q, ATTN_WIDTH)
    k_raw = proj(W_K, KV_WIDTH)
    v = proj(W_V, KV_WIDTH)
    wide(1)
    q = _norm_rope(q_raw, bdq_ref, gq_ref[0], cos, sin, HEAD_DIM ** -0.5 * LOG2E)
    wide(1)
    for s in range(nsub):
        qt_ref[0, s] = q[s * BLOCK:(s + 1) * BLOCK, :].T.astype(BF16)
    wide(1)
    k = _norm_rope(k_raw, bdk_ref, gk_ref[0], cos, sin, 1.0)
    k_ref[0] = k.astype(BF16)
    for s in range(nsub):
        vt_ref[0, s] = v[s * BLOCK:(s + 1) * BLOCK, :].T.astype(BF16)
    fa_ref[0] = proj(W_FA, FOURIER_WIDTH).astype(BF16)
    wide(len(chunks))


def _proj_call(x, mod, w_all, bdq, bdk, gq_all, gk_all, rope_tables, l, *, tile, ctx_row):
    b, length, d = x.shape
    nt = length // tile
    nsub = tile // BLOCK
    const = lambda bi, i: (0, 0)
    tok = lambda bi, i: (bi, i, 0)
    blk = lambda bi, i: (bi, i, 0, 0)
    rope = rope_tables is not None
    if rope:
        cos, sin = rope_tables
        pos, pos_rows = (lambda bi, i: (i, 0)), tile
    else:
        cos = sin = jnp.zeros((MOD_ROWS, LANES), F32)
        pos, pos_rows = const, MOD_ROWS
    kern = functools.partial(_proj_kernel, ctx_row=ctx_row, rope=rope)
    return pl.pallas_call(
        kern,
        grid=(b, nt),
        in_specs=[
            pl.BlockSpec((1, tile, d), tok),
            pl.BlockSpec((1, MOD_ROWS, 3 * d), _layer(l, 3)),
            pl.BlockSpec((1, d, W_TOTAL), _layer(l, 3), pipeline_mode=pl.Buffered(1)),
            pl.BlockSpec((ATTN_WIDTH, ATTN_WIDTH), const),
            pl.BlockSpec((KV_WIDTH, KV_WIDTH), const),
            pl.BlockSpec((1, 1, ATTN_WIDTH), _layer(l, 3)),
            pl.BlockSpec((1, 1, KV_WIDTH), _layer(l, 3)),
            pl.BlockSpec((pos_rows, LANES), pos),
            pl.BlockSpec((pos_rows, LANES), pos),
        ],
        out_specs=[
            pl.BlockSpec((1, tile, FOURIER_WIDTH), tok),
            pl.BlockSpec((1, tile, MAIN_WIDTH), tok),
            pl.BlockSpec((1, nsub, ATTN_WIDTH, BLOCK), blk),
            pl.BlockSpec((1, tile, KV_WIDTH), tok),
            pl.BlockSpec((1, nsub, KV_WIDTH, BLOCK), blk),
        ],
        out_shape=[
            jax.ShapeDtypeStruct((b, length, FOURIER_WIDTH), BF16),
            jax.ShapeDtypeStruct((b, length, MAIN_WIDTH), BF16),
            jax.ShapeDtypeStruct((b, length // BLOCK, ATTN_WIDTH, BLOCK), BF16),
            jax.ShapeDtypeStruct((b, length, KV_WIDTH), BF16),
            jax.ShapeDtypeStruct((b, length // BLOCK, KV_WIDTH, BLOCK), BF16),
        ],
        compiler_params=pltpu.CompilerParams(
            dimension_semantics=("arbitrary", "arbitrary"), vmem_limit_bytes=VMEM_LIMIT),
        name="proj_ctx" if ctx_row is not None else "proj_lat",
    )(x, mod, w_all, bdq, bdk, gq_all, gk_all, cos, sin)


def _dft1_kernel(x_ref, w1_ref, a_ref):
    l1n, pack, cw = DFT_L1, DFT_PACK, FOURIER_WIDTH
    for g in range(w1_ref.shape[0]):
        xg = x_ref[0, :, g].reshape(l1n * pack, cw)
        ag = _dot(w1_ref[g], xg).astype(BF16)
        a_ref[0, :, :, g] = ag.reshape(2, l1n, pack, cw)


def _dft2_kernel(a_ref, w2_ref, cs_ref, y_ref):
    cw, l2n = FOURIER_WIDTH, DFT_L2
    grp = a_ref.shape[2]
    rhs = jnp.concatenate(
        [jnp.concatenate([a_ref[0, 0, j], a_ref[0, 1, j]], axis=0) for j in range(grp)], axis=1)
    f = _dot(w2_ref[...], rhs)
    for j in range(grp):
        cols = slice(j * cw, (j + 1) * cw)
        fc = jnp.concatenate([f[:l2n, cols], f[l2n:, cols]], axis=1).astype(BF16)
        y_ref[0, :, cols] = _dot(fc, cs_ref[...]).astype(BF16)


def _fourier_call(fa, w1, w2, cs):
    b, length, cw = fa.shape
    l1n, l2n, pack = DFT_L1, DFT_L2, DFT_PACK
    params = pltpu.CompilerParams(
        dimension_semantics=("arbitrary", "arbitrary"), vmem_limit_bytes=VMEM_LIMIT)
    g1 = DFT1_GROUPS
    a = pl.pallas_call(
        _dft1_kernel,
        grid=(l2n // pack // g1, b),
        in_specs=[
            pl.BlockSpec((1, l1n, g1, pack, cw), lambda j, bi: (bi, 0, j, 0, 0)),
            pl.BlockSpec((g1, 2 * l1n * pack, l1n * pack), lambda j, bi: (j, 0, 0)),
        ],
        out_specs=pl.BlockSpec((1, 2, l1n, g1, pack, cw), lambda j, bi: (bi, 0, 0, j, 0, 0)),
        out_shape=jax.ShapeDtypeStruct((b, 2, l1n, l2n // pack, pack, cw), BF16),
        compiler_params=params,
        name="dft1_lat",
    )(fa.reshape(b, l1n, l2n // pack, pack, cw), w1)
    grp = DFT2_GROUP
    y = pl.pallas_call(
        _dft2_kernel,
        grid=(b, l1n // grp),
        in_specs=[
            pl.BlockSpec((1, 2, grp, l2n, cw), lambda bi, g: (bi, 0, g, 0, 0)),
            pl.BlockSpec(w2.shape, lambda bi, g: (0, 0)),
            pl.BlockSpec(cs.shape, lambda bi, g: (0, 0)),
        ],
        out_specs=pl.BlockSpec((1, l2n, grp * cw), lambda bi, g: (bi, 0, g)),
        out_shape=jax.ShapeDtypeStruct((b, l2n, l1n * cw), BF16),
        compiler_params=params,
        name="dft2_lat",
    )(a.reshape(b, 2, l1n, l2n, cw), w2, cs)
    return y.reshape(b, length, cw)


def _fourier_ctx_kernel(fa_ref, wc_ref, cs_ref, y_ref):
    n = fa_ref.shape[1]
    f = _dot(wc_ref[...], fa_ref[0])
    fc = jnp.concatenate([f[:n], f[n:]], axis=1).astype(BF16)
    y_ref[0] = _dot(fc, cs_ref[...])


def _fourier_ctx_call(fa, wc, cs):
    b, length, width = fa.shape
    return pl.pallas_call(
        _fourier_ctx_kernel,
        grid=(b,),
        in_specs=[
            pl.BlockSpec((1, length, width), lambda bi: (bi, 0, 0)),
            pl.BlockSpec(wc.shape, lambda bi: (0, 0)),
            pl.BlockSpec(cs.shape, lambda bi: (0, 0)),
        ],
        out_specs=pl.BlockSpec((1, length, width), lambda bi: (bi, 0, 0)),
        out_shape=jax.ShapeDtypeStruct((b, length, width), F32),
        compiler_params=pltpu.CompilerParams(
            dimension_semantics=("arbitrary",), vmem_limit_bytes=VMEM_LIMIT),
        name="fourier_ctx",
    )(fa, wc, cs)


def _mix_kernel(*refs, local, tile, n_blocks, ctx_row, layer):
    if local:
        (x_ref, main_ref, y_ref, qt_ref, kp_ref, km_ref, kn_ref, vtp_ref, vtm_ref, vtn_ref,
         kc_ref, vtc_ref, mod_ref, sink_ref, wsg_ref, bsg_ref,
         wpa_ref, wpb_ref, wpc_ref, wout_ref, o_ref,
         kbuf, vtbuf, ya_s, yb_s, yc_s) = refs
    else:
        (x_ref, main_ref, y_ref, qt_ref, kc_ref, vtc_ref, mod_ref, sink_ref, wsg_ref, bsg_ref,
         wpa_ref, wpb_ref, wpc_ref, wout_ref, o_ref,
         ya_s, yb_s, yc_s) = refs
    d = D_MODEL
    nsub = tile // BLOCK
    tile_idx = pl.program_id(1)
    heads_per_kv = N_HEADS // N_KV_HEADS
    cols_all = N_HEADS * BLOCK

    if local:
        kbuf[0:BLOCK, :] = kp_ref[0]
        kbuf[BLOCK:BLOCK + tile, :] = km_ref[0]
        kbuf[BLOCK + tile:2 * BLOCK + tile, :] = kn_ref[0]
        vtbuf[0] = vtp_ref[0, 0]
        for s in range(nsub):
            vtbuf[1 + s] = vtm_ref[0, s]
        vtbuf[1 + nsub] = vtn_ref[0, 0]

    def with_ones(vt):
        return jnp.concatenate([vt, jnp.ones((ONES_ROWS, vt.shape[1]), BF16)], axis=0)

    vt_ctx = with_ones(jnp.concatenate([vtc_ref[0, s] for s in range(vtc_ref.shape[1])], axis=1))

    ya_s[...] = (y_ref[0].astype(F32) * _silu_of_half(main_ref[0, :, MAIN_ZA:MAIN_ZA + FOURIER_WIDTH].astype(F32))).astype(BF16)

    sgu_grp = lax.broadcasted_iota(jnp.int32, (CHUNK, SGU_WIDTH), 1) // (SGU_WIDTH // SGU_GROUPS)
    zero_head = jnp.zeros((HEAD_DIM, BLOCK), BF16)
    sink_row = jnp.concatenate(
        [jnp.full((1, BLOCK), sink_ref[layer, hh] * LOG2E, F32) for hh in range(N_HEADS)], axis=1)
    kq_diff = (lax.broadcasted_iota(jnp.int32, (BLOCK, cols_all), 0)
               - lax.broadcasted_iota(jnp.int32, (BLOCK, cols_all), 1) % BLOCK)

    def sgu(n):
        rows = slice(n * BLOCK, (n + 1) * BLOCK)
        vs = main_ref[0, rows, MAIN_VS:MAIN_VS + SGU_WIDTH].astype(F32)
        vn = (vs * lax.rsqrt(jnp.mean(vs * vs, axis=-1, keepdims=True) + EPS)).astype(BF16)
        rhs = jnp.concatenate(
            [jnp.where(sgu_grp == g, vn, jnp.zeros_like(vn)) for g in range(SGU_GROUPS)], axis=0)
        mixed = _dot(wsg_ref[0], rhs) + bsg_ref[0]
        u = main_ref[0, rows, MAIN_U:MAIN_U + SGU_WIDTH].astype(F32)
        zb = main_ref[0, rows, MAIN_ZB:MAIN_ZB + SGU_WIDTH].astype(F32)
        yb_s[rows, :] = (u * mixed * _silu_of_half(zb)).astype(BF16)

    def scores(n):
        qt = qt_ref[0, n]
        cols = []
        for hh in range(N_HEADS):
            qh = qt[hh * HEAD_DIM:(hh + 1) * HEAD_DIM, :]
            cols.append(jnp.concatenate(
                [qh, zero_head] if hh < heads_per_kv else [zero_head, qh], axis=0))
        qst = jnp.concatenate(cols, axis=1)

        s_ctx = _dot(kc_ref[0], qst)
        s_loc = _dot(kbuf[n * BLOCK:(n + 3) * BLOCK, :], qst) if local else None
        return s_ctx, s_loc

    def attend(n, s_ctx, s_loc):
        rows = slice(n * BLOCK, (n + 1) * BLOCK)
        m = jnp.maximum(jnp.max(s_ctx, axis=0, keepdims=True), sink_row)
        if local:
            gblk = tile_idx * nsub + n
            s_prev = jnp.where(kq_diff >= jnp.where(gblk > 0, 0, BLOCK), s_loc[0:BLOCK], NEG_INF)
            s_own = s_loc[BLOCK:2 * BLOCK]
            s_next = jnp.where(kq_diff <= jnp.where(gblk < n_blocks - 1, 0, -BLOCK),
                               s_loc[2 * BLOCK:3 * BLOCK], NEG_INF)
            m_loc = jnp.max(jnp.maximum(jnp.maximum(s_prev, s_own), s_next), axis=0, keepdims=True)
            m = jnp.maximum(m, m_loc)
        e_ctx = jnp.exp2(s_ctx - m)
        ot = _dot(vt_ctx, e_ctx.astype(BF16))
        if local:
            e_loc = jnp.concatenate(
                [jnp.exp2(s_prev - m), jnp.exp2(s_own - m), jnp.exp2(s_next - m)], axis=0)
            vt_loc = with_ones(jnp.concatenate([vtbuf[n], vtbuf[n + 1], vtbuf[n + 2]], axis=1))
            ot = ot + _dot(vt_loc, e_loc.astype(BF16))
        den = ot[KV_WIDTH:KV_WIDTH + 1] + jnp.exp2(sink_row - m)
        ot = ot[0:KV_WIDTH] * (1.0 / den)
        for j in range(ATTN_WIDTH // LANES):
            h = (2 * j) // heads_per_kv
            pair = jnp.concatenate(
                [ot[h * HEAD_DIM:(h + 1) * HEAD_DIM, (2 * j) * BLOCK:(2 * j + 1) * BLOCK],
                 ot[h * HEAD_DIM:(h + 1) * HEAD_DIM, (2 * j + 1) * BLOCK:(2 * j + 2) * BLOCK]], axis=0)
            zc = main_ref[0, rows, MAIN_ZC + j * LANES:MAIN_ZC + (j + 1) * LANES].astype(F32)
            yc_s[rows, j * LANES:(j + 1) * LANES] = (pair.T * _silu_of_half(zc)).astype(BF16)

    ahead = scores(0)
    for n in range(nsub):
        sgu(n)
        current, ahead = ahead, (scores(n + 1) if n + 1 < nsub else None)
        attend(n, *current)

    t = jnp.tanh(main_ref[0, :, MAIN_G:MAIN_G + 3 * d].astype(F32))
    ma = _dot(ya_s[...], wpa_ref[0])
    mb = _dot(yb_s[...], wpb_ref[0])
    mc = _dot(yc_s[...], wpc_ref[0])
    merged2 = (ma + t[:, 0:d] * ma) + (mb + t[:, d:2 * d] * mb) + (mc + t[:, 2 * d:3 * d] * mc)
    out = _dot(merged2.astype(BF16), wout_ref[0])
    row = pl.program_id(0) if ctx_row is None else ctx_row
    gate = mod_ref[0, pl.ds(row, 1), 2 * d:3 * d]
    o_ref[0] = x_ref[0] + gate * out


def _mix_call(x, main, y, qt, k, vt, kc, vtc, mod, sinks, wsg, bsg, wpa, wpb, wpc, wout, l,
              *, tile, local, ctx_row):
    b, length, d = x.shape
    nt = length // tile
    n_ctx = kc.shape[1]
    sub = tile // BLOCK
    n_blocks = length // BLOCK
    tok = lambda bi, i: (bi, i, 0)
    blk = lambda bi, i: (bi, i, 0, 0)
    per_b = lambda bi, i: (bi, 0, 0)
    per_b4 = lambda bi, i: (bi, 0, 0, 0)

    in_specs = [
        pl.BlockSpec((1, tile, d), tok),
        pl.BlockSpec((1, tile, MAIN_WIDTH), tok),
        pl.BlockSpec((1, tile, FOURIER_WIDTH), tok),
        pl.BlockSpec((1, sub, ATTN_WIDTH, BLOCK), blk),
    ]
    args = [x, main, y, qt]
    if local:
        prev_blk = lambda bi, i: jnp.maximum(i * sub - 1, 0)
        next_blk = lambda bi, i: jnp.minimum((i + 1) * sub, n_blocks - 1)
        in_specs += [
            pl.BlockSpec((1, BLOCK, KV_WIDTH), lambda bi, i: (bi, prev_blk(bi, i), 0)),
            pl.BlockSpec((1, tile, KV_WIDTH), tok),
            pl.BlockSpec((1, BLOCK, KV_WIDTH), lambda bi, i: (bi, next_blk(bi, i), 0)),
            pl.BlockSpec((1, 1, KV_WIDTH, BLOCK), lambda bi, i: (bi, prev_blk(bi, i), 0, 0)),
            pl.BlockSpec((1, sub, KV_WIDTH, BLOCK), blk),
            pl.BlockSpec((1, 1, KV_WIDTH, BLOCK), lambda bi, i: (bi, next_blk(bi, i), 0, 0)),
        ]
        args += [k, k, k, vt, vt, vt]
    in_specs += [
        pl.BlockSpec((1, n_ctx, KV_WIDTH), per_b),
        pl.BlockSpec((1, n_ctx // BLOCK, KV_WIDTH, BLOCK), per_b4),
        pl.BlockSpec((1, MOD_ROWS, 3 * d), _layer(l, 3)),
        pl.BlockSpec(memory_space=pltpu.SMEM),
    ]
    args += [kc, vtc, mod, sinks]
    for w in (wsg, bsg, wpa, wpb, wpc, wout):
        in_specs.append(pl.BlockSpec((1,) + w.shape[1:], _layer(l, 3)))
        args.append(w)

    scratch = []
    if local:
        scratch += [pltpu.VMEM((tile + 2 * BLOCK, KV_WIDTH), BF16),
                    pltpu.VMEM((sub + 2, KV_WIDTH, BLOCK), BF16)]
    scratch += [
        pltpu.VMEM((tile, FOURIER_WIDTH), BF16),
        pltpu.VMEM((tile, SGU_WIDTH), BF16),
        pltpu.VMEM((tile, ATTN_WIDTH), BF16),
    ]
    kern = functools.partial(_mix_kernel, local=local, tile=tile, n_blocks=n_blocks,
                             ctx_row=ctx_row, layer=l)
    return pl.pallas_call(
        kern,
        grid=(b, nt),
        in_specs=in_specs,
        out_specs=pl.BlockSpec((1, tile, d), tok),
        out_shape=jax.ShapeDtypeStruct((b, length, d), F32),
        scratch_shapes=scratch,
        compiler_params=pltpu.CompilerParams(
            dimension_semantics=("arbitrary", "arbitrary"), vmem_limit_bytes=VMEM_LIMIT),
        name="mix_lat" if local else "mix_ctx",
    )(*args)


def _rope_tables(length):
    nf = HEAD_DIM // 4
    inv = jnp.tile(ROPE_THETA ** (-jnp.arange(nf, dtype=F32) / nf), LANES // nf)[None, :]
    pos = lax.broadcasted_iota(jnp.int32, (length, LANES), 0)
    lane = lax.broadcasted_iota(jnp.int32, (length, LANES), 1)
    axial = jnp.where((lane // (2 * nf)) % 2 == 0, pos // GRID_W, pos % GRID_W)
    ang = axial.astype(F32) * inv
    sign = jnp.where((lane // nf) % 2 == 0, -1.0, 1.0)
    return jnp.cos(ang), jnp.sin(ang) * sign


def _angle(num, den):
    return (num % den).astype(F32) * (2.0 * math.pi / den)


def _channel_dft(width):
    gw = FOURIER_WIDTH // FOURIER_GROUPS
    m = jnp.arange(width)
    same = (m[:, None] // gw) == (m[None, :] // gw)
    th = _angle((m[:, None] % gw) * (m[None, :] % gw), gw)
    cc = jnp.where(same, jnp.cos(th), 0.0) * gw ** -0.5
    sc = jnp.where(same, jnp.sin(th), 0.0) * gw ** -0.5
    return jnp.concatenate([cc, sc], axis=0).astype(BF16)


def _fourier_tables():
    l1n, l2n = DFT_L1, DFT_L2
    length = l1n * l2n
    pack = DFT_PACK
    l2 = jnp.arange(l2n)[:, None, None]
    k1 = jnp.arange(l1n)[None, :, None]
    l1 = jnp.arange(l1n)[None, None, :]
    th = _angle(k1 * l1 * l2n + k1 * l2, length)
    w1 = jnp.stack([jnp.cos(th), -jnp.sin(th)], axis=1) * l1n ** -0.5
    rows, cols = 2 * l1n * pack, l1n * pack
    ws = jnp.transpose(w1.reshape(l2n // pack, pack, 2 * l1n, l1n), (0, 2, 1, 3))
    ws = ws.reshape(l2n // pack, rows, l1n).astype(BF16)
    spread = (jnp.arange(cols)[None, :] // pack == jnp.arange(l1n)[:, None]).astype(BF16)
    same_p = (jnp.arange(rows)[:, None] % pack) == (jnp.arange(cols)[None, :] % pack)
    w1 = jnp.einsum("grb,bc->grc", ws, spread, preferred_element_type=F32)
    w1 = jnp.where(same_p[None], w1, 0.0).astype(BF16)
    k2 = jnp.arange(l2n)
    th2 = _angle(k2[:, None] * k2[None, :], l2n)
    c2, s2 = jnp.cos(th2), jnp.sin(th2)
    w2 = (jnp.concatenate([jnp.concatenate([c2, s2], axis=1),
                           jnp.concatenate([-s2, c2], axis=1)], axis=0) * l2n ** -0.5).astype(BF16)
    return w1, w2


def _fourier_ctx_table(n):
    k = jnp.arange(n)
    th = _angle(k[:, None] * k[None, :], n)
    return (jnp.concatenate([jnp.cos(th), -jnp.sin(th)], axis=0) * n ** -0.5).astype(BF16)


def _block_diag_ones(width):
    m = jnp.arange(width) // HEAD_DIM
    return (m[:, None] == m[None, :]).astype(BF16)


def _prep_w_in(w_in):
    col = jnp.arange(W_TOTAL)
    halved = ((col >= W_ZA) & (col < W_ZA + FOURIER_WIDTH)) | ((col >= W_ZB) & (col < W_Q)) | (col >= W_ZC)
    return (w_in * jnp.where(halved, 0.5, 1.0)).astype(BF16)


def kernel(x, c, ctx, c_ctx, w_ada, b_ada, w_in, sgu_w, sgu_b, q_norm_g, k_norm_g,
           attn_sink, w_pa, w_pb, w_pc, w_out):
    b, length, d = x.shape
    n_ctx = ctx.shape[1]
    assert length == DFT_L1 * DFT_L2 and d == D_MODEL and b + 1 <= MOD_ROWS
    assert w_in.shape == (DEPTH, d, W_TOTAL)
    lat_tile = 512
    ctx_row = b

    cs = jnp.zeros((MOD_ROWS, d), F32).at[:b].set(c).at[b].set(c_ctx)
    mod = _ada_call(cs, w_ada, b_ada)

    rope_tables = _rope_tables(length)
    w1, w2 = _fourier_tables()
    cs_dft = _channel_dft(FOURIER_WIDTH)
    wc = _fourier_ctx_table(n_ctx)
    bdq = _block_diag_ones(ATTN_WIDTH)
    bdk = _block_diag_ones(KV_WIDTH)

    w_all = _prep_w_in(w_in)
    gq_all = jnp.tile(q_norm_g, (1, N_HEADS))[:, None, :]
    gk_all = jnp.tile(k_norm_g, (1, N_KV_HEADS))[:, None, :]
    wsg = jnp.transpose(sgu_w, (0, 2, 1, 3)).reshape(DEPTH, CHUNK, SGU_GROUPS * CHUNK).astype(BF16)
    bsg = jnp.repeat(jnp.swapaxes(sgu_b, 1, 2), SGU_WIDTH // SGU_GROUPS, axis=2)
    wpa, wpb, wpc = w_pa.astype(BF16), w_pb.astype(BF16), w_pc.astype(BF16)
    wout = (0.5 * w_out).astype(BF16)

    xc = ctx
    for l in range(DEPTH):
        last = l == DEPTH - 1
        fa_c, main_c, qt_c, kc, vtc = _proj_call(
            xc.reshape(1, b * n_ctx, d), mod, w_all, bdq, bdk, gq_all, gk_all, None, l,
            tile=2 * n_ctx, ctx_row=ctx_row)
        kc = kc.reshape(b, n_ctx, KV_WIDTH)
        vtc = vtc.reshape(b, n_ctx // BLOCK, KV_WIDTH, BLOCK)
        fa, main, qt, k, vt = _proj_call(
            x, mod, w_all, bdq, bdk, gq_all, gk_all, rope_tables, l, tile=lat_tile, ctx_row=None)
        y = _fourier_call(fa, w1, w2, cs_dft)
        x = _mix_call(x, main, y, qt, k, vt, kc, vtc, mod, attn_sink, wsg, bsg, wpa, wpb, wpc, wout,
                      l, tile=lat_tile, local=True, ctx_row=None)
        if not last:
            y_c = _fourier_ctx_call(fa_c.reshape(b, n_ctx, FOURIER_WIDTH), wc, cs_dft)
            xc = _mix_call(xc, main_c.reshape(b, n_ctx, MAIN_WIDTH), y_c,
                           qt_c.reshape(b, n_ctx // BLOCK, ATTN_WIDTH, BLOCK), None, None, kc, vtc,
                           mod, attn_sink, wsg, bsg, wpa, wpb, wpc, wout, l,
                           tile=n_ctx, local=False, ctx_row=ctx_row)
    return x
```

```python
import functools
import math

import jax
import jax.numpy as jnp
from jax import lax
from jax.experimental import pallas as pl
from jax.experimental.pallas import tpu as pltpu

F32 = jnp.float32
BF16 = jnp.bfloat16

D_MODEL = 1024
DEPTH = 4
GRID_W = 64
FOURIER_WIDTH = 256
FOURIER_GROUPS = 4
SGU_WIDTH = 256
SGU_GROUPS = 4
CHUNK = 128
N_HEADS = 8
N_KV_HEADS = 2
HEAD_DIM = 64
ATTN_WIDTH = N_HEADS * HEAD_DIM
KV_WIDTH = N_KV_HEADS * HEAD_DIM
BLOCK = 128
ROPE_THETA = 10000.0
EPS = 1e-6
NEG_INF = -1e30
LOG2E = math.log2(math.e)

LANES = 128
MOD_ROWS = 8
ONES_ROWS = 16

W_FA = 0
W_ZA = W_FA + FOURIER_WIDTH
W_ZB = W_ZA + FOURIER_WIDTH + 2 * SGU_WIDTH
W_Q = W_ZB + SGU_WIDTH
W_K = W_Q + ATTN_WIDTH
W_V = W_K + KV_WIDTH
W_ZC = W_V + KV_WIDTH
W_G = W_ZC + ATTN_WIDTH
W_TOTAL = W_G + 3 * D_MODEL
MAIN_ZA = 0
MAIN_U = MAIN_ZA + FOURIER_WIDTH
MAIN_VS = MAIN_U + SGU_WIDTH
MAIN_ZB = MAIN_VS + SGU_WIDTH
MAIN_ZC = MAIN_ZB + SGU_WIDTH
MAIN_G = MAIN_ZC + ATTN_WIDTH
MAIN_WIDTH = MAIN_G + 3 * D_MODEL
PROJ_CHUNK = 512

DFT_L1 = 32
DFT_L2 = 256
DFT_PACK = 16
DFT1_GROUPS = 4
DFT2_GROUP = 8

VMEM_LIMIT = 56 * 1024 * 1024


def _silu(z):
    return 0.5 * z * (1.0 + jnp.tanh(0.5 * z))


def _silu_of_half(zh):
    return zh * (1.0 + jnp.tanh(zh))


def _dot(a, b):
    return jnp.dot(a, b, preferred_element_type=F32)


def _layer(l, rank):
    return lambda *_: (l,) + (0,) * (rank - 1)


def _ada_kernel(c_ref, w_ref, b_ref, o_ref):
    s = _silu(c_ref[...]).astype(BF16)
    o_ref[0] = _dot(s, w_ref[0].astype(BF16)) + b_ref[0]


def _ada_call(cs, w_ada, b_ada):
    d = D_MODEL
    return pl.pallas_call(
        _ada_kernel,
        grid=(DEPTH, 3),
        in_specs=[
            pl.BlockSpec((MOD_ROWS, d), lambda l, j: (0, 0)),
            pl.BlockSpec((1, d, d), lambda l, j: (l, 0, j)),
            pl.BlockSpec((1, 1, d), lambda l, j: (l, 0, j)),
        ],
        out_specs=pl.BlockSpec((1, MOD_ROWS, d), lambda l, j: (l, 0, j)),
        out_shape=jax.ShapeDtypeStruct((DEPTH, MOD_ROWS, 3 * d), F32),
        compiler_params=pltpu.CompilerParams(
            dimension_semantics=("arbitrary", "arbitrary"), vmem_limit_bytes=VMEM_LIMIT),
        name="ada_mod",
    )(cs, w_ada, b_ada.reshape(DEPTH, 1, 3 * d))


def _norm_rope(t, bd_ref, gain, cos, sin, out_scale):
    width = t.shape[-1]
    ssq = _dot((t * t).astype(BF16), bd_ref[...])
    tn = t * lax.rsqrt(ssq * (1.0 / HEAD_DIM) + EPS) * gain
    if cos is not None:
        reps = width // LANES
        cos_w = jnp.concatenate([cos] * reps, axis=1) if reps > 1 else cos
        sin_w = jnp.concatenate([sin] * reps, axis=1) if reps > 1 else sin
        lane = lax.broadcasted_iota(jnp.int32, tn.shape, 1)
        first = (lane % 32) < 16
        partner = jnp.where(first, pltpu.roll(tn, width - 16, 1), pltpu.roll(tn, 16, 1))
        tn = tn * cos_w + partner * sin_w
    if out_scale != 1.0:
        tn = tn * out_scale
    return tn


def _proj_kernel(x_ref, mod_ref, w_ref, bdq_ref, bdk_ref, gq_ref, gk_ref, cos_ref, sin_ref,
                 fa_ref, main_ref, qt_ref, k_ref, vt_ref, *, ctx_row, rope):
    d = D_MODEL
    nsub = x_ref.shape[1] // BLOCK
    xt = x_ref[0]
    ms = jnp.mean(xt * xt, axis=-1, keepdims=True)
    row = pl.program_id(0) if ctx_row is None else ctx_row
    modrow = mod_ref[0, pl.ds(row, 1), :]
    shift = modrow[:, 0:d]
    scale = modrow[:, d:2 * d]
    hb = ((xt * lax.rsqrt(ms + EPS)) * (1.0 + scale) + shift).astype(BF16)
    cos = cos_ref[...] if rope else None
    sin = sin_ref[...] if rope else None

    def proj(c0, width):
        return _dot(hb, w_ref[0, :, c0:c0 + width])

    chunks = [(src + c0, dst + c0)
              for src, dst, width in ((W_ZA, MAIN_ZA, W_Q - W_ZA), (W_ZC, MAIN_ZC, W_TOTAL - W_ZC))
              for c0 in range(0, width, PROJ_CHUNK)]

    def wide(count):
        for _ in range(min(count, len(chunks))):
            src, dst = chunks.pop(0)
            main_ref[0, :, dst:dst + PROJ_CHUNK] = proj(src, PROJ_CHUNK).astype(BF16)

    q_raw = proj(W_Q, ATTN_WIDTH)
    k_raw = proj(W_K, KV_WIDTH)
    v = proj(W_V, KV_WIDTH)
    wide(1)
    q = _norm_rope(q_raw, bdq_ref, gq_ref[0], cos, sin, HEAD_DIM ** -0.5 * LOG2E)
    wide(1)
    for s in range(nsub):
        qt_ref[0, s] = q[s * BLOCK:(s + 1) * BLOCK, :].T.astype(BF16)
    wide(1)
    k = _norm_rope(k_raw, bdk_ref, gk_ref[0], cos, sin, 1.0)
    k_ref[0] = k.astype(BF16)
    for s in range(nsub):
        vt_ref[0, s] = v[s * BLOCK:(s + 1) * BLOCK, :].T.astype(BF16)
    fa_ref[0] = proj(W_FA, FOURIER_WIDTH).astype(BF16)
    wide(len(chunks))


def _proj_call(x, mod, w_all, bdq, bdk, gq_all, gk_all, rope_tables, l, *, tile, ctx_row):
    b, length, d = x.shape
    nt = length // tile
    nsub = tile // BLOCK
    const = lambda bi, i: (0, 0)
    tok = lambda bi, i: (bi, i, 0)
    blk = lambda bi, i: (bi, i, 0, 0)
    rope = rope_tables is not None
    if rope:
        cos, sin = rope_tables
        pos, pos_rows = (lambda bi, i: (i, 0)), tile
    else:
        cos = sin = jnp.zeros((MOD_ROWS, LANES), F32)
        pos, pos_rows = const, MOD_ROWS
    kern = functools.partial(_proj_kernel, ctx_row=ctx_row, rope=rope)
    return pl.pallas_call(
        kern,
        grid=(b, nt),
        in_specs=[
            pl.BlockSpec((1, tile, d), tok),
            pl.BlockSpec((1, MOD_ROWS, 3 * d), _layer(l, 3)),
            pl.BlockSpec((1, d, W_TOTAL), _layer(l, 3), pipeline_mode=pl.Buffered(1)),
            pl.BlockSpec((ATTN_WIDTH, ATTN_WIDTH), const),
            pl.BlockSpec((KV_WIDTH, KV_WIDTH), const),
            pl.BlockSpec((1, 1, ATTN_WIDTH), _layer(l, 3)),
            pl.BlockSpec((1, 1, KV_WIDTH), _layer(l, 3)),
            pl.BlockSpec((pos_rows, LANES), pos),
            pl.BlockSpec((pos_rows, LANES), pos),
        ],
        out_specs=[
            pl.BlockSpec((1, tile, FOURIER_WIDTH), tok),
            pl.BlockSpec((1, tile, MAIN_WIDTH), tok),
            pl.BlockSpec((1, nsub, ATTN_WIDTH, BLOCK), blk),
            pl.BlockSpec((1, tile, KV_WIDTH), tok),
            pl.BlockSpec((1, nsub, KV_WIDTH, BLOCK), blk),
        ],
        out_shape=[
            jax.ShapeDtypeStruct((b, length, FOURIER_WIDTH), BF16),
            jax.ShapeDtypeStruct((b, length, MAIN_WIDTH), BF16),
            jax.ShapeDtypeStruct((b, length // BLOCK, ATTN_WIDTH, BLOCK), BF16),
            jax.ShapeDtypeStruct((b, length, KV_WIDTH), BF16),
            jax.ShapeDtypeStruct((b, length // BLOCK, KV_WIDTH, BLOCK), BF16),
        ],
        compiler_params=pltpu.CompilerParams(
            dimension_semantics=("arbitrary", "arbitrary"), vmem_limit_bytes=VMEM_LIMIT),
        name="proj_ctx" if ctx_row is not None else "proj_lat",
    )(x, mod, w_all, bdq, bdk, gq_all, gk_all, cos, sin)


def _dft1_kernel(x_ref, w1_ref, a_ref):
    l1n, pack, cw = DFT_L1, DFT_PACK, FOURIER_WIDTH
    for g in range(w1_ref.shape[0]):
        xg = x_ref[0, :, g].reshape(l1n * pack, cw)
        ag = _dot(w1_ref[g], xg).astype(BF16)
        a_ref[0, :, :, g] = ag.reshape(2, l1n, pack, cw)


def _dft2_kernel(a_ref, w2_ref, cs_ref, y_ref):
    cw, l2n = FOURIER_WIDTH, DFT_L2
    grp = a_ref.shape[2]
    rhs = jnp.concatenate(
        [jnp.concatenate([a_ref[0, 0, j], a_ref[0, 1, j]], axis=0) for j in range(grp)], axis=1)
    f = _dot(w2_ref[...], rhs)
    for j in range(grp):
        cols = slice(j * cw, (j + 1) * cw)
        fc = jnp.concatenate([f[:l2n, cols], f[l2n:, cols]], axis=1).astype(BF16)
        y_ref[0, :, cols] = _dot(fc, cs_ref[...]).astype(BF16)


def _fourier_call(fa, w1, w2, cs):
    b, length, cw = fa.shape
    l1n, l2n, pack = DFT_L1, DFT_L2, DFT_PACK
    params = pltpu.CompilerParams(
        dimension_semantics=("arbitrary", "arbitrary"), vmem_limit_bytes=VMEM_LIMIT)
    g1 = DFT1_GROUPS
    a = pl.pallas_call(
        _dft1_kernel,
        grid=(l2n // pack // g1, b),
        in_specs=[
            pl.BlockSpec((1, l1n, g1, pack, cw), lambda j, bi: (bi, 0, j, 0, 0)),
            pl.BlockSpec((g1, 2 * l1n * pack, l1n * pack), lambda j, bi: (j, 0, 0)),
        ],
        out_specs=pl.BlockSpec((1, 2, l1n, g1, pack, cw), lambda j, bi: (bi, 0, 0, j, 0, 0)),
        out_shape=jax.ShapeDtypeStruct((b, 2, l1n, l2n // pack, pack, cw), BF16),
        compiler_params=params,
        name="dft1_lat",
    )(fa.reshape(b, l1n, l2n // pack, pack, cw), w1)
    grp = DFT2_GROUP
    y = pl.pallas_call(
        _dft2_kernel,
        grid=(b, l1n // grp),
        in_specs=[
            pl.BlockSpec((1, 2, grp, l2n, cw), lambda bi, g: (bi, 0, g, 0, 0)),
            pl.BlockSpec(w2.shape, lambda bi, g: (0, 0)),
            pl.BlockSpec(cs.shape, lambda bi, g: (0, 0)),
        ],
        out_specs=pl.BlockSpec((1, l2n, grp * cw), lambda bi, g: (bi, 0, g)),
        out_shape=jax.ShapeDtypeStruct((b, l2n, l1n * cw), BF16),
        compiler_params=params,
        name="dft2_lat",
    )(a.reshape(b, 2, l1n, l2n, cw), w2, cs)
    return y


def _fourier_ctx_kernel(fa_ref, wc_ref, cs_ref, y_ref):
    n = fa_ref.shape[1]
    f = _dot(wc_ref[...], fa_ref[0])
    fc = jnp.concatenate([f[:n], f[n:]], axis=1).astype(BF16)
    y_ref[0] = _dot(fc, cs_ref[...])


def _fourier_ctx_call(fa, wc, cs):
    b, length, width = fa.shape
    return pl.pallas_call(
        _fourier_ctx_kernel,
        grid=(b,),
        in_specs=[
            pl.BlockSpec((1, length, width), lambda bi: (bi, 0, 0)),
            pl.BlockSpec(wc.shape, lambda bi: (0, 0)),
            pl.BlockSpec(cs.shape, lambda bi: (0, 0)),
        ],
        out_specs=pl.BlockSpec((1, length, width), lambda bi: (bi, 0, 0)),
        out_shape=jax.ShapeDtypeStruct((b, length, width), F32),
        compiler_params=pltpu.CompilerParams(
            dimension_semantics=("arbitrary",), vmem_limit_bytes=VMEM_LIMIT),
        name="fourier_ctx",
    )(fa, wc, cs)


def _mix_kernel(*refs, local, tile, n_blocks, ctx_row, layer):
    if local:
        (x_ref, main_ref, y_ref, perm_ref, qt_ref, kp_ref, km_ref, kn_ref, vtp_ref, vtm_ref, vtn_ref,
         kc_ref, vtc_ref, mod_ref, sink_ref, wsg_ref, bsg_ref,
         wpa_ref, wpb_ref, wpc_ref, wout_ref, o_ref,
         kbuf, vtbuf, ya_s, yb_s, yc_s) = refs
    else:
        (x_ref, main_ref, y_ref, qt_ref, kc_ref, vtc_ref, mod_ref, sink_ref, wsg_ref, bsg_ref,
         wpa_ref, wpb_ref, wpc_ref, wout_ref, o_ref,
         ya_s, yb_s, yc_s) = refs
    d = D_MODEL
    nsub = tile // BLOCK
    tile_idx = pl.program_id(1)
    heads_per_kv = N_HEADS // N_KV_HEADS
    cols_all = N_HEADS * BLOCK

    if local:
        kbuf[0:BLOCK, :] = kp_ref[0]
        kbuf[BLOCK:BLOCK + tile, :] = km_ref[0]
        kbuf[BLOCK + tile:2 * BLOCK + tile, :] = kn_ref[0]
        vtbuf[0] = vtp_ref[0, 0]
        for s in range(nsub):
            vtbuf[1 + s] = vtm_ref[0, s]
        vtbuf[1 + nsub] = vtn_ref[0, 0]

    def with_ones(vt):
        return jnp.concatenate([vt, jnp.ones((ONES_ROWS, vt.shape[1]), BF16)], axis=0)

    vt_ctx = with_ones(jnp.concatenate([vtc_ref[0, s] for s in range(vtc_ref.shape[1])], axis=1))

    if local:
        cw = FOURIER_WIDTH
        stacked = jnp.concatenate(
            [y_ref[0, :, k1 * cw:(k1 + 1) * cw] for k1 in range(DFT_L1)], axis=0)
        y = _dot(perm_ref[...], stacked)
    else:
        y = y_ref[0]
    ya_s[...] = (y * _silu_of_half(main_ref[0, :, MAIN_ZA:MAIN_ZA + FOURIER_WIDTH].astype(F32))).astype(BF16)

    sgu_grp = lax.broadcasted_iota(jnp.int32, (CHUNK, SGU_WIDTH), 1) // (SGU_WIDTH // SGU_GROUPS)
    zero_head = jnp.zeros((HEAD_DIM, BLOCK), BF16)
    sink_row = jnp.concatenate(
        [jnp.full((1, BLOCK), sink_ref[layer, hh] * LOG2E, F32) for hh in range(N_HEADS)], axis=1)
    kq_diff = (lax.broadcasted_iota(jnp.int32, (BLOCK, cols_all), 0)
               - lax.broadcasted_iota(jnp.int32, (BLOCK, cols_all), 1) % BLOCK)

    def sgu(n):
        rows = slice(n * BLOCK, (n + 1) * BLOCK)
        vs = main_ref[0, rows, MAIN_VS:MAIN_VS + SGU_WIDTH].astype(F32)
        vn = (vs * lax.rsqrt(jnp.mean(vs * vs, axis=-1, keepdims=True) + EPS)).astype(BF16)
        rhs = jnp.concatenate(
            [jnp.where(sgu_grp == g, vn, jnp.zeros_like(vn)) for g in range(SGU_GROUPS)], axis=0)
        mixed = _dot(wsg_ref[0], rhs) + bsg_ref[0]
        u = main_ref[0, rows, MAIN_U:MAIN_U + SGU_WIDTH].astype(F32)
        zb = main_ref[0, rows, MAIN_ZB:MAIN_ZB + SGU_WIDTH].astype(F32)
        yb_s[rows, :] = (u * mixed * _silu_of_half(zb)).astype(BF16)

    def scores(n):
        qt = qt_ref[0, n]
        cols = []
        for hh in range(N_HEADS):
            qh = qt[hh * HEAD_DIM:(hh + 1) * HEAD_DIM, :]
            cols.append(jnp.concatenate(
                [qh, zero_head] if hh < heads_per_kv else [zero_head, qh], axis=0))
        qst = jnp.concatenate(cols, axis=1)

        s_ctx = _dot(kc_ref[0], qst)
        s_loc = _dot(kbuf[n * BLOCK:(n + 3) * BLOCK, :], qst) if local else None
        return s_ctx, s_loc

    def attend(n, s_ctx, s_loc):
        rows = slice(n * BLOCK, (n + 1) * BLOCK)
        m = jnp.maximum(jnp.max(s_ctx, axis=0, keepdims=True), sink_row)
        if local:
            gblk = tile_idx * nsub + n
            s_prev = jnp.where(kq_diff >= jnp.where(gblk > 0, 0, BLOCK), s_loc[0:BLOCK], NEG_INF)
            s_own = s_loc[BLOCK:2 * BLOCK]
            s_next = jnp.where(kq_diff <= jnp.where(gblk < n_blocks - 1, 0, -BLOCK),
                               s_loc[2 * BLOCK:3 * BLOCK], NEG_INF)
            m_loc = jnp.max(jnp.maximum(jnp.maximum(s_prev, s_own), s_next), axis=0, keepdims=True)
            m = jnp.maximum(m, m_loc)
        e_ctx = jnp.exp2(s_ctx - m)
        ot = _dot(vt_ctx, e_ctx.astype(BF16))
        if local:
            e_loc = jnp.concatenate(
                [jnp.exp2(s_prev - m), jnp.exp2(s_own - m), jnp.exp2(s_next - m)], axis=0)
            vt_loc = with_ones(jnp.concatenate([vtbuf[n], vtbuf[n + 1], vtbuf[n + 2]], axis=1))
            ot = ot + _dot(vt_loc, e_loc.astype(BF16))
        den = ot[KV_WIDTH:KV_WIDTH + 1] + jnp.exp2(sink_row - m)
        ot = ot[0:KV_WIDTH] * (1.0 / den)
        for j in range(ATTN_WIDTH // LANES):
            h = (2 * j) // heads_per_kv
            pair = jnp.concatenate(
                [ot[h * HEAD_DIM:(h + 1) * HEAD_DIM, (2 * j) * BLOCK:(2 * j + 1) * BLOCK],
                 ot[h * HEAD_DIM:(h + 1) * HEAD_DIM, (2 * j + 1) * BLOCK:(2 * j + 2) * BLOCK]], axis=0)
            zc = main_ref[0, rows, MAIN_ZC + j * LANES:MAIN_ZC + (j + 1) * LANES].astype(F32)
            yc_s[rows, j * LANES:(j + 1) * LANES] = (pair.T * _silu_of_half(zc)).astype(BF16)

    ahead = scores(0)
    for n in range(nsub):
        sgu(n)
        current, ahead = ahead, (scores(n + 1) if n + 1 < nsub else None)
        attend(n, *current)

    t = jnp.tanh(main_ref[0, :, MAIN_G:MAIN_G + 3 * d].astype(F32))
    ma = _dot(ya_s[...], wpa_ref[0])
    mb = _dot(yb_s[...], wpb_ref[0])
    mc = _dot(yc_s[...], wpc_ref[0])
    merged2 = (ma + t[:, 0:d] * ma) + (mb + t[:, d:2 * d] * mb) + (mc + t[:, 2 * d:3 * d] * mc)
    out = _dot(merged2.astype(BF16), wout_ref[0])
    row = pl.program_id(0) if ctx_row is None else ctx_row
    gate = mod_ref[0, pl.ds(row, 1), 2 * d:3 * d]
    o_ref[0] = x_ref[0] + gate * out


def _mix_call(x, main, y, qt, k, vt, kc, vtc, mod, sinks, wsg, bsg, wpa, wpb, wpc, wout, l,
              *, tile, local, ctx_row):
    b, length, d = x.shape
    nt = length // tile
    n_ctx = kc.shape[1]
    sub = tile // BLOCK
    n_blocks = length // BLOCK
    tok = lambda bi, i: (bi, i, 0)
    blk = lambda bi, i: (bi, i, 0, 0)
    per_b = lambda bi, i: (bi, 0, 0)
    per_b4 = lambda bi, i: (bi, 0, 0, 0)

    in_specs = [
        pl.BlockSpec((1, tile, d), tok),
        pl.BlockSpec((1, tile, MAIN_WIDTH), tok),
    ]
    args = [x, main, y]
    if local:
        rows_k2 = tile // DFT_L1
        r = jnp.arange(tile)
        perm = (r[:, None] == (r[None, :] % rows_k2) * DFT_L1 + r[None, :] // rows_k2).astype(BF16)
        in_specs += [pl.BlockSpec((1, rows_k2, DFT_L1 * FOURIER_WIDTH), tok),
                     pl.BlockSpec((tile, tile), lambda bi, i: (0, 0))]
        args.append(perm)
    else:
        in_specs.append(pl.BlockSpec((1, tile, FOURIER_WIDTH), tok))
    in_specs.append(pl.BlockSpec((1, sub, ATTN_WIDTH, BLOCK), blk))
    args.append(qt)
    if local:
        prev_blk = lambda bi, i: jnp.maximum(i * sub - 1, 0)
        next_blk = lambda bi, i: jnp.minimum((i + 1) * sub, n_blocks - 1)
        in_specs += [
            pl.BlockSpec((1, BLOCK, KV_WIDTH), lambda bi, i: (bi, prev_blk(bi, i), 0)),
            pl.BlockSpec((1, tile, KV_WIDTH), tok),
            pl.BlockSpec((1, BLOCK, KV_WIDTH), lambda bi, i: (bi, next_blk(bi, i), 0)),
            pl.BlockSpec((1, 1, KV_WIDTH, BLOCK), lambda bi, i: (bi, prev_blk(bi, i), 0, 0)),
            pl.BlockSpec((1, sub, KV_WIDTH, BLOCK), blk),
            pl.BlockSpec((1, 1, KV_WIDTH, BLOCK), lambda bi, i: (bi, next_blk(bi, i), 0, 0)),
        ]
        args += [k, k, k, vt, vt, vt]
    in_specs += [
        pl.BlockSpec((1, n_ctx, KV_WIDTH), per_b),
        pl.BlockSpec((1, n_ctx // BLOCK, KV_WIDTH, BLOCK), per_b4),
        pl.BlockSpec((1, MOD_ROWS, 3 * d), _layer(l, 3)),
        pl.BlockSpec(memory_space=pltpu.SMEM),
    ]
    args += [kc, vtc, mod, sinks]
    for w in (wsg, bsg, wpa, wpb, wpc, wout):
        in_specs.append(pl.BlockSpec((1,) + w.shape[1:], _layer(l, 3)))
        args.append(w)

    scratch = []
    if local:
        scratch += [pltpu.VMEM((tile + 2 * BLOCK, KV_WIDTH), BF16),
                    pltpu.VMEM((sub + 2, KV_WIDTH, BLOCK), BF16)]
    scratch += [
        pltpu.VMEM((tile, FOURIER_WIDTH), BF16),
        pltpu.VMEM((tile, SGU_WIDTH), BF16),
        pltpu.VMEM((tile, ATTN_WIDTH), BF16),
    ]
    kern = functools.partial(_mix_kernel, local=local, tile=tile, n_blocks=n_blocks,
                             ctx_row=ctx_row, layer=l)
    return pl.pallas_call(
        kern,
        grid=(b, nt),
        in_specs=in_specs,
        out_specs=pl.BlockSpec((1, tile, d), tok),
        out_shape=jax.ShapeDtypeStruct((b, length, d), F32),
        scratch_shapes=scratch,
        compiler_params=pltpu.CompilerParams(
            dimension_semantics=("arbitrary", "arbitrary"), vmem_limit_bytes=VMEM_LIMIT),
        name="mix_lat" if local else "mix_ctx",
    )(*args)


def _rope_tables(length):
    nf = HEAD_DIM // 4
    inv = jnp.tile(ROPE_THETA ** (-jnp.arange(nf, dtype=F32) / nf), LANES // nf)[None, :]
    pos = lax.broadcasted_iota(jnp.int32, (length, LANES), 0)
    lane = lax.broadcasted_iota(jnp.int32, (length, LANES), 1)
    axial = jnp.where((lane // (2 * nf)) % 2 == 0, pos // GRID_W, pos % GRID_W)
    ang = axial.astype(F32) * inv
    sign = jnp.where((lane // nf) % 2 == 0, -1.0, 1.0)
    return jnp.cos(ang), jnp.sin(ang) * sign


def _angle(num, den):
    return (num % den).astype(F32) * (2.0 * math.pi / den)


def _channel_dft(width):
    gw = FOURIER_WIDTH // FOURIER_GROUPS
    m = jnp.arange(width)
    same = (m[:, None] // gw) == (m[None, :] // gw)
    th = _angle((m[:, None] % gw) * (m[None, :] % gw), gw)
    cc = jnp.where(same, jnp.cos(th), 0.0) * gw ** -0.5
    sc = jnp.where(same, jnp.sin(th), 0.0) * gw ** -0.5
    return jnp.concatenate([cc, sc], axis=0).astype(BF16)


def _fourier_tables():
    l1n, l2n = DFT_L1, DFT_L2
    length = l1n * l2n
    pack = DFT_PACK
    l2 = jnp.arange(l2n)[:, None, None]
    k1 = jnp.arange(l1n)[None, :, None]
    l1 = jnp.arange(l1n)[None, None, :]
    th = _angle(k1 * l1 * l2n + k1 * l2, length)
    w1 = jnp.stack([jnp.cos(th), -jnp.sin(th)], axis=1) * l1n ** -0.5
    rows, cols = 2 * l1n * pack, l1n * pack
    ws = jnp.transpose(w1.reshape(l2n // pack, pack, 2 * l1n, l1n), (0, 2, 1, 3))
    ws = ws.reshape(l2n // pack, rows, l1n).astype(BF16)
    spread = (jnp.arange(cols)[None, :] // pack == jnp.arange(l1n)[:, None]).astype(BF16)
    same_p = (jnp.arange(rows)[:, None] % pack) == (jnp.arange(cols)[None, :] % pack)
    w1 = jnp.einsum("grb,bc->grc", ws, spread, preferred_element_type=F32)
    w1 = jnp.where(same_p[None], w1, 0.0).astype(BF16)
    k2 = jnp.arange(l2n)
    th2 = _angle(k2[:, None] * k2[None, :], l2n)
    c2, s2 = jnp.cos(th2), jnp.sin(th2)
    w2 = (jnp.concatenate([jnp.concatenate([c2, s2], axis=1),
                           jnp.concatenate([-s2, c2], axis=1)], axis=0) * l2n ** -0.5).astype(BF16)
    return w1, w2


def _fourier_ctx_table(n):
    k = jnp.arange(n)
    th = _angle(k[:, None] * k[None, :], n)
    return (jnp.concatenate([jnp.cos(th), -jnp.sin(th)], axis=0) * n ** -0.5).astype(BF16)


def _block_diag_ones(width):
    m = jnp.arange(width) // HEAD_DIM
    return (m[:, None] == m[None, :]).astype(BF16)


def _prep_w_in(w_in):
    col = jnp.arange(W_TOTAL)
    halved = ((col >= W_ZA) & (col < W_ZA + FOURIER_WIDTH)) | ((col >= W_ZB) & (col < W_Q)) | (col >= W_ZC)
    return (w_in * jnp.where(halved, 0.5, 1.0)).astype(BF16)


def kernel(x, c, ctx, c_ctx, w_ada, b_ada, w_in, sgu_w, sgu_b, q_norm_g, k_norm_g,
           attn_sink, w_pa, w_pb, w_pc, w_out):
    b, length, d = x.shape
    n_ctx = ctx.shape[1]
    assert length == DFT_L1 * DFT_L2 and d == D_MODEL and b + 1 <= MOD_ROWS
    assert w_in.shape == (DEPTH, d, W_TOTAL)
    lat_tile = 512
    ctx_row = b

    cs = jnp.zeros((MOD_ROWS, d), F32).at[:b].set(c).at[b].set(c_ctx)
    mod = _ada_call(cs, w_ada, b_ada)

    rope_tables = _rope_tables(length)
    w1, w2 = _fourier_tables()
    cs_dft = _channel_dft(FOURIER_WIDTH)
    wc = _fourier_ctx_table(n_ctx)
    bdq = _block_diag_ones(ATTN_WIDTH)
    bdk = _block_diag_ones(KV_WIDTH)

    w_all = _prep_w_in(w_in)
    gq_all = jnp.tile(q_norm_g, (1, N_HEADS))[:, None, :]
    gk_all = jnp.tile(k_norm_g, (1, N_KV_HEADS))[:, None, :]
    wsg = jnp.transpose(sgu_w, (0, 2, 1, 3)).reshape(DEPTH, CHUNK, SGU_GROUPS * CHUNK).astype(BF16)
    bsg = jnp.repeat(jnp.swapaxes(sgu_b, 1, 2), SGU_WIDTH // SGU_GROUPS, axis=2)
    wpa, wpb, wpc = w_pa.astype(BF16), w_pb.astype(BF16), w_pc.astype(BF16)
    wout = (0.5 * w_out).astype(BF16)

    xc = ctx
    for l in range(DEPTH):
        last = l == DEPTH - 1
        fa_c, main_c, qt_c, kc, vtc = _proj_call(
            xc.reshape(1, b * n_ctx, d), mod, w_all, bdq, bdk, gq_all, gk_all, None, l,
            tile=2 * n_ctx, ctx_row=ctx_row)
        kc = kc.reshape(b, n_ctx, KV_WIDTH)
        vtc = vtc.reshape(b, n_ctx // BLOCK, KV_WIDTH, BLOCK)
        fa, main, qt, k, vt = _proj_call(
            x, mod, w_all, bdq, bdk, gq_all, gk_all, rope_tables, l, tile=lat_tile, ctx_row=None)
        y = _fourier_call(fa, w1, w2, cs_dft)
        x = _mix_call(x, main, y, qt, k, vt, kc, vtc, mod, attn_sink, wsg, bsg, wpa, wpb, wpc, wout,
                      l, tile=lat_tile, local=True, ctx_row=None)
        if not last:
            y_c = _fourier_ctx_call(fa_c.reshape(b, n_ctx, FOURIER_WIDTH), wc, cs_dft)
            xc = _mix_call(xc, main_c.reshape(b, n_ctx, MAIN_WIDTH), y_c,
                           qt_c.reshape(b, n_ctx // BLOCK, ATTN_WIDTH, BLOCK), None, None, kc, vtc,
                           mod, attn_sink, wsg, bsg, wpa, wpb, wpc, wout, l,
                           tile=n_ctx, local=False, ctx_row=ctx_row)
    return x
```

```python
import functools
import math

import jax
import jax.numpy as jnp
from jax import lax
from jax.experimental import pallas as pl
from jax.experimental.pallas import tpu as pltpu

F32 = jnp.float32
BF16 = jnp.bfloat16

D_MODEL = 1024
DEPTH = 4
GRID_W = 64
FOURIER_WIDTH = 256
FOURIER_GROUPS = 4
SGU_WIDTH = 256
SGU_GROUPS = 4
CHUNK = 128
N_HEADS = 8
N_KV_HEADS = 2
HEAD_DIM = 64
ATTN_WIDTH = N_HEADS * HEAD_DIM
KV_WIDTH = N_KV_HEADS * HEAD_DIM
BLOCK = 128
ROPE_THETA = 10000.0
EPS = 1e-6
NEG_INF = -1e30
LOG2E = math.log2(math.e)

LANES = 128
MOD_ROWS = 8
ONES_ROWS = 16

W_FA = 0
W_ZA = W_FA + FOURIER_WIDTH
W_ZB = W_ZA + FOURIER_WIDTH + 2 * SGU_WIDTH
W_Q = W_ZB + SGU_WIDTH
W_K = W_Q + ATTN_WIDTH
W_V = W_K + KV_WIDTH
W_ZC = W_V + KV_WIDTH
W_G = W_ZC + ATTN_WIDTH
W_TOTAL = W_G + 3 * D_MODEL
MAIN_ZA = 0
MAIN_U = MAIN_ZA + FOURIER_WIDTH
MAIN_VS = MAIN_U + SGU_WIDTH
MAIN_ZB = MAIN_VS + SGU_WIDTH
MAIN_ZC = MAIN_ZB + SGU_WIDTH
MAIN_G = MAIN_ZC + ATTN_WIDTH
MAIN_WIDTH = MAIN_G + 3 * D_MODEL
PROJ_CHUNK = 512

DFT_L1 = 32
DFT_L2 = 256
DFT_PACK = 16
DFT2_GROUP = 8

VMEM_LIMIT = 56 * 1024 * 1024


def _silu(z):
    return 0.5 * z * (1.0 + jnp.tanh(0.5 * z))


def _silu_of_half(zh):
    return zh * (1.0 + jnp.tanh(zh))


def _dot(a, b):
    return jnp.dot(a, b, preferred_element_type=F32)


def _layer(l, rank):
    return lambda *_: (l,) + (0,) * (rank - 1)


def _ada_kernel(c_ref, w_ref, b_ref, o_ref):
    s = _silu(c_ref[...]).astype(BF16)
    o_ref[0] = _dot(s, w_ref[0].astype(BF16)) + b_ref[0]


def _ada_call(cs, w_ada, b_ada):
    d = D_MODEL
    return pl.pallas_call(
        _ada_kernel,
        grid=(DEPTH, 3),
        in_specs=[
            pl.BlockSpec((MOD_ROWS, d), lambda l, j: (0, 0)),
            pl.BlockSpec((1, d, d), lambda l, j: (l, 0, j)),
            pl.BlockSpec((1, 1, d), lambda l, j: (l, 0, j)),
        ],
        out_specs=pl.BlockSpec((1, MOD_ROWS, d), lambda l, j: (l, 0, j)),
        out_shape=jax.ShapeDtypeStruct((DEPTH, MOD_ROWS, 3 * d), F32),
        compiler_params=pltpu.CompilerParams(
            dimension_semantics=("arbitrary", "arbitrary"), vmem_limit_bytes=VMEM_LIMIT),
        name="ada_mod",
    )(cs, w_ada, b_ada.reshape(DEPTH, 1, 3 * d))


def _norm_rope(t, bd_ref, gain, cos, sin, out_scale):
    width = t.shape[-1]
    ssq = _dot((t * t).astype(BF16), bd_ref[...])
    tn = t * lax.rsqrt(ssq * (1.0 / HEAD_DIM) + EPS) * gain
    if cos is not None:
        reps = width // LANES
        cos_w = jnp.concatenate([cos] * reps, axis=1) if reps > 1 else cos
        sin_w = jnp.concatenate([sin] * reps, axis=1) if reps > 1 else sin
        lane = lax.broadcasted_iota(jnp.int32, tn.shape, 1)
        first = (lane % 32) < 16
        partner = jnp.where(first, pltpu.roll(tn, width - 16, 1), pltpu.roll(tn, 16, 1))
        tn = tn * cos_w + partner * sin_w
    if out_scale != 1.0:
        tn = tn * out_scale
    return tn


def _proj_kernel(x_ref, mod_ref, w_ref, bdq_ref, bdk_ref, gq_ref, gk_ref, cos_ref, sin_ref,
                 fa_ref, main_ref, qt_ref, k_ref, vt_ref, *, ctx_row, rope):
    d = D_MODEL
    nsub = x_ref.shape[1] // BLOCK
    xt = x_ref[0]
    ms = jnp.mean(xt * xt, axis=-1, keepdims=True)
    row = pl.program_id(0) if ctx_row is None else ctx_row
    modrow = mod_ref[0, pl.ds(row, 1), :]
    shift = modrow[:, 0:d]
    scale = modrow[:, d:2 * d]
    hb = ((xt * lax.rsqrt(ms + EPS)) * (1.0 + scale) + shift).astype(BF16)
    cos = cos_ref[...] if rope else None
    sin = sin_ref[...] if rope else None

    def proj(c0, width):
        return _dot(hb, w_ref[0, :, c0:c0 + width])

    chunks = [(src + c0, dst + c0)
              for src, dst, width in ((W_ZA, MAIN_ZA, W_Q - W_ZA), (W_ZC, MAIN_ZC, W_TOTAL - W_ZC))
              for c0 in range(0, width, PROJ_CHUNK)]

    def wide(count):
        for _ in range(min(count, len(chunks))):
            src, dst = chunks.pop(0)
            main_ref[0, :, dst:dst + PROJ_CHUNK] = proj(src, PROJ_CHUNK).astype(BF16)

    q_raw = proj(W_Q, ATTN_WIDTH)
    k_raw = proj(W_K, KV_WIDTH)
    v = proj(W_V, KV_WIDTH)
    wide(1)
    q = _norm_rope(q_raw, bdq_ref, gq_ref[0], cos, sin, HEAD_DIM ** -0.5 * LOG2E)
    wide(1)
    for s in range(nsub):
        qt_ref[0, s] = q[s * BLOCK:(s + 1) * BLOCK, :].T.astype(BF16)
    wide(1)
    k = _norm_rope(k_raw, bdk_ref, gk_ref[0], cos, sin, 1.0)
    k_ref[0] = k.astype(BF16)
    for s in range(nsub):
        vt_ref[0, s] = v[s * BLOCK:(s + 1) * BLOCK, :].T.astype(BF16)
    fa_ref[0] = proj(W_FA, FOURIER_WIDTH).astype(BF16)
    wide(len(chunks))


def _proj_call(x, mod, w_all, bdq, bdk, gq_all, gk_all, rope_tables, l, *, tile, ctx_row):
    b, length, d = x.shape
    nt = length // tile
    nsub = tile // BLOCK
    const = lambda bi, i: (0, 0)
    tok = lambda bi, i: (bi, i, 0)
    blk = lambda bi, i: (bi, i, 0, 0)
    rope = rope_tables is not None
    if rope:
        cos, sin = rope_tables
        pos, pos_rows = (lambda bi, i: (i, 0)), tile
    else:
        cos = sin = jnp.zeros((MOD_ROWS, LANES), F32)
        pos, pos_rows = const, MOD_ROWS
    kern = functools.partial(_proj_kernel, ctx_row=ctx_row, rope=rope)
    return pl.pallas_call(
        kern,
        grid=(b, nt),
        in_specs=[
            pl.BlockSpec((1, tile, d), tok),
            pl.BlockSpec((1, MOD_ROWS, 3 * d), _layer(l, 3)),
            pl.BlockSpec((1, d, W_TOTAL), _layer(l, 3), pipeline_mode=pl.Buffered(1)),
            pl.BlockSpec((ATTN_WIDTH, ATTN_WIDTH), const),
            pl.BlockSpec((KV_WIDTH, KV_WIDTH), const),
            pl.BlockSpec((1, 1, ATTN_WIDTH), _layer(l, 3)),
            pl.BlockSpec((1, 1, KV_WIDTH), _layer(l, 3)),
            pl.BlockSpec((pos_rows, LANES), pos),
            pl.BlockSpec((pos_rows, LANES), pos),
        ],
        out_specs=[
            pl.BlockSpec((1, tile, FOURIER_WIDTH), tok),
            pl.BlockSpec((1, tile, MAIN_WIDTH), tok),
            pl.BlockSpec((1, nsub, ATTN_WIDTH, BLOCK), blk),
            pl.BlockSpec((1, tile, KV_WIDTH), tok),
            pl.BlockSpec((1, nsub, KV_WIDTH, BLOCK), blk),
        ],
        out_shape=[
            jax.ShapeDtypeStruct((b, length, FOURIER_WIDTH), BF16),
            jax.ShapeDtypeStruct((b, length, MAIN_WIDTH), BF16),
            jax.ShapeDtypeStruct((b, length // BLOCK, ATTN_WIDTH, BLOCK), BF16),
            jax.ShapeDtypeStruct((b, length, KV_WIDTH), BF16),
            jax.ShapeDtypeStruct((b, length // BLOCK, KV_WIDTH, BLOCK), BF16),
        ],
        compiler_params=pltpu.CompilerParams(
            dimension_semantics=("arbitrary", "arbitrary"), vmem_limit_bytes=VMEM_LIMIT),
        name="proj_ctx" if ctx_row is not None else "proj_lat",
    )(x, mod, w_all, bdq, bdk, gq_all, gk_all, cos, sin)


def _fourier_kernel(x_ref, w1_ref, w2_ref, cs_ref, y_ref, a_s):
    l1n, l2n, pack, cw = DFT_L1, DFT_L2, DFT_PACK, FOURIER_WIDTH
    for g in range(l2n // pack):
        xg = x_ref[0, :, g].reshape(l1n * pack, cw)
        ag = _dot(w1_ref[g], xg).astype(BF16)
        a_s[:, :, g * pack:(g + 1) * pack, :] = ag.reshape(2, l1n, pack, cw)
    grp = DFT2_GROUP
    for g in range(l1n // grp):
        rhs = jnp.concatenate(
            [jnp.concatenate([a_s[0, g * grp + j], a_s[1, g * grp + j]], axis=0)
             for j in range(grp)], axis=1)
        f = _dot(w2_ref[...], rhs)
        for j in range(grp):
            cols = slice(j * cw, (j + 1) * cw)
            fc = jnp.concatenate([f[:l2n, cols], f[l2n:, cols]], axis=1).astype(BF16)
            out_cols = slice((g * grp + j) * cw, (g * grp + j + 1) * cw)
            y_ref[0, :, out_cols] = _dot(fc, cs_ref[...]).astype(BF16)


def _fourier_call(fa, w1, w2, cs):
    b, length, cw = fa.shape
    l1n, l2n, pack = DFT_L1, DFT_L2, DFT_PACK
    const = lambda bi: (0, 0)
    return pl.pallas_call(
        _fourier_kernel,
        grid=(b,),
        in_specs=[
            pl.BlockSpec((1, l1n, l2n // pack, pack, cw), lambda bi: (bi, 0, 0, 0, 0)),
            pl.BlockSpec(w1.shape, lambda bi: (0, 0, 0), pipeline_mode=pl.Buffered(1)),
            pl.BlockSpec(w2.shape, const),
            pl.BlockSpec(cs.shape, const),
        ],
        out_specs=pl.BlockSpec((1, l2n, l1n * cw), lambda bi: (bi, 0, 0)),
        out_shape=jax.ShapeDtypeStruct((b, l2n, l1n * cw), BF16),
        scratch_shapes=[pltpu.VMEM((2, l1n, l2n, cw), BF16)],
        compiler_params=pltpu.CompilerParams(
            dimension_semantics=("arbitrary",), vmem_limit_bytes=VMEM_LIMIT),
        name="fourier_lat",
    )(fa.reshape(b, l1n, l2n // pack, pack, cw), w1, w2, cs)


def _fourier_ctx_kernel(fa_ref, wc_ref, cs_ref, y_ref):
    n = fa_ref.shape[1]
    f = _dot(wc_ref[...], fa_ref[0])
    fc = jnp.concatenate([f[:n], f[n:]], axis=1).astype(BF16)
    y_ref[0] = _dot(fc, cs_ref[...])


def _fourier_ctx_call(fa, wc, cs):
    b, length, width = fa.shape
    return pl.pallas_call(
        _fourier_ctx_kernel,
        grid=(b,),
        in_specs=[
            pl.BlockSpec((1, length, width), lambda bi: (bi, 0, 0)),
            pl.BlockSpec(wc.shape, lambda bi: (0, 0)),
            pl.BlockSpec(cs.shape, lambda bi: (0, 0)),
        ],
        out_specs=pl.BlockSpec((1, length, width), lambda bi: (bi, 0, 0)),
        out_shape=jax.ShapeDtypeStruct((b, length, width), F32),
        compiler_params=pltpu.CompilerParams(
            dimension_semantics=("arbitrary",), vmem_limit_bytes=VMEM_LIMIT),
        name="fourier_ctx",
    )(fa, wc, cs)


def _mix_kernel(*refs, local, tile, n_blocks, ctx_row, layer):
    if local:
        (x_ref, main_ref, y_ref, perm_ref, qt_ref, kp_ref, km_ref, kn_ref, vtp_ref, vtm_ref, vtn_ref,
         kc_ref, vtc_ref, mod_ref, sink_ref, wsg_ref, bsg_ref,
         wpa_ref, wpb_ref, wpc_ref, wout_ref, o_ref,
         kbuf, vtbuf, ya_s, yb_s, yc_s) = refs
    else:
        (x_ref, main_ref, y_ref, qt_ref, kc_ref, vtc_ref, mod_ref, sink_ref, wsg_ref, bsg_ref,
         wpa_ref, wpb_ref, wpc_ref, wout_ref, o_ref,
         ya_s, yb_s, yc_s) = refs
    d = D_MODEL
    nsub = tile // BLOCK
    tile_idx = pl.program_id(1)
    heads_per_kv = N_HEADS // N_KV_HEADS
    cols_all = N_HEADS * BLOCK

    if local:
        kbuf[0:BLOCK, :] = kp_ref[0]
        kbuf[BLOCK:BLOCK + tile, :] = km_ref[0]
        kbuf[BLOCK + tile:2 * BLOCK + tile, :] = kn_ref[0]
        vtbuf[0] = vtp_ref[0, 0]
        for s in range(nsub):
            vtbuf[1 + s] = vtm_ref[0, s]
        vtbuf[1 + nsub] = vtn_ref[0, 0]

    def with_ones(vt):
        return jnp.concatenate([vt, jnp.ones((ONES_ROWS, vt.shape[1]), BF16)], axis=0)

    vt_ctx = with_ones(jnp.concatenate([vtc_ref[0, s] for s in range(vtc_ref.shape[1])], axis=1))

    if local:
        cw = FOURIER_WIDTH
        stacked = jnp.concatenate(
            [y_ref[0, :, k1 * cw:(k1 + 1) * cw] for k1 in range(DFT_L1)], axis=0)
        y = _dot(perm_ref[...], stacked)
    else:
        y = y_ref[0]
    ya_s[...] = (y * _silu_of_half(main_ref[0, :, MAIN_ZA:MAIN_ZA + FOURIER_WIDTH].astype(F32))).astype(BF16)

    sgu_grp = lax.broadcasted_iota(jnp.int32, (CHUNK, SGU_WIDTH), 1) // (SGU_WIDTH // SGU_GROUPS)
    zero_head = jnp.zeros((HEAD_DIM, BLOCK), BF16)
    sink_row = jnp.concatenate(
        [jnp.full((1, BLOCK), sink_ref[layer, hh] * LOG2E, F32) for hh in range(N_HEADS)], axis=1)
    kq_diff = (lax.broadcasted_iota(jnp.int32, (BLOCK, cols_all), 0)
               - lax.broadcasted_iota(jnp.int32, (BLOCK, cols_all), 1) % BLOCK)

    def sgu(n):
        rows = slice(n * BLOCK, (n + 1) * BLOCK)
        vs = main_ref[0, rows, MAIN_VS:MAIN_VS + SGU_WIDTH].astype(F32)
        vn = (vs * lax.rsqrt(jnp.mean(vs * vs, axis=-1, keepdims=True) + EPS)).astype(BF16)
        rhs = jnp.concatenate(
            [jnp.where(sgu_grp == g, vn, jnp.zeros_like(vn)) for g in range(SGU_GROUPS)], axis=0)
        mixed = _dot(wsg_ref[0], rhs) + bsg_ref[0]
        u = main_ref[0, rows, MAIN_U:MAIN_U + SGU_WIDTH].astype(F32)
        zb = main_ref[0, rows, MAIN_ZB:MAIN_ZB + SGU_WIDTH].astype(F32)
        yb_s[rows, :] = (u * mixed * _silu_of_half(zb)).astype(BF16)

    def scores(n):
        qt = qt_ref[0, n]
        cols = []
        for hh in range(N_HEADS):
            qh = qt[hh * HEAD_DIM:(hh + 1) * HEAD_DIM, :]
            cols.append(jnp.concatenate(
                [qh, zero_head] if hh < heads_per_kv else [zero_head, qh], axis=0))
        qst = jnp.concatenate(cols, axis=1)

        s_ctx = _dot(kc_ref[0], qst)
        s_loc = _dot(kbuf[n * BLOCK:(n + 3) * BLOCK, :], qst) if local else None
        return s_ctx, s_loc

    def attend(n, s_ctx, s_loc):
        rows = slice(n * BLOCK, (n + 1) * BLOCK)
        m = jnp.maximum(jnp.max(s_ctx, axis=0, keepdims=True), sink_row)
        if local:
            gblk = tile_idx * nsub + n
            s_prev = jnp.where(kq_diff >= jnp.where(gblk > 0, 0, BLOCK), s_loc[0:BLOCK], NEG_INF)
            s_own = s_loc[BLOCK:2 * BLOCK]
            s_next = jnp.where(kq_diff <= jnp.where(gblk < n_blocks - 1, 0, -BLOCK),
                               s_loc[2 * BLOCK:3 * BLOCK], NEG_INF)
            m_loc = jnp.max(jnp.maximum(jnp.maximum(s_prev, s_own), s_next), axis=0, keepdims=True)
            m = jnp.maximum(m, m_loc)
        e_ctx = jnp.exp2(s_ctx - m)
        ot = _dot(vt_ctx, e_ctx.astype(BF16))
        if local:
            e_loc = jnp.concatenate(
                [jnp.exp2(s_prev - m), jnp.exp2(s_own - m), jnp.exp2(s_next - m)], axis=0)
            vt_loc = with_ones(jnp.concatenate([vtbuf[n], vtbuf[n + 1], vtbuf[n + 2]], axis=1))
            ot = ot + _dot(vt_loc, e_loc.astype(BF16))
        den = ot[KV_WIDTH:KV_WIDTH + 1] + jnp.exp2(sink_row - m)
        ot = ot[0:KV_WIDTH] * (1.0 / den)
        for j in range(ATTN_WIDTH // LANES):
            h = (2 * j) // heads_per_kv
            pair = jnp.concatenate(
                [ot[h * HEAD_DIM:(h + 1) * HEAD_DIM, (2 * j) * BLOCK:(2 * j + 1) * BLOCK],
                 ot[h * HEAD_DIM:(h + 1) * HEAD_DIM, (2 * j + 1) * BLOCK:(2 * j + 2) * BLOCK]], axis=0)
            zc = main_ref[0, rows, MAIN_ZC + j * LANES:MAIN_ZC + (j + 1) * LANES].astype(F32)
            yc_s[rows, j * LANES:(j + 1) * LANES] = (pair.T * _silu_of_half(zc)).astype(BF16)

    ahead = scores(0)
    for n in range(nsub):
        sgu(n)
        current, ahead = ahead, (scores(n + 1) if n + 1 < nsub else None)
        attend(n, *current)

    t = jnp.tanh(main_ref[0, :, MAIN_G:MAIN_G + 3 * d].astype(F32))
    ma = _dot(ya_s[...], wpa_ref[0])
    mb = _dot(yb_s[...], wpb_ref[0])
    mc = _dot(yc_s[...], wpc_ref[0])
    merged2 = (ma + t[:, 0:d] * ma) + (mb + t[:, d:2 * d] * mb) + (mc + t[:, 2 * d:3 * d] * mc)
    out = _dot(merged2.astype(BF16), wout_ref[0])
    row = pl.program_id(0) if ctx_row is None else ctx_row
    gate = mod_ref[0, pl.ds(row, 1), 2 * d:3 * d]
    o_ref[0] = x_ref[0] + gate * out


def _mix_call(x, main, y, qt, k, vt, kc, vtc, mod, sinks, wsg, bsg, wpa, wpb, wpc, wout, l,
              *, tile, local, ctx_row):
    b, length, d = x.shape
    nt = length // tile
    n_ctx = kc.shape[1]
    sub = tile // BLOCK
    n_blocks = length // BLOCK
    tok = lambda bi, i: (bi, i, 0)
    blk = lambda bi, i: (bi, i, 0, 0)
    per_b = lambda bi, i: (bi, 0, 0)
    per_b4 = lambda bi, i: (bi, 0, 0, 0)

    in_specs = [
        pl.BlockSpec((1, tile, d), tok),
        pl.BlockSpec((1, tile, MAIN_WIDTH), tok),
    ]
    args = [x, main, y]
    if local:
        rows_k2 = tile // DFT_L1
        r = jnp.arange(tile)
        perm = (r[:, None] == (r[None, :] % rows_k2) * DFT_L1 + r[None, :] // rows_k2).astype(BF16)
        in_specs += [pl.BlockSpec((1, rows_k2, DFT_L1 * FOURIER_WIDTH), tok),
                     pl.BlockSpec((tile, tile), lambda bi, i: (0, 0))]
        args.append(perm)
    else:
        in_specs.append(pl.BlockSpec((1, tile, FOURIER_WIDTH), tok))
    in_specs.append(pl.BlockSpec((1, sub, ATTN_WIDTH, BLOCK), blk))
    args.append(qt)
    if local:
        prev_blk = lambda bi, i: jnp.maximum(i * sub - 1, 0)
        next_blk = lambda bi, i: jnp.minimum((i + 1) * sub, n_blocks - 1)
        in_specs += [
            pl.BlockSpec((1, BLOCK, KV_WIDTH), lambda bi, i: (bi, prev_blk(bi, i), 0)),
            pl.BlockSpec((1, tile, KV_WIDTH), tok),
            pl.BlockSpec((1, BLOCK, KV_WIDTH), lambda bi, i: (bi, next_blk(bi, i), 0)),
            pl.BlockSpec((1, 1, KV_WIDTH, BLOCK), lambda bi, i: (bi, prev_blk(bi, i), 0, 0)),
            pl.BlockSpec((1, sub, KV_WIDTH, BLOCK), blk),
            pl.BlockSpec((1, 1, KV_WIDTH, BLOCK), lambda bi, i: (bi, next_blk(bi, i), 0, 0)),
        ]
        args += [k, k, k, vt, vt, vt]
    in_specs += [
        pl.BlockSpec((1, n_ctx, KV_WIDTH), per_b),
        pl.BlockSpec((1, n_ctx // BLOCK, KV_WIDTH, BLOCK), per_b4),
        pl.BlockSpec((1, MOD_ROWS, 3 * d), _layer(l, 3)),
        pl.BlockSpec(memory_space=pltpu.SMEM),
    ]
    args += [kc, vtc, mod, sinks]
    for w in (wsg, bsg, wpa, wpb, wpc, wout):
        in_specs.append(pl.BlockSpec((1,) + w.shape[1:], _layer(l, 3)))
        args.append(w)

    scratch = []
    if local:
        scratch += [pltpu.VMEM((tile + 2 * BLOCK, KV_WIDTH), BF16),
                    pltpu.VMEM((sub + 2, KV_WIDTH, BLOCK), BF16)]
    scratch += [
        pltpu.VMEM((tile, FOURIER_WIDTH), BF16),
        pltpu.VMEM((tile, SGU_WIDTH), BF16),
        pltpu.VMEM((tile, ATTN_WIDTH), BF16),
    ]
    kern = functools.partial(_mix_kernel, local=local, tile=tile, n_blocks=n_blocks,
                             ctx_row=ctx_row, layer=l)
    return pl.pallas_call(
        kern,
        grid=(b, nt),
        in_specs=in_specs,
        out_specs=pl.BlockSpec((1, tile, d), tok),
        out_shape=jax.ShapeDtypeStruct((b, length, d), F32),
        scratch_shapes=scratch,
        compiler_params=pltpu.CompilerParams(
            dimension_semantics=("arbitrary", "arbitrary"), vmem_limit_bytes=VMEM_LIMIT),
        name="mix_lat" if local else "mix_ctx",
    )(*args)


def _rope_tables(length):
    nf = HEAD_DIM // 4
    inv = jnp.tile(ROPE_THETA ** (-jnp.arange(nf, dtype=F32) / nf), LANES // nf)[None, :]
    pos = lax.broadcasted_iota(jnp.int32, (length, LANES), 0)
    lane = lax.broadcasted_iota(jnp.int32, (length, LANES), 1)
    axial = jnp.where((lane // (2 * nf)) % 2 == 0, pos // GRID_W, pos % GRID_W)
    ang = axial.astype(F32) * inv
    sign = jnp.where((lane // nf) % 2 == 0, -1.0, 1.0)
    return jnp.cos(ang), jnp.sin(ang) * sign


def _angle(num, den):
    return (num % den).astype(F32) * (2.0 * math.pi / den)


def _channel_dft(width):
    gw = FOURIER_WIDTH // FOURIER_GROUPS
    m = jnp.arange(width)
    same = (m[:, None] // gw) == (m[None, :] // gw)
    th = _angle((m[:, None] % gw) * (m[None, :] % gw), gw)
    cc = jnp.where(same, jnp.cos(th), 0.0) * gw ** -0.5
    sc = jnp.where(same, jnp.sin(th), 0.0) * gw ** -0.5
    return jnp.concatenate([cc, sc], axis=0).astype(BF16)


def _fourier_tables():
    l1n, l2n = DFT_L1, DFT_L2
    length = l1n * l2n
    pack = DFT_PACK
    l2 = jnp.arange(l2n)[:, None, None]
    k1 = jnp.arange(l1n)[None, :, None]
    l1 = jnp.arange(l1n)[None, None, :]
    th = _angle(k1 * l1 * l2n + k1 * l2, length)
    w1 = jnp.stack([jnp.cos(th), -jnp.sin(th)], axis=1) * l1n ** -0.5
    rows, cols = 2 * l1n * pack, l1n * pack
    ws = jnp.transpose(w1.reshape(l2n // pack, pack, 2 * l1n, l1n), (0, 2, 1, 3))
    ws = ws.reshape(l2n // pack, rows, l1n).astype(BF16)
    spread = (jnp.arange(cols)[None, :] // pack == jnp.arange(l1n)[:, None]).astype(BF16)
    same_p = (jnp.arange(rows)[:, None] % pack) == (jnp.arange(cols)[None, :] % pack)
    w1 = jnp.einsum("grb,bc->grc", ws, spread, preferred_element_type=F32)
    w1 = jnp.where(same_p[None], w1, 0.0).astype(BF16)
    k2 = jnp.arange(l2n)
    th2 = _angle(k2[:, None] * k2[None, :], l2n)
    c2, s2 = jnp.cos(th2), jnp.sin(th2)
    w2 = (jnp.concatenate([jnp.concatenate([c2, s2], axis=1),
                           jnp.concatenate([-s2, c2], axis=1)], axis=0) * l2n ** -0.5).astype(BF16)
    return w1, w2


def _fourier_ctx_table(n):
    k = jnp.arange(n)
    th = _angle(k[:, None] * k[None, :], n)
    return (jnp.concatenate([jnp.cos(th), -jnp.sin(th)], axis=0) * n ** -0.5).astype(BF16)


def _block_diag_ones(width):
    m = jnp.arange(width) // HEAD_DIM
    return (m[:, None] == m[None, :]).astype(BF16)


def _prep_w_in(w_in):
    col = jnp.arange(W_TOTAL)
    halved = ((col >= W_ZA) & (col < W_ZA + FOURIER_WIDTH)) | ((col >= W_ZB) & (col < W_Q)) | (col >= W_ZC)
    return (w_in * jnp.where(halved, 0.5, 1.0)).astype(BF16)


def kernel(x, c, ctx, c_ctx, w_ada, b_ada, w_in, sgu_w, sgu_b, q_norm_g, k_norm_g,
           attn_sink, w_pa, w_pb, w_pc, w_out):
    b, length, d = x.shape
    n_ctx = ctx.shape[1]
    assert length == DFT_L1 * DFT_L2 and d == D_MODEL and b + 1 <= MOD_ROWS
    assert w_in.shape == (DEPTH, d, W_TOTAL)
    lat_tile = 512
    ctx_row = b

    cs = jnp.zeros((MOD_ROWS, d), F32).at[:b].set(c).at[b].set(c_ctx)
    mod = _ada_call(cs, w_ada, b_ada)

    rope_tables = _rope_tables(length)
    w1, w2 = _fourier_tables()
    cs_dft = _channel_dft(FOURIER_WIDTH)
    wc = _fourier_ctx_table(n_ctx)
    bdq = _block_diag_ones(ATTN_WIDTH)
    bdk = _block_diag_ones(KV_WIDTH)

    w_all = _prep_w_in(w_in)
    gq_all = jnp.tile(q_norm_g, (1, N_HEADS))[:, None, :]
    gk_all = jnp.tile(k_norm_g, (1, N_KV_HEADS))[:, None, :]
    wsg = jnp.transpose(sgu_w, (0, 2, 1, 3)).reshape(DEPTH, CHUNK, SGU_GROUPS * CHUNK).astype(BF16)
    bsg = jnp.repeat(jnp.swapaxes(sgu_b, 1, 2), SGU_WIDTH // SGU_GROUPS, axis=2)
    wpa, wpb, wpc = w_pa.astype(BF16), w_pb.astype(BF16), w_pc.astype(BF16)
    wout = (0.5 * w_out).astype(BF16)

    xc = ctx
    for l in range(DEPTH):
        last = l == DEPTH - 1
        fa_c, main_c, qt_c, kc, vtc = _proj_call(
            xc.reshape(1, b * n_ctx, d), mod, w_all, bdq, bdk, gq_all, gk_all, None, l,
            tile=2 * n_ctx, ctx_row=ctx_row)
        kc = kc.reshape(b, n_ctx, KV_WIDTH)
        vtc = vtc.reshape(b, n_ctx // BLOCK, KV_WIDTH, BLOCK)
        fa, main, qt, k, vt = _proj_call(
            x, mod, w_all, bdq, bdk, gq_all, gk_all, rope_tables, l, tile=lat_tile, ctx_row=None)
        y = _fourier_call(fa, w1, w2, cs_dft)
        x = _mix_call(x, main, y, qt, k, vt, kc, vtc, mod, attn_sink, wsg, bsg, wpa, wpb, wpc, wout,
                      l, tile=lat_tile, local=True, ctx_row=None)
        if not last:
            y_c = _fourier_ctx_call(fa_c.reshape(b, n_ctx, FOURIER_WIDTH), wc, cs_dft)
            xc = _mix_call(xc, main_c.reshape(b, n_ctx, MAIN_WIDTH), y_c,
                           qt_c.reshape(b, n_ctx // BLOCK, ATTN_WIDTH, BLOCK), None, None, kc, vtc,
                           mod, attn_sink, wsg, bsg, wpa, wpb, wpc, wout, l,
                           tile=n_ctx, local=False, ctx_row=ctx_row)
    return x
```

```python
import functools
import math

import jax
import jax.numpy as jnp
from jax import lax
from jax.experimental import pallas as pl
from jax.experimental.pallas import tpu as pltpu

F32 = jnp.float32
BF16 = jnp.bfloat16

D_MODEL = 1024
DEPTH = 4
GRID_W = 64
FOURIER_WIDTH = 256
FOURIER_GROUPS = 4
SGU_WIDTH = 256
SGU_GROUPS = 4
CHUNK = 128
N_HEADS = 8
N_KV_HEADS = 2
HEAD_DIM = 64
ATTN_WIDTH = N_HEADS * HEAD_DIM
KV_WIDTH = N_KV_HEADS * HEAD_DIM
BLOCK = 128
ROPE_THETA = 10000.0
EPS = 1e-6
NEG_INF = -1e30
LOG2E = math.log2(math.e)

LANES = 128
MOD_ROWS = 8
ONES_ROWS = 16

W_FA = 0
W_ZA = W_FA + FOURIER_WIDTH
W_ZB = W_ZA + FOURIER_WIDTH + 2 * SGU_WIDTH
W_Q = W_ZB + SGU_WIDTH
W_K = W_Q + ATTN_WIDTH
W_V = W_K + KV_WIDTH
W_ZC = W_V + KV_WIDTH
W_G = W_ZC + ATTN_WIDTH
W_TOTAL = W_G + 3 * D_MODEL
MAIN_ZA = 0
MAIN_U = MAIN_ZA + FOURIER_WIDTH
MAIN_VS = MAIN_U + SGU_WIDTH
MAIN_ZB = MAIN_VS + SGU_WIDTH
MAIN_ZC = MAIN_ZB + SGU_WIDTH
MAIN_G = MAIN_ZC + ATTN_WIDTH
MAIN_WIDTH = MAIN_G + 3 * D_MODEL
PROJ_CHUNK = 512

DFT_L1 = 32
DFT_L2 = 256
DFT_PACK = 16
DFT2_GROUP = 8

VMEM_LIMIT = 56 * 1024 * 1024


def _silu(z):
    return 0.5 * z * (1.0 + jnp.tanh(0.5 * z))


def _silu_of_half(zh):
    return zh * (1.0 + jnp.tanh(zh))


def _dot(a, b):
    return jnp.dot(a, b, preferred_element_type=F32)


def _layer(l, rank):
    return lambda *_: (l,) + (0,) * (rank - 1)


def _ada_kernel(c_ref, w_ref, b_ref, o_ref):
    s = _silu(c_ref[...]).astype(BF16)
    o_ref[0] = _dot(s, w_ref[0].astype(BF16)) + b_ref[0]


def _ada_call(cs, w_ada, b_ada):
    d = D_MODEL
    return pl.pallas_call(
        _ada_kernel,
        grid=(DEPTH, 3),
        in_specs=[
            pl.BlockSpec((MOD_ROWS, d), lambda l, j: (0, 0)),
            pl.BlockSpec((1, d, d), lambda l, j: (l, 0, j)),
            pl.BlockSpec((1, 1, d), lambda l, j: (l, 0, j)),
        ],
        out_specs=pl.BlockSpec((1, MOD_ROWS, d), lambda l, j: (l, 0, j)),
        out_shape=jax.ShapeDtypeStruct((DEPTH, MOD_ROWS, 3 * d), F32),
        compiler_params=pltpu.CompilerParams(
            dimension_semantics=("arbitrary", "arbitrary"), vmem_limit_bytes=VMEM_LIMIT),
        name="ada_mod",
    )(cs, w_ada, b_ada.reshape(DEPTH, 1, 3 * d))


def _norm_rope(t, bd_ref, gain, cos, sin, out_scale):
    width = t.shape[-1]
    ssq = _dot((t * t).astype(BF16), bd_ref[...])
    tn = t * lax.rsqrt(ssq * (1.0 / HEAD_DIM) + EPS) * gain
    if cos is not None:
        reps = width // LANES
        cos_w = jnp.concatenate([cos] * reps, axis=1) if reps > 1 else cos
        sin_w = jnp.concatenate([sin] * reps, axis=1) if reps > 1 else sin
        lane = lax.broadcasted_iota(jnp.int32, tn.shape, 1)
        first = (lane % 32) < 16
        partner = jnp.where(first, pltpu.roll(tn, width - 16, 1), pltpu.roll(tn, 16, 1))
        tn = tn * cos_w + partner * sin_w
    if out_scale != 1.0:
        tn = tn * out_scale
    return tn


def _proj_kernel(x_ref, mod_ref, w_ref, bdq_ref, bdk_ref, gq_ref, gk_ref, cos_ref, sin_ref,
                 fa_ref, main_ref, qt_ref, k_ref, vt_ref, *, ctx_row, rope):
    d = D_MODEL
    nsub = x_ref.shape[1] // BLOCK
    xt = x_ref[0]
    ms = jnp.mean(xt * xt, axis=-1, keepdims=True)
    row = pl.program_id(0) if ctx_row is None else ctx_row
    modrow = mod_ref[0, pl.ds(row, 1), :]
    shift = modrow[:, 0:d]
    scale = modrow[:, d:2 * d]
    hb = ((xt * lax.rsqrt(ms + EPS)) * (1.0 + scale) + shift).astype(BF16)
    cos = cos_ref[...] if rope else None
    sin = sin_ref[...] if rope else None

    def proj(c0, width):
        return _dot(hb, w_ref[0, :, c0:c0 + width])

    chunks = [(src + c0, dst + c0)
              for src, dst, width in ((W_ZA, MAIN_ZA, W_Q - W_ZA), (W_ZC, MAIN_ZC, W_TOTAL - W_ZC))
              for c0 in range(0, width, PROJ_CHUNK)]

    def wide(count):
        for _ in range(min(count, len(chunks))):
            src, dst = chunks.pop(0)
            main_ref[0, :, dst:dst + PROJ_CHUNK] = proj(src, PROJ_CHUNK).astype(BF16)

    q_raw = proj(W_Q, ATTN_WIDTH)
    k_raw = proj(W_K, KV_WIDTH)
    v = proj(W_V, KV_WIDTH)
    wide(1)
    q = _norm_rope(q_raw, bdq_ref, gq_ref[0], cos, sin, HEAD_DIM ** -0.5 * LOG2E)
    wide(1)
    for s in range(nsub):
        qt_ref[0, s] = q[s * BLOCK:(s + 1) * BLOCK, :].T.astype(BF16)
    wide(1)
    k = _norm_rope(k_raw, bdk_ref, gk_ref[0], cos, sin, 1.0)
    k_ref[0] = k.astype(BF16)
    for s in range(nsub):
        vt_ref[0, s] = v[s * BLOCK:(s + 1) * BLOCK, :].T.astype(BF16)
    fa_ref[0] = proj(W_FA, FOURIER_WIDTH).astype(BF16)
    wide(len(chunks))


def _proj_call(x, mod, w_all, bdq, bdk, gq_all, gk_all, rope_tables, l, *, tile, ctx_row):
    b, length, d = x.shape
    nt = length // tile
    nsub = tile // BLOCK
    const = lambda bi, i: (0, 0)
    tok = lambda bi, i: (bi, i, 0)
    blk = lambda bi, i: (bi, i, 0, 0)
    rope = rope_tables is not None
    if rope:
        cos, sin = rope_tables
        pos, pos_rows = (lambda bi, i: (i, 0)), tile
    else:
        cos = sin = jnp.zeros((MOD_ROWS, LANES), F32)
        pos, pos_rows = const, MOD_ROWS
    kern = functools.partial(_proj_kernel, ctx_row=ctx_row, rope=rope)
    return pl.pallas_call(
        kern,
        grid=(b, nt),
        in_specs=[
            pl.BlockSpec((1, tile, d), tok),
            pl.BlockSpec((1, MOD_ROWS, 3 * d), _layer(l, 3)),
            pl.BlockSpec((1, d, W_TOTAL), _layer(l, 3), pipeline_mode=pl.Buffered(1)),
            pl.BlockSpec((ATTN_WIDTH, ATTN_WIDTH), const),
            pl.BlockSpec((KV_WIDTH, KV_WIDTH), const),
            pl.BlockSpec((1, 1, ATTN_WIDTH), _layer(l, 3)),
            pl.BlockSpec((1, 1, KV_WIDTH), _layer(l, 3)),
            pl.BlockSpec((pos_rows, LANES), pos),
            pl.BlockSpec((pos_rows, LANES), pos),
        ],
        out_specs=[
            pl.BlockSpec((1, tile, FOURIER_WIDTH), tok),
            pl.BlockSpec((1, tile, MAIN_WIDTH), tok),
            pl.BlockSpec((1, nsub, ATTN_WIDTH, BLOCK), blk),
            pl.BlockSpec((1, tile, KV_WIDTH), tok),
            pl.BlockSpec((1, nsub, KV_WIDTH, BLOCK), blk),
        ],
        out_shape=[
            jax.ShapeDtypeStruct((b, length, FOURIER_WIDTH), BF16),
            jax.ShapeDtypeStruct((b, length, MAIN_WIDTH), BF16),
            jax.ShapeDtypeStruct((b, length // BLOCK, ATTN_WIDTH, BLOCK), BF16),
            jax.ShapeDtypeStruct((b, length, KV_WIDTH), BF16),
            jax.ShapeDtypeStruct((b, length // BLOCK, KV_WIDTH, BLOCK), BF16),
        ],
        compiler_params=pltpu.CompilerParams(
            dimension_semantics=("arbitrary", "arbitrary"), vmem_limit_bytes=VMEM_LIMIT),
        name="proj_ctx" if ctx_row is not None else "proj_lat",
    )(x, mod, w_all, bdq, bdk, gq_all, gk_all, cos, sin)


def _fourier_kernel(x_ref, w1_ref, w2_ref, cs_ref, y_ref, a_s):
    l1n, l2n, pack, cw = DFT_L1, DFT_L2, DFT_PACK, FOURIER_WIDTH
    for g in range(l2n // pack):
        xg = x_ref[0, :, g].reshape(l1n * pack, cw)
        ag = _dot(w1_ref[g], xg).astype(BF16)
        a_s[:, :, g * pack:(g + 1) * pack, :] = ag.reshape(2, l1n, pack, cw)
    grp = DFT2_GROUP
    for g in range(l1n // grp):
        rhs = jnp.concatenate(
            [jnp.concatenate([a_s[0, g * grp + j], a_s[1, g * grp + j]], axis=0)
             for j in range(grp)], axis=1)
        f = _dot(w2_ref[...], rhs)
        for j in range(grp):
            cols = slice(j * cw, (j + 1) * cw)
            fc = jnp.concatenate([f[:l2n, cols], f[l2n:, cols]], axis=1).astype(BF16)
            out_cols = slice((g * grp + j) * cw, (g * grp + j + 1) * cw)
            y_ref[0, :, out_cols] = _dot(fc, cs_ref[...]).astype(BF16)


def _fourier_call(fa, w1, w2, cs):
    b, length, cw = fa.shape
    l1n, l2n, pack = DFT_L1, DFT_L2, DFT_PACK
    const = lambda bi: (0, 0)
    return pl.pallas_call(
        _fourier_kernel,
        grid=(b,),
        in_specs=[
            pl.BlockSpec((1, l1n, l2n // pack, pack, cw), lambda bi: (bi, 0, 0, 0, 0)),
            pl.BlockSpec(w1.shape, lambda bi: (0, 0, 0), pipeline_mode=pl.Buffered(1)),
            pl.BlockSpec(w2.shape, const),
            pl.BlockSpec(cs.shape, const),
        ],
        out_specs=pl.BlockSpec((1, l2n, l1n * cw), lambda bi: (bi, 0, 0)),
        out_shape=jax.ShapeDtypeStruct((b, l2n, l1n * cw), BF16),
        scratch_shapes=[pltpu.VMEM((2, l1n, l2n, cw), BF16)],
        compiler_params=pltpu.CompilerParams(
            dimension_semantics=("arbitrary",), vmem_limit_bytes=VMEM_LIMIT),
        name="fourier_lat",
    )(fa.reshape(b, l1n, l2n // pack, pack, cw), w1, w2, cs)


def _fourier_ctx_kernel(fa_ref, wc_ref, cs_ref, y_ref):
    n = fa_ref.shape[1]
    f = _dot(wc_ref[...], fa_ref[0])
    fc = jnp.concatenate([f[:n], f[n:]], axis=1).astype(BF16)
    y_ref[0] = _dot(fc, cs_ref[...])


def _fourier_ctx_call(fa, wc, cs):
    b, length, width = fa.shape
    return pl.pallas_call(
        _fourier_ctx_kernel,
        grid=(b,),
        in_specs=[
            pl.BlockSpec((1, length, width), lambda bi: (bi, 0, 0)),
            pl.BlockSpec(wc.shape, lambda bi: (0, 0)),
            pl.BlockSpec(cs.shape, lambda bi: (0, 0)),
        ],
        out_specs=pl.BlockSpec((1, length, width), lambda bi: (bi, 0, 0)),
        out_shape=jax.ShapeDtypeStruct((b, length, width), F32),
        compiler_params=pltpu.CompilerParams(
            dimension_semantics=("arbitrary",), vmem_limit_bytes=VMEM_LIMIT),
        name="fourier_ctx",
    )(fa, wc, cs)


def _mix_kernel(*refs, local, tile, n_blocks, ctx_row, layer):
    if local:
        (x_ref, main_ref, y_ref, perm_ref, qt_ref, kp_ref, km_ref, kn_ref, vtp_ref, vtm_ref, vtn_ref,
         kc_ref, vtc_ref, mod_ref, sink_ref, wsg_ref, bsg_ref,
         wpa_ref, wpb_ref, wpc_ref, wout_ref, o_ref,
         kbuf, vtbuf, ya_s, yb_s, yc_s) = refs
    else:
        (x_ref, main_ref, y_ref, qt_ref, kc_ref, vtc_ref, mod_ref, sink_ref, wsg_ref, bsg_ref,
         wpa_ref, wpb_ref, wpc_ref, wout_ref, o_ref,
         ya_s, yb_s, yc_s) = refs
    d = D_MODEL
    nsub = tile // BLOCK
    tile_idx = pl.program_id(1)
    heads_per_kv = N_HEADS // N_KV_HEADS
    cols_all = N_HEADS * BLOCK

    if local:
        kbuf[0:BLOCK, :] = kp_ref[0]
        kbuf[BLOCK:BLOCK + tile, :] = km_ref[0]
        kbuf[BLOCK + tile:2 * BLOCK + tile, :] = kn_ref[0]
        vtbuf[0] = vtp_ref[0, 0]
        for s in range(nsub):
            vtbuf[1 + s] = vtm_ref[0, s]
        vtbuf[1 + nsub] = vtn_ref[0, 0]

    def with_ones(vt):
        return jnp.concatenate([vt, jnp.ones((ONES_ROWS, vt.shape[1]), BF16)], axis=0)

    vt_ctx = with_ones(jnp.concatenate([vtc_ref[0, s] for s in range(vtc_ref.shape[1])], axis=1))

    if local:
        cw = FOURIER_WIDTH
        stacked = jnp.concatenate(
            [y_ref[0, :, k1 * cw:(k1 + 1) * cw] for k1 in range(DFT_L1)], axis=0)
        y = _dot(perm_ref[...], stacked)
    else:
        y = y_ref[0]
    ya_s[...] = (y * _silu_of_half(main_ref[0, :, MAIN_ZA:MAIN_ZA + FOURIER_WIDTH].astype(F32))).astype(BF16)

    sgu_grp = lax.broadcasted_iota(jnp.int32, (CHUNK, SGU_WIDTH), 1) // (SGU_WIDTH // SGU_GROUPS)
    zero_head = jnp.zeros((HEAD_DIM, BLOCK), BF16)
    sink_row = jnp.concatenate(
        [jnp.full((1, BLOCK), sink_ref[layer, hh] * LOG2E, F32) for hh in range(N_HEADS)], axis=1)
    kq_diff = (lax.broadcasted_iota(jnp.int32, (BLOCK, cols_all), 0)
               - lax.broadcasted_iota(jnp.int32, (BLOCK, cols_all), 1) % BLOCK)

    def sgu(n):
        rows = slice(n * BLOCK, (n + 1) * BLOCK)
        vs = main_ref[0, rows, MAIN_VS:MAIN_VS + SGU_WIDTH].astype(F32)
        vn = (vs * lax.rsqrt(jnp.mean(vs * vs, axis=-1, keepdims=True) + EPS)).astype(BF16)
        rhs = jnp.concatenate(
            [jnp.where(sgu_grp == g, vn, jnp.zeros_like(vn)) for g in range(SGU_GROUPS)], axis=0)
        mixed = _dot(wsg_ref[0], rhs) + bsg_ref[0]
        u = main_ref[0, rows, MAIN_U:MAIN_U + SGU_WIDTH].astype(F32)
        zb = main_ref[0, rows, MAIN_ZB:MAIN_ZB + SGU_WIDTH].astype(F32)
        yb_s[rows, :] = (u * mixed * _silu_of_half(zb)).astype(BF16)

    def scores(n):
        qt = qt_ref[0, n]
        cols = []
        for hh in range(N_HEADS):
            qh = qt[hh * HEAD_DIM:(hh + 1) * HEAD_DIM, :]
            cols.append(jnp.concatenate(
                [qh, zero_head] if hh < heads_per_kv else [zero_head, qh], axis=0))
        qst = jnp.concatenate(cols, axis=1)

        s_ctx = _dot(kc_ref[0], qst)
        s_loc = _dot(kbuf[n * BLOCK:(n + 3) * BLOCK, :], qst) if local else None
        return s_ctx, s_loc

    def attend(n, s_ctx, s_loc):
        rows = slice(n * BLOCK, (n + 1) * BLOCK)
        m = jnp.maximum(jnp.max(s_ctx, axis=0, keepdims=True), sink_row)
        if local:
            gblk = tile_idx * nsub + n
            s_prev = jnp.where(kq_diff >= jnp.where(gblk > 0, 0, BLOCK), s_loc[0:BLOCK], NEG_INF)
            s_own = s_loc[BLOCK:2 * BLOCK]
            s_next = jnp.where(kq_diff <= jnp.where(gblk < n_blocks - 1, 0, -BLOCK),
                               s_loc[2 * BLOCK:3 * BLOCK], NEG_INF)
            m_loc = jnp.max(jnp.maximum(jnp.maximum(s_prev, s_own), s_next), axis=0, keepdims=True)
            m = jnp.maximum(m, m_loc)
        e_ctx = jnp.exp2(s_ctx - m)
        ot = _dot(vt_ctx, e_ctx.astype(BF16))
        if local:
            e_loc = jnp.concatenate(
                [jnp.exp2(s_prev - m), jnp.exp2(s_own - m), jnp.exp2(s_next - m)], axis=0)
            vt_loc = with_ones(jnp.concatenate([vtbuf[n], vtbuf[n + 1], vtbuf[n + 2]], axis=1))
            ot = ot + _dot(vt_loc, e_loc.astype(BF16))
        den = ot[KV_WIDTH:KV_WIDTH + 1] + jnp.exp2(sink_row - m)
        ot = ot[0:KV_WIDTH] * (1.0 / den)
        for j in range(ATTN_WIDTH // LANES):
            h = (2 * j) // heads_per_kv
            pair = jnp.concatenate(
                [ot[h * HEAD_DIM:(h + 1) * HEAD_DIM, (2 * j) * BLOCK:(2 * j + 1) * BLOCK],
                 ot[h * HEAD_DIM:(h + 1) * HEAD_DIM, (2 * j + 1) * BLOCK:(2 * j + 2) * BLOCK]], axis=0)
            zc = main_ref[0, rows, MAIN_ZC + j * LANES:MAIN_ZC + (j + 1) * LANES].astype(F32)
            yc_s[rows, j * LANES:(j + 1) * LANES] = (pair.T * _silu_of_half(zc)).astype(BF16)

    ahead = scores(0)
    for n in range(nsub):
        sgu(n)
        current, ahead = ahead, (scores(n + 1) if n + 1 < nsub else None)
        attend(n, *current)

    t = jnp.tanh(main_ref[0, :, MAIN_G:MAIN_G + 3 * d].astype(F32))
    ma = _dot(ya_s[...], wpa_ref[0])
    mb = _dot(yb_s[...], wpb_ref[0])
    mc = _dot(yc_s[...], wpc_ref[0])
    merged2 = (ma + t[:, 0:d] * ma) + (mb + t[:, d:2 * d] * mb) + (mc + t[:, 2 * d:3 * d] * mc)
    out = _dot(merged2.astype(BF16), wout_ref[0])
    row = pl.program_id(0) if ctx_row is None else ctx_row
    gate = mod_ref[0, pl.ds(row, 1), 2 * d:3 * d]
    o_ref[0] = x_ref[0] + gate * out


def _mix_call(x, main, y, qt, k, vt, kc, vtc, mod, sinks, wsg, bsg, wpa, wpb, wpc, wout, l,
              *, tile, local, ctx_row):
    b, length, d = x.shape
    nt = length // tile
    n_ctx = kc.shape[1]
    sub = tile // BLOCK
    n_blocks = length // BLOCK
    tok = lambda bi, i: (bi, i, 0)
    blk = lambda bi, i: (bi, i, 0, 0)
    per_b = lambda bi, i: (bi, 0, 0)
    per_b4 = lambda bi, i: (bi, 0, 0, 0)

    in_specs = [
        pl.BlockSpec((1, tile, d), tok),
        pl.BlockSpec((1, tile, MAIN_WIDTH), tok),
    ]
    args = [x, main, y]
    if local:
        rows_k2 = tile // DFT_L1
        r = jnp.arange(tile)
        perm = (r[:, None] == (r[None, :] % rows_k2) * DFT_L1 + r[None, :] // rows_k2).astype(BF16)
        in_specs += [pl.BlockSpec((1, rows_k2, DFT_L1 * FOURIER_WIDTH), tok),
                     pl.BlockSpec((tile, tile), lambda bi, i: (0, 0))]
        args.append(perm)
    else:
        in_specs.append(pl.BlockSpec((1, tile, FOURIER_WIDTH), tok))
    in_specs.append(pl.BlockSpec((1, sub, ATTN_WIDTH, BLOCK), blk))
    args.append(qt)
    if local:
        prev_blk = lambda bi, i: jnp.maximum(i * sub - 1, 0)
        next_blk = lambda bi, i: jnp.minimum((i + 1) * sub, n_blocks - 1)
        in_specs += [
            pl.BlockSpec((1, BLOCK, KV_WIDTH), lambda bi, i: (bi, prev_blk(bi, i), 0)),
            pl.BlockSpec((1, tile, KV_WIDTH), tok),
            pl.BlockSpec((1, BLOCK, KV_WIDTH), lambda bi, i: (bi, next_blk(bi, i), 0)),
            pl.BlockSpec((1, 1, KV_WIDTH, BLOCK), lambda bi, i: (bi, prev_blk(bi, i), 0, 0)),
            pl.BlockSpec((1, sub, KV_WIDTH, BLOCK), blk),
            pl.BlockSpec((1, 1, KV_WIDTH, BLOCK), lambda bi, i: (bi, next_blk(bi, i), 0, 0)),
        ]
        args += [k, k, k, vt, vt, vt]
    in_specs += [
        pl.BlockSpec((1, n_ctx, KV_WIDTH), per_b),
        pl.BlockSpec((1, n_ctx // BLOCK, KV_WIDTH, BLOCK), per_b4),
        pl.BlockSpec((1, MOD_ROWS, 3 * d), _layer(l, 3)),
        pl.BlockSpec(memory_space=pltpu.SMEM),
    ]
    args += [kc, vtc, mod, sinks]
    for w in (wsg, bsg, wpa, wpb, wpc, wout):
        in_specs.append(pl.BlockSpec((1,) + w.shape[1:], _layer(l, 3)))
        args.append(w)

    scratch = []
    if local:
        scratch += [pltpu.VMEM((tile + 2 * BLOCK, KV_WIDTH), BF16),
                    pltpu.VMEM((sub + 2, KV_WIDTH, BLOCK), BF16)]
    scratch += [
        pltpu.VMEM((tile, FOURIER_WIDTH), BF16),
        pltpu.VMEM((tile, SGU_WIDTH), BF16),
        pltpu.VMEM((tile, ATTN_WIDTH), BF16),
    ]
    kern = functools.partial(_mix_kernel, local=local, tile=tile, n_blocks=n_blocks,
                             ctx_row=ctx_row, layer=l)
    return pl.pallas_call(
        kern,
        grid=(b, nt),
        in_specs=in_specs,
        out_specs=pl.BlockSpec((1, tile, d), tok),
        out_shape=jax.ShapeDtypeStruct((b, length, d), F32),
        scratch_shapes=scratch,
        compiler_params=pltpu.CompilerParams(
            dimension_semantics=("arbitrary", "arbitrary"), vmem_limit_bytes=VMEM_LIMIT),
        name="mix_lat" if local else "mix_ctx",
    )(*args)


def _rope_tables(length):
    nf = HEAD_DIM // 4
    inv = jnp.tile(ROPE_THETA ** (-jnp.arange(nf, dtype=F32) / nf), LANES // nf)[None, :]
    lane = jnp.arange(LANES)[None, :]
    on_row = (lane // (2 * nf)) % 2 == 0
    sign = jnp.where((lane // nf) % 2 == 0, -1.0, 1.0)
    ang_r = jnp.arange(length // GRID_W, dtype=F32)[:, None] * inv
    ang_c = jnp.arange(GRID_W, dtype=F32)[:, None] * inv

    def table(fn, scale):
        by_row = jnp.where(on_row, fn(ang_r) * scale, 0.0)[:, None, :]
        by_col = jnp.where(on_row, 0.0, fn(ang_c) * scale)[None, :, :]
        return (by_row + by_col).reshape(length, LANES)

    return table(jnp.cos, 1.0), table(jnp.sin, sign)


def _angle(num, den):
    return (num % den).astype(F32) * (2.0 * math.pi / den)


def _channel_dft(width):
    gw = FOURIER_WIDTH // FOURIER_GROUPS
    m = jnp.arange(width)
    same = (m[:, None] // gw) == (m[None, :] // gw)
    th = _angle((m[:, None] % gw) * (m[None, :] % gw), gw)
    cc = jnp.where(same, jnp.cos(th), 0.0) * gw ** -0.5
    sc = jnp.where(same, jnp.sin(th), 0.0) * gw ** -0.5
    return jnp.concatenate([cc, sc], axis=0).astype(BF16)


def _fourier_tables():
    l1n, l2n = DFT_L1, DFT_L2
    length = l1n * l2n
    pack = DFT_PACK
    l2 = jnp.arange(l2n)[:, None, None]
    k1 = jnp.arange(l1n)[None, :, None]
    l1 = jnp.arange(l1n)[None, None, :]
    th = _angle(k1 * l1 * l2n + k1 * l2, length)
    w1 = jnp.stack([jnp.cos(th), -jnp.sin(th)], axis=1) * l1n ** -0.5
    rows, cols = 2 * l1n * pack, l1n * pack
    ws = jnp.transpose(w1.reshape(l2n // pack, pack, 2 * l1n, l1n), (0, 2, 1, 3))
    ws = ws.reshape(l2n // pack, rows, l1n).astype(BF16)
    spread = (jnp.arange(cols)[None, :] // pack == jnp.arange(l1n)[:, None]).astype(BF16)
    same_p = (jnp.arange(rows)[:, None] % pack) == (jnp.arange(cols)[None, :] % pack)
    w1 = jnp.einsum("grb,bc->grc", ws, spread, preferred_element_type=F32)
    w1 = jnp.where(same_p[None], w1, 0.0).astype(BF16)
    k2 = jnp.arange(l2n)
    th2 = _angle(k2[:, None] * k2[None, :], l2n)
    c2, s2 = jnp.cos(th2), jnp.sin(th2)
    w2 = (jnp.concatenate([jnp.concatenate([c2, s2], axis=1),
                           jnp.concatenate([-s2, c2], axis=1)], axis=0) * l2n ** -0.5).astype(BF16)
    return w1, w2


def _fourier_ctx_table(n):
    k = jnp.arange(n)
    th = _angle(k[:, None] * k[None, :], n)
    return (jnp.concatenate([jnp.cos(th), -jnp.sin(th)], axis=0) * n ** -0.5).astype(BF16)


def _block_diag_ones(width):
    m = jnp.arange(width) // HEAD_DIM
    return (m[:, None] == m[None, :]).astype(BF16)


def _prep_w_in(w_in):
    col = jnp.arange(W_TOTAL)
    halved = ((col >= W_ZA) & (col < W_ZA + FOURIER_WIDTH)) | ((col >= W_ZB) & (col < W_Q)) | (col >= W_ZC)
    return (w_in * jnp.where(halved, 0.5, 1.0)).astype(BF16)


def kernel(x, c, ctx, c_ctx, w_ada, b_ada, w_in, sgu_w, sgu_b, q_norm_g, k_norm_g,
           attn_sink, w_pa, w_pb, w_pc, w_out):
    b, length, d = x.shape
    n_ctx = ctx.shape[1]
    assert length == DFT_L1 * DFT_L2 and d == D_MODEL and b + 1 <= MOD_ROWS
    assert w_in.shape == (DEPTH, d, W_TOTAL)
    lat_tile = 512
    ctx_row = b

    cs = jnp.zeros((MOD_ROWS, d), F32).at[:b].set(c).at[b].set(c_ctx)
    mod = _ada_call(cs, w_ada, b_ada)

    rope_tables = _rope_tables(length)
    w1, w2 = _fourier_tables()
    cs_dft = _channel_dft(FOURIER_WIDTH)
    wc = _fourier_ctx_table(n_ctx)
    bdq = _block_diag_ones(ATTN_WIDTH)
    bdk = _block_diag_ones(KV_WIDTH)

    w_all = _prep_w_in(w_in)
    gq_all = jnp.tile(q_norm_g, (1, N_HEADS))[:, None, :]
    gk_all = jnp.tile(k_norm_g, (1, N_KV_HEADS))[:, None, :]
    wsg = jnp.transpose(sgu_w, (0, 2, 1, 3)).reshape(DEPTH, CHUNK, SGU_GROUPS * CHUNK).astype(BF16)
    bsg = jnp.repeat(jnp.swapaxes(sgu_b, 1, 2), SGU_WIDTH // SGU_GROUPS, axis=2)
    wpa, wpb, wpc = w_pa.astype(BF16), w_pb.astype(BF16), w_pc.astype(BF16)
    wout = (0.5 * w_out).astype(BF16)

    xc = ctx
    for l in range(DEPTH):
        last = l == DEPTH - 1
        fa_c, main_c, qt_c, kc, vtc = _proj_call(
            xc.reshape(1, b * n_ctx, d), mod, w_all, bdq, bdk, gq_all, gk_all, None, l,
            tile=2 * n_ctx, ctx_row=ctx_row)
        kc = kc.reshape(b, n_ctx, KV_WIDTH)
        vtc = vtc.reshape(b, n_ctx // BLOCK, KV_WIDTH, BLOCK)
        fa, main, qt, k, vt = _proj_call(
            x, mod, w_all, bdq, bdk, gq_all, gk_all, rope_tables, l, tile=lat_tile, ctx_row=None)
        y = _fourier_call(fa, w1, w2, cs_dft)
        x = _mix_call(x, main, y, qt, k, vt, kc, vtc, mod, attn_sink, wsg, bsg, wpa, wpb, wpc, wout,
                      l, tile=lat_tile, local=True, ctx_row=None)
        if not last:
            y_c = _fourier_ctx_call(fa_c.reshape(b, n_ctx, FOURIER_WIDTH), wc, cs_dft)
            xc = _mix_call(xc, main_c.reshape(b, n_ctx, MAIN_WIDTH), y_c,
                           qt_c.reshape(b, n_ctx // BLOCK, ATTN_WIDTH, BLOCK), None, None, kc, vtc,
                           mod, attn_sink, wsg, bsg, wpa, wpb, wpc, wout, l,
                           tile=n_ctx, local=False, ctx_row=ctx_row)
    return x
```

```python
import functools
import math

import jax
import jax.numpy as jnp
from jax import lax
from jax.experimental import pallas as pl
from jax.experimental.pallas import tpu as pltpu

F32 = jnp.float32
BF16 = jnp.bfloat16

D_MODEL = 1024
DEPTH = 4
GRID_W = 64
FOURIER_WIDTH = 256
FOURIER_GROUPS = 4
SGU_WIDTH = 256
SGU_GROUPS = 4
CHUNK = 128
N_HEADS = 8
N_KV_HEADS = 2
HEAD_DIM = 64
ATTN_WIDTH = N_HEADS * HEAD_DIM
KV_WIDTH = N_KV_HEADS * HEAD_DIM
BLOCK = 128
ROPE_THETA = 10000.0
EPS = 1e-6
NEG_INF = -1e30
LOG2E = math.log2(math.e)

LANES = 128
MOD_ROWS = 8
ONES_ROWS = 16

W_FA = 0
W_ZA = W_FA + FOURIER_WIDTH
W_ZB = W_ZA + FOURIER_WIDTH + 2 * SGU_WIDTH
W_Q = W_ZB + SGU_WIDTH
W_K = W_Q + ATTN_WIDTH
W_V = W_K + KV_WIDTH
W_ZC = W_V + KV_WIDTH
W_G = W_ZC + ATTN_WIDTH
W_TOTAL = W_G + 3 * D_MODEL
MAIN_ZA = 0
MAIN_U = MAIN_ZA + FOURIER_WIDTH
MAIN_VS = MAIN_U + SGU_WIDTH
MAIN_ZB = MAIN_VS + SGU_WIDTH
MAIN_ZC = MAIN_ZB + SGU_WIDTH
MAIN_G = MAIN_ZC + ATTN_WIDTH
MAIN_WIDTH = MAIN_G + 3 * D_MODEL
PROJ_CHUNK = 512

DFT_L1 = 32
DFT_L2 = 256
DFT_PACK = 16
DFT2_GROUP = 8

VMEM_LIMIT = 56 * 1024 * 1024


def _silu(z):
    return 0.5 * z * (1.0 + jnp.tanh(0.5 * z))


def _silu_of_half(zh):
    return zh * (1.0 + jnp.tanh(zh))


def _dot(a, b):
    return jnp.dot(a, b, preferred_element_type=F32)


def _layer(l, rank):
    return lambda *_: (l,) + (0,) * (rank - 1)


def _ada_kernel(c_ref, w_ref, b_ref, o_ref):
    s = _silu(c_ref[...]).astype(BF16)
    o_ref[0] = _dot(s, w_ref[0].astype(BF16)) + b_ref[0]


def _ada_call(cs, w_ada, b_ada):
    d = D_MODEL
    return pl.pallas_call(
        _ada_kernel,
        grid=(DEPTH, 3),
        in_specs=[
            pl.BlockSpec((MOD_ROWS, d), lambda l, j: (0, 0)),
            pl.BlockSpec((1, d, d), lambda l, j: (l, 0, j)),
            pl.BlockSpec((1, 1, d), lambda l, j: (l, 0, j)),
        ],
        out_specs=pl.BlockSpec((1, MOD_ROWS, d), lambda l, j: (l, 0, j)),
        out_shape=jax.ShapeDtypeStruct((DEPTH, MOD_ROWS, 3 * d), F32),
        compiler_params=pltpu.CompilerParams(
            dimension_semantics=("arbitrary", "arbitrary"), vmem_limit_bytes=VMEM_LIMIT),
        name="ada_mod",
    )(cs, w_ada, b_ada.reshape(DEPTH, 1, 3 * d))


def _norm_rope(t, bd_ref, gain, cos, sin, out_scale):
    width = t.shape[-1]
    ssq = _dot((t * t).astype(BF16), bd_ref[...])
    tn = t * lax.rsqrt(ssq * (1.0 / HEAD_DIM) + EPS) * gain
    if cos is not None:
        reps = width // LANES
        cos_w = jnp.concatenate([cos] * reps, axis=1) if reps > 1 else cos
        sin_w = jnp.concatenate([sin] * reps, axis=1) if reps > 1 else sin
        lane = lax.broadcasted_iota(jnp.int32, tn.shape, 1)
        first = (lane % 32) < 16
        partner = jnp.where(first, pltpu.roll(tn, width - 16, 1), pltpu.roll(tn, 16, 1))
        tn = tn * cos_w + partner * sin_w
    if out_scale != 1.0:
        tn = tn * out_scale
    return tn


def _proj_kernel(x_ref, mod_ref, w_ref, bdq_ref, bdk_ref, gq_ref, gk_ref, cos_ref, sin_ref,
                 fa_ref, main_ref, qt_ref, k_ref, vt_ref, *, ctx_row, rope):
    d = D_MODEL
    nsub = x_ref.shape[1] // BLOCK
    xt = x_ref[0]
    ms = jnp.mean(xt * xt, axis=-1, keepdims=True)
    row = pl.program_id(0) if ctx_row is None else ctx_row
    modrow = mod_ref[0, pl.ds(row, 1), :]
    shift = modrow[:, 0:d]
    scale = modrow[:, d:2 * d]
    hb = ((xt * lax.rsqrt(ms + EPS)) * (1.0 + scale) + shift).astype(BF16)
    cos = cos_ref[...] if rope else None
    sin = sin_ref[...] if rope else None

    def proj(c0, width):
        return _dot(hb, w_ref[0, :, c0:c0 + width])

    chunks = [(src + c0, dst + c0)
              for src, dst, width in ((W_ZA, MAIN_ZA, W_Q - W_ZA), (W_ZC, MAIN_ZC, W_TOTAL - W_ZC))
              for c0 in range(0, width, PROJ_CHUNK)]

    def wide(count):
        for _ in range(min(count, len(chunks))):
            src, dst = chunks.pop(0)
            main_ref[0, :, dst:dst + PROJ_CHUNK] = proj(src, PROJ_CHUNK).astype(BF16)

    q_raw = proj(W_Q, ATTN_WIDTH)
    k_raw = proj(W_K, KV_WIDTH)
    v = proj(W_V, KV_WIDTH)
    wide(1)
    q = _norm_rope(q_raw, bdq_ref, gq_ref[0], cos, sin, HEAD_DIM ** -0.5 * LOG2E)
    wide(1)
    for s in range(nsub):
        qt_ref[0, s] = q[s * BLOCK:(s + 1) * BLOCK, :].T.astype(BF16)
    wide(1)
    k = _norm_rope(k_raw, bdk_ref, gk_ref[0], cos, sin, 1.0)
    k_ref[0] = k.astype(BF16)
    for s in range(nsub):
        vt_ref[0, s] = v[s * BLOCK:(s + 1) * BLOCK, :].T.astype(BF16)
    fa_ref[0] = proj(W_FA, FOURIER_WIDTH).astype(BF16)
    wide(len(chunks))


def _proj_call(x, mod, w_all, bdq, bdk, gq_all, gk_all, rope_tables, l, *, tile, ctx_row):
    b, length, d = x.shape
    nt = length // tile
    nsub = tile // BLOCK
    const = lambda bi, i: (0, 0)
    tok = lambda bi, i: (bi, i, 0)
    blk = lambda bi, i: (bi, i, 0, 0)
    rope = rope_tables is not None
    if rope:
        cos, sin = rope_tables
        pos, pos_rows = (lambda bi, i: (i, 0)), tile
    else:
        cos = sin = jnp.zeros((MOD_ROWS, LANES), F32)
        pos, pos_rows = const, MOD_ROWS
    kern = functools.partial(_proj_kernel, ctx_row=ctx_row, rope=rope)
    return pl.pallas_call(
        kern,
        grid=(b, nt),
        in_specs=[
            pl.BlockSpec((1, tile, d), tok),
            pl.BlockSpec((1, MOD_ROWS, 3 * d), _layer(l, 3)),
            pl.BlockSpec((1, d, W_TOTAL), _layer(l, 3), pipeline_mode=pl.Buffered(1)),
            pl.BlockSpec((ATTN_WIDTH, ATTN_WIDTH), const),
            pl.BlockSpec((KV_WIDTH, KV_WIDTH), const),
            pl.BlockSpec((1, 1, ATTN_WIDTH), _layer(l, 3)),
            pl.BlockSpec((1, 1, KV_WIDTH), _layer(l, 3)),
            pl.BlockSpec((pos_rows, LANES), pos),
            pl.BlockSpec((pos_rows, LANES), pos),
        ],
        out_specs=[
            pl.BlockSpec((1, tile, FOURIER_WIDTH), tok),
            pl.BlockSpec((1, tile, MAIN_WIDTH), tok),
            pl.BlockSpec((1, nsub, ATTN_WIDTH, BLOCK), blk),
            pl.BlockSpec((1, tile, KV_WIDTH), tok),
            pl.BlockSpec((1, nsub, KV_WIDTH, BLOCK), blk),
        ],
        out_shape=[
            jax.ShapeDtypeStruct((b, length, FOURIER_WIDTH), BF16),
            jax.ShapeDtypeStruct((b, length, MAIN_WIDTH), BF16),
            jax.ShapeDtypeStruct((b, length // BLOCK, ATTN_WIDTH, BLOCK), BF16),
            jax.ShapeDtypeStruct((b, length, KV_WIDTH), BF16),
            jax.ShapeDtypeStruct((b, length // BLOCK, KV_WIDTH, BLOCK), BF16),
        ],
        compiler_params=pltpu.CompilerParams(
            dimension_semantics=("arbitrary", "arbitrary"), vmem_limit_bytes=VMEM_LIMIT),
        name="proj_ctx" if ctx_row is not None else "proj_lat",
    )(x, mod, w_all, bdq, bdk, gq_all, gk_all, cos, sin)


def _fourier_kernel(x_ref, w1_ref, w2_ref, cs_ref, y_ref, a_s):
    l1n, l2n, pack, cw = DFT_L1, DFT_L2, DFT_PACK, FOURIER_WIDTH
    for g in range(l2n // pack):
        xg = x_ref[0, :, g].reshape(l1n * pack, cw)
        ag = _dot(w1_ref[g], xg).astype(BF16)
        a_s[:, :, g * pack:(g + 1) * pack, :] = ag.reshape(2, l1n, pack, cw)
    grp = DFT2_GROUP
    for g in range(l1n // grp):
        rhs = jnp.concatenate(
            [jnp.concatenate([a_s[0, g * grp + j], a_s[1, g * grp + j]], axis=0)
             for j in range(grp)], axis=1)
        f = _dot(w2_ref[...], rhs)
        for j in range(grp):
            cols = slice(j * cw, (j + 1) * cw)
            fc = jnp.concatenate([f[:l2n, cols], f[l2n:, cols]], axis=1).astype(BF16)
            out_cols = slice((g * grp + j) * cw, (g * grp + j + 1) * cw)
            y_ref[0, :, out_cols] = _dot(fc, cs_ref[...]).astype(BF16)


def _fourier_call(fa, w1, w2, cs):
    b, length, cw = fa.shape
    l1n, l2n, pack = DFT_L1, DFT_L2, DFT_PACK
    const = lambda bi: (0, 0)
    return pl.pallas_call(
        _fourier_kernel,
        grid=(b,),
        in_specs=[
            pl.BlockSpec((1, l1n, l2n // pack, pack, cw), lambda bi: (bi, 0, 0, 0, 0)),
            pl.BlockSpec(w1.shape, lambda bi: (0, 0, 0), pipeline_mode=pl.Buffered(1)),
            pl.BlockSpec(w2.shape, const),
            pl.BlockSpec(cs.shape, const),
        ],
        out_specs=pl.BlockSpec((1, l2n, l1n * cw), lambda bi: (bi, 0, 0)),
        out_shape=jax.ShapeDtypeStruct((b, l2n, l1n * cw), BF16),
        scratch_shapes=[pltpu.VMEM((2, l1n, l2n, cw), BF16)],
        compiler_params=pltpu.CompilerParams(
            dimension_semantics=("arbitrary",), vmem_limit_bytes=VMEM_LIMIT),
        name="fourier_lat",
    )(fa.reshape(b, l1n, l2n // pack, pack, cw), w1, w2, cs)


def _fourier_ctx_kernel(fa_ref, wc_ref, cs_ref, y_ref):
    n = fa_ref.shape[1]
    f = _dot(wc_ref[...], fa_ref[0])
    fc = jnp.concatenate([f[:n], f[n:]], axis=1).astype(BF16)
    y_ref[0] = _dot(fc, cs_ref[...])


def _fourier_ctx_call(fa, wc, cs):
    b, length, width = fa.shape
    return pl.pallas_call(
        _fourier_ctx_kernel,
        grid=(b,),
        in_specs=[
            pl.BlockSpec((1, length, width), lambda bi: (bi, 0, 0)),
            pl.BlockSpec(wc.shape, lambda bi: (0, 0)),
            pl.BlockSpec(cs.shape, lambda bi: (0, 0)),
        ],
        out_specs=pl.BlockSpec((1, length, width), lambda bi: (bi, 0, 0)),
        out_shape=jax.ShapeDtypeStruct((b, length, width), F32),
        compiler_params=pltpu.CompilerParams(
            dimension_semantics=("arbitrary",), vmem_limit_bytes=VMEM_LIMIT),
        name="fourier_ctx",
    )(fa, wc, cs)


def _mix_kernel(*refs, local, tile, n_blocks, ctx_row, layer):
    if local:
        (x_ref, main_ref, y_ref, perm_ref, qt_ref, kp_ref, km_ref, kn_ref, vtp_ref, vtm_ref, vtn_ref,
         kc_ref, vtc_ref, mod_ref, sink_ref, wsg_ref, bsg_ref,
         wpa_ref, wpb_ref, wpc_ref, wout_ref, o_ref,
         kbuf, vtbuf, ya_s, yb_s, yc_s, ma_s) = refs
    else:
        (x_ref, main_ref, y_ref, qt_ref, kc_ref, vtc_ref, mod_ref, sink_ref, wsg_ref, bsg_ref,
         wpa_ref, wpb_ref, wpc_ref, wout_ref, o_ref,
         ya_s, yb_s, yc_s, ma_s) = refs
    d = D_MODEL
    nsub = tile // BLOCK
    tile_idx = pl.program_id(1)
    heads_per_kv = N_HEADS // N_KV_HEADS
    cols_all = N_HEADS * BLOCK

    if local:
        kbuf[0:BLOCK, :] = kp_ref[0]
        kbuf[BLOCK:BLOCK + tile, :] = km_ref[0]
        kbuf[BLOCK + tile:2 * BLOCK + tile, :] = kn_ref[0]
        vtbuf[0] = vtp_ref[0, 0]
        for s in range(nsub):
            vtbuf[1 + s] = vtm_ref[0, s]
        vtbuf[1 + nsub] = vtn_ref[0, 0]

    def with_ones(vt):
        return jnp.concatenate([vt, jnp.ones((ONES_ROWS, vt.shape[1]), BF16)], axis=0)

    vt_ctx = with_ones(jnp.concatenate([vtc_ref[0, s] for s in range(vtc_ref.shape[1])], axis=1))

    if local:
        cw = FOURIER_WIDTH
        stacked = jnp.concatenate(
            [y_ref[0, :, k1 * cw:(k1 + 1) * cw] for k1 in range(DFT_L1)], axis=0)
        y = _dot(perm_ref[...], stacked)
    else:
        y = y_ref[0]
    ya_s[...] = (y * _silu_of_half(main_ref[0, :, MAIN_ZA:MAIN_ZA + FOURIER_WIDTH].astype(F32))).astype(BF16)

    sgu_grp = lax.broadcasted_iota(jnp.int32, (CHUNK, SGU_WIDTH), 1) // (SGU_WIDTH // SGU_GROUPS)
    zero_head = jnp.zeros((HEAD_DIM, BLOCK), BF16)
    sink_row = jnp.concatenate(
        [jnp.full((1, BLOCK), sink_ref[layer, hh] * LOG2E, F32) for hh in range(N_HEADS)], axis=1)
    kq_diff = (lax.broadcasted_iota(jnp.int32, (BLOCK, cols_all), 0)
               - lax.broadcasted_iota(jnp.int32, (BLOCK, cols_all), 1) % BLOCK)

    def sgu(n):
        rows = slice(n * BLOCK, (n + 1) * BLOCK)
        vs = main_ref[0, rows, MAIN_VS:MAIN_VS + SGU_WIDTH].astype(F32)
        vn = (vs * lax.rsqrt(jnp.mean(vs * vs, axis=-1, keepdims=True) + EPS)).astype(BF16)
        rhs = jnp.concatenate(
            [jnp.where(sgu_grp == g, vn, jnp.zeros_like(vn)) for g in range(SGU_GROUPS)], axis=0)
        mixed = _dot(wsg_ref[0], rhs) + bsg_ref[0]
        u = main_ref[0, rows, MAIN_U:MAIN_U + SGU_WIDTH].astype(F32)
        zb = main_ref[0, rows, MAIN_ZB:MAIN_ZB + SGU_WIDTH].astype(F32)
        yb_s[rows, :] = (u * mixed * _silu_of_half(zb)).astype(BF16)

    def scores(n):
        qt = qt_ref[0, n]
        cols = []
        for hh in range(N_HEADS):
            qh = qt[hh * HEAD_DIM:(hh + 1) * HEAD_DIM, :]
            cols.append(jnp.concatenate(
                [qh, zero_head] if hh < heads_per_kv else [zero_head, qh], axis=0))
        qst = jnp.concatenate(cols, axis=1)

        s_ctx = _dot(kc_ref[0], qst)
        s_loc = _dot(kbuf[n * BLOCK:(n + 3) * BLOCK, :], qst) if local else None
        return s_ctx, s_loc

    def attend(n, s_ctx, s_loc):
        rows = slice(n * BLOCK, (n + 1) * BLOCK)
        m = jnp.maximum(jnp.max(s_ctx, axis=0, keepdims=True), sink_row)
        if local:
            gblk = tile_idx * nsub + n
            s_prev = jnp.where(kq_diff >= jnp.where(gblk > 0, 0, BLOCK), s_loc[0:BLOCK], NEG_INF)
            s_own = s_loc[BLOCK:2 * BLOCK]
            s_next = jnp.where(kq_diff <= jnp.where(gblk < n_blocks - 1, 0, -BLOCK),
                               s_loc[2 * BLOCK:3 * BLOCK], NEG_INF)
            m_loc = jnp.max(jnp.maximum(jnp.maximum(s_prev, s_own), s_next), axis=0, keepdims=True)
            m = jnp.maximum(m, m_loc)
        e_ctx = jnp.exp2(s_ctx - m)
        ot = _dot(vt_ctx, e_ctx.astype(BF16))
        if local:
            e_loc = jnp.concatenate(
                [jnp.exp2(s_prev - m), jnp.exp2(s_own - m), jnp.exp2(s_next - m)], axis=0)
            vt_loc = with_ones(jnp.concatenate([vtbuf[n], vtbuf[n + 1], vtbuf[n + 2]], axis=1))
            ot = ot + _dot(vt_loc, e_loc.astype(BF16))
        den = ot[KV_WIDTH:KV_WIDTH + 1] + jnp.exp2(sink_row - m)
        ot = ot[0:KV_WIDTH] * (1.0 / den)
        for j in range(ATTN_WIDTH // LANES):
            h = (2 * j) // heads_per_kv
            pair = jnp.concatenate(
                [ot[h * HEAD_DIM:(h + 1) * HEAD_DIM, (2 * j) * BLOCK:(2 * j + 1) * BLOCK],
                 ot[h * HEAD_DIM:(h + 1) * HEAD_DIM, (2 * j + 1) * BLOCK:(2 * j + 2) * BLOCK]], axis=0)
            zc = main_ref[0, rows, MAIN_ZC + j * LANES:MAIN_ZC + (j + 1) * LANES].astype(F32)
            yc_s[rows, j * LANES:(j + 1) * LANES] = (pair.T * _silu_of_half(zc)).astype(BF16)

    ahead = scores(0)
    for n in range(nsub):
        sgu(n)
        current, ahead = ahead, (scores(n + 1) if n + 1 < nsub else None)
        cols = slice(n * (d // nsub), (n + 1) * (d // nsub))
        ma_s[:, cols] = _dot(ya_s[...], wpa_ref[0, :, cols])
        attend(n, *current)

    t = jnp.tanh(main_ref[0, :, MAIN_G:MAIN_G + 3 * d].astype(F32))
    ma = ma_s[...]
    mb = _dot(yb_s[...], wpb_ref[0])
    mc = _dot(yc_s[...], wpc_ref[0])
    merged2 = (ma + t[:, 0:d] * ma) + (mb + t[:, d:2 * d] * mb) + (mc + t[:, 2 * d:3 * d] * mc)
    out = _dot(merged2.astype(BF16), wout_ref[0])
    row = pl.program_id(0) if ctx_row is None else ctx_row
    gate = mod_ref[0, pl.ds(row, 1), 2 * d:3 * d]
    o_ref[0] = x_ref[0] + gate * out


def _mix_call(x, main, y, qt, k, vt, kc, vtc, mod, sinks, wsg, bsg, wpa, wpb, wpc, wout, l,
              *, tile, local, ctx_row):
    b, length, d = x.shape
    nt = length // tile
    n_ctx = kc.shape[1]
    sub = tile // BLOCK
    n_blocks = length // BLOCK
    tok = lambda bi, i: (bi, i, 0)
    blk = lambda bi, i: (bi, i, 0, 0)
    per_b = lambda bi, i: (bi, 0, 0)
    per_b4 = lambda bi, i: (bi, 0, 0, 0)

    in_specs = [
        pl.BlockSpec((1, tile, d), tok),
        pl.BlockSpec((1, tile, MAIN_WIDTH), tok),
    ]
    args = [x, main, y]
    if local:
        rows_k2 = tile // DFT_L1
        r = jnp.arange(tile)
        perm = (r[:, None] == (r[None, :] % rows_k2) * DFT_L1 + r[None, :] // rows_k2).astype(BF16)
        in_specs += [pl.BlockSpec((1, rows_k2, DFT_L1 * FOURIER_WIDTH), tok),
                     pl.BlockSpec((tile, tile), lambda bi, i: (0, 0))]
        args.append(perm)
    else:
        in_specs.append(pl.BlockSpec((1, tile, FOURIER_WIDTH), tok))
    in_specs.append(pl.BlockSpec((1, sub, ATTN_WIDTH, BLOCK), blk))
    args.append(qt)
    if local:
        prev_blk = lambda bi, i: jnp.maximum(i * sub - 1, 0)
        next_blk = lambda bi, i: jnp.minimum((i + 1) * sub, n_blocks - 1)
        in_specs += [
            pl.BlockSpec((1, BLOCK, KV_WIDTH), lambda bi, i: (bi, prev_blk(bi, i), 0)),
            pl.BlockSpec((1, tile, KV_WIDTH), tok),
            pl.BlockSpec((1, BLOCK, KV_WIDTH), lambda bi, i: (bi, next_blk(bi, i), 0)),
            pl.BlockSpec((1, 1, KV_WIDTH, BLOCK), lambda bi, i: (bi, prev_blk(bi, i), 0, 0)),
            pl.BlockSpec((1, sub, KV_WIDTH, BLOCK), blk),
            pl.BlockSpec((1, 1, KV_WIDTH, BLOCK), lambda bi, i: (bi, next_blk(bi, i), 0, 0)),
        ]
        args += [k, k, k, vt, vt, vt]
    in_specs += [
        pl.BlockSpec((1, n_ctx, KV_WIDTH), per_b),
        pl.BlockSpec((1, n_ctx // BLOCK, KV_WIDTH, BLOCK), per_b4),
        pl.BlockSpec((1, MOD_ROWS, 3 * d), _layer(l, 3)),
        pl.BlockSpec(memory_space=pltpu.SMEM),
    ]
    args += [kc, vtc, mod, sinks]
    for w in (wsg, bsg, wpa, wpb, wpc, wout):
        in_specs.append(pl.BlockSpec((1,) + w.shape[1:], _layer(l, 3)))
        args.append(w)

    scratch = []
    if local:
        scratch += [pltpu.VMEM((tile + 2 * BLOCK, KV_WIDTH), BF16),
                    pltpu.VMEM((sub + 2, KV_WIDTH, BLOCK), BF16)]
    scratch += [
        pltpu.VMEM((tile, FOURIER_WIDTH), BF16),
        pltpu.VMEM((tile, SGU_WIDTH), BF16),
        pltpu.VMEM((tile, ATTN_WIDTH), BF16),
        pltpu.VMEM((tile, d), F32),
    ]
    kern = functools.partial(_mix_kernel, local=local, tile=tile, n_blocks=n_blocks,
                             ctx_row=ctx_row, layer=l)
    return pl.pallas_call(
        kern,
        grid=(b, nt),
        in_specs=in_specs,
        out_specs=pl.BlockSpec((1, tile, d), tok),
        out_shape=jax.ShapeDtypeStruct((b, length, d), F32),
        scratch_shapes=scratch,
        compiler_params=pltpu.CompilerParams(
            dimension_semantics=("arbitrary", "arbitrary"), vmem_limit_bytes=VMEM_LIMIT),
        name="mix_lat" if local else "mix_ctx",
    )(*args)


def _rope_tables(length):
    nf = HEAD_DIM // 4
    inv = jnp.tile(ROPE_THETA ** (-jnp.arange(nf, dtype=F32) / nf), LANES // nf)[None, :]
    lane = jnp.arange(LANES)[None, :]
    on_row = (lane // (2 * nf)) % 2 == 0
    sign = jnp.where((lane // nf) % 2 == 0, -1.0, 1.0)
    ang_r = jnp.arange(length // GRID_W, dtype=F32)[:, None] * inv
    ang_c = jnp.arange(GRID_W, dtype=F32)[:, None] * inv

    def table(fn, scale):
        by_row = jnp.where(on_row, fn(ang_r) * scale, 0.0)[:, None, :]
        by_col = jnp.where(on_row, 0.0, fn(ang_c) * scale)[None, :, :]
        return (by_row + by_col).reshape(length, LANES)

    return table(jnp.cos, 1.0), table(jnp.sin, sign)


def _angle(num, den):
    return (num % den).astype(F32) * (2.0 * math.pi / den)


def _channel_dft(width):
    gw = FOURIER_WIDTH // FOURIER_GROUPS
    m = jnp.arange(width)
    same = (m[:, None] // gw) == (m[None, :] // gw)
    th = _angle((m[:, None] % gw) * (m[None, :] % gw), gw)
    cc = jnp.where(same, jnp.cos(th), 0.0) * gw ** -0.5
    sc = jnp.where(same, jnp.sin(th), 0.0) * gw ** -0.5
    return jnp.concatenate([cc, sc], axis=0).astype(BF16)


def _fourier_tables():
    l1n, l2n = DFT_L1, DFT_L2
    length = l1n * l2n
    pack = DFT_PACK
    l2 = jnp.arange(l2n)[:, None, None]
    k1 = jnp.arange(l1n)[None, :, None]
    l1 = jnp.arange(l1n)[None, None, :]
    th = _angle(k1 * l1 * l2n + k1 * l2, length)
    w1 = jnp.stack([jnp.cos(th), -jnp.sin(th)], axis=1) * l1n ** -0.5
    rows, cols = 2 * l1n * pack, l1n * pack
    ws = jnp.transpose(w1.reshape(l2n // pack, pack, 2 * l1n, l1n), (0, 2, 1, 3))
    ws = ws.reshape(l2n // pack, rows, l1n).astype(BF16)
    spread = (jnp.arange(cols)[None, :] // pack == jnp.arange(l1n)[:, None]).astype(BF16)
    same_p = (jnp.arange(rows)[:, None] % pack) == (jnp.arange(cols)[None, :] % pack)
    w1 = jnp.einsum("grb,bc->grc", ws, spread, preferred_element_type=F32)
    w1 = jnp.where(same_p[None], w1, 0.0).astype(BF16)
    k2 = jnp.arange(l2n)
    th2 = _angle(k2[:, None] * k2[None, :], l2n)
    c2, s2 = jnp.cos(th2), jnp.sin(th2)
    w2 = (jnp.concatenate([jnp.concatenate([c2, s2], axis=1),
                           jnp.concatenate([-s2, c2], axis=1)], axis=0) * l2n ** -0.5).astype(BF16)
    return w1, w2


def _fourier_ctx_table(n):
    k = jnp.arange(n)
    th = _angle(k[:, None] * k[None, :], n)
    return (jnp.concatenate([jnp.cos(th), -jnp.sin(th)], axis=0) * n ** -0.5).astype(BF16)


def _block_diag_ones(width):
    m = jnp.arange(width) // HEAD_DIM
    return (m[:, None] == m[None, :]).astype(BF16)


def _prep_w_in(w_in):
    col = jnp.arange(W_TOTAL)
    halved = ((col >= W_ZA) & (col < W_ZA + FOURIER_WIDTH)) | ((col >= W_ZB) & (col < W_Q)) | (col >= W_ZC)
    return (w_in * jnp.where(halved, 0.5, 1.0)).astype(BF16)


def kernel(x, c, ctx, c_ctx, w_ada, b_ada, w_in, sgu_w, sgu_b, q_norm_g, k_norm_g,
           attn_sink, w_pa, w_pb, w_pc, w_out):
    b, length, d = x.shape
    n_ctx = ctx.shape[1]
    assert length == DFT_L1 * DFT_L2 and d == D_MODEL and b + 1 <= MOD_ROWS
    assert w_in.shape == (DEPTH, d, W_TOTAL)
    lat_tile = 512
    ctx_row = b

    cs = jnp.zeros((MOD_ROWS, d), F32).at[:b].set(c).at[b].set(c_ctx)
    mod = _ada_call(cs, w_ada, b_ada)

    rope_tables = _rope_tables(length)
    w1, w2 = _fourier_tables()
    cs_dft = _channel_dft(FOURIER_WIDTH)
    wc = _fourier_ctx_table(n_ctx)
    bdq = _block_diag_ones(ATTN_WIDTH)
    bdk = _block_diag_ones(KV_WIDTH)

    w_all = _prep_w_in(w_in)
    gq_all = jnp.tile(q_norm_g, (1, N_HEADS))[:, None, :]
    gk_all = jnp.tile(k_norm_g, (1, N_KV_HEADS))[:, None, :]
    wsg = jnp.transpose(sgu_w, (0, 2, 1, 3)).reshape(DEPTH, CHUNK, SGU_GROUPS * CHUNK).astype(BF16)
    bsg = jnp.repeat(jnp.swapaxes(sgu_b, 1, 2), SGU_WIDTH // SGU_GROUPS, axis=2)
    wpa, wpb, wpc = w_pa.astype(BF16), w_pb.astype(BF16), w_pc.astype(BF16)
    wout = (0.5 * w_out).astype(BF16)

    xc = ctx
    for l in range(DEPTH):
        last = l == DEPTH - 1
        fa_c, main_c, qt_c, kc, vtc = _proj_call(
            xc.reshape(1, b * n_ctx, d), mod, w_all, bdq, bdk, gq_all, gk_all, None, l,
            tile=2 * n_ctx, ctx_row=ctx_row)
        kc = kc.reshape(b, n_ctx, KV_WIDTH)
        vtc = vtc.reshape(b, n_ctx // BLOCK, KV_WIDTH, BLOCK)
        fa, main, qt, k, vt = _proj_call(
            x, mod, w_all, bdq, bdk, gq_all, gk_all, rope_tables, l, tile=lat_tile, ctx_row=None)
        y = _fourier_call(fa, w1, w2, cs_dft)
        x = _mix_call(x, main, y, qt, k, vt, kc, vtc, mod, attn_sink, wsg, bsg, wpa, wpb, wpc, wout,
                      l, tile=lat_tile, local=True, ctx_row=None)
        if not last:
            y_c = _fourier_ctx_call(fa_c.reshape(b, n_ctx, FOURIER_WIDTH), wc, cs_dft)
            xc = _mix_call(xc, main_c.reshape(b, n_ctx, MAIN_WIDTH), y_c,
                           qt_c.reshape(b, n_ctx // BLOCK, ATTN_WIDTH, BLOCK), None, None, kc, vtc,
                           mod, attn_sink, wsg, bsg, wpa, wpb, wpc, wout, l,
                           tile=n_ctx, local=False, ctx_row=ctx_row)
    return x
```

```python
import functools
import math

import jax
import jax.numpy as jnp
from jax import lax
from jax.experimental import pallas as pl
from jax.experimental.pallas import tpu as pltpu

F32 = jnp.float32
BF16 = jnp.bfloat16

D_MODEL = 1024
DEPTH = 4
GRID_W = 64
FOURIER_WIDTH = 256
FOURIER_GROUPS = 4
SGU_WIDTH = 256
SGU_GROUPS = 4
CHUNK = 128
N_HEADS = 8
N_KV_HEADS = 2
HEAD_DIM = 64
ATTN_WIDTH = N_HEADS * HEAD_DIM
KV_WIDTH = N_KV_HEADS * HEAD_DIM
BLOCK = 128
ROPE_THETA = 10000.0
EPS = 1e-6
NEG_INF = -1e30
LOG2E = math.log2(math.e)

LANES = 128
MOD_ROWS = 8
ONES_ROWS = 16

W_FA = 0
W_ZA = W_FA + FOURIER_WIDTH
W_ZB = W_ZA + FOURIER_WIDTH + 2 * SGU_WIDTH
W_Q = W_ZB + SGU_WIDTH
W_K = W_Q + ATTN_WIDTH
W_V = W_K + KV_WIDTH
W_ZC = W_V + KV_WIDTH
W_G = W_ZC + ATTN_WIDTH
W_TOTAL = W_G + 3 * D_MODEL
MAIN_ZA = 0
MAIN_U = MAIN_ZA + FOURIER_WIDTH
MAIN_VS = MAIN_U + SGU_WIDTH
MAIN_ZB = MAIN_VS + SGU_WIDTH
MAIN_ZC = MAIN_ZB + SGU_WIDTH
MAIN_G = MAIN_ZC + ATTN_WIDTH
MAIN_WIDTH = MAIN_G + 3 * D_MODEL
PROJ_CHUNK = 512

DFT_L1 = 32
DFT_L2 = 256
DFT_PACK = 16
DFT2_GROUP = 8

VMEM_LIMIT = 56 * 1024 * 1024


def _silu(z):
    return 0.5 * z * (1.0 + jnp.tanh(0.5 * z))


def _silu_of_half(zh):
    return zh * (1.0 + jnp.tanh(zh))


def _dot(a, b):
    return jnp.dot(a, b, preferred_element_type=F32)


def _layer(l, rank):
    return lambda *_: (l,) + (0,) * (rank - 1)


def _ada_kernel(c_ref, w_ref, b_ref, o_ref):
    s = _silu(c_ref[...]).astype(BF16)
    o_ref[0] = _dot(s, w_ref[0].astype(BF16)) + b_ref[0]


def _ada_call(cs, w_ada, b_ada):
    d = D_MODEL
    return pl.pallas_call(
        _ada_kernel,
        grid=(DEPTH, 3),
        in_specs=[
            pl.BlockSpec((MOD_ROWS, d), lambda l, j: (0, 0)),
            pl.BlockSpec((1, d, d), lambda l, j: (l, 0, j)),
            pl.BlockSpec((1, 1, d), lambda l, j: (l, 0, j)),
        ],
        out_specs=pl.BlockSpec((1, MOD_ROWS, d), lambda l, j: (l, 0, j)),
        out_shape=jax.ShapeDtypeStruct((DEPTH, MOD_ROWS, 3 * d), F32),
        compiler_params=pltpu.CompilerParams(
            dimension_semantics=("arbitrary", "arbitrary"), vmem_limit_bytes=VMEM_LIMIT),
        name="ada_mod",
    )(cs, w_ada, b_ada.reshape(DEPTH, 1, 3 * d))


def _norm_rope(t, bd_ref, gain, cos, sin, out_scale):
    width = t.shape[-1]
    ssq = _dot((t * t).astype(BF16), bd_ref[...])
    tn = t * lax.rsqrt(ssq * (1.0 / HEAD_DIM) + EPS) * gain
    if cos is not None:
        reps = width // LANES
        cos_w = jnp.concatenate([cos] * reps, axis=1) if reps > 1 else cos
        sin_w = jnp.concatenate([sin] * reps, axis=1) if reps > 1 else sin
        lane = lax.broadcasted_iota(jnp.int32, tn.shape, 1)
        first = (lane % 32) < 16
        partner = jnp.where(first, pltpu.roll(tn, width - 16, 1), pltpu.roll(tn, 16, 1))
        tn = tn * cos_w + partner * sin_w
    if out_scale != 1.0:
        tn = tn * out_scale
    return tn


def _proj_kernel(x_ref, mod_ref, w_ref, bdq_ref, bdk_ref, gq_ref, gk_ref, cos_ref, sin_ref,
                 fa_ref, main_ref, qt_ref, k_ref, vt_ref, *, ctx_row, rope):
    d = D_MODEL
    nsub = x_ref.shape[1] // BLOCK
    xt = x_ref[0]
    ms = jnp.mean(xt * xt, axis=-1, keepdims=True)
    row = pl.program_id(0) if ctx_row is None else ctx_row
    modrow = mod_ref[0, pl.ds(row, 1), :]
    shift = modrow[:, 0:d]
    scale = modrow[:, d:2 * d]
    hb = ((xt * lax.rsqrt(ms + EPS)) * (1.0 + scale) + shift).astype(BF16)
    cos = cos_ref[...] if rope else None
    sin = sin_ref[...] if rope else None

    def proj(c0, width):
        return _dot(hb, w_ref[0, :, c0:c0 + width])

    chunks = [(src + c0, dst + c0)
              for src, dst, width in ((W_ZA, MAIN_ZA, W_Q - W_ZA), (W_ZC, MAIN_ZC, W_TOTAL - W_ZC))
              for c0 in range(0, width, PROJ_CHUNK)]

    def wide(count):
        for _ in range(min(count, len(chunks))):
            src, dst = chunks.pop(0)
            main_ref[0, :, dst:dst + PROJ_CHUNK] = proj(src, PROJ_CHUNK).astype(BF16)

    q_raw = proj(W_Q, ATTN_WIDTH)
    k_raw = proj(W_K, KV_WIDTH)
    v = proj(W_V, KV_WIDTH)
    wide(1)
    q = _norm_rope(q_raw, bdq_ref, gq_ref[0], cos, sin, HEAD_DIM ** -0.5 * LOG2E)
    wide(1)
    for s in range(nsub):
        qt_ref[0, s] = q[s * BLOCK:(s + 1) * BLOCK, :].T.astype(BF16)
    wide(1)
    k = _norm_rope(k_raw, bdk_ref, gk_ref[0], cos, sin, 1.0)
    k_ref[0] = k.astype(BF16)
    for s in range(nsub):
        vt_ref[0, s] = v[s * BLOCK:(s + 1) * BLOCK, :].T.astype(BF16)
    fa_ref[0] = proj(W_FA, FOURIER_WIDTH).astype(BF16)
    wide(len(chunks))


def _proj_call(x, mod, w_all, bdq, bdk, gq_all, gk_all, rope_tables, l, *, tile, ctx_row):
    b, length, d = x.shape
    nt = length // tile
    nsub = tile // BLOCK
    const = lambda bi, i: (0, 0)
    tok = lambda bi, i: (bi, i, 0)
    blk = lambda bi, i: (bi, i, 0, 0)
    rope = rope_tables is not None
    if rope:
        cos, sin = rope_tables
        pos, pos_rows = (lambda bi, i: (i, 0)), tile
    else:
        cos = sin = jnp.zeros((MOD_ROWS, LANES), F32)
        pos, pos_rows = const, MOD_ROWS
    kern = functools.partial(_proj_kernel, ctx_row=ctx_row, rope=rope)
    return pl.pallas_call(
        kern,
        grid=(b, nt),
        in_specs=[
            pl.BlockSpec((1, tile, d), tok),
            pl.BlockSpec((1, MOD_ROWS, 3 * d), _layer(l, 3)),
            pl.BlockSpec((1, d, W_TOTAL), _layer(l, 3), pipeline_mode=pl.Buffered(1)),
            pl.BlockSpec((ATTN_WIDTH, ATTN_WIDTH), const),
            pl.BlockSpec((KV_WIDTH, KV_WIDTH), const),
            pl.BlockSpec((1, 1, ATTN_WIDTH), _layer(l, 3)),
            pl.BlockSpec((1, 1, KV_WIDTH), _layer(l, 3)),
            pl.BlockSpec((pos_rows, LANES), pos),
            pl.BlockSpec((pos_rows, LANES), pos),
        ],
        out_specs=[
            pl.BlockSpec((1, tile, FOURIER_WIDTH), tok),
            pl.BlockSpec((1, tile, MAIN_WIDTH), tok),
            pl.BlockSpec((1, nsub, ATTN_WIDTH, BLOCK), blk),
            pl.BlockSpec((1, tile, KV_WIDTH), tok),
            pl.BlockSpec((1, nsub, KV_WIDTH, BLOCK), blk),
        ],
        out_shape=[
            jax.ShapeDtypeStruct((b, length, FOURIER_WIDTH), BF16),
            jax.ShapeDtypeStruct((b, length, MAIN_WIDTH), BF16),
            jax.ShapeDtypeStruct((b, length // BLOCK, ATTN_WIDTH, BLOCK), BF16),
            jax.ShapeDtypeStruct((b, length, KV_WIDTH), BF16),
            jax.ShapeDtypeStruct((b, length // BLOCK, KV_WIDTH, BLOCK), BF16),
        ],
        compiler_params=pltpu.CompilerParams(
            dimension_semantics=("arbitrary", "arbitrary"), vmem_limit_bytes=VMEM_LIMIT),
        name="proj_ctx" if ctx_row is not None else "proj_lat",
    )(x, mod, w_all, bdq, bdk, gq_all, gk_all, cos, sin)


def _fourier_kernel(x_ref, w1_ref, w2_ref, cs_ref, y_ref, a_s):
    l1n, l2n, pack, cw = DFT_L1, DFT_L2, DFT_PACK, FOURIER_WIDTH
    for g in range(l2n // pack):
        xg = x_ref[0, :, g].reshape(l1n * pack, cw)
        ag = _dot(w1_ref[g], xg).astype(BF16)
        a_s[:, :, g * pack:(g + 1) * pack, :] = ag.reshape(2, l1n, pack, cw)
    grp = DFT2_GROUP
    for g in range(l1n // grp):
        rhs = jnp.concatenate(
            [jnp.concatenate([a_s[0, g * grp + j], a_s[1, g * grp + j]], axis=0)
             for j in range(grp)], axis=1)
        f = _dot(w2_ref[...], rhs)
        for j in range(grp):
            cols = slice(j * cw, (j + 1) * cw)
            fc = jnp.concatenate([f[:l2n, cols], f[l2n:, cols]], axis=1).astype(BF16)
            out_cols = slice((g * grp + j) * cw, (g * grp + j + 1) * cw)
            y_ref[0, :, out_cols] = _dot(fc, cs_ref[...]).astype(BF16)


def _fourier_call(fa, w1, w2, cs):
    b, length, cw = fa.shape
    l1n, l2n, pack = DFT_L1, DFT_L2, DFT_PACK
    const = lambda bi: (0, 0)
    return pl.pallas_call(
        _fourier_kernel,
        grid=(b,),
        in_specs=[
            pl.BlockSpec((1, l1n, l2n // pack, pack, cw), lambda bi: (bi, 0, 0, 0, 0)),
            pl.BlockSpec(w1.shape, lambda bi: (0, 0, 0), pipeline_mode=pl.Buffered(1)),
            pl.BlockSpec(w2.shape, const),
            pl.BlockSpec(cs.shape, const),
        ],
        out_specs=pl.BlockSpec((1, l2n, l1n * cw), lambda bi: (bi, 0, 0)),
        out_shape=jax.ShapeDtypeStruct((b, l2n, l1n * cw), BF16),
        scratch_shapes=[pltpu.VMEM((2, l1n, l2n, cw), BF16)],
        compiler_params=pltpu.CompilerParams(
            dimension_semantics=("arbitrary",), vmem_limit_bytes=VMEM_LIMIT),
        name="fourier_lat",
    )(fa.reshape(b, l1n, l2n // pack, pack, cw), w1, w2, cs)


def _fourier_ctx_kernel(fa_ref, wc_ref, cs_ref, y_ref):
    n = fa_ref.shape[1]
    f = _dot(wc_ref[...], fa_ref[0])
    fc = jnp.concatenate([f[:n], f[n:]], axis=1).astype(BF16)
    y_ref[0] = _dot(fc, cs_ref[...])


def _fourier_ctx_call(fa, wc, cs):
    b, length, width = fa.shape
    return pl.pallas_call(
        _fourier_ctx_kernel,
        grid=(b,),
        in_specs=[
            pl.BlockSpec((1, length, width), lambda bi: (bi, 0, 0)),
            pl.BlockSpec(wc.shape, lambda bi: (0, 0)),
            pl.BlockSpec(cs.shape, lambda bi: (0, 0)),
        ],
        out_specs=pl.BlockSpec((1, length, width), lambda bi: (bi, 0, 0)),
        out_shape=jax.ShapeDtypeStruct((b, length, width), F32),
        compiler_params=pltpu.CompilerParams(
            dimension_semantics=("arbitrary",), vmem_limit_bytes=VMEM_LIMIT),
        name="fourier_ctx",
    )(fa, wc, cs)


def _mix_kernel(*refs, local, tile, n_blocks, ctx_row, layer):
    if local:
        (x_ref, main_ref, y_ref, perm_ref, qt_ref, kp_ref, km_ref, kn_ref, vtp_ref, vtm_ref, vtn_ref,
         kc_ref, vtc_ref, mod_ref, sink_ref, wsg_ref, bsg_ref,
         wpa_ref, wpb_ref, wpc_ref, wout_ref, o_ref,
         kbuf, vtbuf, ya_s, yb_s, yc_s, ma_s) = refs
    else:
        (x_ref, main_ref, y_ref, qt_ref, kc_ref, vtc_ref, mod_ref, sink_ref, wsg_ref, bsg_ref,
         wpa_ref, wpb_ref, wpc_ref, wout_ref, o_ref,
         ya_s, yb_s, yc_s, ma_s) = refs
    d = D_MODEL
    nsub = tile // BLOCK
    tile_idx = pl.program_id(1)
    heads_per_kv = N_HEADS // N_KV_HEADS
    cols_all = N_HEADS * BLOCK

    if local:
        kbuf[0:BLOCK, :] = kp_ref[0]
        kbuf[BLOCK:BLOCK + tile, :] = km_ref[0]
        kbuf[BLOCK + tile:2 * BLOCK + tile, :] = kn_ref[0]
        vtbuf[0] = vtp_ref[0, 0]
        for s in range(nsub):
            vtbuf[1 + s] = vtm_ref[0, s]
        vtbuf[1 + nsub] = vtn_ref[0, 0]

    def with_ones(vt):
        return jnp.concatenate([vt, jnp.ones((ONES_ROWS, vt.shape[1]), BF16)], axis=0)

    vt_ctx = with_ones(jnp.concatenate([vtc_ref[0, s] for s in range(vtc_ref.shape[1])], axis=1))

    if local:
        cw = FOURIER_WIDTH
        stacked = jnp.concatenate(
            [y_ref[0, :, k1 * cw:(k1 + 1) * cw] for k1 in range(DFT_L1)], axis=0)
        y = _dot(perm_ref[...], stacked)
    else:
        y = y_ref[0]
    ya_s[...] = (y * _silu_of_half(main_ref[0, :, MAIN_ZA:MAIN_ZA + FOURIER_WIDTH].astype(F32))).astype(BF16)

    sgu_grp = lax.broadcasted_iota(jnp.int32, (CHUNK, SGU_WIDTH), 1) // (SGU_WIDTH // SGU_GROUPS)
    zero_head = jnp.zeros((HEAD_DIM, BLOCK), BF16)
    sink_row = jnp.concatenate(
        [jnp.full((1, BLOCK), sink_ref[layer, hh] * LOG2E, F32) for hh in range(N_HEADS)], axis=1)
    kq_diff = (lax.broadcasted_iota(jnp.int32, (BLOCK, cols_all), 0)
               - lax.broadcasted_iota(jnp.int32, (BLOCK, cols_all), 1) % BLOCK)

    def sgu(n):
        rows = slice(n * BLOCK, (n + 1) * BLOCK)
        vs = main_ref[0, rows, MAIN_VS:MAIN_VS + SGU_WIDTH].astype(F32)
        vn = (vs * lax.rsqrt(jnp.mean(vs * vs, axis=-1, keepdims=True) + EPS)).astype(BF16)
        rhs = jnp.concatenate(
            [jnp.where(sgu_grp == g, vn, jnp.zeros_like(vn)) for g in range(SGU_GROUPS)], axis=0)
        mixed = _dot(wsg_ref[0], rhs) + bsg_ref[0]
        u = main_ref[0, rows, MAIN_U:MAIN_U + SGU_WIDTH].astype(F32)
        zb = main_ref[0, rows, MAIN_ZB:MAIN_ZB + SGU_WIDTH].astype(F32)
        yb_s[rows, :] = (u * mixed * _silu_of_half(zb)).astype(BF16)

    def scores(n):
        qt = qt_ref[0, n]
        cols = []
        for hh in range(N_HEADS):
            qh = qt[hh * HEAD_DIM:(hh + 1) * HEAD_DIM, :]
            cols.append(jnp.concatenate(
                [qh, zero_head] if hh < heads_per_kv else [zero_head, qh], axis=0))
        qst = jnp.concatenate(cols, axis=1)

        s_ctx = _dot(kc_ref[0], qst)
        s_loc = _dot(kbuf[n * BLOCK:(n + 3) * BLOCK, :], qst) if local else None
        return s_ctx, s_loc

    def attend(n, j, s_ctx, s_loc, vt_loc):
        rows = slice(n * BLOCK, (n + 1) * BLOCK)
        lanes = slice(2 * j * BLOCK, (2 * j + 2) * BLOCK)
        sink_j = sink_row[:, lanes]
        s_ctx = s_ctx[:, lanes]
        m = jnp.maximum(jnp.max(s_ctx, axis=0, keepdims=True), sink_j)
        if local:
            gblk = tile_idx * nsub + n
            diff = kq_diff[:, lanes]
            s_prev = jnp.where(diff >= jnp.where(gblk > 0, 0, BLOCK), s_loc[0:BLOCK, lanes], NEG_INF)
            s_own = s_loc[BLOCK:2 * BLOCK, lanes]
            s_next = jnp.where(diff <= jnp.where(gblk < n_blocks - 1, 0, -BLOCK),
                               s_loc[2 * BLOCK:3 * BLOCK, lanes], NEG_INF)
            m_loc = jnp.max(jnp.maximum(jnp.maximum(s_prev, s_own), s_next), axis=0, keepdims=True)
            m = jnp.maximum(m, m_loc)
        e_ctx = jnp.exp2(s_ctx - m)
        ot = _dot(vt_ctx, e_ctx.astype(BF16))
        if local:
            e_loc = jnp.concatenate(
                [jnp.exp2(s_prev - m), jnp.exp2(s_own - m), jnp.exp2(s_next - m)], axis=0)
            ot = ot + _dot(vt_loc, e_loc.astype(BF16))
        den = ot[KV_WIDTH:KV_WIDTH + 1] + jnp.exp2(sink_j - m)
        h = (2 * j) // heads_per_kv
        ot = ot[h * HEAD_DIM:(h + 1) * HEAD_DIM] * (1.0 / den)
        pair = jnp.concatenate([ot[:, 0:BLOCK], ot[:, BLOCK:2 * BLOCK]], axis=0)
        zc = main_ref[0, rows, MAIN_ZC + j * LANES:MAIN_ZC + (j + 1) * LANES].astype(F32)
        yc_s[rows, j * LANES:(j + 1) * LANES] = (pair.T * _silu_of_half(zc)).astype(BF16)

    ahead = scores(0)
    for n in range(nsub):
        sgu(n)
        current, ahead = ahead, (scores(n + 1) if n + 1 < nsub else None)
        cols = slice(n * (d // nsub), (n + 1) * (d // nsub))
        ma_s[:, cols] = _dot(ya_s[...], wpa_ref[0, :, cols])
        vt_loc = None
        if local:
            vt_loc = with_ones(jnp.concatenate([vtbuf[n], vtbuf[n + 1], vtbuf[n + 2]], axis=1))
        for j in range(ATTN_WIDTH // LANES):
            attend(n, j, *current, vt_loc)

    t = jnp.tanh(main_ref[0, :, MAIN_G:MAIN_G + 3 * d].astype(F32))
    ma = ma_s[...]
    mb = _dot(yb_s[...], wpb_ref[0])
    mc = _dot(yc_s[...], wpc_ref[0])
    merged2 = (ma + t[:, 0:d] * ma) + (mb + t[:, d:2 * d] * mb) + (mc + t[:, 2 * d:3 * d] * mc)
    out = _dot(merged2.astype(BF16), wout_ref[0])
    row = pl.program_id(0) if ctx_row is None else ctx_row
    gate = mod_ref[0, pl.ds(row, 1), 2 * d:3 * d]
    o_ref[0] = x_ref[0] + gate * out


def _mix_call(x, main, y, qt, k, vt, kc, vtc, mod, sinks, wsg, bsg, wpa, wpb, wpc, wout, l,
              *, tile, local, ctx_row):
    b, length, d = x.shape
    nt = length // tile
    n_ctx = kc.shape[1]
    sub = tile // BLOCK
    n_blocks = length // BLOCK
    tok = lambda bi, i: (bi, i, 0)
    blk = lambda bi, i: (bi, i, 0, 0)
    per_b = lambda bi, i: (bi, 0, 0)
    per_b4 = lambda bi, i: (bi, 0, 0, 0)

    in_specs = [
        pl.BlockSpec((1, tile, d), tok),
        pl.BlockSpec((1, tile, MAIN_WIDTH), tok),
    ]
    args = [x, main, y]
    if local:
        rows_k2 = tile // DFT_L1
        r = jnp.arange(tile)
        perm = (r[:, None] == (r[None, :] % rows_k2) * DFT_L1 + r[None, :] // rows_k2).astype(BF16)
        in_specs += [pl.BlockSpec((1, rows_k2, DFT_L1 * FOURIER_WIDTH), tok),
                     pl.BlockSpec((tile, tile), lambda bi, i: (0, 0))]
        args.append(perm)
    else:
        in_specs.append(pl.BlockSpec((1, tile, FOURIER_WIDTH), tok))
    in_specs.append(pl.BlockSpec((1, sub, ATTN_WIDTH, BLOCK), blk))
    args.append(qt)
    if local:
        prev_blk = lambda bi, i: jnp.maximum(i * sub - 1, 0)
        next_blk = lambda bi, i: jnp.minimum((i + 1) * sub, n_blocks - 1)
        in_specs += [
            pl.BlockSpec((1, BLOCK, KV_WIDTH), lambda bi, i: (bi, prev_blk(bi, i), 0)),
            pl.BlockSpec((1, tile, KV_WIDTH), tok),
            pl.BlockSpec((1, BLOCK, KV_WIDTH), lambda bi, i: (bi, next_blk(bi, i), 0)),
            pl.BlockSpec((1, 1, KV_WIDTH, BLOCK), lambda bi, i: (bi, prev_blk(bi, i), 0, 0)),
            pl.BlockSpec((1, sub, KV_WIDTH, BLOCK), blk),
            pl.BlockSpec((1, 1, KV_WIDTH, BLOCK), lambda bi, i: (bi, next_blk(bi, i), 0, 0)),
        ]
        args += [k, k, k, vt, vt, vt]
    in_specs += [
        pl.BlockSpec((1, n_ctx, KV_WIDTH), per_b),
        pl.BlockSpec((1, n_ctx // BLOCK, KV_WIDTH, BLOCK), per_b4),
        pl.BlockSpec((1, MOD_ROWS, 3 * d), _layer(l, 3)),
        pl.BlockSpec(memory_space=pltpu.SMEM),
    ]
    args += [kc, vtc, mod, sinks]
    for w in (wsg, bsg, wpa, wpb, wpc, wout):
        in_specs.append(pl.BlockSpec((1,) + w.shape[1:], _layer(l, 3)))
        args.append(w)

    scratch = []
    if local:
        scratch += [pltpu.VMEM((tile + 2 * BLOCK, KV_WIDTH), BF16),
                    pltpu.VMEM((sub + 2, KV_WIDTH, BLOCK), BF16)]
    scratch += [
        pltpu.VMEM((tile, FOURIER_WIDTH), BF16),
        pltpu.VMEM((tile, SGU_WIDTH), BF16),
        pltpu.VMEM((tile, ATTN_WIDTH), BF16),
        pltpu.VMEM((tile, d), F32),
    ]
    kern = functools.partial(_mix_kernel, local=local, tile=tile, n_blocks=n_blocks,
                             ctx_row=ctx_row, layer=l)
    return pl.pallas_call(
        kern,
        grid=(b, nt),
        in_specs=in_specs,
        out_specs=pl.BlockSpec((1, tile, d), tok),
        out_shape=jax.ShapeDtypeStruct((b, length, d), F32),
        scratch_shapes=scratch,
        compiler_params=pltpu.CompilerParams(
            dimension_semantics=("arbitrary", "arbitrary"), vmem_limit_bytes=VMEM_LIMIT),
        name="mix_lat" if local else "mix_ctx",
    )(*args)


def _rope_tables(length):
    nf = HEAD_DIM // 4
    inv = jnp.tile(ROPE_THETA ** (-jnp.arange(nf, dtype=F32) / nf), LANES // nf)[None, :]
    lane = jnp.arange(LANES)[None, :]
    on_row = (lane // (2 * nf)) % 2 == 0
    sign = jnp.where((lane // nf) % 2 == 0, -1.0, 1.0)
    ang_r = jnp.arange(length // GRID_W, dtype=F32)[:, None] * inv
    ang_c = jnp.arange(GRID_W, dtype=F32)[:, None] * inv

    def table(fn, scale):
        by_row = jnp.where(on_row, fn(ang_r) * scale, 0.0)[:, None, :]
        by_col = jnp.where(on_row, 0.0, fn(ang_c) * scale)[None, :, :]
        return (by_row + by_col).reshape(length, LANES)

    return table(jnp.cos, 1.0), table(jnp.sin, sign)


def _angle(num, den):
    return (num % den).astype(F32) * (2.0 * math.pi / den)


def _channel_dft(width):
    gw = FOURIER_WIDTH // FOURIER_GROUPS
    m = jnp.arange(width)
    same = (m[:, None] // gw) == (m[None, :] // gw)
    th = _angle((m[:, None] % gw) * (m[None, :] % gw), gw)
    cc = jnp.where(same, jnp.cos(th), 0.0) * gw ** -0.5
    sc = jnp.where(same, jnp.sin(th), 0.0) * gw ** -0.5
    return jnp.concatenate([cc, sc], axis=0).astype(BF16)


def _fourier_tables():
    l1n, l2n = DFT_L1, DFT_L2
    length = l1n * l2n
    pack = DFT_PACK
    l2 = jnp.arange(l2n)[:, None, None]
    k1 = jnp.arange(l1n)[None, :, None]
    l1 = jnp.arange(l1n)[None, None, :]
    th = _angle(k1 * l1 * l2n + k1 * l2, length)
    w1 = jnp.stack([jnp.cos(th), -jnp.sin(th)], axis=1) * l1n ** -0.5
    rows, cols = 2 * l1n * pack, l1n * pack
    ws = jnp.transpose(w1.reshape(l2n // pack, pack, 2 * l1n, l1n), (0, 2, 1, 3))
    ws = ws.reshape(l2n // pack, rows, l1n).astype(BF16)
    spread = (jnp.arange(cols)[None, :] // pack == jnp.arange(l1n)[:, None]).astype(BF16)
    same_p = (jnp.arange(rows)[:, None] % pack) == (jnp.arange(cols)[None, :] % pack)
    w1 = jnp.einsum("grb,bc->grc", ws, spread, preferred_element_type=F32)
    w1 = jnp.where(same_p[None], w1, 0.0).astype(BF16)
    k2 = jnp.arange(l2n)
    th2 = _angle(k2[:, None] * k2[None, :], l2n)
    c2, s2 = jnp.cos(th2), jnp.sin(th2)
    w2 = (jnp.concatenate([jnp.concatenate([c2, s2], axis=1),
                           jnp.concatenate([-s2, c2], axis=1)], axis=0) * l2n ** -0.5).astype(BF16)
    return w1, w2


def _fourier_ctx_table(n):
    k = jnp.arange(n)
    th = _angle(k[:, None] * k[None, :], n)
    return (jnp.concatenate([jnp.cos(th), -jnp.sin(th)], axis=0) * n ** -0.5).astype(BF16)


def _block_diag_ones(width):
    m = jnp.arange(width) // HEAD_DIM
    return (m[:, None] == m[None, :]).astype(BF16)


def _prep_w_in(w_in):
    col = jnp.arange(W_TOTAL)
    halved = ((col >= W_ZA) & (col < W_ZA + FOURIER_WIDTH)) | ((col >= W_ZB) & (col < W_Q)) | (col >= W_ZC)
    return (w_in * jnp.where(halved, 0.5, 1.0)).astype(BF16)


def kernel(x, c, ctx, c_ctx, w_ada, b_ada, w_in, sgu_w, sgu_b, q_norm_g, k_norm_g,
           attn_sink, w_pa, w_pb, w_pc, w_out):
    b, length, d = x.shape
    n_ctx = ctx.shape[1]
    assert length == DFT_L1 * DFT_L2 and d == D_MODEL and b + 1 <= MOD_ROWS
    assert w_in.shape == (DEPTH, d, W_TOTAL)
    lat_tile = 512
    ctx_row = b

    cs = jnp.zeros((MOD_ROWS, d), F32).at[:b].set(c).at[b].set(c_ctx)
    mod = _ada_call(cs, w_ada, b_ada)

    rope_tables = _rope_tables(length)
    w1, w2 = _fourier_tables()
    cs_dft = _channel_dft(FOURIER_WIDTH)
    wc = _fourier_ctx_table(n_ctx)
    bdq = _block_diag_ones(ATTN_WIDTH)
    bdk = _block_diag_ones(KV_WIDTH)

    w_all = _prep_w_in(w_in)
    gq_all = jnp.tile(q_norm_g, (1, N_HEADS))[:, None, :]
    gk_all = jnp.tile(k_norm_g, (1, N_KV_HEADS))[:, None, :]
    wsg = jnp.transpose(sgu_w, (0, 2, 1, 3)).reshape(DEPTH, CHUNK, SGU_GROUPS * CHUNK).astype(BF16)
    bsg = jnp.repeat(jnp.swapaxes(sgu_b, 1, 2), SGU_WIDTH // SGU_GROUPS, axis=2)
    wpa, wpb, wpc = w_pa.astype(BF16), w_pb.astype(BF16), w_pc.astype(BF16)
    wout = (0.5 * w_out).astype(BF16)

    xc = ctx
    for l in range(DEPTH):
        last = l == DEPTH - 1
        fa_c, main_c, qt_c, kc, vtc = _proj_call(
            xc.reshape(1, b * n_ctx, d), mod, w_all, bdq, bdk, gq_all, gk_all, None, l,
            tile=2 * n_ctx, ctx_row=ctx_row)
        kc = kc.reshape(b, n_ctx, KV_WIDTH)
        vtc = vtc.reshape(b, n_ctx // BLOCK, KV_WIDTH, BLOCK)
        fa, main, qt, k, vt = _proj_call(
            x, mod, w_all, bdq, bdk, gq_all, gk_all, rope_tables, l, tile=lat_tile, ctx_row=None)
        y = _fourier_call(fa, w1, w2, cs_dft)
        x = _mix_call(x, main, y, qt, k, vt, kc, vtc, mod, attn_sink, wsg, bsg, wpa, wpb, wpc, wout,
                      l, tile=lat_tile, local=True, ctx_row=None)
        if not last:
            y_c = _fourier_ctx_call(fa_c.reshape(b, n_ctx, FOURIER_WIDTH), wc, cs_dft)
            xc = _mix_call(xc, main_c.reshape(b, n_ctx, MAIN_WIDTH), y_c,
                           qt_c.reshape(b, n_ctx // BLOCK, ATTN_WIDTH, BLOCK), None, None, kc, vtc,
                           mod, attn_sink, wsg, bsg, wpa, wpb, wpc, wout, l,
                           tile=n_ctx, local=False, ctx_row=ctx_row)
    return x
```

```python
import functools
import math

import jax
import jax.numpy as jnp
from jax import lax
from jax.experimental import pallas as pl
from jax.experimental.pallas import tpu as pltpu

F32 = jnp.float32
BF16 = jnp.bfloat16

D_MODEL = 1024
DEPTH = 4
GRID_W = 64
FOURIER_WIDTH = 256
FOURIER_GROUPS = 4
SGU_WIDTH = 256
SGU_GROUPS = 4
CHUNK = 128
N_HEADS = 8
N_KV_HEADS = 2
HEAD_DIM = 64
ATTN_WIDTH = N_HEADS * HEAD_DIM
KV_WIDTH = N_KV_HEADS * HEAD_DIM
BLOCK = 128
ROPE_THETA = 10000.0
EPS = 1e-6
NEG_INF = -1e30
LOG2E = math.log2(math.e)

LANES = 128
MOD_ROWS = 8
ONES_ROWS = 16

W_FA = 0
W_ZA = W_FA + FOURIER_WIDTH
W_ZB = W_ZA + FOURIER_WIDTH + 2 * SGU_WIDTH
W_Q = W_ZB + SGU_WIDTH
W_K = W_Q + ATTN_WIDTH
W_V = W_K + KV_WIDTH
W_ZC = W_V + KV_WIDTH
W_G = W_ZC + ATTN_WIDTH
W_TOTAL = W_G + 3 * D_MODEL
MAIN_ZA = 0
MAIN_U = MAIN_ZA + FOURIER_WIDTH
MAIN_VS = MAIN_U + SGU_WIDTH
MAIN_ZB = MAIN_VS + SGU_WIDTH
MAIN_ZC = MAIN_ZB + SGU_WIDTH
MAIN_G = MAIN_ZC + ATTN_WIDTH
MAIN_WIDTH = MAIN_G + 3 * D_MODEL
PROJ_CHUNK = 512

DFT_L1 = 32
DFT_L2 = 256
DFT_PACK = 16
DFT2_GROUP = 8

VMEM_LIMIT = 56 * 1024 * 1024


def _silu(z):
    return 0.5 * z * (1.0 + jnp.tanh(0.5 * z))


def _silu_of_half(zh):
    return zh * (1.0 + jnp.tanh(zh))


def _dot(a, b):
    return jnp.dot(a, b, preferred_element_type=F32)


def _layer(l, rank):
    return lambda *_: (l,) + (0,) * (rank - 1)


def _ada_kernel(c_ref, w_ref, b_ref, o_ref):
    s = _silu(c_ref[...]).astype(BF16)
    o_ref[0] = _dot(s, w_ref[0].astype(BF16)) + b_ref[0]


def _ada_call(cs, w_ada, b_ada):
    d = D_MODEL
    return pl.pallas_call(
        _ada_kernel,
        grid=(DEPTH, 3),
        in_specs=[
            pl.BlockSpec((MOD_ROWS, d), lambda l, j: (0, 0)),
            pl.BlockSpec((1, d, d), lambda l, j: (l, 0, j)),
            pl.BlockSpec((1, 1, d), lambda l, j: (l, 0, j)),
        ],
        out_specs=pl.BlockSpec((1, MOD_ROWS, d), lambda l, j: (l, 0, j)),
        out_shape=jax.ShapeDtypeStruct((DEPTH, MOD_ROWS, 3 * d), F32),
        compiler_params=pltpu.CompilerParams(
            dimension_semantics=("arbitrary", "arbitrary"), vmem_limit_bytes=VMEM_LIMIT),
        name="ada_mod",
    )(cs, w_ada, b_ada.reshape(DEPTH, 1, 3 * d))


def _norm_rope(t, bd_ref, gain, cos, sin, out_scale):
    width = t.shape[-1]
    ssq = _dot((t * t).astype(BF16), bd_ref[...])
    tn = t * lax.rsqrt(ssq * (1.0 / HEAD_DIM) + EPS) * gain
    if cos is not None:
        reps = width // LANES
        cos_w = jnp.concatenate([cos] * reps, axis=1) if reps > 1 else cos
        sin_w = jnp.concatenate([sin] * reps, axis=1) if reps > 1 else sin
        lane = lax.broadcasted_iota(jnp.int32, tn.shape, 1)
        first = (lane % 32) < 16
        partner = jnp.where(first, pltpu.roll(tn, width - 16, 1), pltpu.roll(tn, 16, 1))
        tn = tn * cos_w + partner * sin_w
    if out_scale != 1.0:
        tn = tn * out_scale
    return tn


def _proj_kernel(x_ref, mod_ref, w_ref, bdq_ref, bdk_ref, gq_ref, gk_ref, cos_ref, sin_ref,
                 fa_ref, main_ref, qt_ref, k_ref, vt_ref, *, ctx_row, rope):
    d = D_MODEL
    nsub = x_ref.shape[1] // BLOCK
    xt = x_ref[0]
    ms = jnp.mean(xt * xt, axis=-1, keepdims=True)
    row = pl.program_id(0) if ctx_row is None else ctx_row
    modrow = mod_ref[0, pl.ds(row, 1), :]
    shift = modrow[:, 0:d]
    scale = modrow[:, d:2 * d]
    hb = ((xt * lax.rsqrt(ms + EPS)) * (1.0 + scale) + shift).astype(BF16)
    cos = cos_ref[...] if rope else None
    sin = sin_ref[...] if rope else None

    def proj(c0, width):
        return _dot(hb, w_ref[0, :, c0:c0 + width])

    chunks = [(src + c0, dst + c0)
              for src, dst, width in ((W_ZA, MAIN_ZA, W_Q - W_ZA), (W_ZC, MAIN_ZC, W_TOTAL - W_ZC))
              for c0 in range(0, width, PROJ_CHUNK)]

    def wide(count):
        for _ in range(min(count, len(chunks))):
            src, dst = chunks.pop(0)
            main_ref[0, :, dst:dst + PROJ_CHUNK] = proj(src, PROJ_CHUNK).astype(BF16)

    q_raw = proj(W_Q, ATTN_WIDTH)
    k_raw = proj(W_K, KV_WIDTH)
    v = proj(W_V, KV_WIDTH)
    wide(1)
    q = _norm_rope(q_raw, bdq_ref, gq_ref[0], cos, sin, HEAD_DIM ** -0.5 * LOG2E)
    wide(1)
    for s in range(nsub):
        qt_ref[0, s] = q[s * BLOCK:(s + 1) * BLOCK, :].T.astype(BF16)
    wide(1)
    k = _norm_rope(k_raw, bdk_ref, gk_ref[0], cos, sin, 1.0)
    k_ref[0] = k.astype(BF16)
    for s in range(nsub):
        vt_ref[0, s] = v[s * BLOCK:(s + 1) * BLOCK, :].T.astype(BF16)
    fa_ref[0] = proj(W_FA, FOURIER_WIDTH).astype(BF16)
    wide(len(chunks))


def _proj_call(x, mod, w_all, bdq, bdk, gq_all, gk_all, rope_tables, l, *, tile, ctx_row):
    b, length, d = x.shape
    nt = length // tile
    nsub = tile // BLOCK
    const = lambda bi, i: (0, 0)
    tok = lambda bi, i: (bi, i, 0)
    blk = lambda bi, i: (bi, i, 0, 0)
    rope = rope_tables is not None
    if rope:
        cos, sin = rope_tables
        pos, pos_rows = (lambda bi, i: (i, 0)), tile
    else:
        cos = sin = jnp.zeros((MOD_ROWS, LANES), F32)
        pos, pos_rows = const, MOD_ROWS
    kern = functools.partial(_proj_kernel, ctx_row=ctx_row, rope=rope)
    return pl.pallas_call(
        kern,
        grid=(b, nt),
        in_specs=[
            pl.BlockSpec((1, tile, d), tok),
            pl.BlockSpec((1, MOD_ROWS, 3 * d), _layer(l, 3)),
            pl.BlockSpec((1, d, W_TOTAL), _layer(l, 3), pipeline_mode=pl.Buffered(1)),
            pl.BlockSpec((ATTN_WIDTH, ATTN_WIDTH), const),
            pl.BlockSpec((KV_WIDTH, KV_WIDTH), const),
            pl.BlockSpec((1, 1, ATTN_WIDTH), _layer(l, 3)),
            pl.BlockSpec((1, 1, KV_WIDTH), _layer(l, 3)),
            pl.BlockSpec((pos_rows, LANES), pos),
            pl.BlockSpec((pos_rows, LANES), pos),
        ],
        out_specs=[
            pl.BlockSpec((1, tile, FOURIER_WIDTH), tok),
            pl.BlockSpec((1, tile, MAIN_WIDTH), tok),
            pl.BlockSpec((1, nsub, ATTN_WIDTH, BLOCK), blk),
            pl.BlockSpec((1, tile, KV_WIDTH), tok),
            pl.BlockSpec((1, nsub, KV_WIDTH, BLOCK), blk),
        ],
        out_shape=[
            jax.ShapeDtypeStruct((b, length, FOURIER_WIDTH), BF16),
            jax.ShapeDtypeStruct((b, length, MAIN_WIDTH), BF16),
            jax.ShapeDtypeStruct((b, length // BLOCK, ATTN_WIDTH, BLOCK), BF16),
            jax.ShapeDtypeStruct((b, length, KV_WIDTH), BF16),
            jax.ShapeDtypeStruct((b, length // BLOCK, KV_WIDTH, BLOCK), BF16),
        ],
        compiler_params=pltpu.CompilerParams(
            dimension_semantics=("arbitrary", "arbitrary"), vmem_limit_bytes=VMEM_LIMIT),
        name="proj_ctx" if ctx_row is not None else "proj_lat",
    )(x, mod, w_all, bdq, bdk, gq_all, gk_all, cos, sin)


def _fourier_kernel(x_ref, w1_ref, w2_ref, cs_ref, y_ref, a_s):
    l1n, l2n, pack, cw = DFT_L1, DFT_L2, DFT_PACK, FOURIER_WIDTH
    for g in range(l2n // pack):
        xg = x_ref[0, :, g].reshape(l1n * pack, cw)
        ag = _dot(w1_ref[g], xg).astype(BF16)
        a_s[:, :, g * pack:(g + 1) * pack, :] = ag.reshape(2, l1n, pack, cw)
    grp = DFT2_GROUP
    for g in range(l1n // grp):
        rhs = jnp.concatenate(
            [jnp.concatenate([a_s[0, g * grp + j], a_s[1, g * grp + j]], axis=0)
             for j in range(grp)], axis=1)
        f = _dot(w2_ref[...], rhs)
        for j in range(grp):
            cols = slice(j * cw, (j + 1) * cw)
            fc = jnp.concatenate([f[:l2n, cols], f[l2n:, cols]], axis=1).astype(BF16)
            out_cols = slice((g * grp + j) * cw, (g * grp + j + 1) * cw)
            y_ref[0, :, out_cols] = _dot(fc, cs_ref[...]).astype(BF16)


def _fourier_call(fa, w1, w2, cs):
    b, length, cw = fa.shape
    l1n, l2n, pack = DFT_L1, DFT_L2, DFT_PACK
    const = lambda bi: (0, 0)
    return pl.pallas_call(
        _fourier_kernel,
        grid=(b,),
        in_specs=[
            pl.BlockSpec((1, l1n, l2n // pack, pack, cw), lambda bi: (bi, 0, 0, 0, 0)),
            pl.BlockSpec(w1.shape, lambda bi: (0, 0, 0), pipeline_mode=pl.Buffered(1)),
            pl.BlockSpec(w2.shape, const),
            pl.BlockSpec(cs.shape, const),
        ],
        out_specs=pl.BlockSpec((1, l2n, l1n * cw), lambda bi: (bi, 0, 0)),
        out_shape=jax.ShapeDtypeStruct((b, l2n, l1n * cw), BF16),
        scratch_shapes=[pltpu.VMEM((2, l1n, l2n, cw), BF16)],
        compiler_params=pltpu.CompilerParams(
            dimension_semantics=("arbitrary",), vmem_limit_bytes=VMEM_LIMIT),
        name="fourier_lat",
    )(fa.reshape(b, l1n, l2n // pack, pack, cw), w1, w2, cs)


def _fourier_ctx_kernel(fa_ref, wc_ref, cs_ref, y_ref):
    n = fa_ref.shape[1]
    f = _dot(wc_ref[...], fa_ref[0])
    fc = jnp.concatenate([f[:n], f[n:]], axis=1).astype(BF16)
    y_ref[0] = _dot(fc, cs_ref[...])


def _fourier_ctx_call(fa, wc, cs):
    b, length, width = fa.shape
    return pl.pallas_call(
        _fourier_ctx_kernel,
        grid=(b,),
        in_specs=[
            pl.BlockSpec((1, length, width), lambda bi: (bi, 0, 0)),
            pl.BlockSpec(wc.shape, lambda bi: (0, 0)),
            pl.BlockSpec(cs.shape, lambda bi: (0, 0)),
        ],
        out_specs=pl.BlockSpec((1, length, width), lambda bi: (bi, 0, 0)),
        out_shape=jax.ShapeDtypeStruct((b, length, width), F32),
        compiler_params=pltpu.CompilerParams(
            dimension_semantics=("arbitrary",), vmem_limit_bytes=VMEM_LIMIT),
        name="fourier_ctx",
    )(fa, wc, cs)


def _mix_kernel(*refs, local, tile, n_blocks, ctx_row, layer):
    if local:
        (x_ref, main_ref, y_ref, perm_ref, qt_ref, kp_ref, km_ref, kn_ref, vtp_ref, vtm_ref, vtn_ref,
         kc_ref, vtc_ref, mod_ref, sink_ref, wsg_ref, bsg_ref,
         wpa_ref, wpb_ref, wpc_ref, wout_ref, o_ref,
         kbuf, vtbuf, ya_s, yb_s, yc_s, ma_s, mg_s) = refs
    else:
        (x_ref, main_ref, y_ref, qt_ref, kc_ref, vtc_ref, mod_ref, sink_ref, wsg_ref, bsg_ref,
         wpa_ref, wpb_ref, wpc_ref, wout_ref, o_ref,
         ya_s, yb_s, yc_s, ma_s, mg_s) = refs
    d = D_MODEL
    nsub = tile // BLOCK
    tile_idx = pl.program_id(1)
    heads_per_kv = N_HEADS // N_KV_HEADS
    cols_all = N_HEADS * BLOCK

    if local:
        kbuf[0:BLOCK, :] = kp_ref[0]
        kbuf[BLOCK:BLOCK + tile, :] = km_ref[0]
        kbuf[BLOCK + tile:2 * BLOCK + tile, :] = kn_ref[0]
        vtbuf[0] = vtp_ref[0, 0]
        for s in range(nsub):
            vtbuf[1 + s] = vtm_ref[0, s]
        vtbuf[1 + nsub] = vtn_ref[0, 0]

    def with_ones(vt):
        return jnp.concatenate([vt, jnp.ones((ONES_ROWS, vt.shape[1]), BF16)], axis=0)

    vt_ctx = with_ones(jnp.concatenate([vtc_ref[0, s] for s in range(vtc_ref.shape[1])], axis=1))

    if local:
        cw = FOURIER_WIDTH
        stacked = jnp.concatenate(
            [y_ref[0, :, k1 * cw:(k1 + 1) * cw] for k1 in range(DFT_L1)], axis=0)
        y = _dot(perm_ref[...], stacked)
    else:
        y = y_ref[0]
    ya_s[...] = (y * _silu_of_half(main_ref[0, :, MAIN_ZA:MAIN_ZA + FOURIER_WIDTH].astype(F32))).astype(BF16)

    sgu_grp = lax.broadcasted_iota(jnp.int32, (CHUNK, SGU_WIDTH), 1) // (SGU_WIDTH // SGU_GROUPS)
    zero_head = jnp.zeros((HEAD_DIM, BLOCK), BF16)
    sink_row = jnp.concatenate(
        [jnp.full((1, BLOCK), sink_ref[layer, hh] * LOG2E, F32) for hh in range(N_HEADS)], axis=1)
    kq_diff = (lax.broadcasted_iota(jnp.int32, (BLOCK, cols_all), 0)
               - lax.broadcasted_iota(jnp.int32, (BLOCK, cols_all), 1) % BLOCK)

    def sgu(n):
        rows = slice(n * BLOCK, (n + 1) * BLOCK)
        vs = main_ref[0, rows, MAIN_VS:MAIN_VS + SGU_WIDTH].astype(F32)
        vn = (vs * lax.rsqrt(jnp.mean(vs * vs, axis=-1, keepdims=True) + EPS)).astype(BF16)
        rhs = jnp.concatenate(
            [jnp.where(sgu_grp == g, vn, jnp.zeros_like(vn)) for g in range(SGU_GROUPS)], axis=0)
        mixed = _dot(wsg_ref[0], rhs) + bsg_ref[0]
        u = main_ref[0, rows, MAIN_U:MAIN_U + SGU_WIDTH].astype(F32)
        zb = main_ref[0, rows, MAIN_ZB:MAIN_ZB + SGU_WIDTH].astype(F32)
        yb_s[rows, :] = (u * mixed * _silu_of_half(zb)).astype(BF16)

    def scores(n):
        qt = qt_ref[0, n]
        cols = []
        for hh in range(N_HEADS):
            qh = qt[hh * HEAD_DIM:(hh + 1) * HEAD_DIM, :]
            cols.append(jnp.concatenate(
                [qh, zero_head] if hh < heads_per_kv else [zero_head, qh], axis=0))
        qst = jnp.concatenate(cols, axis=1)

        s_ctx = _dot(kc_ref[0], qst)
        s_loc = _dot(kbuf[n * BLOCK:(n + 3) * BLOCK, :], qst) if local else None
        return s_ctx, s_loc

    def attend(n, j, s_ctx, s_loc, vt_loc):
        rows = slice(n * BLOCK, (n + 1) * BLOCK)
        lanes = slice(2 * j * BLOCK, (2 * j + 2) * BLOCK)
        sink_j = sink_row[:, lanes]
        s_ctx = s_ctx[:, lanes]
        m = jnp.maximum(jnp.max(s_ctx, axis=0, keepdims=True), sink_j)
        if local:
            gblk = tile_idx * nsub + n
            diff = kq_diff[:, lanes]
            s_prev = jnp.where(diff >= jnp.where(gblk > 0, 0, BLOCK), s_loc[0:BLOCK, lanes], NEG_INF)
            s_own = s_loc[BLOCK:2 * BLOCK, lanes]
            s_next = jnp.where(diff <= jnp.where(gblk < n_blocks - 1, 0, -BLOCK),
                               s_loc[2 * BLOCK:3 * BLOCK, lanes], NEG_INF)
            m_loc = jnp.max(jnp.maximum(jnp.maximum(s_prev, s_own), s_next), axis=0, keepdims=True)
            m = jnp.maximum(m, m_loc)
        e_ctx = jnp.exp2(s_ctx - m)
        ot = _dot(vt_ctx, e_ctx.astype(BF16))
        if local:
            e_loc = jnp.concatenate(
                [jnp.exp2(s_prev - m), jnp.exp2(s_own - m), jnp.exp2(s_next - m)], axis=0)
            ot = ot + _dot(vt_loc, e_loc.astype(BF16))
        den = ot[KV_WIDTH:KV_WIDTH + 1] + jnp.exp2(sink_j - m)
        h = (2 * j) // heads_per_kv
        ot = ot[h * HEAD_DIM:(h + 1) * HEAD_DIM] * (1.0 / den)
        pair = jnp.concatenate([ot[:, 0:BLOCK], ot[:, BLOCK:2 * BLOCK]], axis=0)
        zc = main_ref[0, rows, MAIN_ZC + j * LANES:MAIN_ZC + (j + 1) * LANES].astype(F32)
        yc_s[rows, j * LANES:(j + 1) * LANES] = (pair.T * _silu_of_half(zc)).astype(BF16)

    ahead = scores(0)
    for n in range(nsub):
        sgu(n)
        current, ahead = ahead, (scores(n + 1) if n + 1 < nsub else None)
        cols = slice(n * (d // nsub), (n + 1) * (d // nsub))
        ma_s[:, cols] = _dot(ya_s[...], wpa_ref[0, :, cols])
        vt_loc = None
        if local:
            vt_loc = with_ones(jnp.concatenate([vtbuf[n], vtbuf[n + 1], vtbuf[n + 2]], axis=1))
        for j in range(ATTN_WIDTH // LANES):
            attend(n, j, *current, vt_loc)

    row = pl.program_id(0) if ctx_row is None else ctx_row
    gate = mod_ref[0, pl.ds(row, 1), 2 * d:3 * d]
    width = 2 * LANES
    for c0 in range(0, d, width):
        cols = slice(c0, c0 + width)
        ma = ma_s[:, cols]
        mb = _dot(yb_s[...], wpb_ref[0, :, cols])
        mc = _dot(yc_s[...], wpc_ref[0, :, cols])
        ta, tb, tc = (jnp.tanh(main_ref[0, :, MAIN_G + i * d + c0:MAIN_G + i * d + c0 + width].astype(F32))
                      for i in range(3))
        mg_s[:, cols] = ((ma + ta * ma) + (mb + tb * mb) + (mc + tc * mc)).astype(BF16)
    for c0 in range(0, d, width):
        cols = slice(c0, c0 + width)
        o_ref[0, :, cols] = x_ref[0, :, cols] + gate[:, cols] * _dot(mg_s[...], wout_ref[0, :, cols])


def _mix_call(x, main, y, qt, k, vt, kc, vtc, mod, sinks, wsg, bsg, wpa, wpb, wpc, wout, l,
              *, tile, local, ctx_row):
    b, length, d = x.shape
    nt = length // tile
    n_ctx = kc.shape[1]
    sub = tile // BLOCK
    n_blocks = length // BLOCK
    tok = lambda bi, i: (bi, i, 0)
    blk = lambda bi, i: (bi, i, 0, 0)
    per_b = lambda bi, i: (bi, 0, 0)
    per_b4 = lambda bi, i: (bi, 0, 0, 0)

    in_specs = [
        pl.BlockSpec((1, tile, d), tok),
        pl.BlockSpec((1, tile, MAIN_WIDTH), tok),
    ]
    args = [x, main, y]
    if local:
        rows_k2 = tile // DFT_L1
        r = jnp.arange(tile)
        perm = (r[:, None] == (r[None, :] % rows_k2) * DFT_L1 + r[None, :] // rows_k2).astype(BF16)
        in_specs += [pl.BlockSpec((1, rows_k2, DFT_L1 * FOURIER_WIDTH), tok),
                     pl.BlockSpec((tile, tile), lambda bi, i: (0, 0))]
        args.append(perm)
    else:
        in_specs.append(pl.BlockSpec((1, tile, FOURIER_WIDTH), tok))
    in_specs.append(pl.BlockSpec((1, sub, ATTN_WIDTH, BLOCK), blk))
    args.append(qt)
    if local:
        prev_blk = lambda bi, i: jnp.maximum(i * sub - 1, 0)
        next_blk = lambda bi, i: jnp.minimum((i + 1) * sub, n_blocks - 1)
        in_specs += [
            pl.BlockSpec((1, BLOCK, KV_WIDTH), lambda bi, i: (bi, prev_blk(bi, i), 0)),
            pl.BlockSpec((1, tile, KV_WIDTH), tok),
            pl.BlockSpec((1, BLOCK, KV_WIDTH), lambda bi, i: (bi, next_blk(bi, i), 0)),
            pl.BlockSpec((1, 1, KV_WIDTH, BLOCK), lambda bi, i: (bi, prev_blk(bi, i), 0, 0)),
            pl.BlockSpec((1, sub, KV_WIDTH, BLOCK), blk),
            pl.BlockSpec((1, 1, KV_WIDTH, BLOCK), lambda bi, i: (bi, next_blk(bi, i), 0, 0)),
        ]
        args += [k, k, k, vt, vt, vt]
    in_specs += [
        pl.BlockSpec((1, n_ctx, KV_WIDTH), per_b),
        pl.BlockSpec((1, n_ctx // BLOCK, KV_WIDTH, BLOCK), per_b4),
        pl.BlockSpec((1, MOD_ROWS, 3 * d), _layer(l, 3)),
        pl.BlockSpec(memory_space=pltpu.SMEM),
    ]
    args += [kc, vtc, mod, sinks]
    for w in (wsg, bsg, wpa, wpb, wpc, wout):
        in_specs.append(pl.BlockSpec((1,) + w.shape[1:], _layer(l, 3)))
        args.append(w)

    scratch = []
    if local:
        scratch += [pltpu.VMEM((tile + 2 * BLOCK, KV_WIDTH), BF16),
                    pltpu.VMEM((sub + 2, KV_WIDTH, BLOCK), BF16)]
    scratch += [
        pltpu.VMEM((tile, FOURIER_WIDTH), BF16),
        pltpu.VMEM((tile, SGU_WIDTH), BF16),
        pltpu.VMEM((tile, ATTN_WIDTH), BF16),
        pltpu.VMEM((tile, d), F32),
        pltpu.VMEM((tile, d), BF16),
    ]
    kern = functools.partial(_mix_kernel, local=local, tile=tile, n_blocks=n_blocks,
                             ctx_row=ctx_row, layer=l)
    return pl.pallas_call(
        kern,
        grid=(b, nt),
        in_specs=in_specs,
        out_specs=pl.BlockSpec((1, tile, d), tok),
        out_shape=jax.ShapeDtypeStruct((b, length, d), F32),
        scratch_shapes=scratch,
        compiler_params=pltpu.CompilerParams(
            dimension_semantics=("arbitrary", "arbitrary"), vmem_limit_bytes=VMEM_LIMIT),
        name="mix_lat" if local else "mix_ctx",
    )(*args)


def _rope_tables(length):
    nf = HEAD_DIM // 4
    inv = jnp.tile(ROPE_THETA ** (-jnp.arange(nf, dtype=F32) / nf), LANES // nf)[None, :]
    lane = jnp.arange(LANES)[None, :]
    on_row = (lane // (2 * nf)) % 2 == 0
    sign = jnp.where((lane // nf) % 2 == 0, -1.0, 1.0)
    ang_r = jnp.arange(length // GRID_W, dtype=F32)[:, None] * inv
    ang_c = jnp.arange(GRID_W, dtype=F32)[:, None] * inv

    def table(fn, scale):
        by_row = jnp.where(on_row, fn(ang_r) * scale, 0.0)[:, None, :]
        by_col = jnp.where(on_row, 0.0, fn(ang_c) * scale)[None, :, :]
        return (by_row + by_col).reshape(length, LANES)

    return table(jnp.cos, 1.0), table(jnp.sin, sign)


def _angle(num, den):
    return (num % den).astype(F32) * (2.0 * math.pi / den)


def _channel_dft(width):
    gw = FOURIER_WIDTH // FOURIER_GROUPS
    m = jnp.arange(width)
    same = (m[:, None] // gw) == (m[None, :] // gw)
    th = _angle((m[:, None] % gw) * (m[None, :] % gw), gw)
    cc = jnp.where(same, jnp.cos(th), 0.0) * gw ** -0.5
    sc = jnp.where(same, jnp.sin(th), 0.0) * gw ** -0.5
    return jnp.concatenate([cc, sc], axis=0).astype(BF16)


def _fourier_tables():
    l1n, l2n = DFT_L1, DFT_L2
    length = l1n * l2n
    pack = DFT_PACK
    l2 = jnp.arange(l2n)[:, None, None]
    k1 = jnp.arange(l1n)[None, :, None]
    l1 = jnp.arange(l1n)[None, None, :]
    th = _angle(k1 * l1 * l2n + k1 * l2, length)
    w1 = jnp.stack([jnp.cos(th), -jnp.sin(th)], axis=1) * l1n ** -0.5
    rows, cols = 2 * l1n * pack, l1n * pack
    ws = jnp.transpose(w1.reshape(l2n // pack, pack, 2 * l1n, l1n), (0, 2, 1, 3))
    ws = ws.reshape(l2n // pack, rows, l1n).astype(BF16)
    spread = (jnp.arange(cols)[None, :] // pack == jnp.arange(l1n)[:, None]).astype(BF16)
    same_p = (jnp.arange(rows)[:, None] % pack) == (jnp.arange(cols)[None, :] % pack)
    w1 = jnp.einsum("grb,bc->grc", ws, spread, preferred_element_type=F32)
    w1 = jnp.where(same_p[None], w1, 0.0).astype(BF16)
    k2 = jnp.arange(l2n)
    th2 = _angle(k2[:, None] * k2[None, :], l2n)
    c2, s2 = jnp.cos(th2), jnp.sin(th2)
    w2 = (jnp.concatenate([jnp.concatenate([c2, s2], axis=1),
                           jnp.concatenate([-s2, c2], axis=1)], axis=0) * l2n ** -0.5).astype(BF16)
    return w1, w2


def _fourier_ctx_table(n):
    k = jnp.arange(n)
    th = _angle(k[:, None] * k[None, :], n)
    return (jnp.concatenate([jnp.cos(th), -jnp.sin(th)], axis=0) * n ** -0.5).astype(BF16)


def _block_diag_ones(width):
    m = jnp.arange(width) // HEAD_DIM
    return (m[:, None] == m[None, :]).astype(BF16)


def _prep_w_in(w_in):
    col = jnp.arange(W_TOTAL)
    halved = ((col >= W_ZA) & (col < W_ZA + FOURIER_WIDTH)) | ((col >= W_ZB) & (col < W_Q)) | (col >= W_ZC)
    return (w_in * jnp.where(halved, 0.5, 1.0)).astype(BF16)


def kernel(x, c, ctx, c_ctx, w_ada, b_ada, w_in, sgu_w, sgu_b, q_norm_g, k_norm_g,
           attn_sink, w_pa, w_pb, w_pc, w_out):
    b, length, d = x.shape
    n_ctx = ctx.shape[1]
    assert length == DFT_L1 * DFT_L2 and d == D_MODEL and b + 1 <= MOD_ROWS
    assert w_in.shape == (DEPTH, d, W_TOTAL)
    lat_tile = 512
    ctx_row = b

    cs = jnp.zeros((MOD_ROWS, d), F32).at[:b].set(c).at[b].set(c_ctx)
    mod = _ada_call(cs, w_ada, b_ada)

    rope_tables = _rope_tables(length)
    w1, w2 = _fourier_tables()
    cs_dft = _channel_dft(FOURIER_WIDTH)
    wc = _fourier_ctx_table(n_ctx)
    bdq = _block_diag_ones(ATTN_WIDTH)
    bdk = _block_diag_ones(KV_WIDTH)

    w_all = _prep_w_in(w_in)
    gq_all = jnp.tile(q_norm_g, (1, N_HEADS))[:, None, :]
    gk_all = jnp.tile(k_norm_g, (1, N_KV_HEADS))[:, None, :]
    wsg = jnp.transpose(sgu_w, (0, 2, 1, 3)).reshape(DEPTH, CHUNK, SGU_GROUPS * CHUNK).astype(BF16)
    bsg = jnp.repeat(jnp.swapaxes(sgu_b, 1, 2), SGU_WIDTH // SGU_GROUPS, axis=2)
    wpa, wpb, wpc = w_pa.astype(BF16), w_pb.astype(BF16), w_pc.astype(BF16)
    wout = (0.5 * w_out).astype(BF16)

    xc = ctx
    for l in range(DEPTH):
        last = l == DEPTH - 1
        fa_c, main_c, qt_c, kc, vtc = _proj_call(
            xc.reshape(1, b * n_ctx, d), mod, w_all, bdq, bdk, gq_all, gk_all, None, l,
            tile=2 * n_ctx, ctx_row=ctx_row)
        kc = kc.reshape(b, n_ctx, KV_WIDTH)
        vtc = vtc.reshape(b, n_ctx // BLOCK, KV_WIDTH, BLOCK)
        fa, main, qt, k, vt = _proj_call(
            x, mod, w_all, bdq, bdk, gq_all, gk_all, rope_tables, l, tile=lat_tile, ctx_row=None)
        y = _fourier_call(fa, w1, w2, cs_dft)
        x = _mix_call(x, main, y, qt, k, vt, kc, vtc, mod, attn_sink, wsg, bsg, wpa, wpb, wpc, wout,
                      l, tile=lat_tile, local=True, ctx_row=None)
        if not last:
            y_c = _fourier_ctx_call(fa_c.reshape(b, n_ctx, FOURIER_WIDTH), wc, cs_dft)
            xc = _mix_call(xc, main_c.reshape(b, n_ctx, MAIN_WIDTH), y_c,
                           qt_c.reshape(b, n_ctx // BLOCK, ATTN_WIDTH, BLOCK), None, None, kc, vtc,
                           mod, attn_sink, wsg, bsg, wpa, wpb, wpc, wout, l,
                           tile=n_ctx, local=False, ctx_row=ctx_row)
    return x
```

```python
import functools
import math

import jax
import jax.numpy as jnp
from jax import lax
from jax.experimental import pallas as pl
from jax.experimental.pallas import tpu as pltpu

F32 = jnp.float32
BF16 = jnp.bfloat16

D_MODEL = 1024
DEPTH = 4
GRID_W = 64
FOURIER_WIDTH = 256
FOURIER_GROUPS = 4
SGU_WIDTH = 256
SGU_GROUPS = 4
CHUNK = 128
N_HEADS = 8
N_KV_HEADS = 2
HEAD_DIM = 64
ATTN_WIDTH = N_HEADS * HEAD_DIM
KV_WIDTH = N_KV_HEADS * HEAD_DIM
BLOCK = 128
ROPE_THETA = 10000.0
EPS = 1e-6
NEG_INF = -1e30
LOG2E = math.log2(math.e)

LANES = 128
MOD_ROWS = 8
ONES_ROWS = 16

W_FA = 0
W_ZA = W_FA + FOURIER_WIDTH
W_ZB = W_ZA + FOURIER_WIDTH + 2 * SGU_WIDTH
W_Q = W_ZB + SGU_WIDTH
W_K = W_Q + ATTN_WIDTH
W_V = W_K + KV_WIDTH
W_ZC = W_V + KV_WIDTH
W_G = W_ZC + ATTN_WIDTH
W_TOTAL = W_G + 3 * D_MODEL
MAIN_ZA = 0
MAIN_U = MAIN_ZA + FOURIER_WIDTH
MAIN_VS = MAIN_U + SGU_WIDTH
MAIN_ZB = MAIN_VS + SGU_WIDTH
MAIN_ZC = MAIN_ZB + SGU_WIDTH
MAIN_G = MAIN_ZC + ATTN_WIDTH
MAIN_WIDTH = MAIN_G + 3 * D_MODEL
PROJ_CHUNK = 512

DFT_L1 = 32
DFT_L2 = 256
DFT_PACK = 16
DFT2_GROUP = 8

VMEM_LIMIT = 56 * 1024 * 1024


def _silu(z):
    return 0.5 * z * (1.0 + jnp.tanh(0.5 * z))


def _silu_of_half(zh):
    return zh * (1.0 + jnp.tanh(zh))


def _dot(a, b):
    return jnp.dot(a, b, preferred_element_type=F32)


def _layer(l, rank):
    return lambda *_: (l,) + (0,) * (rank - 1)


def _ada_kernel(c_ref, w_ref, b_ref, o_ref):
    s = _silu(c_ref[...]).astype(BF16)
    o_ref[0] = _dot(s, w_ref[0].astype(BF16)) + b_ref[0]


def _ada_call(cs, w_ada, b_ada):
    d = D_MODEL
    return pl.pallas_call(
        _ada_kernel,
        grid=(DEPTH, 3),
        in_specs=[
            pl.BlockSpec((MOD_ROWS, d), lambda l, j: (0, 0)),
            pl.BlockSpec((1, d, d), lambda l, j: (l, 0, j)),
            pl.BlockSpec((1, 1, d), lambda l, j: (l, 0, j)),
        ],
        out_specs=pl.BlockSpec((1, MOD_ROWS, d), lambda l, j: (l, 0, j)),
        out_shape=jax.ShapeDtypeStruct((DEPTH, MOD_ROWS, 3 * d), F32),
        compiler_params=pltpu.CompilerParams(
            dimension_semantics=("arbitrary", "arbitrary"), vmem_limit_bytes=VMEM_LIMIT),
        name="ada_mod",
    )(cs, w_ada, b_ada.reshape(DEPTH, 1, 3 * d))


def _norm_rope(t, bd_ref, gain, cos, sin, out_scale):
    width = t.shape[-1]
    ssq = _dot((t * t).astype(BF16), bd_ref[...])
    tn = t * lax.rsqrt(ssq * (1.0 / HEAD_DIM) + EPS) * gain
    if cos is not None:
        reps = width // LANES
        cos_w = jnp.concatenate([cos] * reps, axis=1) if reps > 1 else cos
        sin_w = jnp.concatenate([sin] * reps, axis=1) if reps > 1 else sin
        lane = lax.broadcasted_iota(jnp.int32, tn.shape, 1)
        first = (lane % 32) < 16
        partner = jnp.where(first, pltpu.roll(tn, width - 16, 1), pltpu.roll(tn, 16, 1))
        tn = tn * cos_w + partner * sin_w
    if out_scale != 1.0:
        tn = tn * out_scale
    return tn


def _proj_kernel(x_ref, mod_ref, w_ref, bdq_ref, bdk_ref, gq_ref, gk_ref, cos_ref, sin_ref,
                 fa_ref, main_ref, qt_ref, k_ref, vt_ref, *, ctx_row, rope):
    d = D_MODEL
    nsub = x_ref.shape[1] // BLOCK
    xt = x_ref[0]
    ms = jnp.mean(xt * xt, axis=-1, keepdims=True)
    row = pl.program_id(0) if ctx_row is None else ctx_row
    modrow = mod_ref[0, pl.ds(row, 1), :]
    shift = modrow[:, 0:d]
    scale = modrow[:, d:2 * d]
    hb = ((xt * lax.rsqrt(ms + EPS)) * (1.0 + scale) + shift).astype(BF16)
    cos = cos_ref[...] if rope else None
    sin = sin_ref[...] if rope else None

    def proj(c0, width):
        return _dot(hb, w_ref[0, :, c0:c0 + width])

    chunks = [(src + c0, dst + c0)
              for src, dst, width in ((W_ZA, MAIN_ZA, W_Q - W_ZA), (W_ZC, MAIN_ZC, W_TOTAL - W_ZC))
              for c0 in range(0, width, PROJ_CHUNK)]

    def wide(count):
        for _ in range(min(count, len(chunks))):
            src, dst = chunks.pop(0)
            main_ref[0, :, dst:dst + PROJ_CHUNK] = proj(src, PROJ_CHUNK).astype(BF16)

    q_raw = proj(W_Q, ATTN_WIDTH)
    k_raw = proj(W_K, KV_WIDTH)
    v = proj(W_V, KV_WIDTH)
    wide(1)
    q = _norm_rope(q_raw, bdq_ref, gq_ref[0], cos, sin, HEAD_DIM ** -0.5 * LOG2E)
    wide(1)
    for s in range(nsub):
        qt_ref[0, s] = q[s * BLOCK:(s + 1) * BLOCK, :].T.astype(BF16)
    wide(1)
    k = _norm_rope(k_raw, bdk_ref, gk_ref[0], cos, sin, 1.0)
    k_ref[0] = k.astype(BF16)
    for s in range(nsub):
        vt_ref[0, s] = v[s * BLOCK:(s + 1) * BLOCK, :].T.astype(BF16)
    fa_ref[0] = proj(W_FA, FOURIER_WIDTH).astype(BF16)
    wide(len(chunks))


def _proj_call(x, mod, w_all, bdq, bdk, gq_all, gk_all, rope_tables, l, *, tile, ctx_row):
    b, length, d = x.shape
    nt = length // tile
    nsub = tile // BLOCK
    const = lambda bi, i: (0, 0)
    tok = lambda bi, i: (bi, i, 0)
    blk = lambda bi, i: (bi, i, 0, 0)
    rope = rope_tables is not None
    if rope:
        cos, sin = rope_tables
        pos, pos_rows = (lambda bi, i: (i, 0)), tile
    else:
        cos = sin = jnp.zeros((MOD_ROWS, LANES), F32)
        pos, pos_rows = const, MOD_ROWS
    kern = functools.partial(_proj_kernel, ctx_row=ctx_row, rope=rope)
    return pl.pallas_call(
        kern,
        grid=(b, nt),
        in_specs=[
            pl.BlockSpec((1, tile, d), tok),
            pl.BlockSpec((1, MOD_ROWS, 3 * d), _layer(l, 3)),
            pl.BlockSpec((1, d, W_TOTAL), _layer(l, 3), pipeline_mode=pl.Buffered(1)),
            pl.BlockSpec((ATTN_WIDTH, ATTN_WIDTH), const),
            pl.BlockSpec((KV_WIDTH, KV_WIDTH), const),
            pl.BlockSpec((1, 1, ATTN_WIDTH), _layer(l, 3)),
            pl.BlockSpec((1, 1, KV_WIDTH), _layer(l, 3)),
            pl.BlockSpec((pos_rows, LANES), pos),
            pl.BlockSpec((pos_rows, LANES), pos),
        ],
        out_specs=[
            pl.BlockSpec((1, tile, FOURIER_WIDTH), tok),
            pl.BlockSpec((1, tile, MAIN_WIDTH), tok),
            pl.BlockSpec((1, nsub, ATTN_WIDTH, BLOCK), blk),
            pl.BlockSpec((1, tile, KV_WIDTH), tok),
            pl.BlockSpec((1, nsub, KV_WIDTH, BLOCK), blk),
        ],
        out_shape=[
            jax.ShapeDtypeStruct((b, length, FOURIER_WIDTH), BF16),
            jax.ShapeDtypeStruct((b, length, MAIN_WIDTH), BF16),
            jax.ShapeDtypeStruct((b, length // BLOCK, ATTN_WIDTH, BLOCK), BF16),
            jax.ShapeDtypeStruct((b, length, KV_WIDTH), BF16),
            jax.ShapeDtypeStruct((b, length // BLOCK, KV_WIDTH, BLOCK), BF16),
        ],
        compiler_params=pltpu.CompilerParams(
            dimension_semantics=("arbitrary", "arbitrary"), vmem_limit_bytes=VMEM_LIMIT),
        name="proj_ctx" if ctx_row is not None else "proj_lat",
    )(x, mod, w_all, bdq, bdk, gq_all, gk_all, cos, sin)


def _fourier_kernel(x_ref, w1_ref, w2_ref, cs_ref, y_ref, a_s):
    l1n, l2n, pack, cw = DFT_L1, DFT_L2, DFT_PACK, FOURIER_WIDTH
    for g in range(l2n // pack):
        xg = x_ref[0, :, g].reshape(l1n * pack, cw)
        ag = _dot(w1_ref[g], xg).astype(BF16)
        a_s[:, :, g * pack:(g + 1) * pack, :] = ag.reshape(2, l1n, pack, cw)
    grp = DFT2_GROUP
    for g in range(l1n // grp):
        rhs = jnp.concatenate(
            [jnp.concatenate([a_s[0, g * grp + j], a_s[1, g * grp + j]], axis=0)
             for j in range(grp)], axis=1)
        f = _dot(w2_ref[...], rhs)
        for j in range(grp):
            cols = slice(j * cw, (j + 1) * cw)
            fc = jnp.concatenate([f[:l2n, cols], f[l2n:, cols]], axis=1).astype(BF16)
            out_cols = slice((g * grp + j) * cw, (g * grp + j + 1) * cw)
            y_ref[0, :, out_cols] = _dot(fc, cs_ref[...]).astype(BF16)


def _fourier_call(fa, w1, w2, cs):
    b, length, cw = fa.shape
    l1n, l2n, pack = DFT_L1, DFT_L2, DFT_PACK
    const = lambda bi: (0, 0)
    return pl.pallas_call(
        _fourier_kernel,
        grid=(b,),
        in_specs=[
            pl.BlockSpec((1, l1n, l2n // pack, pack, cw), lambda bi: (bi, 0, 0, 0, 0)),
            pl.BlockSpec(w1.shape, lambda bi: (0, 0, 0), pipeline_mode=pl.Buffered(1)),
            pl.BlockSpec(w2.shape, const),
            pl.BlockSpec(cs.shape, const),
        ],
        out_specs=pl.BlockSpec((1, l2n, l1n * cw), lambda bi: (bi, 0, 0)),
        out_shape=jax.ShapeDtypeStruct((b, l2n, l1n * cw), BF16),
        scratch_shapes=[pltpu.VMEM((2, l1n, l2n, cw), BF16)],
        compiler_params=pltpu.CompilerParams(
            dimension_semantics=("arbitrary",), vmem_limit_bytes=VMEM_LIMIT),
        name="fourier_lat",
    )(fa.reshape(b, l1n, l2n // pack, pack, cw), w1, w2, cs)


def _fourier_ctx_kernel(fa_ref, wc_ref, cs_ref, y_ref):
    n = fa_ref.shape[1]
    f = _dot(wc_ref[...], fa_ref[0])
    fc = jnp.concatenate([f[:n], f[n:]], axis=1).astype(BF16)
    y_ref[0] = _dot(fc, cs_ref[...])


def _fourier_ctx_call(fa, wc, cs):
    b, length, width = fa.shape
    return pl.pallas_call(
        _fourier_ctx_kernel,
        grid=(b,),
        in_specs=[
            pl.BlockSpec((1, length, width), lambda bi: (bi, 0, 0)),
            pl.BlockSpec(wc.shape, lambda bi: (0, 0)),
            pl.BlockSpec(cs.shape, lambda bi: (0, 0)),
        ],
        out_specs=pl.BlockSpec((1, length, width), lambda bi: (bi, 0, 0)),
        out_shape=jax.ShapeDtypeStruct((b, length, width), F32),
        compiler_params=pltpu.CompilerParams(
            dimension_semantics=("arbitrary",), vmem_limit_bytes=VMEM_LIMIT),
        name="fourier_ctx",
    )(fa, wc, cs)


def _mix_kernel(*refs, local, tile, n_blocks, ctx_row, layer):
    if local:
        (x_ref, main_ref, y_ref, perm_ref, qt_ref, kp_ref, km_ref, kn_ref, vtp_ref, vtm_ref, vtn_ref,
         kc_ref, vtc_ref, mod_ref, sink_ref, wsg_ref, bsg_ref,
         wpa_ref, wpb_ref, wpc_ref, wout_ref, o_ref,
         kbuf, vtbuf, ya_s, yb_s, yc_s, ma_s, mg_s) = refs
    else:
        (x_ref, main_ref, y_ref, qt_ref, kc_ref, vtc_ref, mod_ref, sink_ref, wsg_ref, bsg_ref,
         wpa_ref, wpb_ref, wpc_ref, wout_ref, o_ref,
         ya_s, yb_s, yc_s, ma_s, mg_s) = refs
    d = D_MODEL
    nsub = tile // BLOCK
    tile_idx = pl.program_id(1)
    heads_per_kv = N_HEADS // N_KV_HEADS
    cols_all = N_HEADS * BLOCK

    n_ctx = kc_ref.shape[1]
    ctx_blocks = [vtc_ref[0, s] for s in range(vtc_ref.shape[1])]
    if local:
        kbuf[0:BLOCK, :] = kp_ref[0]
        kbuf[BLOCK:BLOCK + tile, :] = km_ref[0]
        kbuf[BLOCK + tile:2 * BLOCK + tile, :] = kn_ref[0]
        blocks = ctx_blocks + [vtp_ref[0, 0]] + [vtm_ref[0, s] for s in range(nsub)] + [vtn_ref[0, 0]]
        for h in range(N_KV_HEADS):
            vtbuf[h, HEAD_DIM:, :] = jnp.ones((ONES_ROWS, vtbuf.shape[2]), BF16)
            for i, blk in enumerate(blocks):
                vtbuf[h, 0:HEAD_DIM, i * BLOCK:(i + 1) * BLOCK] = blk[h * HEAD_DIM:(h + 1) * HEAD_DIM]
    else:
        vt = jnp.concatenate(ctx_blocks, axis=1)
        ones = jnp.ones((ONES_ROWS, n_ctx), BF16)
        vt_ctx = [jnp.concatenate([vt[h * HEAD_DIM:(h + 1) * HEAD_DIM], ones], axis=0)
                  for h in range(N_KV_HEADS)]

    if local:
        cw = FOURIER_WIDTH
        stacked = jnp.concatenate(
            [y_ref[0, :, k1 * cw:(k1 + 1) * cw] for k1 in range(DFT_L1)], axis=0)
        y = _dot(perm_ref[...], stacked)
    else:
        y = y_ref[0]
    ya_s[...] = (y * _silu_of_half(main_ref[0, :, MAIN_ZA:MAIN_ZA + FOURIER_WIDTH].astype(F32))).astype(BF16)

    sgu_grp = lax.broadcasted_iota(jnp.int32, (CHUNK, SGU_WIDTH), 1) // (SGU_WIDTH // SGU_GROUPS)
    zero_head = jnp.zeros((HEAD_DIM, BLOCK), BF16)
    sink_row = jnp.concatenate(
        [jnp.full((1, BLOCK), sink_ref[layer, hh] * LOG2E, F32) for hh in range(N_HEADS)], axis=1)
    kq_diff = (lax.broadcasted_iota(jnp.int32, (BLOCK, cols_all), 0)
               - lax.broadcasted_iota(jnp.int32, (BLOCK, cols_all), 1) % BLOCK)

    def sgu(n):
        rows = slice(n * BLOCK, (n + 1) * BLOCK)
        vs = main_ref[0, rows, MAIN_VS:MAIN_VS + SGU_WIDTH].astype(F32)
        vn = (vs * lax.rsqrt(jnp.mean(vs * vs, axis=-1, keepdims=True) + EPS)).astype(BF16)
        rhs = jnp.concatenate(
            [jnp.where(sgu_grp == g, vn, jnp.zeros_like(vn)) for g in range(SGU_GROUPS)], axis=0)
        mixed = _dot(wsg_ref[0], rhs) + bsg_ref[0]
        u = main_ref[0, rows, MAIN_U:MAIN_U + SGU_WIDTH].astype(F32)
        zb = main_ref[0, rows, MAIN_ZB:MAIN_ZB + SGU_WIDTH].astype(F32)
        yb_s[rows, :] = (u * mixed * _silu_of_half(zb)).astype(BF16)

    def scores(n):
        qt = qt_ref[0, n]
        cols = []
        for hh in range(N_HEADS):
            qh = qt[hh * HEAD_DIM:(hh + 1) * HEAD_DIM, :]
            cols.append(jnp.concatenate(
                [qh, zero_head] if hh < heads_per_kv else [zero_head, qh], axis=0))
        qst = jnp.concatenate(cols, axis=1)

        s_ctx = _dot(kc_ref[0], qst)
        s_loc = _dot(kbuf[n * BLOCK:(n + 3) * BLOCK, :], qst) if local else None
        return s_ctx, s_loc

    def attend(n, j, s_ctx, s_loc):
        rows = slice(n * BLOCK, (n + 1) * BLOCK)
        lanes = slice(2 * j * BLOCK, (2 * j + 2) * BLOCK)
        sink_j = sink_row[:, lanes]
        s_ctx = s_ctx[:, lanes]
        m = jnp.maximum(jnp.max(s_ctx, axis=0, keepdims=True), sink_j)
        if local:
            gblk = tile_idx * nsub + n
            diff = kq_diff[:, lanes]
            s_prev = jnp.where(diff >= jnp.where(gblk > 0, 0, BLOCK), s_loc[0:BLOCK, lanes], NEG_INF)
            s_own = s_loc[BLOCK:2 * BLOCK, lanes]
            s_next = jnp.where(diff <= jnp.where(gblk < n_blocks - 1, 0, -BLOCK),
                               s_loc[2 * BLOCK:3 * BLOCK, lanes], NEG_INF)
            m_loc = jnp.max(jnp.maximum(jnp.maximum(s_prev, s_own), s_next), axis=0, keepdims=True)
            m = jnp.maximum(m, m_loc)
        e_ctx = jnp.exp2(s_ctx - m)
        h = (2 * j) // heads_per_kv
        if local:
            e = jnp.concatenate(
                [e_ctx, jnp.exp2(s_prev - m), jnp.exp2(s_own - m), jnp.exp2(s_next - m)], axis=0)
            first = n_ctx + n * BLOCK
            vt = jnp.concatenate([vtbuf[h, :, 0:n_ctx], vtbuf[h, :, first:first + 3 * BLOCK]], axis=1)
            ot = _dot(vt, e.astype(BF16))
        else:
            ot = _dot(vt_ctx[h], e_ctx.astype(BF16))
        den = ot[HEAD_DIM:HEAD_DIM + 1] + jnp.exp2(sink_j - m)
        ot = ot[0:HEAD_DIM] * (1.0 / den)
        pair = jnp.concatenate([ot[:, 0:BLOCK], ot[:, BLOCK:2 * BLOCK]], axis=0)
        zc = main_ref[0, rows, MAIN_ZC + j * LANES:MAIN_ZC + (j + 1) * LANES].astype(F32)
        yc_s[rows, j * LANES:(j + 1) * LANES] = (pair.T * _silu_of_half(zc)).astype(BF16)

    ahead = scores(0)
    for n in range(nsub):
        sgu(n)
        current, ahead = ahead, (scores(n + 1) if n + 1 < nsub else None)
        cols = slice(n * (d // nsub), (n + 1) * (d // nsub))
        ma_s[:, cols] = _dot(ya_s[...], wpa_ref[0, :, cols])
        for j in range(ATTN_WIDTH // LANES):
            attend(n, j, *current)

    row = pl.program_id(0) if ctx_row is None else ctx_row
    gate = mod_ref[0, pl.ds(row, 1), 2 * d:3 * d]
    width = 2 * LANES
    for c0 in range(0, d, width):
        cols = slice(c0, c0 + width)
        ma = ma_s[:, cols]
        mb = _dot(yb_s[...], wpb_ref[0, :, cols])
        mc = _dot(yc_s[...], wpc_ref[0, :, cols])
        ta, tb, tc = (jnp.tanh(main_ref[0, :, MAIN_G + i * d + c0:MAIN_G + i * d + c0 + width].astype(F32))
                      for i in range(3))
        mg_s[:, cols] = ((ma + ta * ma) + (mb + tb * mb) + (mc + tc * mc)).astype(BF16)
    for c0 in range(0, d, width):
        cols = slice(c0, c0 + width)
        o_ref[0, :, cols] = x_ref[0, :, cols] + gate[:, cols] * _dot(mg_s[...], wout_ref[0, :, cols])


def _mix_call(x, main, y, qt, k, vt, kc, vtc, mod, sinks, wsg, bsg, wpa, wpb, wpc, wout, l,
              *, tile, local, ctx_row):
    b, length, d = x.shape
    nt = length // tile
    n_ctx = kc.shape[1]
    sub = tile // BLOCK
    n_blocks = length // BLOCK
    tok = lambda bi, i: (bi, i, 0)
    blk = lambda bi, i: (bi, i, 0, 0)
    per_b = lambda bi, i: (bi, 0, 0)
    per_b4 = lambda bi, i: (bi, 0, 0, 0)

    in_specs = [
        pl.BlockSpec((1, tile, d), tok),
        pl.BlockSpec((1, tile, MAIN_WIDTH), tok),
    ]
    args = [x, main, y]
    if local:
        rows_k2 = tile // DFT_L1
        r = jnp.arange(tile)
        perm = (r[:, None] == (r[None, :] % rows_k2) * DFT_L1 + r[None, :] // rows_k2).astype(BF16)
        in_specs += [pl.BlockSpec((1, rows_k2, DFT_L1 * FOURIER_WIDTH), tok),
                     pl.BlockSpec((tile, tile), lambda bi, i: (0, 0))]
        args.append(perm)
    else:
        in_specs.append(pl.BlockSpec((1, tile, FOURIER_WIDTH), tok))
    in_specs.append(pl.BlockSpec((1, sub, ATTN_WIDTH, BLOCK), blk))
    args.append(qt)
    if local:
        prev_blk = lambda bi, i: jnp.maximum(i * sub - 1, 0)
        next_blk = lambda bi, i: jnp.minimum((i + 1) * sub, n_blocks - 1)
        in_specs += [
            pl.BlockSpec((1, BLOCK, KV_WIDTH), lambda bi, i: (bi, prev_blk(bi, i), 0)),
            pl.BlockSpec((1, tile, KV_WIDTH), tok),
            pl.BlockSpec((1, BLOCK, KV_WIDTH), lambda bi, i: (bi, next_blk(bi, i), 0)),
            pl.BlockSpec((1, 1, KV_WIDTH, BLOCK), lambda bi, i: (bi, prev_blk(bi, i), 0, 0)),
            pl.BlockSpec((1, sub, KV_WIDTH, BLOCK), blk),
            pl.BlockSpec((1, 1, KV_WIDTH, BLOCK), lambda bi, i: (bi, next_blk(bi, i), 0, 0)),
        ]
        args += [k, k, k, vt, vt, vt]
    in_specs += [
        pl.BlockSpec((1, n_ctx, KV_WIDTH), per_b),
        pl.BlockSpec((1, n_ctx // BLOCK, KV_WIDTH, BLOCK), per_b4),
        pl.BlockSpec((1, MOD_ROWS, 3 * d), _layer(l, 3)),
        pl.BlockSpec(memory_space=pltpu.SMEM),
    ]
    args += [kc, vtc, mod, sinks]
    for w in (wsg, bsg, wpa, wpb, wpc, wout):
        in_specs.append(pl.BlockSpec((1,) + w.shape[1:], _layer(l, 3)))
        args.append(w)

    scratch = []
    if local:
        scratch += [pltpu.VMEM((tile + 2 * BLOCK, KV_WIDTH), BF16),
                    pltpu.VMEM((N_KV_HEADS, HEAD_DIM + ONES_ROWS, n_ctx + (sub + 2) * BLOCK), BF16)]
    scratch += [
        pltpu.VMEM((tile, FOURIER_WIDTH), BF16),
        pltpu.VMEM((tile, SGU_WIDTH), BF16),
        pltpu.VMEM((tile, ATTN_WIDTH), BF16),
        pltpu.VMEM((tile, d), F32),
        pltpu.VMEM((tile, d), BF16),
    ]
    kern = functools.partial(_mix_kernel, local=local, tile=tile, n_blocks=n_blocks,
                             ctx_row=ctx_row, layer=l)
    return pl.pallas_call(
        kern,
        grid=(b, nt),
        in_specs=in_specs,
        out_specs=pl.BlockSpec((1, tile, d), tok),
        out_shape=jax.ShapeDtypeStruct((b, length, d), F32),
        scratch_shapes=scratch,
        compiler_params=pltpu.CompilerParams(
            dimension_semantics=("arbitrary", "arbitrary"), vmem_limit_bytes=VMEM_LIMIT),
        name="mix_lat" if local else "mix_ctx",
    )(*args)


def _rope_tables(length):
    nf = HEAD_DIM // 4
    inv = jnp.tile(ROPE_THETA ** (-jnp.arange(nf, dtype=F32) / nf), LANES // nf)[None, :]
    lane = jnp.arange(LANES)[None, :]
    on_row = (lane // (2 * nf)) % 2 == 0
    sign = jnp.where((lane // nf) % 2 == 0, -1.0, 1.0)
    ang_r = jnp.arange(length // GRID_W, dtype=F32)[:, None] * inv
    ang_c = jnp.arange(GRID_W, dtype=F32)[:, None] * inv

    def table(fn, scale):
        by_row = jnp.where(on_row, fn(ang_r) * scale, 0.0)[:, None, :]
        by_col = jnp.where(on_row, 0.0, fn(ang_c) * scale)[None, :, :]
        return (by_row + by_col).reshape(length, LANES)

    return table(jnp.cos, 1.0), table(jnp.sin, sign)


def _angle(num, den):
    return (num % den).astype(F32) * (2.0 * math.pi / den)


def _channel_dft(width):
    gw = FOURIER_WIDTH // FOURIER_GROUPS
    m = jnp.arange(width)
    same = (m[:, None] // gw) == (m[None, :] // gw)
    th = _angle((m[:, None] % gw) * (m[None, :] % gw), gw)
    cc = jnp.where(same, jnp.cos(th), 0.0) * gw ** -0.5
    sc = jnp.where(same, jnp.sin(th), 0.0) * gw ** -0.5
    return jnp.concatenate([cc, sc], axis=0).astype(BF16)


def _fourier_tables():
    l1n, l2n = DFT_L1, DFT_L2
    length = l1n * l2n
    pack = DFT_PACK
    l2 = jnp.arange(l2n)[:, None, None]
    k1 = jnp.arange(l1n)[None, :, None]
    l1 = jnp.arange(l1n)[None, None, :]
    th = _angle(k1 * l1 * l2n + k1 * l2, length)
    w1 = jnp.stack([jnp.cos(th), -jnp.sin(th)], axis=1) * l1n ** -0.5
    rows, cols = 2 * l1n * pack, l1n * pack
    ws = jnp.transpose(w1.reshape(l2n // pack, pack, 2 * l1n, l1n), (0, 2, 1, 3))
    ws = ws.reshape(l2n // pack, rows, l1n).astype(BF16)
    spread = (jnp.arange(cols)[None, :] // pack == jnp.arange(l1n)[:, None]).astype(BF16)
    same_p = (jnp.arange(rows)[:, None] % pack) == (jnp.arange(cols)[None, :] % pack)
    w1 = jnp.einsum("grb,bc->grc", ws, spread, preferred_element_type=F32)
    w1 = jnp.where(same_p[None], w1, 0.0).astype(BF16)
    k2 = jnp.arange(l2n)
    th2 = _angle(k2[:, None] * k2[None, :], l2n)
    c2, s2 = jnp.cos(th2), jnp.sin(th2)
    w2 = (jnp.concatenate([jnp.concatenate([c2, s2], axis=1),
                           jnp.concatenate([-s2, c2], axis=1)], axis=0) * l2n ** -0.5).astype(BF16)
    return w1, w2


def _fourier_ctx_table(n):
    k = jnp.arange(n)
    th = _angle(k[:, None] * k[None, :], n)
    return (jnp.concatenate([jnp.cos(th), -jnp.sin(th)], axis=0) * n ** -0.5).astype(BF16)


def _block_diag_ones(width):
    m = jnp.arange(width) // HEAD_DIM
    return (m[:, None] == m[None, :]).astype(BF16)


def _prep_w_in(w_in):
    col = jnp.arange(W_TOTAL)
    halved = ((col >= W_ZA) & (col < W_ZA + FOURIER_WIDTH)) | ((col >= W_ZB) & (col < W_Q)) | (col >= W_ZC)
    return (w_in * jnp.where(halved, 0.5, 1.0)).astype(BF16)


def kernel(x, c, ctx, c_ctx, w_ada, b_ada, w_in, sgu_w, sgu_b, q_norm_g, k_norm_g,
           attn_sink, w_pa, w_pb, w_pc, w_out):
    b, length, d = x.shape
    n_ctx = ctx.shape[1]
    assert length == DFT_L1 * DFT_L2 and d == D_MODEL and b + 1 <= MOD_ROWS
    assert w_in.shape == (DEPTH, d, W_TOTAL)
    lat_tile = 512
    ctx_row = b

    cs = jnp.zeros((MOD_ROWS, d), F32).at[:b].set(c).at[b].set(c_ctx)
    mod = _ada_call(cs, w_ada, b_ada)

    rope_tables = _rope_tables(length)
    w1, w2 = _fourier_tables()
    cs_dft = _channel_dft(FOURIER_WIDTH)
    wc = _fourier_ctx_table(n_ctx)
    bdq = _block_diag_ones(ATTN_WIDTH)
    bdk = _block_diag_ones(KV_WIDTH)

    w_all = _prep_w_in(w_in)
    gq_all = jnp.tile(q_norm_g, (1, N_HEADS))[:, None, :]
    gk_all = jnp.tile(k_norm_g, (1, N_KV_HEADS))[:, None, :]
    wsg = jnp.transpose(sgu_w, (0, 2, 1, 3)).reshape(DEPTH, CHUNK, SGU_GROUPS * CHUNK).astype(BF16)
    bsg = jnp.repeat(jnp.swapaxes(sgu_b, 1, 2), SGU_WIDTH // SGU_GROUPS, axis=2)
    wpa, wpb, wpc = w_pa.astype(BF16), w_pb.astype(BF16), w_pc.astype(BF16)
    wout = (0.5 * w_out).astype(BF16)

    xc = ctx
    for l in range(DEPTH):
        last = l == DEPTH - 1
        fa_c, main_c, qt_c, kc, vtc = _proj_call(
            xc.reshape(1, b * n_ctx, d), mod, w_all, bdq, bdk, gq_all, gk_all, None, l,
            tile=2 * n_ctx, ctx_row=ctx_row)
        kc = kc.reshape(b, n_ctx, KV_WIDTH)
        vtc = vtc.reshape(b, n_ctx // BLOCK, KV_WIDTH, BLOCK)
        fa, main, qt, k, vt = _proj_call(
            x, mod, w_all, bdq, bdk, gq_all, gk_all, rope_tables, l, tile=lat_tile, ctx_row=None)
        y = _fourier_call(fa, w1, w2, cs_dft)
        x = _mix_call(x, main, y, qt, k, vt, kc, vtc, mod, attn_sink, wsg, bsg, wpa, wpb, wpc, wout,
                      l, tile=lat_tile, local=True, ctx_row=None)
        if not last:
            y_c = _fourier_ctx_call(fa_c.reshape(b, n_ctx, FOURIER_WIDTH), wc, cs_dft)
            xc = _mix_call(xc, main_c.reshape(b, n_ctx, MAIN_WIDTH), y_c,
                           qt_c.reshape(b, n_ctx // BLOCK, ATTN_WIDTH, BLOCK), None, None, kc, vtc,
                           mod, attn_sink, wsg, bsg, wpa, wpb, wpc, wout, l,
                           tile=n_ctx, local=False, ctx_row=ctx_row)
    return x
```

```python
import functools
import math

import jax
import jax.numpy as jnp
from jax import lax
from jax.experimental import pallas as pl
from jax.experimental.pallas import tpu as pltpu

F32 = jnp.float32
BF16 = jnp.bfloat16

D_MODEL = 1024
DEPTH = 4
GRID_W = 64
FOURIER_WIDTH = 256
FOURIER_GROUPS = 4
SGU_WIDTH = 256
SGU_GROUPS = 4
CHUNK = 128
N_HEADS = 8
N_KV_HEADS = 2
HEAD_DIM = 64
ATTN_WIDTH = N_HEADS * HEAD_DIM
KV_WIDTH = N_KV_HEADS * HEAD_DIM
BLOCK = 128
ROPE_THETA = 10000.0
EPS = 1e-6
NEG_INF = -1e30
LOG2E = math.log2(math.e)

LANES = 128
MOD_ROWS = 8
ONES_ROWS = 16

W_FA = 0
W_ZA = W_FA + FOURIER_WIDTH
W_ZB = W_ZA + FOURIER_WIDTH + 2 * SGU_WIDTH
W_Q = W_ZB + SGU_WIDTH
W_K = W_Q + ATTN_WIDTH
W_V = W_K + KV_WIDTH
W_ZC = W_V + KV_WIDTH
W_G = W_ZC + ATTN_WIDTH
W_TOTAL = W_G + 3 * D_MODEL
MAIN_ZA = 0
MAIN_U = MAIN_ZA + FOURIER_WIDTH
MAIN_VS = MAIN_U + SGU_WIDTH
MAIN_ZB = MAIN_VS + SGU_WIDTH
MAIN_ZC = MAIN_ZB + SGU_WIDTH
MAIN_WIDTH = MAIN_ZC + ATTN_WIDTH
PROJ_CHUNK = 512

DFT_L1 = 32
DFT_L2 = 256
DFT_PACK = 16
DFT2_GROUP = 8

VMEM_LIMIT = 56 * 1024 * 1024


def _silu(z):
    return 0.5 * z * (1.0 + jnp.tanh(0.5 * z))


def _silu_of_half(zh):
    return zh * (1.0 + jnp.tanh(zh))


def _dot(a, b):
    return jnp.dot(a, b, preferred_element_type=F32)


def _layer(l, rank):
    return lambda *_: (l,) + (0,) * (rank - 1)


def _ada_kernel(c_ref, w_ref, b_ref, o_ref):
    s = _silu(c_ref[...]).astype(BF16)
    o_ref[0] = _dot(s, w_ref[0].astype(BF16)) + b_ref[0]


def _ada_call(cs, w_ada, b_ada):
    d = D_MODEL
    return pl.pallas_call(
        _ada_kernel,
        grid=(DEPTH, 3),
        in_specs=[
            pl.BlockSpec((MOD_ROWS, d), lambda l, j: (0, 0)),
            pl.BlockSpec((1, d, d), lambda l, j: (l, 0, j)),
            pl.BlockSpec((1, 1, d), lambda l, j: (l, 0, j)),
        ],
        out_specs=pl.BlockSpec((1, MOD_ROWS, d), lambda l, j: (l, 0, j)),
        out_shape=jax.ShapeDtypeStruct((DEPTH, MOD_ROWS, 3 * d), F32),
        compiler_params=pltpu.CompilerParams(
            dimension_semantics=("arbitrary", "arbitrary"), vmem_limit_bytes=VMEM_LIMIT),
        name="ada_mod",
    )(cs, w_ada, b_ada.reshape(DEPTH, 1, 3 * d))


def _norm_rope(t, bd_ref, gain, cos, sin, out_scale):
    width = t.shape[-1]
    ssq = _dot((t * t).astype(BF16), bd_ref[...])
    tn = t * lax.rsqrt(ssq * (1.0 / HEAD_DIM) + EPS) * gain
    if cos is not None:
        reps = width // LANES
        cos_w = jnp.concatenate([cos] * reps, axis=1) if reps > 1 else cos
        sin_w = jnp.concatenate([sin] * reps, axis=1) if reps > 1 else sin
        lane = lax.broadcasted_iota(jnp.int32, tn.shape, 1)
        first = (lane % 32) < 16
        partner = jnp.where(first, pltpu.roll(tn, width - 16, 1), pltpu.roll(tn, 16, 1))
        tn = tn * cos_w + partner * sin_w
    if out_scale != 1.0:
        tn = tn * out_scale
    return tn


def _proj_kernel(x_ref, mod_ref, w_ref, bdq_ref, bdk_ref, gq_ref, gk_ref, cos_ref, sin_ref,
                 fa_ref, main_ref, hb_ref, qt_ref, k_ref, vt_ref, *, ctx_row, rope):
    d = D_MODEL
    nsub = x_ref.shape[1] // BLOCK
    xt = x_ref[0]
    ms = jnp.mean(xt * xt, axis=-1, keepdims=True)
    row = pl.program_id(0) if ctx_row is None else ctx_row
    modrow = mod_ref[0, pl.ds(row, 1), :]
    shift = modrow[:, 0:d]
    scale = modrow[:, d:2 * d]
    hb = ((xt * lax.rsqrt(ms + EPS)) * (1.0 + scale) + shift).astype(BF16)
    hb_ref[0] = hb
    cos = cos_ref[...] if rope else None
    sin = sin_ref[...] if rope else None

    def proj(c0, width):
        return _dot(hb, w_ref[0, :, c0:c0 + width])

    chunks = [(src + c0, dst + c0)
              for src, dst, width in ((W_ZA, MAIN_ZA, W_Q - W_ZA), (W_ZC, MAIN_ZC, W_G - W_ZC))
              for c0 in range(0, width, PROJ_CHUNK)]

    def wide(count):
        for _ in range(min(count, len(chunks))):
            src, dst = chunks.pop(0)
            main_ref[0, :, dst:dst + PROJ_CHUNK] = proj(src, PROJ_CHUNK).astype(BF16)

    q_raw = proj(W_Q, ATTN_WIDTH)
    k_raw = proj(W_K, KV_WIDTH)
    v = proj(W_V, KV_WIDTH)
    wide(1)
    q = _norm_rope(q_raw, bdq_ref, gq_ref[0], cos, sin, HEAD_DIM ** -0.5 * LOG2E)
    wide(1)
    for s in range(nsub):
        qt_ref[0, s] = q[s * BLOCK:(s + 1) * BLOCK, :].T.astype(BF16)
    wide(1)
    k = _norm_rope(k_raw, bdk_ref, gk_ref[0], cos, sin, 1.0)
    k_ref[0] = k.astype(BF16)
    for s in range(nsub):
        vt_ref[0, s] = v[s * BLOCK:(s + 1) * BLOCK, :].T.astype(BF16)
    fa_ref[0] = proj(W_FA, FOURIER_WIDTH).astype(BF16)
    wide(len(chunks))


def _proj_call(x, mod, w_all, bdq, bdk, gq_all, gk_all, rope_tables, l, *, tile, ctx_row):
    b, length, d = x.shape
    nt = length // tile
    nsub = tile // BLOCK
    const = lambda bi, i: (0, 0)
    tok = lambda bi, i: (bi, i, 0)
    blk = lambda bi, i: (bi, i, 0, 0)
    rope = rope_tables is not None
    if rope:
        cos, sin = rope_tables
        pos, pos_rows = (lambda bi, i: (i, 0)), tile
    else:
        cos = sin = jnp.zeros((MOD_ROWS, LANES), F32)
        pos, pos_rows = const, MOD_ROWS
    kern = functools.partial(_proj_kernel, ctx_row=ctx_row, rope=rope)
    return pl.pallas_call(
        kern,
        grid=(b, nt),
        in_specs=[
            pl.BlockSpec((1, tile, d), tok),
            pl.BlockSpec((1, MOD_ROWS, 3 * d), _layer(l, 3)),
            pl.BlockSpec((1, d, W_G), _layer(l, 3), pipeline_mode=pl.Buffered(1)),
            pl.BlockSpec((ATTN_WIDTH, ATTN_WIDTH), const),
            pl.BlockSpec((KV_WIDTH, KV_WIDTH), const),
            pl.BlockSpec((1, 1, ATTN_WIDTH), _layer(l, 3)),
            pl.BlockSpec((1, 1, KV_WIDTH), _layer(l, 3)),
            pl.BlockSpec((pos_rows, LANES), pos),
            pl.BlockSpec((pos_rows, LANES), pos),
        ],
        out_specs=[
            pl.BlockSpec((1, tile, FOURIER_WIDTH), tok),
            pl.BlockSpec((1, tile, MAIN_WIDTH), tok),
            pl.BlockSpec((1, tile, d), tok),
            pl.BlockSpec((1, nsub, ATTN_WIDTH, BLOCK), blk),
            pl.BlockSpec((1, tile, KV_WIDTH), tok),
            pl.BlockSpec((1, nsub, KV_WIDTH, BLOCK), blk),
        ],
        out_shape=[
            jax.ShapeDtypeStruct((b, length, FOURIER_WIDTH), BF16),
            jax.ShapeDtypeStruct((b, length, MAIN_WIDTH), BF16),
            jax.ShapeDtypeStruct((b, length, d), BF16),
            jax.ShapeDtypeStruct((b, length // BLOCK, ATTN_WIDTH, BLOCK), BF16),
            jax.ShapeDtypeStruct((b, length, KV_WIDTH), BF16),
            jax.ShapeDtypeStruct((b, length // BLOCK, KV_WIDTH, BLOCK), BF16),
        ],
        compiler_params=pltpu.CompilerParams(
            dimension_semantics=("arbitrary", "arbitrary"), vmem_limit_bytes=VMEM_LIMIT),
        name="proj_ctx" if ctx_row is not None else "proj_lat",
    )(x, mod, w_all, bdq, bdk, gq_all, gk_all, cos, sin)


def _fourier_kernel(x_ref, w1_ref, w2_ref, cs_ref, y_ref, a_s):
    l1n, l2n, pack, cw = DFT_L1, DFT_L2, DFT_PACK, FOURIER_WIDTH
    for g in range(l2n // pack):
        xg = x_ref[0, :, g].reshape(l1n * pack, cw)
        ag = _dot(w1_ref[g], xg).astype(BF16)
        a_s[:, :, g * pack:(g + 1) * pack, :] = ag.reshape(2, l1n, pack, cw)
    grp = DFT2_GROUP
    for g in range(l1n // grp):
        rhs = jnp.concatenate(
            [jnp.concatenate([a_s[0, g * grp + j], a_s[1, g * grp + j]], axis=0)
             for j in range(grp)], axis=1)
        f = _dot(w2_ref[...], rhs)
        for j in range(grp):
            cols = slice(j * cw, (j + 1) * cw)
            fc = jnp.concatenate([f[:l2n, cols], f[l2n:, cols]], axis=1).astype(BF16)
            out_cols = slice((g * grp + j) * cw, (g * grp + j + 1) * cw)
            y_ref[0, :, out_cols] = _dot(fc, cs_ref[...]).astype(BF16)


def _fourier_call(fa, w1, w2, cs):
    b, length, cw = fa.shape
    l1n, l2n, pack = DFT_L1, DFT_L2, DFT_PACK
    const = lambda bi: (0, 0)
    return pl.pallas_call(
        _fourier_kernel,
        grid=(b,),
        in_specs=[
            pl.BlockSpec((1, l1n, l2n // pack, pack, cw), lambda bi: (bi, 0, 0, 0, 0)),
            pl.BlockSpec(w1.shape, lambda bi: (0, 0, 0), pipeline_mode=pl.Buffered(1)),
            pl.BlockSpec(w2.shape, const),
            pl.BlockSpec(cs.shape, const),
        ],
        out_specs=pl.BlockSpec((1, l2n, l1n * cw), lambda bi: (bi, 0, 0)),
        out_shape=jax.ShapeDtypeStruct((b, l2n, l1n * cw), BF16),
        scratch_shapes=[pltpu.VMEM((2, l1n, l2n, cw), BF16)],
        compiler_params=pltpu.CompilerParams(
            dimension_semantics=("arbitrary",), vmem_limit_bytes=VMEM_LIMIT),
        name="fourier_lat",
    )(fa.reshape(b, l1n, l2n // pack, pack, cw), w1, w2, cs)


def _fourier_ctx_kernel(fa_ref, wc_ref, cs_ref, y_ref):
    n = fa_ref.shape[1]
    f = _dot(wc_ref[...], fa_ref[0])
    fc = jnp.concatenate([f[:n], f[n:]], axis=1).astype(BF16)
    y_ref[0] = _dot(fc, cs_ref[...])


def _fourier_ctx_call(fa, wc, cs):
    b, length, width = fa.shape
    return pl.pallas_call(
        _fourier_ctx_kernel,
        grid=(b,),
        in_specs=[
            pl.BlockSpec((1, length, width), lambda bi: (bi, 0, 0)),
            pl.BlockSpec(wc.shape, lambda bi: (0, 0)),
            pl.BlockSpec(cs.shape, lambda bi: (0, 0)),
        ],
        out_specs=pl.BlockSpec((1, length, width), lambda bi: (bi, 0, 0)),
        out_shape=jax.ShapeDtypeStruct((b, length, width), F32),
        compiler_params=pltpu.CompilerParams(
            dimension_semantics=("arbitrary",), vmem_limit_bytes=VMEM_LIMIT),
        name="fourier_ctx",
    )(fa, wc, cs)


def _mix_kernel(*refs, local, tile, n_blocks, ctx_row, layer):
    if local:
        (x_ref, main_ref, hb_ref, y_ref, perm_ref, qt_ref, kp_ref, km_ref, kn_ref,
         vtp_ref, vtm_ref, vtn_ref, kc_ref, vtc_ref, mod_ref, sink_ref, wsg_ref, bsg_ref,
         wpa_ref, wpb_ref, wpc_ref, wout_ref, wg_ref, o_ref,
         kbuf, vtbuf, ya_s, yb_s, yc_s, ma_s, g_s, mg_s) = refs
    else:
        (x_ref, main_ref, hb_ref, y_ref, qt_ref, kc_ref, vtc_ref, mod_ref, sink_ref, wsg_ref, bsg_ref,
         wpa_ref, wpb_ref, wpc_ref, wout_ref, wg_ref, o_ref,
         ya_s, yb_s, yc_s, ma_s, g_s, mg_s) = refs
    d = D_MODEL
    nsub = tile // BLOCK
    tile_idx = pl.program_id(1)
    heads_per_kv = N_HEADS // N_KV_HEADS
    cols_all = N_HEADS * BLOCK

    if local:
        kbuf[0:BLOCK, :] = kp_ref[0]
        kbuf[BLOCK:BLOCK + tile, :] = km_ref[0]
        kbuf[BLOCK + tile:2 * BLOCK + tile, :] = kn_ref[0]
        vtbuf[0] = vtp_ref[0, 0]
        for s in range(nsub):
            vtbuf[1 + s] = vtm_ref[0, s]
        vtbuf[1 + nsub] = vtn_ref[0, 0]

    def with_ones(vt):
        return jnp.concatenate([vt, jnp.ones((ONES_ROWS, vt.shape[1]), BF16)], axis=0)

    vt_ctx = with_ones(jnp.concatenate([vtc_ref[0, s] for s in range(vtc_ref.shape[1])], axis=1))

    if local:
        cw = FOURIER_WIDTH
        stacked = jnp.concatenate(
            [y_ref[0, :, k1 * cw:(k1 + 1) * cw] for k1 in range(DFT_L1)], axis=0)
        y = _dot(perm_ref[...], stacked)
    else:
        y = y_ref[0]
    ya_s[...] = (y * _silu_of_half(main_ref[0, :, MAIN_ZA:MAIN_ZA + FOURIER_WIDTH].astype(F32))).astype(BF16)

    sgu_grp = lax.broadcasted_iota(jnp.int32, (CHUNK, SGU_WIDTH), 1) // (SGU_WIDTH // SGU_GROUPS)
    zero_head = jnp.zeros((HEAD_DIM, BLOCK), BF16)
    sink_row = jnp.concatenate(
        [jnp.full((1, BLOCK), sink_ref[layer, hh] * LOG2E, F32) for hh in range(N_HEADS)], axis=1)
    kq_diff = (lax.broadcasted_iota(jnp.int32, (BLOCK, cols_all), 0)
               - lax.broadcasted_iota(jnp.int32, (BLOCK, cols_all), 1) % BLOCK)

    def sgu(n):
        rows = slice(n * BLOCK, (n + 1) * BLOCK)
        vs = main_ref[0, rows, MAIN_VS:MAIN_VS + SGU_WIDTH].astype(F32)
        vn = (vs * lax.rsqrt(jnp.mean(vs * vs, axis=-1, keepdims=True) + EPS)).astype(BF16)
        rhs = jnp.concatenate(
            [jnp.where(sgu_grp == g, vn, jnp.zeros_like(vn)) for g in range(SGU_GROUPS)], axis=0)
        mixed = _dot(wsg_ref[0], rhs) + bsg_ref[0]
        u = main_ref[0, rows, MAIN_U:MAIN_U + SGU_WIDTH].astype(F32)
        zb = main_ref[0, rows, MAIN_ZB:MAIN_ZB + SGU_WIDTH].astype(F32)
        yb_s[rows, :] = (u * mixed * _silu_of_half(zb)).astype(BF16)

    def scores(n):
        qt = qt_ref[0, n]
        cols = []
        for hh in range(N_HEADS):
            qh = qt[hh * HEAD_DIM:(hh + 1) * HEAD_DIM, :]
            cols.append(jnp.concatenate(
                [qh, zero_head] if hh < heads_per_kv else [zero_head, qh], axis=0))
        qst = jnp.concatenate(cols, axis=1)

        s_ctx = _dot(kc_ref[0], qst)
        s_loc = _dot(kbuf[n * BLOCK:(n + 3) * BLOCK, :], qst) if local else None
        return s_ctx, s_loc

    def attend(n, j, s_ctx, s_loc, vt_loc):
        rows = slice(n * BLOCK, (n + 1) * BLOCK)
        lanes = slice(2 * j * BLOCK, (2 * j + 2) * BLOCK)
        sink_j = sink_row[:, lanes]
        s_ctx = s_ctx[:, lanes]
        m = jnp.maximum(jnp.max(s_ctx, axis=0, keepdims=True), sink_j)
        if local:
            gblk = tile_idx * nsub + n
            diff = kq_diff[:, lanes]
            s_prev = jnp.where(diff >= jnp.where(gblk > 0, 0, BLOCK), s_loc[0:BLOCK, lanes], NEG_INF)
            s_own = s_loc[BLOCK:2 * BLOCK, lanes]
            s_next = jnp.where(diff <= jnp.where(gblk < n_blocks - 1, 0, -BLOCK),
                               s_loc[2 * BLOCK:3 * BLOCK, lanes], NEG_INF)
            m_loc = jnp.max(jnp.maximum(jnp.maximum(s_prev, s_own), s_next), axis=0, keepdims=True)
            m = jnp.maximum(m, m_loc)
        e_ctx = jnp.exp2(s_ctx - m)
        ot = _dot(vt_ctx, e_ctx.astype(BF16))
        if local:
            e_loc = jnp.concatenate(
                [jnp.exp2(s_prev - m), jnp.exp2(s_own - m), jnp.exp2(s_next - m)], axis=0)
            ot = ot + _dot(vt_loc, e_loc.astype(BF16))
        den = ot[KV_WIDTH:KV_WIDTH + 1] + jnp.exp2(sink_j - m)
        h = (2 * j) // heads_per_kv
        ot = ot[h * HEAD_DIM:(h + 1) * HEAD_DIM] * (1.0 / den)
        pair = jnp.concatenate([ot[:, 0:BLOCK], ot[:, BLOCK:2 * BLOCK]], axis=0)
        zc = main_ref[0, rows, MAIN_ZC + j * LANES:MAIN_ZC + (j + 1) * LANES].astype(F32)
        yc_s[rows, j * LANES:(j + 1) * LANES] = (pair.T * _silu_of_half(zc)).astype(BF16)

    ahead = scores(0)
    for n in range(nsub):
        sgu(n)
        current, ahead = ahead, (scores(n + 1) if n + 1 < nsub else None)
        cols = slice(n * (d // nsub), (n + 1) * (d // nsub))
        ma_s[:, cols] = _dot(ya_s[...], wpa_ref[0, :, cols])
        vt_loc = None
        if local:
            vt_loc = with_ones(jnp.concatenate([vtbuf[n], vtbuf[n + 1], vtbuf[n + 2]], axis=1))
        for j in range(ATTN_WIDTH // LANES):
            if j > 0:
                c0 = (j - 1) * d + n * (d // nsub)
                g_s[j - 1, :, cols] = _dot(hb_ref[0], wg_ref[0, :, c0:c0 + d // nsub])
            attend(n, j, *current, vt_loc)

    row = pl.program_id(0) if ctx_row is None else ctx_row
    gate = mod_ref[0, pl.ds(row, 1), 2 * d:3 * d]
    width = 2 * LANES
    for c0 in range(0, d, width):
        cols = slice(c0, c0 + width)
        ma = ma_s[:, cols]
        mb = _dot(yb_s[...], wpb_ref[0, :, cols])
        mc = _dot(yc_s[...], wpc_ref[0, :, cols])
        ta, tb, tc = (jnp.tanh(g_s[i, :, cols]) for i in range(3))
        mg_s[:, cols] = ((ma + ta * ma) + (mb + tb * mb) + (mc + tc * mc)).astype(BF16)
    for c0 in range(0, d, width):
        cols = slice(c0, c0 + width)
        o_ref[0, :, cols] = x_ref[0, :, cols] + gate[:, cols] * _dot(mg_s[...], wout_ref[0, :, cols])


def _mix_call(x, main, hb, y, qt, k, vt, kc, vtc, mod, sinks, wsg, bsg, wpa, wpb, wpc, wout, wg, l,
              *, tile, local, ctx_row):
    b, length, d = x.shape
    nt = length // tile
    n_ctx = kc.shape[1]
    sub = tile // BLOCK
    n_blocks = length // BLOCK
    tok = lambda bi, i: (bi, i, 0)
    blk = lambda bi, i: (bi, i, 0, 0)
    per_b = lambda bi, i: (bi, 0, 0)
    per_b4 = lambda bi, i: (bi, 0, 0, 0)

    in_specs = [
        pl.BlockSpec((1, tile, d), tok),
        pl.BlockSpec((1, tile, MAIN_WIDTH), tok),
        pl.BlockSpec((1, tile, d), tok),
    ]
    args = [x, main, hb, y]
    if local:
        rows_k2 = tile // DFT_L1
        r = jnp.arange(tile)
        perm = (r[:, None] == (r[None, :] % rows_k2) * DFT_L1 + r[None, :] // rows_k2).astype(BF16)
        in_specs += [pl.BlockSpec((1, rows_k2, DFT_L1 * FOURIER_WIDTH), tok),
                     pl.BlockSpec((tile, tile), lambda bi, i: (0, 0))]
        args.append(perm)
    else:
        in_specs.append(pl.BlockSpec((1, tile, FOURIER_WIDTH), tok))
    in_specs.append(pl.BlockSpec((1, sub, ATTN_WIDTH, BLOCK), blk))
    args.append(qt)
    if local:
        prev_blk = lambda bi, i: jnp.maximum(i * sub - 1, 0)
        next_blk = lambda bi, i: jnp.minimum((i + 1) * sub, n_blocks - 1)
        in_specs += [
            pl.BlockSpec((1, BLOCK, KV_WIDTH), lambda bi, i: (bi, prev_blk(bi, i), 0)),
            pl.BlockSpec((1, tile, KV_WIDTH), tok),
            pl.BlockSpec((1, BLOCK, KV_WIDTH), lambda bi, i: (bi, next_blk(bi, i), 0)),
            pl.BlockSpec((1, 1, KV_WIDTH, BLOCK), lambda bi, i: (bi, prev_blk(bi, i), 0, 0)),
            pl.BlockSpec((1, sub, KV_WIDTH, BLOCK), blk),
            pl.BlockSpec((1, 1, KV_WIDTH, BLOCK), lambda bi, i: (bi, next_blk(bi, i), 0, 0)),
        ]
        args += [k, k, k, vt, vt, vt]
    in_specs += [
        pl.BlockSpec((1, n_ctx, KV_WIDTH), per_b),
        pl.BlockSpec((1, n_ctx // BLOCK, KV_WIDTH, BLOCK), per_b4),
        pl.BlockSpec((1, MOD_ROWS, 3 * d), _layer(l, 3)),
        pl.BlockSpec(memory_space=pltpu.SMEM),
    ]
    args += [kc, vtc, mod, sinks]
    for w in (wsg, bsg, wpa, wpb, wpc, wout):
        in_specs.append(pl.BlockSpec((1,) + w.shape[1:], _layer(l, 3)))
        args.append(w)
    in_specs.append(pl.BlockSpec((1,) + wg.shape[1:], _layer(l, 3), pipeline_mode=pl.Buffered(1)))
    args.append(wg)

    scratch = []
    if local:
        scratch += [pltpu.VMEM((tile + 2 * BLOCK, KV_WIDTH), BF16),
                    pltpu.VMEM((sub + 2, KV_WIDTH, BLOCK), BF16)]
    scratch += [
        pltpu.VMEM((tile, FOURIER_WIDTH), BF16),
        pltpu.VMEM((tile, SGU_WIDTH), BF16),
        pltpu.VMEM((tile, ATTN_WIDTH), BF16),
        pltpu.VMEM((tile, d), F32),
        pltpu.VMEM((3, tile, d), F32),
        pltpu.VMEM((tile, d), BF16),
    ]
    kern = functools.partial(_mix_kernel, local=local, tile=tile, n_blocks=n_blocks,
                             ctx_row=ctx_row, layer=l)
    return pl.pallas_call(
        kern,
        grid=(b, nt),
        in_specs=in_specs,
        out_specs=pl.BlockSpec((1, tile, d), tok),
        out_shape=jax.ShapeDtypeStruct((b, length, d), F32),
        scratch_shapes=scratch,
        compiler_params=pltpu.CompilerParams(
            dimension_semantics=("arbitrary", "arbitrary"), vmem_limit_bytes=VMEM_LIMIT),
        name="mix_lat" if local else "mix_ctx",
    )(*args)


def _rope_tables(length):
    nf = HEAD_DIM // 4
    inv = jnp.tile(ROPE_THETA ** (-jnp.arange(nf, dtype=F32) / nf), LANES // nf)[None, :]
    lane = jnp.arange(LANES)[None, :]
    on_row = (lane // (2 * nf)) % 2 == 0
    sign = jnp.where((lane // nf) % 2 == 0, -1.0, 1.0)
    ang_r = jnp.arange(length // GRID_W, dtype=F32)[:, None] * inv
    ang_c = jnp.arange(GRID_W, dtype=F32)[:, None] * inv

    def table(fn, scale):
        by_row = jnp.where(on_row, fn(ang_r) * scale, 0.0)[:, None, :]
        by_col = jnp.where(on_row, 0.0, fn(ang_c) * scale)[None, :, :]
        return (by_row + by_col).reshape(length, LANES)

    return table(jnp.cos, 1.0), table(jnp.sin, sign)


def _angle(num, den):
    return (num % den).astype(F32) * (2.0 * math.pi / den)


def _channel_dft(width):
    gw = FOURIER_WIDTH // FOURIER_GROUPS
    m = jnp.arange(width)
    same = (m[:, None] // gw) == (m[None, :] // gw)
    th = _angle((m[:, None] % gw) * (m[None, :] % gw), gw)
    cc = jnp.where(same, jnp.cos(th), 0.0) * gw ** -0.5
    sc = jnp.where(same, jnp.sin(th), 0.0) * gw ** -0.5
    return jnp.concatenate([cc, sc], axis=0).astype(BF16)


def _fourier_tables():
    l1n, l2n = DFT_L1, DFT_L2
    length = l1n * l2n
    pack = DFT_PACK
    l2 = jnp.arange(l2n)[:, None, None]
    k1 = jnp.arange(l1n)[None, :, None]
    l1 = jnp.arange(l1n)[None, None, :]
    th = _angle(k1 * l1 * l2n + k1 * l2, length)
    w1 = jnp.stack([jnp.cos(th), -jnp.sin(th)], axis=1) * l1n ** -0.5
    rows, cols = 2 * l1n * pack, l1n * pack
    ws = jnp.transpose(w1.reshape(l2n // pack, pack, 2 * l1n, l1n), (0, 2, 1, 3))
    ws = ws.reshape(l2n // pack, rows, l1n).astype(BF16)
    spread = (jnp.arange(cols)[None, :] // pack == jnp.arange(l1n)[:, None]).astype(BF16)
    same_p = (jnp.arange(rows)[:, None] % pack) == (jnp.arange(cols)[None, :] % pack)
    w1 = jnp.einsum("grb,bc->grc", ws, spread, preferred_element_type=F32)
    w1 = jnp.where(same_p[None], w1, 0.0).astype(BF16)
    k2 = jnp.arange(l2n)
    th2 = _angle(k2[:, None] * k2[None, :], l2n)
    c2, s2 = jnp.cos(th2), jnp.sin(th2)
    w2 = (jnp.concatenate([jnp.concatenate([c2, s2], axis=1),
                           jnp.concatenate([-s2, c2], axis=1)], axis=0) * l2n ** -0.5).astype(BF16)
    return w1, w2


def _fourier_ctx_table(n):
    k = jnp.arange(n)
    th = _angle(k[:, None] * k[None, :], n)
    return (jnp.concatenate([jnp.cos(th), -jnp.sin(th)], axis=0) * n ** -0.5).astype(BF16)


def _block_diag_ones(width):
    m = jnp.arange(width) // HEAD_DIM
    return (m[:, None] == m[None, :]).astype(BF16)


def _prep_w_in(w_in):
    col = jnp.arange(W_G)
    halved = ((col >= W_ZA) & (col < W_ZA + FOURIER_WIDTH)) | ((col >= W_ZB) & (col < W_Q)) | (col >= W_ZC)
    w_proj = (w_in[:, :, :W_G] * jnp.where(halved, 0.5, 1.0)).astype(BF16)
    return w_proj, (0.5 * w_in[:, :, W_G:]).astype(BF16)


def kernel(x, c, ctx, c_ctx, w_ada, b_ada, w_in, sgu_w, sgu_b, q_norm_g, k_norm_g,
           attn_sink, w_pa, w_pb, w_pc, w_out):
    b, length, d = x.shape
    n_ctx = ctx.shape[1]
    assert length == DFT_L1 * DFT_L2 and d == D_MODEL and b + 1 <= MOD_ROWS
    assert w_in.shape == (DEPTH, d, W_TOTAL)
    lat_tile = 512
    ctx_row = b

    cs = jnp.zeros((MOD_ROWS, d), F32).at[:b].set(c).at[b].set(c_ctx)
    mod = _ada_call(cs, w_ada, b_ada)

    rope_tables = _rope_tables(length)
    w1, w2 = _fourier_tables()
    cs_dft = _channel_dft(FOURIER_WIDTH)
    wc = _fourier_ctx_table(n_ctx)
    bdq = _block_diag_ones(ATTN_WIDTH)
    bdk = _block_diag_ones(KV_WIDTH)

    w_all, wg = _prep_w_in(w_in)
    gq_all = jnp.tile(q_norm_g, (1, N_HEADS))[:, None, :]
    gk_all = jnp.tile(k_norm_g, (1, N_KV_HEADS))[:, None, :]
    wsg = jnp.transpose(sgu_w, (0, 2, 1, 3)).reshape(DEPTH, CHUNK, SGU_GROUPS * CHUNK).astype(BF16)
    bsg = jnp.repeat(jnp.swapaxes(sgu_b, 1, 2), SGU_WIDTH // SGU_GROUPS, axis=2)
    wpa, wpb, wpc = w_pa.astype(BF16), w_pb.astype(BF16), w_pc.astype(BF16)
    wout = (0.5 * w_out).astype(BF16)

    xc = ctx
    for l in range(DEPTH):
        last = l == DEPTH - 1
        fa_c, main_c, hb_c, qt_c, kc, vtc = _proj_call(
            xc.reshape(1, b * n_ctx, d), mod, w_all, bdq, bdk, gq_all, gk_all, None, l,
            tile=2 * n_ctx, ctx_row=ctx_row)
        kc = kc.reshape(b, n_ctx, KV_WIDTH)
        vtc = vtc.reshape(b, n_ctx // BLOCK, KV_WIDTH, BLOCK)
        fa, main, hb, qt, k, vt = _proj_call(
            x, mod, w_all, bdq, bdk, gq_all, gk_all, rope_tables, l, tile=lat_tile, ctx_row=None)
        y = _fourier_call(fa, w1, w2, cs_dft)
        x = _mix_call(x, main, hb, y, qt, k, vt, kc, vtc, mod, attn_sink, wsg, bsg, wpa, wpb, wpc, wout,
                      wg, l, tile=lat_tile, local=True, ctx_row=None)
        if not last:
            y_c = _fourier_ctx_call(fa_c.reshape(b, n_ctx, FOURIER_WIDTH), wc, cs_dft)
            xc = _mix_call(xc, main_c.reshape(b, n_ctx, MAIN_WIDTH), hb_c.reshape(b, n_ctx, d), y_c,
                           qt_c.reshape(b, n_ctx // BLOCK, ATTN_WIDTH, BLOCK), None, None, kc, vtc,
                           mod, attn_sink, wsg, bsg, wpa, wpb, wpc, wout, wg, l,
                           tile=n_ctx, local=False, ctx_row=ctx_row)
    return x
```

```python
import functools
import math

import jax
import jax.numpy as jnp
from jax import lax
from jax.experimental import pallas as pl
from jax.experimental.pallas import tpu as pltpu

F32 = jnp.float32
BF16 = jnp.bfloat16

D_MODEL = 1024
DEPTH = 4
GRID_W = 64
FOURIER_WIDTH = 256
FOURIER_GROUPS = 4
SGU_WIDTH = 256
SGU_GROUPS = 4
CHUNK = 128
N_HEADS = 8
N_KV_HEADS = 2
HEAD_DIM = 64
ATTN_WIDTH = N_HEADS * HEAD_DIM
KV_WIDTH = N_KV_HEADS * HEAD_DIM
BLOCK = 128
ROPE_THETA = 10000.0
EPS = 1e-6
NEG_INF = -1e30
LOG2E = math.log2(math.e)

LANES = 128
MOD_ROWS = 8
ONES_ROWS = 16

W_FA = 0
W_ZA = W_FA + FOURIER_WIDTH
W_ZB = W_ZA + FOURIER_WIDTH + 2 * SGU_WIDTH
W_Q = W_ZB + SGU_WIDTH
W_K = W_Q + ATTN_WIDTH
W_V = W_K + KV_WIDTH
W_ZC = W_V + KV_WIDTH
W_G = W_ZC + ATTN_WIDTH
W_TOTAL = W_G + 3 * D_MODEL
MIX_GATES = 2
W_PROJ = W_TOTAL - MIX_GATES * D_MODEL
MAIN_ZA = 0
MAIN_U = MAIN_ZA + FOURIER_WIDTH
MAIN_VS = MAIN_U + SGU_WIDTH
MAIN_ZB = MAIN_VS + SGU_WIDTH
MAIN_ZC = MAIN_ZB + SGU_WIDTH
MAIN_G = MAIN_ZC + ATTN_WIDTH
MAIN_WIDTH = MAIN_G + (3 - MIX_GATES) * D_MODEL
PROJ_CHUNK = 512

DFT_L1 = 32
DFT_L2 = 256
DFT_PACK = 16
DFT2_GROUP = 8

VMEM_LIMIT = 56 * 1024 * 1024


def _silu(z):
    return 0.5 * z * (1.0 + jnp.tanh(0.5 * z))


def _silu_of_half(zh):
    return zh * (1.0 + jnp.tanh(zh))


def _dot(a, b):
    return jnp.dot(a, b, preferred_element_type=F32)


def _layer(l, rank):
    return lambda *_: (l,) + (0,) * (rank - 1)


def _ada_kernel(c_ref, w_ref, b_ref, o_ref):
    s = _silu(c_ref[...]).astype(BF16)
    o_ref[0] = _dot(s, w_ref[0].astype(BF16)) + b_ref[0]


def _ada_call(cs, w_ada, b_ada):
    d = D_MODEL
    return pl.pallas_call(
        _ada_kernel,
        grid=(DEPTH, 3),
        in_specs=[
            pl.BlockSpec((MOD_ROWS, d), lambda l, j: (0, 0)),
            pl.BlockSpec((1, d, d), lambda l, j: (l, 0, j)),
            pl.BlockSpec((1, 1, d), lambda l, j: (l, 0, j)),
        ],
        out_specs=pl.BlockSpec((1, MOD_ROWS, d), lambda l, j: (l, 0, j)),
        out_shape=jax.ShapeDtypeStruct((DEPTH, MOD_ROWS, 3 * d), F32),
        compiler_params=pltpu.CompilerParams(
            dimension_semantics=("arbitrary", "arbitrary"), vmem_limit_bytes=VMEM_LIMIT),
        name="ada_mod",
    )(cs, w_ada, b_ada.reshape(DEPTH, 1, 3 * d))


def _norm_rope(t, bd_ref, gain, cos, sin, out_scale):
    width = t.shape[-1]
    ssq = _dot((t * t).astype(BF16), bd_ref[...])
    tn = t * lax.rsqrt(ssq * (1.0 / HEAD_DIM) + EPS) * gain
    if cos is not None:
        reps = width // LANES
        cos_w = jnp.concatenate([cos] * reps, axis=1) if reps > 1 else cos
        sin_w = jnp.concatenate([sin] * reps, axis=1) if reps > 1 else sin
        lane = lax.broadcasted_iota(jnp.int32, tn.shape, 1)
        first = (lane % 32) < 16
        partner = jnp.where(first, pltpu.roll(tn, width - 16, 1), pltpu.roll(tn, 16, 1))
        tn = tn * cos_w + partner * sin_w
    if out_scale != 1.0:
        tn = tn * out_scale
    return tn


def _proj_kernel(x_ref, mod_ref, w_ref, bdq_ref, bdk_ref, gq_ref, gk_ref, cos_ref, sin_ref,
                 fa_ref, main_ref, hb_ref, qt_ref, k_ref, vt_ref, *, ctx_row, rope):
    d = D_MODEL
    nsub = x_ref.shape[1] // BLOCK
    xt = x_ref[0]
    ms = jnp.mean(xt * xt, axis=-1, keepdims=True)
    row = pl.program_id(0) if ctx_row is None else ctx_row
    modrow = mod_ref[0, pl.ds(row, 1), :]
    shift = modrow[:, 0:d]
    scale = modrow[:, d:2 * d]
    hb = ((xt * lax.rsqrt(ms + EPS)) * (1.0 + scale) + shift).astype(BF16)
    hb_ref[0] = hb
    cos = cos_ref[...] if rope else None
    sin = sin_ref[...] if rope else None

    def proj(c0, width):
        return _dot(hb, w_ref[0, :, c0:c0 + width])

    chunks = [(src + c0, dst + c0)
              for src, dst, width in ((W_ZA, MAIN_ZA, W_Q - W_ZA), (W_ZC, MAIN_ZC, W_PROJ - W_ZC))
              for c0 in range(0, width, PROJ_CHUNK)]

    def wide(count):
        for _ in range(min(count, len(chunks))):
            src, dst = chunks.pop(0)
            main_ref[0, :, dst:dst + PROJ_CHUNK] = proj(src, PROJ_CHUNK).astype(BF16)

    q_raw = proj(W_Q, ATTN_WIDTH)
    k_raw = proj(W_K, KV_WIDTH)
    v = proj(W_V, KV_WIDTH)
    wide(1)
    q = _norm_rope(q_raw, bdq_ref, gq_ref[0], cos, sin, HEAD_DIM ** -0.5 * LOG2E)
    wide(1)
    for s in range(nsub):
        qt_ref[0, s] = q[s * BLOCK:(s + 1) * BLOCK, :].T.astype(BF16)
    wide(1)
    k = _norm_rope(k_raw, bdk_ref, gk_ref[0], cos, sin, 1.0)
    k_ref[0] = k.astype(BF16)
    for s in range(nsub):
        vt_ref[0, s] = v[s * BLOCK:(s + 1) * BLOCK, :].T.astype(BF16)
    fa_ref[0] = proj(W_FA, FOURIER_WIDTH).astype(BF16)
    wide(len(chunks))


def _proj_call(x, mod, w_all, bdq, bdk, gq_all, gk_all, rope_tables, l, *, tile, ctx_row):
    b, length, d = x.shape
    nt = length // tile
    nsub = tile // BLOCK
    const = lambda bi, i: (0, 0)
    tok = lambda bi, i: (bi, i, 0)
    blk = lambda bi, i: (bi, i, 0, 0)
    rope = rope_tables is not None
    if rope:
        cos, sin = rope_tables
        pos, pos_rows = (lambda bi, i: (i, 0)), tile
    else:
        cos = sin = jnp.zeros((MOD_ROWS, LANES), F32)
        pos, pos_rows = const, MOD_ROWS
    kern = functools.partial(_proj_kernel, ctx_row=ctx_row, rope=rope)
    return pl.pallas_call(
        kern,
        grid=(b, nt),
        in_specs=[
            pl.BlockSpec((1, tile, d), tok),
            pl.BlockSpec((1, MOD_ROWS, 3 * d), _layer(l, 3)),
            pl.BlockSpec((1, d, W_PROJ), _layer(l, 3), pipeline_mode=pl.Buffered(1)),
            pl.BlockSpec((ATTN_WIDTH, ATTN_WIDTH), const),
            pl.BlockSpec((KV_WIDTH, KV_WIDTH), const),
            pl.BlockSpec((1, 1, ATTN_WIDTH), _layer(l, 3)),
            pl.BlockSpec((1, 1, KV_WIDTH), _layer(l, 3)),
            pl.BlockSpec((pos_rows, LANES), pos),
            pl.BlockSpec((pos_rows, LANES), pos),
        ],
        out_specs=[
            pl.BlockSpec((1, tile, FOURIER_WIDTH), tok),
            pl.BlockSpec((1, tile, MAIN_WIDTH), tok),
            pl.BlockSpec((1, tile, d), tok),
            pl.BlockSpec((1, nsub, ATTN_WIDTH, BLOCK), blk),
            pl.BlockSpec((1, tile, KV_WIDTH), tok),
            pl.BlockSpec((1, nsub, KV_WIDTH, BLOCK), blk),
        ],
        out_shape=[
            jax.ShapeDtypeStruct((b, length, FOURIER_WIDTH), BF16),
            jax.ShapeDtypeStruct((b, length, MAIN_WIDTH), BF16),
            jax.ShapeDtypeStruct((b, length, d), BF16),
            jax.ShapeDtypeStruct((b, length // BLOCK, ATTN_WIDTH, BLOCK), BF16),
            jax.ShapeDtypeStruct((b, length, KV_WIDTH), BF16),
            jax.ShapeDtypeStruct((b, length // BLOCK, KV_WIDTH, BLOCK), BF16),
        ],
        compiler_params=pltpu.CompilerParams(
            dimension_semantics=("arbitrary", "arbitrary"), vmem_limit_bytes=VMEM_LIMIT),
        name="proj_ctx" if ctx_row is not None else "proj_lat",
    )(x, mod, w_all, bdq, bdk, gq_all, gk_all, cos, sin)


def _fourier_kernel(x_ref, w1_ref, w2_ref, cs_ref, y_ref, a_s):
    l1n, l2n, pack, cw = DFT_L1, DFT_L2, DFT_PACK, FOURIER_WIDTH
    for g in range(l2n // pack):
        xg = x_ref[0, :, g].reshape(l1n * pack, cw)
        ag = _dot(w1_ref[g], xg).astype(BF16)
        a_s[:, :, g * pack:(g + 1) * pack, :] = ag.reshape(2, l1n, pack, cw)
    grp = DFT2_GROUP
    for g in range(l1n // grp):
        rhs = jnp.concatenate(
            [jnp.concatenate([a_s[0, g * grp + j], a_s[1, g * grp + j]], axis=0)
             for j in range(grp)], axis=1)
        f = _dot(w2_ref[...], rhs)
        for j in range(grp):
            cols = slice(j * cw, (j + 1) * cw)
            fc = jnp.concatenate([f[:l2n, cols], f[l2n:, cols]], axis=1).astype(BF16)
            out_cols = slice((g * grp + j) * cw, (g * grp + j + 1) * cw)
            y_ref[0, :, out_cols] = _dot(fc, cs_ref[...]).astype(BF16)


def _fourier_call(fa, w1, w2, cs):
    b, length, cw = fa.shape
    l1n, l2n, pack = DFT_L1, DFT_L2, DFT_PACK
    const = lambda bi: (0, 0)
    return pl.pallas_call(
        _fourier_kernel,
        grid=(b,),
        in_specs=[
            pl.BlockSpec((1, l1n, l2n // pack, pack, cw), lambda bi: (bi, 0, 0, 0, 0)),
            pl.BlockSpec(w1.shape, lambda bi: (0, 0, 0), pipeline_mode=pl.Buffered(1)),
            pl.BlockSpec(w2.shape, const),
            pl.BlockSpec(cs.shape, const),
        ],
        out_specs=pl.BlockSpec((1, l2n, l1n * cw), lambda bi: (bi, 0, 0)),
        out_shape=jax.ShapeDtypeStruct((b, l2n, l1n * cw), BF16),
        scratch_shapes=[pltpu.VMEM((2, l1n, l2n, cw), BF16)],
        compiler_params=pltpu.CompilerParams(
            dimension_semantics=("arbitrary",), vmem_limit_bytes=VMEM_LIMIT),
        name="fourier_lat",
    )(fa.reshape(b, l1n, l2n // pack, pack, cw), w1, w2, cs)


def _fourier_ctx_kernel(fa_ref, wc_ref, cs_ref, y_ref):
    n = fa_ref.shape[1]
    f = _dot(wc_ref[...], fa_ref[0])
    fc = jnp.concatenate([f[:n], f[n:]], axis=1).astype(BF16)
    y_ref[0] = _dot(fc, cs_ref[...])


def _fourier_ctx_call(fa, wc, cs):
    b, length, width = fa.shape
    return pl.pallas_call(
        _fourier_ctx_kernel,
        grid=(b,),
        in_specs=[
            pl.BlockSpec((1, length, width), lambda bi: (bi, 0, 0)),
            pl.BlockSpec(wc.shape, lambda bi: (0, 0)),
            pl.BlockSpec(cs.shape, lambda bi: (0, 0)),
        ],
        out_specs=pl.BlockSpec((1, length, width), lambda bi: (bi, 0, 0)),
        out_shape=jax.ShapeDtypeStruct((b, length, width), F32),
        compiler_params=pltpu.CompilerParams(
            dimension_semantics=("arbitrary",), vmem_limit_bytes=VMEM_LIMIT),
        name="fourier_ctx",
    )(fa, wc, cs)


def _mix_kernel(*refs, local, tile, n_blocks, ctx_row, layer):
    if local:
        (x_ref, main_ref, hb_ref, y_ref, perm_ref, qt_ref, kp_ref, km_ref, kn_ref,
         vtp_ref, vtm_ref, vtn_ref, kc_ref, vtc_ref, mod_ref, sink_ref, wsg_ref, bsg_ref,
         wpa_ref, wpb_ref, wpc_ref, wout_ref, wg_ref, o_ref,
         kbuf, vtbuf, ya_s, yb_s, yc_s, ma_s, g_s, mg_s) = refs
    else:
        (x_ref, main_ref, hb_ref, y_ref, qt_ref, kc_ref, vtc_ref, mod_ref, sink_ref, wsg_ref, bsg_ref,
         wpa_ref, wpb_ref, wpc_ref, wout_ref, wg_ref, o_ref,
         ya_s, yb_s, yc_s, ma_s, g_s, mg_s) = refs
    d = D_MODEL
    nsub = tile // BLOCK
    tile_idx = pl.program_id(1)
    heads_per_kv = N_HEADS // N_KV_HEADS
    cols_all = N_HEADS * BLOCK

    if local:
        kbuf[0:BLOCK, :] = kp_ref[0]
        kbuf[BLOCK:BLOCK + tile, :] = km_ref[0]
        kbuf[BLOCK + tile:2 * BLOCK + tile, :] = kn_ref[0]
        vtbuf[0] = vtp_ref[0, 0]
        for s in range(nsub):
            vtbuf[1 + s] = vtm_ref[0, s]
        vtbuf[1 + nsub] = vtn_ref[0, 0]

    def with_ones(vt):
        return jnp.concatenate([vt, jnp.ones((ONES_ROWS, vt.shape[1]), BF16)], axis=0)

    vt_ctx = with_ones(jnp.concatenate([vtc_ref[0, s] for s in range(vtc_ref.shape[1])], axis=1))

    if local:
        cw = FOURIER_WIDTH
        stacked = jnp.concatenate(
            [y_ref[0, :, k1 * cw:(k1 + 1) * cw] for k1 in range(DFT_L1)], axis=0)
        y = _dot(perm_ref[...], stacked)
    else:
        y = y_ref[0]
    ya_s[...] = (y * _silu_of_half(main_ref[0, :, MAIN_ZA:MAIN_ZA + FOURIER_WIDTH].astype(F32))).astype(BF16)

    sgu_grp = lax.broadcasted_iota(jnp.int32, (CHUNK, SGU_WIDTH), 1) // (SGU_WIDTH // SGU_GROUPS)
    zero_head = jnp.zeros((HEAD_DIM, BLOCK), BF16)
    sink_row = jnp.concatenate(
        [jnp.full((1, BLOCK), sink_ref[layer, hh] * LOG2E, F32) for hh in range(N_HEADS)], axis=1)
    kq_diff = (lax.broadcasted_iota(jnp.int32, (BLOCK, cols_all), 0)
               - lax.broadcasted_iota(jnp.int32, (BLOCK, cols_all), 1) % BLOCK)

    def sgu(n):
        rows = slice(n * BLOCK, (n + 1) * BLOCK)
        vs = main_ref[0, rows, MAIN_VS:MAIN_VS + SGU_WIDTH].astype(F32)
        vn = (vs * lax.rsqrt(jnp.mean(vs * vs, axis=-1, keepdims=True) + EPS)).astype(BF16)
        rhs = jnp.concatenate(
            [jnp.where(sgu_grp == g, vn, jnp.zeros_like(vn)) for g in range(SGU_GROUPS)], axis=0)
        mixed = _dot(wsg_ref[0], rhs) + bsg_ref[0]
        u = main_ref[0, rows, MAIN_U:MAIN_U + SGU_WIDTH].astype(F32)
        zb = main_ref[0, rows, MAIN_ZB:MAIN_ZB + SGU_WIDTH].astype(F32)
        yb_s[rows, :] = (u * mixed * _silu_of_half(zb)).astype(BF16)

    def scores(n):
        qt = qt_ref[0, n]
        cols = []
        for hh in range(N_HEADS):
            qh = qt[hh * HEAD_DIM:(hh + 1) * HEAD_DIM, :]
            cols.append(jnp.concatenate(
                [qh, zero_head] if hh < heads_per_kv else [zero_head, qh], axis=0))
        qst = jnp.concatenate(cols, axis=1)

        s_ctx = _dot(kc_ref[0], qst)
        s_loc = _dot(kbuf[n * BLOCK:(n + 3) * BLOCK, :], qst) if local else None
        return s_ctx, s_loc

    def attend(n, j, s_ctx, s_loc, vt_loc):
        rows = slice(n * BLOCK, (n + 1) * BLOCK)
        lanes = slice(2 * j * BLOCK, (2 * j + 2) * BLOCK)
        sink_j = sink_row[:, lanes]
        s_ctx = s_ctx[:, lanes]
        m = jnp.maximum(jnp.max(s_ctx, axis=0, keepdims=True), sink_j)
        if local:
            gblk = tile_idx * nsub + n
            diff = kq_diff[:, lanes]
            s_prev = jnp.where(diff >= jnp.where(gblk > 0, 0, BLOCK), s_loc[0:BLOCK, lanes], NEG_INF)
            s_own = s_loc[BLOCK:2 * BLOCK, lanes]
            s_next = jnp.where(diff <= jnp.where(gblk < n_blocks - 1, 0, -BLOCK),
                               s_loc[2 * BLOCK:3 * BLOCK, lanes], NEG_INF)
            m_loc = jnp.max(jnp.maximum(jnp.maximum(s_prev, s_own), s_next), axis=0, keepdims=True)
            m = jnp.maximum(m, m_loc)
        e_ctx = jnp.exp2(s_ctx - m)
        ot = _dot(vt_ctx, e_ctx.astype(BF16))
        if local:
            e_loc = jnp.concatenate(
                [jnp.exp2(s_prev - m), jnp.exp2(s_own - m), jnp.exp2(s_next - m)], axis=0)
            ot = ot + _dot(vt_loc, e_loc.astype(BF16))
        den = ot[KV_WIDTH:KV_WIDTH + 1] + jnp.exp2(sink_j - m)
        h = (2 * j) // heads_per_kv
        ot = ot[h * HEAD_DIM:(h + 1) * HEAD_DIM] * (1.0 / den)
        pair = jnp.concatenate([ot[:, 0:BLOCK], ot[:, BLOCK:2 * BLOCK]], axis=0)
        zc = main_ref[0, rows, MAIN_ZC + j * LANES:MAIN_ZC + (j + 1) * LANES].astype(F32)
        yc_s[rows, j * LANES:(j + 1) * LANES] = (pair.T * _silu_of_half(zc)).astype(BF16)

    ahead = scores(0)
    for n in range(nsub):
        sgu(n)
        current, ahead = ahead, (scores(n + 1) if n + 1 < nsub else None)
        cols = slice(n * (d // nsub), (n + 1) * (d // nsub))
        ma_s[:, cols] = _dot(ya_s[...], wpa_ref[0, :, cols])
        vt_loc = None
        if local:
            vt_loc = with_ones(jnp.concatenate([vtbuf[n], vtbuf[n + 1], vtbuf[n + 2]], axis=1))
        for j in range(ATTN_WIDTH // LANES):
            if 0 < j <= MIX_GATES:
                c0 = (j - 1) * d + n * (d // nsub)
                g_s[j - 1, :, cols] = _dot(hb_ref[0], wg_ref[0, :, c0:c0 + d // nsub])
            attend(n, j, *current, vt_loc)

    row = pl.program_id(0) if ctx_row is None else ctx_row
    gate = mod_ref[0, pl.ds(row, 1), 2 * d:3 * d]
    width = 2 * LANES
    for c0 in range(0, d, width):
        cols = slice(c0, c0 + width)
        ma = ma_s[:, cols]
        mb = _dot(yb_s[...], wpb_ref[0, :, cols])
        mc = _dot(yc_s[...], wpc_ref[0, :, cols])
        ta, tb = (jnp.tanh(g_s[i, :, cols]) for i in range(MIX_GATES))
        tc = jnp.tanh(main_ref[0, :, MAIN_G + c0:MAIN_G + c0 + width].astype(F32))
        mg_s[:, cols] = ((ma + ta * ma) + (mb + tb * mb) + (mc + tc * mc)).astype(BF16)
    for c0 in range(0, d, width):
        cols = slice(c0, c0 + width)
        o_ref[0, :, cols] = x_ref[0, :, cols] + gate[:, cols] * _dot(mg_s[...], wout_ref[0, :, cols])


def _mix_call(x, main, hb, y, qt, k, vt, kc, vtc, mod, sinks, wsg, bsg, wpa, wpb, wpc, wout, wg, l,
              *, tile, local, ctx_row):
    b, length, d = x.shape
    nt = length // tile
    n_ctx = kc.shape[1]
    sub = tile // BLOCK
    n_blocks = length // BLOCK
    tok = lambda bi, i: (bi, i, 0)
    blk = lambda bi, i: (bi, i, 0, 0)
    per_b = lambda bi, i: (bi, 0, 0)
    per_b4 = lambda bi, i: (bi, 0, 0, 0)

    in_specs = [
        pl.BlockSpec((1, tile, d), tok),
        pl.BlockSpec((1, tile, MAIN_WIDTH), tok),
        pl.BlockSpec((1, tile, d), tok),
    ]
    args = [x, main, hb, y]
    if local:
        rows_k2 = tile // DFT_L1
        r = jnp.arange(tile)
        perm = (r[:, None] == (r[None, :] % rows_k2) * DFT_L1 + r[None, :] // rows_k2).astype(BF16)
        in_specs += [pl.BlockSpec((1, rows_k2, DFT_L1 * FOURIER_WIDTH), tok),
                     pl.BlockSpec((tile, tile), lambda bi, i: (0, 0))]
        args.append(perm)
    else:
        in_specs.append(pl.BlockSpec((1, tile, FOURIER_WIDTH), tok))
    in_specs.append(pl.BlockSpec((1, sub, ATTN_WIDTH, BLOCK), blk))
    args.append(qt)
    if local:
        prev_blk = lambda bi, i: jnp.maximum(i * sub - 1, 0)
        next_blk = lambda bi, i: jnp.minimum((i + 1) * sub, n_blocks - 1)
        in_specs += [
            pl.BlockSpec((1, BLOCK, KV_WIDTH), lambda bi, i: (bi, prev_blk(bi, i), 0)),
            pl.BlockSpec((1, tile, KV_WIDTH), tok),
            pl.BlockSpec((1, BLOCK, KV_WIDTH), lambda bi, i: (bi, next_blk(bi, i), 0)),
            pl.BlockSpec((1, 1, KV_WIDTH, BLOCK), lambda bi, i: (bi, prev_blk(bi, i), 0, 0)),
            pl.BlockSpec((1, sub, KV_WIDTH, BLOCK), blk),
            pl.BlockSpec((1, 1, KV_WIDTH, BLOCK), lambda bi, i: (bi, next_blk(bi, i), 0, 0)),
        ]
        args += [k, k, k, vt, vt, vt]
    in_specs += [
        pl.BlockSpec((1, n_ctx, KV_WIDTH), per_b),
        pl.BlockSpec((1, n_ctx // BLOCK, KV_WIDTH, BLOCK), per_b4),
        pl.BlockSpec((1, MOD_ROWS, 3 * d), _layer(l, 3)),
        pl.BlockSpec(memory_space=pltpu.SMEM),
    ]
    args += [kc, vtc, mod, sinks]
    for w in (wsg, bsg, wpa, wpb, wpc, wout):
        in_specs.append(pl.BlockSpec((1,) + w.shape[1:], _layer(l, 3)))
        args.append(w)
    in_specs.append(pl.BlockSpec((1,) + wg.shape[1:], _layer(l, 3), pipeline_mode=pl.Buffered(1)))
    args.append(wg)

    scratch = []
    if local:
        scratch += [pltpu.VMEM((tile + 2 * BLOCK, KV_WIDTH), BF16),
                    pltpu.VMEM((sub + 2, KV_WIDTH, BLOCK), BF16)]
    scratch += [
        pltpu.VMEM((tile, FOURIER_WIDTH), BF16),
        pltpu.VMEM((tile, SGU_WIDTH), BF16),
        pltpu.VMEM((tile, ATTN_WIDTH), BF16),
        pltpu.VMEM((tile, d), F32),
        pltpu.VMEM((MIX_GATES, tile, d), F32),
        pltpu.VMEM((tile, d), BF16),
    ]
    kern = functools.partial(_mix_kernel, local=local, tile=tile, n_blocks=n_blocks,
                             ctx_row=ctx_row, layer=l)
    return pl.pallas_call(
        kern,
        grid=(b, nt),
        in_specs=in_specs,
        out_specs=pl.BlockSpec((1, tile, d), tok),
        out_shape=jax.ShapeDtypeStruct((b, length, d), F32),
        scratch_shapes=scratch,
        compiler_params=pltpu.CompilerParams(
            dimension_semantics=("arbitrary", "arbitrary"), vmem_limit_bytes=VMEM_LIMIT),
        name="mix_lat" if local else "mix_ctx",
    )(*args)


def _rope_tables(length):
    nf = HEAD_DIM // 4
    inv = jnp.tile(ROPE_THETA ** (-jnp.arange(nf, dtype=F32) / nf), LANES // nf)[None, :]
    lane = jnp.arange(LANES)[None, :]
    on_row = (lane // (2 * nf)) % 2 == 0
    sign = jnp.where((lane // nf) % 2 == 0, -1.0, 1.0)
    ang_r = jnp.arange(length // GRID_W, dtype=F32)[:, None] * inv
    ang_c = jnp.arange(GRID_W, dtype=F32)[:, None] * inv

    def table(fn, scale):
        by_row = jnp.where(on_row, fn(ang_r) * scale, 0.0)[:, None, :]
        by_col = jnp.where(on_row, 0.0, fn(ang_c) * scale)[None, :, :]
        return (by_row + by_col).reshape(length, LANES)

    return table(jnp.cos, 1.0), table(jnp.sin, sign)


def _angle(num, den):
    return (num % den).astype(F32) * (2.0 * math.pi / den)


def _channel_dft(width):
    gw = FOURIER_WIDTH // FOURIER_GROUPS
    m = jnp.arange(width)
    same = (m[:, None] // gw) == (m[None, :] // gw)
    th = _angle((m[:, None] % gw) * (m[None, :] % gw), gw)
    cc = jnp.where(same, jnp.cos(th), 0.0) * gw ** -0.5
    sc = jnp.where(same, jnp.sin(th), 0.0) * gw ** -0.5
    return jnp.concatenate([cc, sc], axis=0).astype(BF16)


def _fourier_tables():
    l1n, l2n = DFT_L1, DFT_L2
    length = l1n * l2n
    pack = DFT_PACK
    l2 = jnp.arange(l2n)[:, None, None]
    k1 = jnp.arange(l1n)[None, :, None]
    l1 = jnp.arange(l1n)[None, None, :]
    th = _angle(k1 * l1 * l2n + k1 * l2, length)
    w1 = jnp.stack([jnp.cos(th), -jnp.sin(th)], axis=1) * l1n ** -0.5
    rows, cols = 2 * l1n * pack, l1n * pack
    ws = jnp.transpose(w1.reshape(l2n // pack, pack, 2 * l1n, l1n), (0, 2, 1, 3))
    ws = ws.reshape(l2n // pack, rows, l1n).astype(BF16)
    spread = (jnp.arange(cols)[None, :] // pack == jnp.arange(l1n)[:, None]).astype(BF16)
    same_p = (jnp.arange(rows)[:, None] % pack) == (jnp.arange(cols)[None, :] % pack)
    w1 = jnp.einsum("grb,bc->grc", ws, spread, preferred_element_type=F32)
    w1 = jnp.where(same_p[None], w1, 0.0).astype(BF16)
    k2 = jnp.arange(l2n)
    th2 = _angle(k2[:, None] * k2[None, :], l2n)
    c2, s2 = jnp.cos(th2), jnp.sin(th2)
    w2 = (jnp.concatenate([jnp.concatenate([c2, s2], axis=1),
                           jnp.concatenate([-s2, c2], axis=1)], axis=0) * l2n ** -0.5).astype(BF16)
    return w1, w2


def _fourier_ctx_table(n):
    k = jnp.arange(n)
    th = _angle(k[:, None] * k[None, :], n)
    return (jnp.concatenate([jnp.cos(th), -jnp.sin(th)], axis=0) * n ** -0.5).astype(BF16)


def _block_diag_ones(width):
    m = jnp.arange(width) // HEAD_DIM
    return (m[:, None] == m[None, :]).astype(BF16)


def _prep_w_in(w_in):
    col = jnp.arange(W_PROJ)
    halved = ((col >= W_ZA) & (col < W_ZA + FOURIER_WIDTH)) | ((col >= W_ZB) & (col < W_Q)) | (col >= W_ZC)
    mix_end = W_G + MIX_GATES * D_MODEL
    w_proj = jnp.concatenate([w_in[:, :, :W_G], w_in[:, :, mix_end:]], axis=2)
    w_proj = (w_proj * jnp.where(halved, 0.5, 1.0)).astype(BF16)
    return w_proj, (0.5 * w_in[:, :, W_G:mix_end]).astype(BF16)


def kernel(x, c, ctx, c_ctx, w_ada, b_ada, w_in, sgu_w, sgu_b, q_norm_g, k_norm_g,
           attn_sink, w_pa, w_pb, w_pc, w_out):
    b, length, d = x.shape
    n_ctx = ctx.shape[1]
    assert length == DFT_L1 * DFT_L2 and d == D_MODEL and b + 1 <= MOD_ROWS
    assert w_in.shape == (DEPTH, d, W_TOTAL)
    lat_tile = 512
    ctx_row = b

    cs = jnp.zeros((MOD_ROWS, d), F32).at[:b].set(c).at[b].set(c_ctx)
    mod = _ada_call(cs, w_ada, b_ada)

    rope_tables = _rope_tables(length)
    w1, w2 = _fourier_tables()
    cs_dft = _channel_dft(FOURIER_WIDTH)
    wc = _fourier_ctx_table(n_ctx)
    bdq = _block_diag_ones(ATTN_WIDTH)
    bdk = _block_diag_ones(KV_WIDTH)

    w_all, wg = _prep_w_in(w_in)
    gq_all = jnp.tile(q_norm_g, (1, N_HEADS))[:, None, :]
    gk_all = jnp.tile(k_norm_g, (1, N_KV_HEADS))[:, None, :]
    wsg = jnp.transpose(sgu_w, (0, 2, 1, 3)).reshape(DEPTH, CHUNK, SGU_GROUPS * CHUNK).astype(BF16)
    bsg = jnp.repeat(jnp.swapaxes(sgu_b, 1, 2), SGU_WIDTH // SGU_GROUPS, axis=2)
    wpa, wpb, wpc = w_pa.astype(BF16), w_pb.astype(BF16), w_pc.astype(BF16)
    wout = (0.5 * w_out).astype(BF16)

    xc = ctx
    for l in range(DEPTH):
        last = l == DEPTH - 1
        fa_c, main_c, hb_c, qt_c, kc, vtc = _proj_call(
            xc.reshape(1, b * n_ctx, d), mod, w_all, bdq, bdk, gq_all, gk_all, None, l,
            tile=2 * n_ctx, ctx_row=ctx_row)
        kc = kc.reshape(b, n_ctx, KV_WIDTH)
        vtc = vtc.reshape(b, n_ctx // BLOCK, KV_WIDTH, BLOCK)
        fa, main, hb, qt, k, vt = _proj_call(
            x, mod, w_all, bdq, bdk, gq_all, gk_all, rope_tables, l, tile=lat_tile, ctx_row=None)
        y = _fourier_call(fa, w1, w2, cs_dft)
        x = _mix_call(x, main, hb, y, qt, k, vt, kc, vtc, mod, attn_sink, wsg, bsg, wpa, wpb, wpc, wout,
                      wg, l, tile=lat_tile, local=True, ctx_row=None)
        if not last:
            y_c = _fourier_ctx_call(fa_c.reshape(b, n_ctx, FOURIER_WIDTH), wc, cs_dft)
            xc = _mix_call(xc, main_c.reshape(b, n_ctx, MAIN_WIDTH), hb_c.reshape(b, n_ctx, d), y_c,
                           qt_c.reshape(b, n_ctx // BLOCK, ATTN_WIDTH, BLOCK), None, None, kc, vtc,
                           mod, attn_sink, wsg, bsg, wpa, wpb, wpc, wout, wg, l,
                           tile=n_ctx, local=False, ctx_row=ctx_row)
    return x
```

```python
import functools
import math

import jax
import jax.numpy as jnp
from jax import lax
from jax.experimental import pallas as pl
from jax.experimental.pallas import tpu as pltpu

F32 = jnp.float32
BF16 = jnp.bfloat16

D_MODEL = 1024
DEPTH = 4
GRID_W = 64
FOURIER_WIDTH = 256
FOURIER_GROUPS = 4
SGU_WIDTH = 256
SGU_GROUPS = 4
CHUNK = 128
N_HEADS = 8
N_KV_HEADS = 2
HEAD_DIM = 64
ATTN_WIDTH = N_HEADS * HEAD_DIM
KV_WIDTH = N_KV_HEADS * HEAD_DIM
BLOCK = 128
ROPE_THETA = 10000.0
EPS = 1e-6
NEG_INF = -1e30
LOG2E = math.log2(math.e)

LANES = 128
MOD_ROWS = 8
ONES_ROWS = 16

W_FA = 0
W_ZA = W_FA + FOURIER_WIDTH
W_ZB = W_ZA + FOURIER_WIDTH + 2 * SGU_WIDTH
W_Q = W_ZB + SGU_WIDTH
W_K = W_Q + ATTN_WIDTH
W_V = W_K + KV_WIDTH
W_ZC = W_V + KV_WIDTH
W_G = W_ZC + ATTN_WIDTH
W_TOTAL = W_G + 3 * D_MODEL
MIX_GATES = 2
W_MIX_END = W_G + MIX_GATES * D_MODEL
MAIN_ZA = 0
MAIN_U = MAIN_ZA + FOURIER_WIDTH
MAIN_VS = MAIN_U + SGU_WIDTH
MAIN_ZB = MAIN_VS + SGU_WIDTH
MAIN_ZC = MAIN_ZB + SGU_WIDTH
MAIN_G = MAIN_ZC + ATTN_WIDTH
MAIN_WIDTH = MAIN_G + (3 - MIX_GATES) * D_MODEL
PROJ_CHUNK = 512

DFT_L1 = 32
DFT_L2 = 256
DFT_PACK = 16
DFT2_GROUP = 8

VMEM_LIMIT = 56 * 1024 * 1024


def _silu(z):
    return 0.5 * z * (1.0 + jnp.tanh(0.5 * z))


def _silu_of_half(zh):
    return zh * (1.0 + jnp.tanh(zh))


def _dot(a, b):
    return jnp.dot(a, b, preferred_element_type=F32)


def _layer(l, rank):
    return lambda *_: (l,) + (0,) * (rank - 1)


def _ada_kernel(c_ref, w_ref, b_ref, o_ref):
    s = _silu(c_ref[...]).astype(BF16)
    o_ref[0] = _dot(s, w_ref[0].astype(BF16)) + b_ref[0]


def _ada_call(cs, w_ada, b_ada):
    d = D_MODEL
    return pl.pallas_call(
        _ada_kernel,
        grid=(DEPTH, 3),
        in_specs=[
            pl.BlockSpec((MOD_ROWS, d), lambda l, j: (0, 0)),
            pl.BlockSpec((1, d, d), lambda l, j: (l, 0, j)),
            pl.BlockSpec((1, 1, d), lambda l, j: (l, 0, j)),
        ],
        out_specs=pl.BlockSpec((1, MOD_ROWS, d), lambda l, j: (l, 0, j)),
        out_shape=jax.ShapeDtypeStruct((DEPTH, MOD_ROWS, 3 * d), F32),
        compiler_params=pltpu.CompilerParams(
            dimension_semantics=("arbitrary", "arbitrary"), vmem_limit_bytes=VMEM_LIMIT),
        name="ada_mod",
    )(cs, w_ada, b_ada.reshape(DEPTH, 1, 3 * d))


def _norm_rope(t, bd_ref, gain, cos, sin, out_scale):
    width = t.shape[-1]
    ssq = _dot((t * t).astype(BF16), bd_ref[...])
    tn = t * lax.rsqrt(ssq * (1.0 / HEAD_DIM) + EPS) * gain
    if cos is not None:
        reps = width // LANES
        cos_w = jnp.concatenate([cos] * reps, axis=1) if reps > 1 else cos
        sin_w = jnp.concatenate([sin] * reps, axis=1) if reps > 1 else sin
        lane = lax.broadcasted_iota(jnp.int32, tn.shape, 1)
        first = (lane % 32) < 16
        partner = jnp.where(first, pltpu.roll(tn, width - 16, 1), pltpu.roll(tn, 16, 1))
        tn = tn * cos_w + partner * sin_w
    if out_scale != 1.0:
        tn = tn * out_scale
    return tn


def _proj_kernel(x_ref, mod_ref, w_ref, bdq_ref, bdk_ref, gq_ref, gk_ref, cos_ref, sin_ref,
                 fa_ref, main_ref, hb_ref, qt_ref, k_ref, vt_ref, *, ctx_row, rope):
    d = D_MODEL
    nsub = x_ref.shape[1] // BLOCK
    xt = x_ref[0]
    ms = jnp.mean(xt * xt, axis=-1, keepdims=True)
    row = pl.program_id(0) if ctx_row is None else ctx_row
    modrow = mod_ref[0, pl.ds(row, 1), :]
    shift = modrow[:, 0:d]
    scale = modrow[:, d:2 * d]
    hb = ((xt * lax.rsqrt(ms + EPS)) * (1.0 + scale) + shift).astype(BF16)
    hb_ref[0] = hb
    cos = cos_ref[...] if rope else None
    sin = sin_ref[...] if rope else None

    def proj(c0, width):
        return _dot(hb, w_ref[0, :, c0:c0 + width])

    chunks = [(src + c0, dst + c0)
              for src, dst, width in ((W_ZA, MAIN_ZA, W_Q - W_ZA), (W_ZC, MAIN_ZC, W_G - W_ZC),
                                      (W_MIX_END, MAIN_G, W_TOTAL - W_MIX_END))
              for c0 in range(0, width, PROJ_CHUNK)]

    def wide(count):
        for _ in range(min(count, len(chunks))):
            src, dst = chunks.pop(0)
            main_ref[0, :, dst:dst + PROJ_CHUNK] = proj(src, PROJ_CHUNK).astype(BF16)

    q_raw = proj(W_Q, ATTN_WIDTH)
    k_raw = proj(W_K, KV_WIDTH)
    v = proj(W_V, KV_WIDTH)
    wide(1)
    q = _norm_rope(q_raw, bdq_ref, gq_ref[0], cos, sin, HEAD_DIM ** -0.5 * LOG2E)
    wide(1)
    for s in range(nsub):
        qt_ref[0, s] = q[s * BLOCK:(s + 1) * BLOCK, :].T.astype(BF16)
    wide(1)
    k = _norm_rope(k_raw, bdk_ref, gk_ref[0], cos, sin, 1.0)
    k_ref[0] = k.astype(BF16)
    for s in range(nsub):
        vt_ref[0, s] = v[s * BLOCK:(s + 1) * BLOCK, :].T.astype(BF16)
    fa_ref[0] = proj(W_FA, FOURIER_WIDTH).astype(BF16)
    wide(len(chunks))


def _proj_call(x, mod, w_all, bdq, bdk, gq_all, gk_all, rope_tables, l, *, tile, ctx_row):
    b, length, d = x.shape
    nt = length // tile
    nsub = tile // BLOCK
    const = lambda bi, i: (0, 0)
    tok = lambda bi, i: (bi, i, 0)
    blk = lambda bi, i: (bi, i, 0, 0)
    rope = rope_tables is not None
    if rope:
        cos, sin = rope_tables
        pos, pos_rows = (lambda bi, i: (i, 0)), tile
    else:
        cos = sin = jnp.zeros((MOD_ROWS, LANES), F32)
        pos, pos_rows = const, MOD_ROWS
    kern = functools.partial(_proj_kernel, ctx_row=ctx_row, rope=rope)
    return pl.pallas_call(
        kern,
        grid=(b, nt),
        in_specs=[
            pl.BlockSpec((1, tile, d), tok),
            pl.BlockSpec((1, MOD_ROWS, 3 * d), _layer(l, 3)),
            pl.BlockSpec((1, d, W_TOTAL), _layer(l, 3), pipeline_mode=pl.Buffered(1)),
            pl.BlockSpec((ATTN_WIDTH, ATTN_WIDTH), const),
            pl.BlockSpec((KV_WIDTH, KV_WIDTH), const),
            pl.BlockSpec((1, 1, ATTN_WIDTH), _layer(l, 3)),
            pl.BlockSpec((1, 1, KV_WIDTH), _layer(l, 3)),
            pl.BlockSpec((pos_rows, LANES), pos),
            pl.BlockSpec((pos_rows, LANES), pos),
        ],
        out_specs=[
            pl.BlockSpec((1, tile, FOURIER_WIDTH), tok),
            pl.BlockSpec((1, tile, MAIN_WIDTH), tok),
            pl.BlockSpec((1, tile, d), tok),
            pl.BlockSpec((1, nsub, ATTN_WIDTH, BLOCK), blk),
            pl.BlockSpec((1, tile, KV_WIDTH), tok),
            pl.BlockSpec((1, nsub, KV_WIDTH, BLOCK), blk),
        ],
        out_shape=[
            jax.ShapeDtypeStruct((b, length, FOURIER_WIDTH), BF16),
            jax.ShapeDtypeStruct((b, length, MAIN_WIDTH), BF16),
            jax.ShapeDtypeStruct((b, length, d), BF16),
            jax.ShapeDtypeStruct((b, length // BLOCK, ATTN_WIDTH, BLOCK), BF16),
            jax.ShapeDtypeStruct((b, length, KV_WIDTH), BF16),
            jax.ShapeDtypeStruct((b, length // BLOCK, KV_WIDTH, BLOCK), BF16),
        ],
        compiler_params=pltpu.CompilerParams(
            dimension_semantics=("arbitrary", "arbitrary"), vmem_limit_bytes=VMEM_LIMIT),
        name="proj_ctx" if ctx_row is not None else "proj_lat",
    )(x, mod, w_all, bdq, bdk, gq_all, gk_all, cos, sin)


def _fourier_kernel(x_ref, w1_ref, w2_ref, cs_ref, y_ref, a_s):
    l1n, l2n, pack, cw = DFT_L1, DFT_L2, DFT_PACK, FOURIER_WIDTH
    for g in range(l2n // pack):
        xg = x_ref[0, :, g].reshape(l1n * pack, cw)
        ag = _dot(w1_ref[g], xg).astype(BF16)
        a_s[:, :, g * pack:(g + 1) * pack, :] = ag.reshape(2, l1n, pack, cw)
    grp = DFT2_GROUP
    for g in range(l1n // grp):
        rhs = jnp.concatenate(
            [jnp.concatenate([a_s[0, g * grp + j], a_s[1, g * grp + j]], axis=0)
             for j in range(grp)], axis=1)
        f = _dot(w2_ref[...], rhs)
        for j in range(grp):
            cols = slice(j * cw, (j + 1) * cw)
            fc = jnp.concatenate([f[:l2n, cols], f[l2n:, cols]], axis=1).astype(BF16)
            out_cols = slice((g * grp + j) * cw, (g * grp + j + 1) * cw)
            y_ref[0, :, out_cols] = _dot(fc, cs_ref[...]).astype(BF16)


def _fourier_call(fa, w1, w2, cs):
    b, length, cw = fa.shape
    l1n, l2n, pack = DFT_L1, DFT_L2, DFT_PACK
    const = lambda bi: (0, 0)
    return pl.pallas_call(
        _fourier_kernel,
        grid=(b,),
        in_specs=[
            pl.BlockSpec((1, l1n, l2n // pack, pack, cw), lambda bi: (bi, 0, 0, 0, 0)),
            pl.BlockSpec(w1.shape, lambda bi: (0, 0, 0), pipeline_mode=pl.Buffered(1)),
            pl.BlockSpec(w2.shape, const),
            pl.BlockSpec(cs.shape, const),
        ],
        out_specs=pl.BlockSpec((1, l2n, l1n * cw), lambda bi: (bi, 0, 0)),
        out_shape=jax.ShapeDtypeStruct((b, l2n, l1n * cw), BF16),
        scratch_shapes=[pltpu.VMEM((2, l1n, l2n, cw), BF16)],
        compiler_params=pltpu.CompilerParams(
            dimension_semantics=("arbitrary",), vmem_limit_bytes=VMEM_LIMIT),
        name="fourier_lat",
    )(fa.reshape(b, l1n, l2n // pack, pack, cw), w1, w2, cs)


def _fourier_ctx_kernel(fa_ref, wc_ref, cs_ref, y_ref):
    n = fa_ref.shape[1]
    f = _dot(wc_ref[...], fa_ref[0])
    fc = jnp.concatenate([f[:n], f[n:]], axis=1).astype(BF16)
    y_ref[0] = _dot(fc, cs_ref[...])


def _fourier_ctx_call(fa, wc, cs):
    b, length, width = fa.shape
    return pl.pallas_call(
        _fourier_ctx_kernel,
        grid=(b,),
        in_specs=[
            pl.BlockSpec((1, length, width), lambda bi: (bi, 0, 0)),
            pl.BlockSpec(wc.shape, lambda bi: (0, 0)),
            pl.BlockSpec(cs.shape, lambda bi: (0, 0)),
        ],
        out_specs=pl.BlockSpec((1, length, width), lambda bi: (bi, 0, 0)),
        out_shape=jax.ShapeDtypeStruct((b, length, width), F32),
        compiler_params=pltpu.CompilerParams(
            dimension_semantics=("arbitrary",), vmem_limit_bytes=VMEM_LIMIT),
        name="fourier_ctx",
    )(fa, wc, cs)


def _mix_kernel(*refs, local, tile, n_blocks, ctx_row, layer):
    if local:
        (x_ref, main_ref, hb_ref, y_ref, perm_ref, qt_ref, kp_ref, km_ref, kn_ref,
         vtp_ref, vtm_ref, vtn_ref, kc_ref, vtc_ref, mod_ref, sink_ref, wsg_ref, bsg_ref,
         wpa_ref, wpb_ref, wpc_ref, wout_ref, wg_ref, o_ref,
         kbuf, vtbuf, ya_s, yb_s, yc_s, ma_s, g_s, mg_s) = refs
    else:
        (x_ref, main_ref, hb_ref, y_ref, qt_ref, kc_ref, vtc_ref, mod_ref, sink_ref, wsg_ref, bsg_ref,
         wpa_ref, wpb_ref, wpc_ref, wout_ref, wg_ref, o_ref,
         ya_s, yb_s, yc_s, ma_s, g_s, mg_s) = refs
    d = D_MODEL
    nsub = tile // BLOCK
    tile_idx = pl.program_id(1)
    heads_per_kv = N_HEADS // N_KV_HEADS
    cols_all = N_HEADS * BLOCK

    if local:
        kbuf[0:BLOCK, :] = kp_ref[0]
        kbuf[BLOCK:BLOCK + tile, :] = km_ref[0]
        kbuf[BLOCK + tile:2 * BLOCK + tile, :] = kn_ref[0]
        vtbuf[0] = vtp_ref[0, 0]
        for s in range(nsub):
            vtbuf[1 + s] = vtm_ref[0, s]
        vtbuf[1 + nsub] = vtn_ref[0, 0]

    def with_ones(vt):
        return jnp.concatenate([vt, jnp.ones((ONES_ROWS, vt.shape[1]), BF16)], axis=0)

    vt_ctx = with_ones(jnp.concatenate([vtc_ref[0, s] for s in range(vtc_ref.shape[1])], axis=1))

    if local:
        cw = FOURIER_WIDTH
        stacked = jnp.concatenate(
            [y_ref[0, :, k1 * cw:(k1 + 1) * cw] for k1 in range(DFT_L1)], axis=0)
        y = _dot(perm_ref[...], stacked)
    else:
        y = y_ref[0]
    ya_s[...] = (y * _silu_of_half(main_ref[0, :, MAIN_ZA:MAIN_ZA + FOURIER_WIDTH].astype(F32))).astype(BF16)

    sgu_grp = lax.broadcasted_iota(jnp.int32, (CHUNK, SGU_WIDTH), 1) // (SGU_WIDTH // SGU_GROUPS)
    zero_head = jnp.zeros((HEAD_DIM, BLOCK), BF16)
    sink_row = jnp.concatenate(
        [jnp.full((1, BLOCK), sink_ref[layer, hh] * LOG2E, F32) for hh in range(N_HEADS)], axis=1)
    kq_diff = (lax.broadcasted_iota(jnp.int32, (BLOCK, cols_all), 0)
               - lax.broadcasted_iota(jnp.int32, (BLOCK, cols_all), 1) % BLOCK)

    def sgu(n):
        rows = slice(n * BLOCK, (n + 1) * BLOCK)
        vs = main_ref[0, rows, MAIN_VS:MAIN_VS + SGU_WIDTH].astype(F32)
        vn = (vs * lax.rsqrt(jnp.mean(vs * vs, axis=-1, keepdims=True) + EPS)).astype(BF16)
        rhs = jnp.concatenate(
            [jnp.where(sgu_grp == g, vn, jnp.zeros_like(vn)) for g in range(SGU_GROUPS)], axis=0)
        mixed = _dot(wsg_ref[0], rhs) + bsg_ref[0]
        u = main_ref[0, rows, MAIN_U:MAIN_U + SGU_WIDTH].astype(F32)
        zb = main_ref[0, rows, MAIN_ZB:MAIN_ZB + SGU_WIDTH].astype(F32)
        yb_s[rows, :] = (u * mixed * _silu_of_half(zb)).astype(BF16)

    def scores(n):
        qt = qt_ref[0, n]
        cols = []
        for hh in range(N_HEADS):
            qh = qt[hh * HEAD_DIM:(hh + 1) * HEAD_DIM, :]
            cols.append(jnp.concatenate(
                [qh, zero_head] if hh < heads_per_kv else [zero_head, qh], axis=0))
        qst = jnp.concatenate(cols, axis=1)

        s_ctx = _dot(kc_ref[0], qst)
        s_loc = _dot(kbuf[n * BLOCK:(n + 3) * BLOCK, :], qst) if local else None
        return s_ctx, s_loc

    def attend(n, j, s_ctx, s_loc, vt_loc):
        rows = slice(n * BLOCK, (n + 1) * BLOCK)
        lanes = slice(2 * j * BLOCK, (2 * j + 2) * BLOCK)
        sink_j = sink_row[:, lanes]
        s_ctx = s_ctx[:, lanes]
        m = jnp.maximum(jnp.max(s_ctx, axis=0, keepdims=True), sink_j)
        if local:
            gblk = tile_idx * nsub + n
            diff = kq_diff[:, lanes]
            s_prev = jnp.where(diff >= jnp.where(gblk > 0, 0, BLOCK), s_loc[0:BLOCK, lanes], NEG_INF)
            s_own = s_loc[BLOCK:2 * BLOCK, lanes]
            s_next = jnp.where(diff <= jnp.where(gblk < n_blocks - 1, 0, -BLOCK),
                               s_loc[2 * BLOCK:3 * BLOCK, lanes], NEG_INF)
            m_loc = jnp.max(jnp.maximum(jnp.maximum(s_prev, s_own), s_next), axis=0, keepdims=True)
            m = jnp.maximum(m, m_loc)
        e_ctx = jnp.exp2(s_ctx - m)
        ot = _dot(vt_ctx, e_ctx.astype(BF16))
        if local:
            e_loc = jnp.concatenate(
                [jnp.exp2(s_prev - m), jnp.exp2(s_own - m), jnp.exp2(s_next - m)], axis=0)
            ot = ot + _dot(vt_loc, e_loc.astype(BF16))
        den = ot[KV_WIDTH:KV_WIDTH + 1] + jnp.exp2(sink_j - m)
        h = (2 * j) // heads_per_kv
        ot = ot[h * HEAD_DIM:(h + 1) * HEAD_DIM] * (1.0 / den)
        pair = jnp.concatenate([ot[:, 0:BLOCK], ot[:, BLOCK:2 * BLOCK]], axis=0)
        zc = main_ref[0, rows, MAIN_ZC + j * LANES:MAIN_ZC + (j + 1) * LANES].astype(F32)
        yc_s[rows, j * LANES:(j + 1) * LANES] = (pair.T * _silu_of_half(zc)).astype(BF16)

    ahead = scores(0)
    for n in range(nsub):
        sgu(n)
        current, ahead = ahead, (scores(n + 1) if n + 1 < nsub else None)
        cols = slice(n * (d // nsub), (n + 1) * (d // nsub))
        ma_s[:, cols] = _dot(ya_s[...], wpa_ref[0, :, cols])
        vt_loc = None
        if local:
            vt_loc = with_ones(jnp.concatenate([vtbuf[n], vtbuf[n + 1], vtbuf[n + 2]], axis=1))
        for j in range(ATTN_WIDTH // LANES):
            if 0 < j <= MIX_GATES:
                c0 = (j - 1) * d + n * (d // nsub)
                g_s[j - 1, :, cols] = _dot(hb_ref[0], wg_ref[0, :, c0:c0 + d // nsub])
            attend(n, j, *current, vt_loc)

    row = pl.program_id(0) if ctx_row is None else ctx_row
    gate = mod_ref[0, pl.ds(row, 1), 2 * d:3 * d]
    width = 2 * LANES
    for c0 in range(0, d, width):
        cols = slice(c0, c0 + width)
        ma = ma_s[:, cols]
        mb = _dot(yb_s[...], wpb_ref[0, :, cols])
        mc = _dot(yc_s[...], wpc_ref[0, :, cols])
        ta, tb = (jnp.tanh(g_s[i, :, cols]) for i in range(MIX_GATES))
        tc = jnp.tanh(main_ref[0, :, MAIN_G + c0:MAIN_G + c0 + width].astype(F32))
        mg_s[:, cols] = ((ma + ta * ma) + (mb + tb * mb) + (mc + tc * mc)).astype(BF16)
    for c0 in range(0, d, width):
        cols = slice(c0, c0 + width)
        o_ref[0, :, cols] = x_ref[0, :, cols] + gate[:, cols] * _dot(mg_s[...], wout_ref[0, :, cols])


def _mix_call(x, main, hb, y, qt, k, vt, kc, vtc, mod, sinks, wsg, bsg, wpa, wpb, wpc, wout, wg, l,
              *, tile, local, ctx_row):
    b, length, d = x.shape
    nt = length // tile
    n_ctx = kc.shape[1]
    sub = tile // BLOCK
    n_blocks = length // BLOCK
    tok = lambda bi, i: (bi, i, 0)
    blk = lambda bi, i: (bi, i, 0, 0)
    per_b = lambda bi, i: (bi, 0, 0)
    per_b4 = lambda bi, i: (bi, 0, 0, 0)

    in_specs = [
        pl.BlockSpec((1, tile, d), tok),
        pl.BlockSpec((1, tile, MAIN_WIDTH), tok),
        pl.BlockSpec((1, tile, d), tok),
    ]
    args = [x, main, hb, y]
    if local:
        rows_k2 = tile // DFT_L1
        r = jnp.arange(tile)
        perm = (r[:, None] == (r[None, :] % rows_k2) * DFT_L1 + r[None, :] // rows_k2).astype(BF16)
        in_specs += [pl.BlockSpec((1, rows_k2, DFT_L1 * FOURIER_WIDTH), tok),
                     pl.BlockSpec((tile, tile), lambda bi, i: (0, 0))]
        args.append(perm)
    else:
        in_specs.append(pl.BlockSpec((1, tile, FOURIER_WIDTH), tok))
    in_specs.append(pl.BlockSpec((1, sub, ATTN_WIDTH, BLOCK), blk))
    args.append(qt)
    if local:
        prev_blk = lambda bi, i: jnp.maximum(i * sub - 1, 0)
        next_blk = lambda bi, i: jnp.minimum((i + 1) * sub, n_blocks - 1)
        in_specs += [
            pl.BlockSpec((1, BLOCK, KV_WIDTH), lambda bi, i: (bi, prev_blk(bi, i), 0)),
            pl.BlockSpec((1, tile, KV_WIDTH), tok),
            pl.BlockSpec((1, BLOCK, KV_WIDTH), lambda bi, i: (bi, next_blk(bi, i), 0)),
            pl.BlockSpec((1, 1, KV_WIDTH, BLOCK), lambda bi, i: (bi, prev_blk(bi, i), 0, 0)),
            pl.BlockSpec((1, sub, KV_WIDTH, BLOCK), blk),
            pl.BlockSpec((1, 1, KV_WIDTH, BLOCK), lambda bi, i: (bi, next_blk(bi, i), 0, 0)),
        ]
        args += [k, k, k, vt, vt, vt]
    in_specs += [
        pl.BlockSpec((1, n_ctx, KV_WIDTH), per_b),
        pl.BlockSpec((1, n_ctx // BLOCK, KV_WIDTH, BLOCK), per_b4),
        pl.BlockSpec((1, MOD_ROWS, 3 * d), _layer(l, 3)),
        pl.BlockSpec(memory_space=pltpu.SMEM),
    ]
    args += [kc, vtc, mod, sinks]
    for w in (wsg, bsg, wpa, wpb, wpc, wout):
        in_specs.append(pl.BlockSpec((1,) + w.shape[1:], _layer(l, 3)))
        args.append(w)
    in_specs.append(pl.BlockSpec((1,) + wg.shape[1:], _layer(l, 3), pipeline_mode=pl.Buffered(1)))
    args.append(wg)

    scratch = []
    if local:
        scratch += [pltpu.VMEM((tile + 2 * BLOCK, KV_WIDTH), BF16),
                    pltpu.VMEM((sub + 2, KV_WIDTH, BLOCK), BF16)]
    scratch += [
        pltpu.VMEM((tile, FOURIER_WIDTH), BF16),
        pltpu.VMEM((tile, SGU_WIDTH), BF16),
        pltpu.VMEM((tile, ATTN_WIDTH), BF16),
        pltpu.VMEM((tile, d), F32),
        pltpu.VMEM((MIX_GATES, tile, d), F32),
        pltpu.VMEM((tile, d), BF16),
    ]
    kern = functools.partial(_mix_kernel, local=local, tile=tile, n_blocks=n_blocks,
                             ctx_row=ctx_row, layer=l)
    return pl.pallas_call(
        kern,
        grid=(b, nt),
        in_specs=in_specs,
        out_specs=pl.BlockSpec((1, tile, d), tok),
        out_shape=jax.ShapeDtypeStruct((b, length, d), F32),
        scratch_shapes=scratch,
        compiler_params=pltpu.CompilerParams(
            dimension_semantics=("arbitrary", "arbitrary"), vmem_limit_bytes=VMEM_LIMIT),
        name="mix_lat" if local else "mix_ctx",
    )(*args)


def _rope_tables(length):
    nf = HEAD_DIM // 4
    inv = jnp.tile(ROPE_THETA ** (-jnp.arange(nf, dtype=F32) / nf), LANES // nf)[None, :]
    lane = jnp.arange(LANES)[None, :]
    on_row = (lane // (2 * nf)) % 2 == 0
    sign = jnp.where((lane // nf) % 2 == 0, -1.0, 1.0)
    ang_r = jnp.arange(length // GRID_W, dtype=F32)[:, None] * inv
    ang_c = jnp.arange(GRID_W, dtype=F32)[:, None] * inv

    def table(fn, scale):
        by_row = jnp.where(on_row, fn(ang_r) * scale, 0.0)[:, None, :]
        by_col = jnp.where(on_row, 0.0, fn(ang_c) * scale)[None, :, :]
        return (by_row + by_col).reshape(length, LANES)

    return table(jnp.cos, 1.0), table(jnp.sin, sign)


def _angle(num, den):
    return (num % den).astype(F32) * (2.0 * math.pi / den)


def _channel_dft(width):
    gw = FOURIER_WIDTH // FOURIER_GROUPS
    m = jnp.arange(width)
    same = (m[:, None] // gw) == (m[None, :] // gw)
    th = _angle((m[:, None] % gw) * (m[None, :] % gw), gw)
    cc = jnp.where(same, jnp.cos(th), 0.0) * gw ** -0.5
    sc = jnp.where(same, jnp.sin(th), 0.0) * gw ** -0.5
    return jnp.concatenate([cc, sc], axis=0).astype(BF16)


def _fourier_tables():
    l1n, l2n = DFT_L1, DFT_L2
    length = l1n * l2n
    pack = DFT_PACK
    l2 = jnp.arange(l2n)[:, None, None]
    k1 = jnp.arange(l1n)[None, :, None]
    l1 = jnp.arange(l1n)[None, None, :]
    th = _angle(k1 * l1 * l2n + k1 * l2, length)
    w1 = jnp.stack([jnp.cos(th), -jnp.sin(th)], axis=1) * l1n ** -0.5
    rows, cols = 2 * l1n * pack, l1n * pack
    ws = jnp.transpose(w1.reshape(l2n // pack, pack, 2 * l1n, l1n), (0, 2, 1, 3))
    ws = ws.reshape(l2n // pack, rows, l1n).astype(BF16)
    spread = (jnp.arange(cols)[None, :] // pack == jnp.arange(l1n)[:, None]).astype(BF16)
    same_p = (jnp.arange(rows)[:, None] % pack) == (jnp.arange(cols)[None, :] % pack)
    w1 = jnp.einsum("grb,bc->grc", ws, spread, preferred_element_type=F32)
    w1 = jnp.where(same_p[None], w1, 0.0).astype(BF16)
    k2 = jnp.arange(l2n)
    th2 = _angle(k2[:, None] * k2[None, :], l2n)
    c2, s2 = jnp.cos(th2), jnp.sin(th2)
    w2 = (jnp.concatenate([jnp.concatenate([c2, s2], axis=1),
                           jnp.concatenate([-s2, c2], axis=1)], axis=0) * l2n ** -0.5).astype(BF16)
    return w1, w2


def _fourier_ctx_table(n):
    k = jnp.arange(n)
    th = _angle(k[:, None] * k[None, :], n)
    return (jnp.concatenate([jnp.cos(th), -jnp.sin(th)], axis=0) * n ** -0.5).astype(BF16)


def _block_diag_ones(width):
    m = jnp.arange(width) // HEAD_DIM
    return (m[:, None] == m[None, :]).astype(BF16)


def _prep_w_in(w_in):
    col = jnp.arange(W_TOTAL)
    halved = ((col >= W_ZA) & (col < W_ZA + FOURIER_WIDTH)) | ((col >= W_ZB) & (col < W_Q)) | (col >= W_ZC)
    w_all = (w_in * jnp.where(halved, 0.5, 1.0)).astype(BF16)
    return w_all, w_all[:, :, W_G:W_MIX_END]


def kernel(x, c, ctx, c_ctx, w_ada, b_ada, w_in, sgu_w, sgu_b, q_norm_g, k_norm_g,
           attn_sink, w_pa, w_pb, w_pc, w_out):
    b, length, d = x.shape
    n_ctx = ctx.shape[1]
    assert length == DFT_L1 * DFT_L2 and d == D_MODEL and b + 1 <= MOD_ROWS
    assert w_in.shape == (DEPTH, d, W_TOTAL)
    lat_tile = 512
    ctx_row = b

    cs = jnp.zeros((MOD_ROWS, d), F32).at[:b].set(c).at[b].set(c_ctx)
    mod = _ada_call(cs, w_ada, b_ada)

    rope_tables = _rope_tables(length)
    w1, w2 = _fourier_tables()
    cs_dft = _channel_dft(FOURIER_WIDTH)
    wc = _fourier_ctx_table(n_ctx)
    bdq = _block_diag_ones(ATTN_WIDTH)
    bdk = _block_diag_ones(KV_WIDTH)

    w_all, wg = _prep_w_in(w_in)
    gq_all = jnp.tile(q_norm_g, (1, N_HEADS))[:, None, :]
    gk_all = jnp.tile(k_norm_g, (1, N_KV_HEADS))[:, None, :]
    wsg = jnp.transpose(sgu_w, (0, 2, 1, 3)).reshape(DEPTH, CHUNK, SGU_GROUPS * CHUNK).astype(BF16)
    bsg = jnp.repeat(jnp.swapaxes(sgu_b, 1, 2), SGU_WIDTH // SGU_GROUPS, axis=2)
    wpa, wpb, wpc = w_pa.astype(BF16), w_pb.astype(BF16), w_pc.astype(BF16)
    wout = (0.5 * w_out).astype(BF16)

    xc = ctx
    for l in range(DEPTH):
        last = l == DEPTH - 1
        fa_c, main_c, hb_c, qt_c, kc, vtc = _proj_call(
            xc.reshape(1, b * n_ctx, d), mod, w_all, bdq, bdk, gq_all, gk_all, None, l,
            tile=2 * n_ctx, ctx_row=ctx_row)
        kc = kc.reshape(b, n_ctx, KV_WIDTH)
        vtc = vtc.reshape(b, n_ctx // BLOCK, KV_WIDTH, BLOCK)
        fa, main, hb, qt, k, vt = _proj_call(
            x, mod, w_all, bdq, bdk, gq_all, gk_all, rope_tables, l, tile=lat_tile, ctx_row=None)
        y = _fourier_call(fa, w1, w2, cs_dft)
        x = _mix_call(x, main, hb, y, qt, k, vt, kc, vtc, mod, attn_sink, wsg, bsg, wpa, wpb, wpc, wout,
                      wg, l, tile=lat_tile, local=True, ctx_row=None)
        if not last:
            y_c = _fourier_ctx_call(fa_c.reshape(b, n_ctx, FOURIER_WIDTH), wc, cs_dft)
            xc = _mix_call(xc, main_c.reshape(b, n_ctx, MAIN_WIDTH), hb_c.reshape(b, n_ctx, d), y_c,
                           qt_c.reshape(b, n_ctx // BLOCK, ATTN_WIDTH, BLOCK), None, None, kc, vtc,
                           mod, attn_sink, wsg, bsg, wpa, wpb, wpc, wout, wg, l,
                           tile=n_ctx, local=False, ctx_row=ctx_row)
    return x
```
